```python
import math
import jax, jax.numpy as jnp
from jax import lax
import numpy as np

D_MODEL = 1024
BATCH = 8
SEQ = 2048
DEPTH = 4
DEC_BATCH = 128
DEC_SEQ = 1
PAST_LEN = 2048
PAGE_SIZE = 128

N_EVEN = (DEPTH + 1) // 2
N_ODD = DEPTH // 2

POOL_WIDTH = D_MODEL // 2
POOL_WINDOWS = (2, 4, 8, 16)
POOL_GROUP = POOL_WIDTH // len(POOL_WINDOWS)
POOL_BUF = max(POOL_WINDOWS) - 1
DIFF_HEADS = 4
DIFF_HALF = 64
DIFF_HEAD_DIM = 2 * DIFF_HALF
DIFF_WIDTH = DIFF_HEADS * DIFF_HEAD_DIM
Q_BLOCK = 128
EVEN_IN = POOL_WIDTH + 3 * DIFF_WIDTH
EVEN_MIX = POOL_WIDTH + DIFF_WIDTH
REL_BUCKETS = 32
REL_MAX_DIST = 128
GDN_HEADS = 8
GDN_DK = 128
GDN_DV = 128
GDN_CONV = 4
GDN_CHUNK = 64
GDN_QKV = GDN_HEADS * (2 * GDN_DK + GDN_DV)
ODD_IN = GDN_QKV + GDN_HEADS * GDN_DV + 2 * GDN_HEADS
D_FF = -(-8 * D_MODEL // (3 * 256)) * 256
EPS = 1e-6

kernel_name = 'hybrid_pool_diffattn_gdn_decode_step'


def rms_norm(x, g):
    xf = x.astype(jnp.float32)
    y = xf * lax.rsqrt(jnp.mean(xf * xf, -1, keepdims=True) + EPS)
    return (y * g.astype(jnp.float32)).astype(x.dtype)


def l2_normalize(x):
    xf = x.astype(jnp.float32)
    return xf * lax.rsqrt(jnp.sum(xf * xf, -1, keepdims=True) + EPS)


def swiglu(h, w_gu, w_d):
    gu = h @ w_gu
    return (jax.nn.silu(gu[..., :D_FF]) * gu[..., D_FF:]) @ w_d


def pool_mix(u, buf, pos, pool_w, pool_scale):
    T = u.shape[1]
    ext = jnp.concatenate([buf, u.astype(buf.dtype)], axis=1)
    cs = jnp.pad(jnp.cumsum(ext.astype(jnp.float32), axis=1), ((0, 0), (1, 0), (0, 0)))
    uf = u.astype(jnp.float32)
    outs = []
    for g, w in enumerate(POOL_WINDOWS):
        sl = slice(g * POOL_GROUP, (g + 1) * POOL_GROUP)
        win = cs[:, POOL_BUF + 1:POOL_BUF + 1 + T, sl] - cs[:, POOL_BUF + 1 - w:POOL_BUF + 1 - w + T, sl]
        cnt = jnp.minimum(pos + 1, w).astype(jnp.float32)[None, :, None]
        d = win / cnt - uf[:, :, sl]
        outs.append(jnp.einsum('btc,cd->btd', d, pool_w[g].astype(jnp.float32)))
    y = jnp.concatenate(outs, -1) * pool_scale.astype(jnp.float32)
    return y.astype(u.dtype), ext[:, -POOL_BUF:]


def rel_bucket(n):
    max_exact = REL_BUCKETS // 2
    nf = jnp.maximum(n, 1).astype(jnp.float32)
    large = max_exact + (jnp.log(nf / max_exact) / math.log(REL_MAX_DIST / max_exact)
                         * (REL_BUCKETS - max_exact)).astype(jnp.int32)
    large = jnp.minimum(large, REL_BUCKETS - 1)
    return jnp.where(n < max_exact, n, large)


def diff_attention(q, k, v, q_pos, k_pos, rel_table, lam):
    B, T, H, Dq = q.shape
    qb = min(Q_BLOCK, T)
    nb = -(-T // qb)
    pad = nb * qb - T
    qp = jnp.pad(q, ((0, 0), (0, pad), (0, 0), (0, 0)))
    posp = jnp.pad(q_pos, (0, pad), mode='edge')
    q_blocks = jnp.moveaxis(qp.reshape(B, nb, qb, H, Dq), 1, 0)
    p_blocks = posp.reshape(nb, qb)
    kf = k.astype(jnp.float32)
    vf = v.astype(jnp.float32)
    k1, k2 = kf[..., :DIFF_HALF], kf[..., DIFF_HALF:]
    table = rel_table.astype(jnp.float32)
    scale = DIFF_HALF ** -0.5

    def block(args):
        q_blk, p_blk = args
        qf = q_blk.astype(jnp.float32)
        dist = p_blk[:, None] - k_pos[None, :]
        causal = (dist >= 0)[None, None]
        bias = jnp.transpose(table[rel_bucket(jnp.maximum(dist, 0))], (2, 0, 1))[None]

        def probs(qh, kh):
            s = jnp.einsum('bqhd,bkhd->bhqk', qh, kh) * scale + bias
            return jax.nn.softmax(jnp.where(causal, s, -jnp.inf), axis=-1)

        a = probs(qf[..., :DIFF_HALF], k1) - lam * probs(qf[..., DIFF_HALF:], k2)
        return jnp.einsum('bhqk,bkhd->bqhd', a, vf)

    out = lax.map(block, (q_blocks, p_blocks))
    return jnp.moveaxis(out, 0, 1).reshape(B, nb * qb, H, v.shape[-1])[:, :T]


def pool_diff_mixer(h, pool_buf, k_past, v_past, layer, rel_bias, w_in, pool_w, pool_scale,
                    lq1, lk1, lq2, lk2, subln_w, w_out):
    B, T, _ = h.shape
    P = k_past.shape[1]
    proj = h @ w_in
    u = proj[..., :POOL_WIDTH]
    o0 = POOL_WIDTH
    q = proj[..., o0:o0 + DIFF_WIDTH].reshape(B, T, DIFF_HEADS, DIFF_HEAD_DIM)
    k_new = proj[..., o0 + DIFF_WIDTH:o0 + 2 * DIFF_WIDTH].reshape(B, T, DIFF_HEADS, DIFF_HEAD_DIM)
    v_new = proj[..., o0 + 2 * DIFF_WIDTH:].reshape(B, T, DIFF_HEADS, DIFF_HEAD_DIM)
    q_pos = P + jnp.arange(T, dtype=jnp.int32)
    y_pool, new_buf = pool_mix(u, pool_buf, q_pos, pool_w, pool_scale)
    lam_init = 0.8 - 0.6 * math.exp(-0.3 * layer)
    f = lambda a: a.astype(jnp.float32)
    lam = jnp.exp(jnp.sum(f(lq1) * f(lk1))) - jnp.exp(jnp.sum(f(lq2) * f(lk2))) + lam_init
    k_all = jnp.concatenate([k_past, k_new.astype(k_past.dtype)], axis=1)
    v_all = jnp.concatenate([v_past, v_new.astype(v_past.dtype)], axis=1)
    k_pos = jnp.arange(P + T, dtype=jnp.int32)
    o = diff_attention(q, k_all, v_all, q_pos, k_pos, rel_bias, lam)
    o = rms_norm(o, subln_w) * (1.0 - lam_init)
    mixed = jnp.concatenate([y_pool, o.reshape(B, T, DIFF_WIDTH).astype(h.dtype)], axis=-1)
    return mixed @ w_out, new_buf, k_new, v_new


def _chunks(a, C, nc):
    B, T = a.shape[:2]
    a = jnp.pad(a.astype(jnp.float32), [(0, 0), (0, nc * C - T)] + [(0, 0)] * (a.ndim - 2))
    a = a.reshape((B, nc, C) + a.shape[2:])
    return jnp.moveaxis(jnp.moveaxis(a, 3, 2), 1, 0)


def gated_delta_rule(q, k, v, g, beta, S0):
    B, T, H, _ = q.shape
    DV = v.shape[-1]
    C = min(GDN_CHUNK, T)
    nc = -(-T // C)
    qc, kc, vc = _chunks(q, C, nc), _chunks(k, C, nc), _chunks(v, C, nc)
    gc, bc = _chunks(g, C, nc), _chunks(beta, C, nc)
    Gc = jnp.cumsum(gc, axis=-1)
    idx = jnp.arange(C)
    incl = idx[:, None] >= idx[None, :]
    strict = idx[:, None] > idx[None, :]
    decay = jnp.exp(jnp.where(incl, Gc[..., :, None] - Gc[..., None, :], -jnp.inf))
    kk = jnp.einsum('...id,...jd->...ij', kc, kc)
    M = jnp.eye(C, dtype=jnp.float32) + jnp.where(strict, bc[..., :, None] * kk * decay, 0.0)
    gam = jnp.exp(Gc)
    W = lax.linalg.triangular_solve(M, (bc * gam)[..., None] * kc, left_side=True, lower=True, unit_diagonal=True)
    U = lax.linalg.triangular_solve(M, bc[..., None] * vc, left_side=True, lower=True, unit_diagonal=True)
    qk = jnp.einsum('...id,...jd->...ij', qc, kc) * decay
    q_dec = gam[..., None] * qc
    k_dec = jnp.exp(Gc[..., -1:] - Gc)[..., None] * kc
    g_last = jnp.exp(Gc[..., -1])

    def step(S, xs):
        W_, U_, qk_, qd_, kd_, gl_ = xs
        u_ = U_ - jnp.einsum('bhck,bhkv->bhcv', W_, S)
        o_ = jnp.einsum('bhck,bhkv->bhcv', qd_, S) + jnp.einsum('bhij,bhjv->bhiv', qk_, u_)
        S = gl_[..., None, None] * S + jnp.einsum('bhck,bhcv->bhkv', kd_, u_)
        return S, o_

    S_fin, O = lax.scan(step, S0.astype(jnp.float32), (W, U, qk, q_dec, k_dec, g_last))
    O = jnp.transpose(O, (1, 0, 3, 2, 4)).reshape(B, nc * C, H, DV)[:, :T]
    return O, S_fin


def gdn_mixer(h, conv_buf, S0, w_in, conv_w, a_log, dt_bias, o_norm, w_out):
    B, T, _ = h.shape
    HK, HV = GDN_HEADS * GDN_DK, GDN_HEADS * GDN_DV
    proj = h @ w_in
    qkv = proj[..., :GDN_QKV]
    z = proj[..., GDN_QKV:GDN_QKV + HV]
    a = proj[..., GDN_QKV + HV:GDN_QKV + HV + GDN_HEADS]
    b = proj[..., GDN_QKV + HV + GDN_HEADS:]
    ext = jnp.concatenate([conv_buf, qkv.astype(conv_buf.dtype)], axis=1)
    cw = conv_w.astype(jnp.float32)
    conv = ext[:, 0:T].astype(jnp.float32) * cw[0]
    for i in range(1, GDN_CONV):
        conv = conv + ext[:, i:i + T].astype(jnp.float32) * cw[i]
    conv = jax.nn.silu(conv)
    q = l2_normalize(conv[..., :HK].reshape(B, T, GDN_HEADS, GDN_DK)) * (GDN_DK ** -0.5)
    k = l2_normalize(conv[..., HK:2 * HK].reshape(B, T, GDN_HEADS, GDN_DK))
    v = conv[..., 2 * HK:].reshape(B, T, GDN_HEADS, GDN_DV)
    g = -jnp.exp(a_log.astype(jnp.float32)) * jax.nn.softplus(a.astype(jnp.float32) + dt_bias.astype(jnp.float32))
    beta = jax.nn.sigmoid(b.astype(jnp.float32))
    o, S = gated_delta_rule(q, k, v, g, beta, S0)
    o = rms_norm(o, o_norm) * jax.nn.silu(z.astype(jnp.float32).reshape(B, T, GDN_HEADS, GDN_DV))
    y = o.reshape(B, T, HV).astype(h.dtype) @ w_out
    return y, ext[:, -(GDN_CONV - 1):], S.astype(S0.dtype)


def setup_inputs(seed: int = 0) -> dict:
    key = jax.random.key(seed)
    ks = jax.random.split(key, 32)
    f32 = jnp.float32

    def nrm(i, shape, scale):
        return jax.random.normal(ks[i], shape, f32) * scale

    n_pages = PAST_LEN // PAGE_SIZE
    n_used = DEC_BATCH * n_pages
    n_pool = n_used + -(-n_used // 4)
    perm = jax.random.permutation(ks[4], n_pool)
    page_table = perm[:n_used].reshape(DEC_BATCH, n_pages).astype(jnp.int32)
    dt = jnp.exp(jax.random.uniform(ks[24], (N_ODD, GDN_HEADS), f32, math.log(1e-3), math.log(1e-1)))
    return {
        'x_prompt': nrm(0, (BATCH, SEQ, D_MODEL), 1.0),
        'x_sample': nrm(1, (DEC_BATCH, DEC_SEQ, D_MODEL), 1.0),
        'cache_k': nrm(2, (n_pool, N_EVEN, PAGE_SIZE, DIFF_HEADS, DIFF_HEAD_DIM), 1.0),
        'cache_v': nrm(3, (n_pool, N_EVEN, PAGE_SIZE, DIFF_HEADS, DIFF_HEAD_DIM), 1.0),
        'page_table': page_table,
        'state_pool': nrm(5, (N_EVEN, DEC_BATCH, POOL_BUF, POOL_WIDTH), 1.0),
        'state_conv': nrm(6, (N_ODD, DEC_BATCH, GDN_CONV - 1, GDN_QKV), 1.0),
        'state_delta': nrm(7, (N_ODD, DEC_BATCH, GDN_HEADS, GDN_DK, GDN_DV), 0.5),
        'norm_mix': 1.0 + nrm(8, (DEPTH, D_MODEL), 0.02),
        'norm_ffn': 1.0 + nrm(9, (DEPTH, D_MODEL), 0.02),
        'norm_final': 1.0 + nrm(10, (D_MODEL,), 0.02),
        'rel_bias': nrm(11, (REL_BUCKETS, DIFF_HEADS), 0.5),
        'w_in_even': nrm(12, (N_EVEN, D_MODEL, EVEN_IN), D_MODEL ** -0.5),
        'pool_w': nrm(13, (N_EVEN, len(POOL_WINDOWS), POOL_GROUP, POOL_GROUP), POOL_GROUP ** -0.5),
        'pool_scale': 1.0 + nrm(14, (N_EVEN, POOL_WIDTH), 0.02),
        'lambda_q1': nrm(15, (N_EVEN, DIFF_HALF), 0.1),
        'lambda_k1': nrm(16, (N_EVEN, DIFF_HALF), 0.1),
        'lambda_q2': nrm(17, (N_EVEN, DIFF_HALF), 0.1),
        'lambda_k2': nrm(18, (N_EVEN, DIFF_HALF), 0.1),
        'subln_w': 1.0 + nrm(19, (N_EVEN, DIFF_HEAD_DIM), 0.02),
        'w_out_even': nrm(20, (N_EVEN, EVEN_MIX, D_MODEL), EVEN_MIX ** -0.5),
        'w_in_odd': nrm(21, (N_ODD, D_MODEL, ODD_IN), D_MODEL ** -0.5),
        'conv_w': nrm(22, (N_ODD, GDN_CONV, GDN_QKV), 0.5),
        'a_log': jnp.log(jax.random.uniform(ks[23], (N_ODD, GDN_HEADS), f32, 1.0, 16.0)),
        'dt_bias': dt + jnp.log(-jnp.expm1(-dt)),
        'o_norm': 1.0 + nrm(25, (N_ODD, GDN_DV), 0.02),
        'w_out_odd': nrm(26, (N_ODD, GDN_HEADS * GDN_DV, D_MODEL), (GDN_HEADS * GDN_DV) ** -0.5),
        'w_gate_up': nrm(27, (DEPTH, D_MODEL, 2 * D_FF), D_MODEL ** -0.5),
        'w_down': nrm(28, (DEPTH, D_FF, D_MODEL), D_FF ** -0.5),
    }


def reference(x_prompt, x_sample, cache_k, cache_v, page_table, state_pool, state_conv, state_delta,
              norm_mix, norm_ffn, norm_final, rel_bias,
              w_in_even, pool_w, pool_scale, lambda_q1, lambda_k1, lambda_q2, lambda_k2, subln_w, w_out_even,
              w_in_odd, conv_w, a_log, dt_bias, o_norm, w_out_odd,
              w_gate_up, w_down):
    hp, hs = x_prompt, x_sample
    bp, bs = hp.shape[0], hs.shape[0]
    n_pages = page_table.shape[1]
    k_p, v_p, k_s, v_s, pool_p, pool_s = [], [], [], [], [], []
    conv_p, conv_s, delta_p, delta_s = [], [], [], []
    for layer in range(DEPTH):
        j = layer // 2
        zp = rms_norm(hp, norm_mix[layer])
        zs = rms_norm(hs, norm_mix[layer])
        if layer % 2 == 0:
            wts = (rel_bias, w_in_even[j], pool_w[j], pool_scale[j], lambda_q1[j], lambda_k1[j],
                   lambda_q2[j], lambda_k2[j], subln_w[j], w_out_even[j])
            no_past = jnp.zeros((bp, 0, DIFF_HEADS, DIFF_HEAD_DIM), hp.dtype)
            yp, pb, kp, vp = pool_diff_mixer(zp, jnp.zeros((bp, POOL_BUF, POOL_WIDTH), hp.dtype),
                                             no_past, no_past, layer, *wts)
            k_past = cache_k[page_table, j].reshape(bs, n_pages * PAGE_SIZE, DIFF_HEADS, DIFF_HEAD_DIM)
            v_past = cache_v[page_table, j].reshape(bs, n_pages * PAGE_SIZE, DIFF_HEADS, DIFF_HEAD_DIM)
            ys, sb, ks_, vs_ = pool_diff_mixer(zs, state_pool[j], k_past, v_past, layer, *wts)
            k_p.append(kp); v_p.append(vp); k_s.append(ks_); v_s.append(vs_)
            pool_p.append(pb); pool_s.append(sb)
        else:
            wts = (w_in_odd[j], conv_w[j], a_log[j], dt_bias[j], o_norm[j], w_out_odd[j])
            yp, cb_p, S_p = gdn_mixer(zp, jnp.zeros((bp, GDN_CONV - 1, GDN_QKV), hp.dtype),
                                      jnp.zeros((bp, GDN_HEADS, GDN_DK, GDN_DV), hp.dtype), *wts)
            ys, cb_s, S_s = gdn_mixer(zs, state_conv[j], state_delta[j], *wts)
            conv_p.append(cb_p); conv_s.append(cb_s); delta_p.append(S_p); delta_s.append(S_s)
        hp = hp + yp
        hs = hs + ys
        hp = hp + swiglu(rms_norm(hp, norm_ffn[layer]), w_gate_up[layer], w_down[layer])
        hs = hs + swiglu(rms_norm(hs, norm_ffn[layer]), w_gate_up[layer], w_down[layer])
    y_prompt = rms_norm(hp, norm_final)
    y_sample = rms_norm(hs, norm_final)
    return (y_prompt, y_sample,
            jnp.stack(k_p, axis=1), jnp.stack(v_p, axis=1), jnp.stack(k_s, axis=1), jnp.stack(v_s, axis=1),
            jnp.stack(pool_p), jnp.stack(pool_s), jnp.stack(conv_p), jnp.stack(conv_s),
            jnp.stack(delta_p), jnp.stack(delta_s))
```

```python
import functools
import math

import jax
import jax.numpy as jnp
from jax import lax
from jax.experimental import pallas as pl
from jax.experimental.pallas import tpu as pltpu

F32 = jnp.float32
BF16 = jnp.bfloat16

D_MODEL = 1024
DEPTH = 4
PAGE_SIZE = 128
POOL_WIDTH = 512
POOL_WINDOWS = (2, 4, 8, 16)
POOL_GROUP = 128
POOL_BUF = 15
DIFF_HEADS = 4
DIFF_HALF = 64
DIFF_HEAD_DIM = 128
DIFF_WIDTH = 512
EVEN_IN = 2048
REL_BUCKETS = 32
REL_MAX_DIST = 128
GDN_HEADS = 8
GDN_DK = 128
GDN_DV = 128
GDN_CONV = 4
GDN_QKV = 3072
GDN_MAIN = 4096
D_FF = 2816
EPS = 1e-6

LANES = 128
VMEM_LIMIT = 48 * 1024 * 1024
NEG_BIG = -1e30

ATT_TQ = 256
ATT_TK = 256
GDN_C = 128
INV_BLOCK = 16
FFN_TF = 256

_NT = (((1,), (1,)), ((), ()))
_TN = (((0,), (0,)), ((), ()))


def _cparams(sem):
    return pltpu.CompilerParams(dimension_semantics=sem, vmem_limit_bytes=VMEM_LIMIT)


def _sigmoid(x):
    return 1.0 / (1.0 + jnp.exp(-x))


def _softplus(x):
    return jnp.maximum(x, 0.0) + jnp.log1p(jnp.exp(-jnp.abs(x)))


def _rms(x, gain):
    return x * lax.rsqrt(jnp.mean(x * x, axis=-1, keepdims=True) + EPS) * gain


def _mm(a, b):
    return jnp.dot(a.astype(BF16), b.astype(BF16), preferred_element_type=F32)


def _norm_mm_body(x_ref, g_ref, w_ref, o_ref, xn_ref):
    @pl.when(pl.program_id(1) == 0)
    def _():
        xn_ref[...] = _rms(x_ref[...], g_ref[...]).astype(BF16)

    o_ref[...] = jnp.dot(xn_ref[...], w_ref[...], preferred_element_type=F32)


def _norm_mm_gate_body(x_ref, g_ref, w_ref, wst_ref, o_ref, ost_ref, xn_ref):
    @pl.when(pl.program_id(1) == 0)
    def _():
        xn = _rms(x_ref[...], g_ref[...]).astype(BF16)
        xn_ref[...] = xn
        ost_ref[...] = lax.dot_general(wst_ref[...], xn, _NT, preferred_element_type=F32)

    o_ref[...] = jnp.dot(xn_ref[...], w_ref[...], preferred_element_type=F32)


def norm_matmul(x, gain, w, w_gates_t=None, *, tm, tn):
    m, d = x.shape
    n = w.shape[1]
    grid = (m // tm, n // tn)
    in_specs = [
        pl.BlockSpec((tm, d), lambda i, j: (i, 0)),
        pl.BlockSpec((1, d), lambda i, j: (0, 0)),
        pl.BlockSpec((d, tn), lambda i, j: (0, j)),
    ]
    out_specs = pl.BlockSpec((tm, tn), lambda i, j: (i, j))
    out_shape = jax.ShapeDtypeStruct((m, n), F32)
    args = [x, gain.reshape(1, d), w]
    body = _norm_mm_body
    if w_gates_t is not None:
        in_specs.append(pl.BlockSpec((LANES, d), lambda i, j: (0, 0)))
        out_specs = [out_specs, pl.BlockSpec((LANES, tm), lambda i, j: (0, i))]
        out_shape = [out_shape, jax.ShapeDtypeStruct((LANES, m), F32)]
        args.append(w_gates_t)
        body = _norm_mm_gate_body
    return pl.pallas_call(
        body,
        grid=grid,
        in_specs=in_specs,
        out_specs=out_specs,
        out_shape=out_shape,
        scratch_shapes=[pltpu.VMEM((tm, d), BF16)],
        compiler_params=_cparams(("parallel", "arbitrary")),
        name="norm_matmul",
    )(*args)


def _ffn_body(*refs, final):
    if final:
        (res_ref, a0_ref, a1_ref, wo0_ref, wo1_ref, gf_ref, wg_ref, wu_ref, wd_ref, gfin_ref,
         o_ref, h1_ref, xn_ref, acc_ref) = refs
    else:
        (res_ref, a0_ref, a1_ref, wo0_ref, wo1_ref, gf_ref, wg_ref, wu_ref, wd_ref,
         o_ref, h1_ref, xn_ref, acc_ref) = refs
    f = pl.program_id(1)

    @pl.when(f == 0)
    def _():
        h1 = (res_ref[...]
              + jnp.dot(a0_ref[...], wo0_ref[...], preferred_element_type=F32)
              + jnp.dot(a1_ref[...], wo1_ref[...], preferred_element_type=F32))
        h1_ref[...] = h1
        xn_ref[...] = _rms(h1, gf_ref[...]).astype(BF16)
        acc_ref[...] = jnp.zeros_like(acc_ref)

    xn = xn_ref[...]
    g = jnp.dot(xn, wg_ref[...], preferred_element_type=F32)
    u = jnp.dot(xn, wu_ref[...], preferred_element_type=F32)
    act = (g * _sigmoid(g) * u).astype(BF16)
    acc_ref[...] += jnp.dot(act, wd_ref[...], preferred_element_type=F32)

    @pl.when(f == pl.num_programs(1) - 1)
    def _():
        y = h1_ref[...] + acc_ref[...]
        if final:
            y = _rms(y, gfin_ref[...])
        o_ref[...] = y


def outproj_ffn(res, mix0, mix1, w_out, g_ffn, w_gu, w_d, g_final=None, *, tm):
    m, d = res.shape
    half = w_out.shape[0] // 2
    nf = D_FF // FFN_TF
    (m0, c0), (m1, c1) = mix0, mix1
    final = g_final is not None
    in_specs = [
        pl.BlockSpec((tm, d), lambda i, f: (i, 0)),
        pl.BlockSpec((tm, half), lambda i, f: (i, c0)),
        pl.BlockSpec((tm, half), lambda i, f: (i, c1)),
        pl.BlockSpec((half, d), lambda i, f: (0, 0)),
        pl.BlockSpec((half, d), lambda i, f: (1, 0)),
        pl.BlockSpec((1, d), lambda i, f: (0, 0)),
        pl.BlockSpec((d, FFN_TF), lambda i, f: (0, f)),
        pl.BlockSpec((d, FFN_TF), lambda i, f: (0, nf + f)),
        pl.BlockSpec((FFN_TF, d), lambda i, f: (f, 0)),
    ]
    args = [res, m0, m1, w_out, w_out, g_ffn.reshape(1, d), w_gu, w_gu, w_d]
    if final:
        in_specs.append(pl.BlockSpec((1, d), lambda i, f: (0, 0)))
        args.append(g_final.reshape(1, d))
    return pl.pallas_call(
        functools.partial(_ffn_body, final=final),
        grid=(m // tm, nf),
        in_specs=in_specs,
        out_specs=pl.BlockSpec((tm, d), lambda i, f: (i, 0)),
        out_shape=jax.ShapeDtypeStruct((m, d), F32),
        scratch_shapes=[pltpu.VMEM((tm, d), F32), pltpu.VMEM((tm, d), BF16), pltpu.VMEM((tm, d), F32)],
        compiler_params=_cparams(("parallel", "arbitrary")),
        name="outproj_ffn",
    )(*args)


POOL_PAD = 16
POOL_ROWS = 512


def _pool_prompt_body(u_ref, pw_ref, ps_ref, o_ref, ext_ref):
    t_len = u_ref.shape[0]
    ext_ref[0:POOL_PAD, :] = jnp.zeros((POOL_PAD, POOL_WIDTH), F32)
    ext_ref[POOL_PAD:, :] = u_ref[...]
    for r0 in range(0, t_len, POOL_ROWS):
        pos = lax.broadcasted_iota(jnp.int32, (POOL_ROWS, 1), 0) + r0
        for g, w in enumerate(POOL_WINDOWS):
            sl = slice(g * POOL_GROUP, (g + 1) * POOL_GROUP)
            cur = ext_ref[POOL_PAD + r0:POOL_PAD + r0 + POOL_ROWS, sl]
            win = cur
            for i in range(1, w):
                win = win + ext_ref[POOL_PAD + r0 - i:POOL_PAD + r0 - i + POOL_ROWS, sl]
            cnt = jnp.minimum(pos + 1, w).astype(F32)
            dlt = win / cnt - cur
            y = jnp.dot(dlt.astype(BF16), pw_ref[g], preferred_element_type=F32) * ps_ref[:, sl]
            o_ref[r0:r0 + POOL_ROWS, sl] = y.astype(BF16)


def pool_prompt(proj, pool_w, pool_scale):
    b, t, _ = proj.shape
    return pl.pallas_call(
        _pool_prompt_body,
        grid=(b,),
        in_specs=[
            pl.BlockSpec((None, t, POOL_WIDTH), lambda i: (i, 0, 0)),
            pl.BlockSpec((len(POOL_WINDOWS), POOL_GROUP, POOL_GROUP), lambda i: (0, 0, 0)),
            pl.BlockSpec((1, POOL_WIDTH), lambda i: (0, 0)),
        ],
        out_specs=pl.BlockSpec((None, t, POOL_WIDTH), lambda i: (i, 0, 0)),
        out_shape=jax.ShapeDtypeStruct((b, t, POOL_WIDTH), BF16),
        scratch_shapes=[pltpu.VMEM((t + POOL_PAD, POOL_WIDTH), F32)],
        compiler_params=_cparams(("parallel",)),
        name="pool_prompt",
    )(proj, pool_w, pool_scale.reshape(1, POOL_WIDTH))


def _pool_decode_body(st_ref, u_ref, pw_ref, ps_ref, o_ref):
    for g, w in enumerate(POOL_WINDOWS):
        sl = slice(g * POOL_GROUP, (g + 1) * POOL_GROUP)
        cur = u_ref[:, sl]
        win = cur
        for i in range(1, w):
            win = win + st_ref[POOL_BUF - i, :, sl]
        dlt = win / float(w) - cur
        y = jnp.dot(dlt.astype(BF16), pw_ref[g], preferred_element_type=F32) * ps_ref[:, sl]
        o_ref[:, sl] = y.astype(BF16)


def pool_decode(state_t, proj, pool_w, pool_scale):
    nb = proj.shape[0]
    return pl.pallas_call(
        _pool_decode_body,
        grid=(1,),
        in_specs=[
            pl.BlockSpec((POOL_BUF, nb, POOL_WIDTH), lambda i: (0, 0, 0)),
            pl.BlockSpec((nb, POOL_WIDTH), lambda i: (0, 0)),
            pl.BlockSpec((len(POOL_WINDOWS), POOL_GROUP, POOL_GROUP), lambda i: (0, 0, 0)),
            pl.BlockSpec((1, POOL_WIDTH), lambda i: (0, 0)),
        ],
        out_specs=pl.BlockSpec((nb, POOL_WIDTH), lambda i: (0, 0)),
        out_shape=jax.ShapeDtypeStruct((nb, POOL_WIDTH), BF16),
        compiler_params=_cparams(("arbitrary",)),
        name="pool_decode",
    )(state_t, proj, pool_w, pool_scale.reshape(1, POOL_WIDTH))


def _rel_bucket(n):
    max_exact = REL_BUCKETS // 2
    nf = jnp.maximum(n, 1).astype(F32)
    large = max_exact + (jnp.log(nf / max_exact) / math.log(REL_MAX_DIST / max_exact)
                         * (REL_BUCKETS - max_exact)).astype(jnp.int32)
    large = jnp.minimum(large, REL_BUCKETS - 1)
    return jnp.where(n < max_exact, n, large)


def _table_lookup(tab_ref, bucket, h):
    out = jnp.zeros(bucket.shape, F32)
    for b in range(REL_BUCKETS):
        out = jnp.where(bucket == b, tab_ref[b, h], out)
    return out


def _rel_bias_body(tab_ref, bp_ref, bd_ref, bn_ref, *, past):
    ii = lax.broadcasted_iota(jnp.int32, (ATT_TQ, ATT_TK), 0)
    jj = lax.broadcasted_iota(jnp.int32, (ATT_TQ, ATT_TK), 1)
    for h in range(DIFF_HEADS):
        for t in range(3):
            dist = t * ATT_TK + ii - jj
            bias = _table_lookup(tab_ref, _rel_bucket(jnp.maximum(dist, 0)), h)
            bias = jnp.where(dist >= 0, bias, NEG_BIG)
            bp_ref[h, t, 0:ATT_TQ, :] = bias
            bp_ref[h, t, ATT_TQ:2 * ATT_TQ, :] = bias
    row = lax.broadcasted_iota(jnp.int32, (2 * DIFF_HEADS, past), 0)
    col = lax.broadcasted_iota(jnp.int32, (2 * DIFF_HEADS, past), 1)
    bucket = _rel_bucket(past - col)
    rown = lax.broadcasted_iota(jnp.int32, (2 * DIFF_HEADS, LANES), 0)
    bd = jnp.zeros((2 * DIFF_HEADS, past), F32)
    bn = jnp.zeros((2 * DIFF_HEADS, LANES), F32)
    for h in range(DIFF_HEADS):
        bd = jnp.where(row // 2 == h, _table_lookup(tab_ref, bucket, h), bd)
        bn = jnp.where(rown // 2 == h, tab_ref[0, h], bn)
    bd_ref[...] = bd
    bn_ref[...] = bn


def rel_bias_tiles(rel_bias, past):
    return pl.pallas_call(
        functools.partial(_rel_bias_body, past=past),
        in_specs=[pl.BlockSpec(memory_space=pltpu.SMEM)],
        out_specs=[
            pl.BlockSpec(memory_space=pltpu.VMEM),
            pl.BlockSpec(memory_space=pltpu.VMEM),
            pl.BlockSpec(memory_space=pltpu.VMEM),
        ],
        out_shape=[
            jax.ShapeDtypeStruct((DIFF_HEADS, 3, 2 * ATT_TQ, ATT_TK), F32),
            jax.ShapeDtypeStruct((2 * DIFF_HEADS, past), F32),
            jax.ShapeDtypeStruct((2 * DIFF_HEADS, LANES), F32),
        ],
        compiler_params=pltpu.CompilerParams(vmem_limit_bytes=VMEM_LIMIT),
        name="rel_bias_tiles",
    )(rel_bias)


def _lambda(lam_ref, lam_init):
    lp = lam_ref[...]
    s1 = jnp.sum(lp[0:1, :] * lp[1:2, :], axis=-1, keepdims=True)
    s2 = jnp.sum(lp[2:3, :] * lp[3:4, :], axis=-1, keepdims=True)
    return jnp.exp(s1) - jnp.exp(s2) + lam_init


def _attn_prompt_body(q_ref, k_ref, v_ref, bias_ref, lam_ref, sub_ref, o_ref,
                      kb_ref, vb_ref, m_ref, l_ref, acc_ref, *, lam_init):
    qi = pl.program_id(2)

    @pl.when(qi == 0)
    def _():
        kb_ref[...] = k_ref[...].astype(BF16)
        vb_ref[...] = v_ref[...].astype(BF16)

    q = q_ref[...] * (DIFF_HALF ** -0.5)
    lane = lax.broadcasted_iota(jnp.int32, q.shape, 1)
    qq = jnp.concatenate([jnp.where(lane < DIFF_HALF, q, 0.0),
                          jnp.where(lane >= DIFF_HALF, q, 0.0)], axis=0).astype(BF16)
    m_ref[...] = jnp.full(m_ref.shape, -jnp.inf, F32)
    l_ref[...] = jnp.zeros_like(l_ref)
    acc_ref[...] = jnp.zeros_like(acc_ref)

    def step(j, carry):
        off = pl.multiple_of(j * ATT_TK, ATT_TK)
        kj = kb_ref[pl.ds(off, ATT_TK), :]
        vj = vb_ref[pl.ds(off, ATT_TK), :]
        s = lax.dot_general(qq, kj, _NT, preferred_element_type=F32) + bias_ref[jnp.minimum(qi - j, 2)]
        m_prev = m_ref[...]
        m_new = jnp.maximum(m_prev, jnp.max(s, axis=-1, keepdims=True))
        alpha = jnp.exp(m_prev - m_new)
        p = jnp.exp(s - m_new)
        l_ref[...] = alpha * l_ref[...] + jnp.sum(p, axis=-1, keepdims=True)
        acc_ref[...] = alpha * acc_ref[...] + jnp.dot(p.astype(BF16), vj, preferred_element_type=F32)
        m_ref[...] = m_new
        return carry

    lax.fori_loop(0, qi + 1, step, 0)

    o_all = acc_ref[...] / l_ref[...]
    o = o_all[0:ATT_TQ, :] - _lambda(lam_ref, lam_init) * o_all[ATT_TQ:2 * ATT_TQ, :]
    o_ref[...] = (_rms(o, sub_ref[...]) * (1.0 - lam_init)).astype(BF16)


def attn_prompt(proj, bias_p, lam_params, subln, lam_init):
    b, t, _ = proj.shape
    nh = DIFF_HEADS
    return pl.pallas_call(
        functools.partial(_attn_prompt_body, lam_init=lam_init),
        grid=(b, nh, t // ATT_TQ),
        in_specs=[
            pl.BlockSpec((None, ATT_TQ, LANES), lambda i, h, q: (i, q, nh + h)),
            pl.BlockSpec((None, t, LANES), lambda i, h, q: (i, 0, 2 * nh + h)),
            pl.BlockSpec((None, t, LANES), lambda i, h, q: (i, 0, 3 * nh + h)),
            pl.BlockSpec((None, 3, 2 * ATT_TQ, ATT_TK), lambda i, h, q: (h, 0, 0, 0)),
            pl.BlockSpec((4, DIFF_HALF), lambda i, h, q: (0, 0)),
            pl.BlockSpec((1, LANES), lambda i, h, q: (0, 0)),
        ],
        out_specs=pl.BlockSpec((None, ATT_TQ, LANES), lambda i, h, q: (i, q, h)),
        out_shape=jax.ShapeDtypeStruct((b, t, DIFF_WIDTH), BF16),
        scratch_shapes=[
            pltpu.VMEM((t, LANES), BF16),
            pltpu.VMEM((t, LANES), BF16),
            pltpu.VMEM((2 * ATT_TQ, 1), F32),
            pltpu.VMEM((2 * ATT_TQ, 1), F32),
            pltpu.VMEM((2 * ATT_TQ, LANES), F32),
        ],
        compiler_params=_cparams(("parallel", "parallel", "arbitrary")),
        name="attn_prompt",
    )(proj, proj, proj, bias_p, lam_params, subln.reshape(1, LANES))


def _attn_decode_body(pt_ref, x_ref, *rest, n_pages, lam_init):
    k_refs = rest[:n_pages]
    v_refs = rest[n_pages:2 * n_pages]
    bd_ref, bn_ref, lam_ref, sub_ref, o_ref = rest[2 * n_pages:]
    nrow = 2 * DIFF_HEADS
    q = x_ref[:, DIFF_WIDTH:2 * DIFF_WIDTH] * (DIFF_HALF ** -0.5)
    k_new = x_ref[:, 2 * DIFF_WIDTH:3 * DIFF_WIDTH]
    v_new = x_ref[:, 3 * DIFF_WIDTH:4 * DIFF_WIDTH]
    row = lax.broadcasted_iota(jnp.int32, (nrow, DIFF_WIDTH), 0)
    lane = lax.broadcasted_iota(jnp.int32, (nrow, DIFF_WIDTH), 1)
    own_head = (lane // DIFF_HEAD_DIM) == (row // 2)
    own_half = ((lane % DIFF_HEAD_DIM) // DIFF_HALF) == (row % 2)
    q8 = jnp.where(own_head & own_half, jnp.broadcast_to(q, (nrow, DIFF_WIDTH)), 0.0)
    q8b = q8.astype(BF16)
    s = jnp.concatenate(
        [lax.dot_general(q8b, k_refs[p][...].astype(BF16), _NT, preferred_element_type=F32)
         for p in range(n_pages)], axis=1) + bd_ref[...]
    s_new = jnp.sum(q8 * k_new, axis=-1, keepdims=True) + bn_ref[:, 0:1]
    m = jnp.maximum(jnp.max(s, axis=-1, keepdims=True), s_new)
    p = jnp.exp(s - m)
    p_new = jnp.exp(s_new - m)
    inv_l = 1.0 / (jnp.sum(p, axis=-1, keepdims=True) + p_new)
    a = (p * inv_l).astype(BF16)
    r = (p_new * inv_l) * v_new
    for pg in range(n_pages):
        r = r + jnp.dot(a[:, pg * PAGE_SIZE:(pg + 1) * PAGE_SIZE], v_refs[pg][...].astype(BF16),
                        preferred_element_type=F32)
    lam = _lambda(lam_ref, lam_init)
    coef = jnp.where(own_head, jnp.where(row % 2 == 0, 1.0, -lam), 0.0)
    o = jnp.sum(r * coef, axis=0, keepdims=True)
    for h in range(DIFF_HEADS):
        sl = slice(h * DIFF_HEAD_DIM, (h + 1) * DIFF_HEAD_DIM)
        o_ref[:, sl] = (_rms(o[:, sl], sub_ref[...]) * (1.0 - lam_init)).astype(BF16)


def attn_decode(proj, cache_k, cache_v, page_table, layer_j, bias_d, bias_n, lam_params, subln, lam_init):
    nb = proj.shape[0]
    n_pages = page_table.shape[1]
    n_pool, n_even = cache_k.shape[:2]
    ck = cache_k.reshape(n_pool, n_even, PAGE_SIZE, DIFF_WIDTH)
    cv = cache_v.reshape(n_pool, n_even, PAGE_SIZE, DIFF_WIDTH)

    def page_spec(p):
        return pl.BlockSpec((None, None, PAGE_SIZE, DIFF_WIDTH),
                            lambda i, pt: (pt[i * n_pages + p], layer_j, 0, 0))

    full = lambda shape: pl.BlockSpec(shape, lambda i, pt: (0,) * len(shape))
    grid_spec = pltpu.PrefetchScalarGridSpec(
        num_scalar_prefetch=1,
        grid=(nb,),
        in_specs=([pl.BlockSpec((None, 1, EVEN_IN), lambda i, pt: (i, 0, 0))]
                  + [page_spec(p) for p in range(n_pages)] * 2
                  + [full(bias_d.shape), full(bias_n.shape), full((4, DIFF_HALF)), full((1, LANES))]),
        out_specs=pl.BlockSpec((None, 1, DIFF_WIDTH), lambda i, pt: (i, 0, 0)),
    )
    out = pl.pallas_call(
        functools.partial(_attn_decode_body, n_pages=n_pages, lam_init=lam_init),
        grid_spec=grid_spec,
        out_shape=jax.ShapeDtypeStruct((nb, 1, DIFF_WIDTH), BF16),
        compiler_params=_cparams(("arbitrary",)),
        name="attn_decode",
    )(page_table.reshape(-1), proj.reshape(nb, 1, EVEN_IN), *([ck] * n_pages), *([cv] * n_pages),
      bias_d, bias_n, lam_params, subln.reshape(1, LANES))
    return out.reshape(nb, DIFF_WIDTH)


def _unit_lower_inverse(low, ii, jj):
    eye = jnp.where(ii == jj, 1.0, 0.0)
    in_block = (ii // INV_BLOCK) == (jj // INV_BLOCK)
    p = jnp.where(in_block, -low, 0.0)
    off = jnp.where(in_block, 0.0, low)
    dinv = eye + p
    span = 2
    while span < INV_BLOCK:
        p = _mm(p, p)
        dinv = dinv + _mm(dinv, p)
        span *= 2
    n = -_mm(dinv, off)
    powers = [n]
    span = 2
    while span < GDN_C // INV_BLOCK:
        powers.append(_mm(powers[-1], powers[-1]))
        span *= 2
    x = dinv
    for pw in reversed(powers):
        x = x + _mm(pw, x)
    return x


def _gdn_prompt_body(x_ref, xp_ref, abt_ref, cw_ref, alog_ref, dtb_ref, on_ref, o_ref, s_ref, ext_ref):
    c = pl.program_id(1)
    nh = GDN_HEADS
    hk = nh * GDN_DK

    @pl.when(c == 0)
    def _():
        s_ref[...] = jnp.zeros_like(s_ref)

    ext_ref[0:8, :] = jnp.where(c > 0, xp_ref[:, 0:GDN_QKV], 0.0)
    ext_ref[8:8 + GDN_C, :] = x_ref[:, 0:GDN_QKV]

    def conv_act(col):
        sl = slice(col, col + LANES)
        acc = ext_ref[8:8 + GDN_C, sl] * cw_ref[3:4, sl]
        for i in range(GDN_CONV - 1):
            acc = acc + ext_ref[5 + i:5 + i + GDN_C, sl] * cw_ref[i:i + 1, sl]
        return acc * _sigmoid(acc)

    lane = lax.broadcasted_iota(jnp.int32, (nh, GDN_C), 1)
    g = -jnp.exp(alog_ref[...]) * _softplus(abt_ref[0:nh, :] + dtb_ref[...])
    beta = _sigmoid(abt_ref[nh:2 * nh, :])
    gc = g
    shift = 1
    while shift < GDN_C:
        gc = gc + jnp.where(lane >= shift, pltpu.roll(gc, shift, 1), 0.0)
        shift *= 2
    g_last = gc[:, GDN_C - 1:GDN_C]
    gam = jnp.exp(gc)
    kdec = jnp.exp(g_last - gc)
    glast = jnp.broadcast_to(jnp.exp(g_last), (nh, GDN_C))
    rows = jnp.concatenate([gc, beta, gam, kdec, beta * gam,
                            jnp.zeros((LANES - 5 * nh, GDN_C), F32)], axis=0)
    cols = jnp.transpose(rows)

    ii = lax.broadcasted_iota(jnp.int32, (GDN_C, GDN_C), 0)
    jj = lax.broadcasted_iota(jnp.int32, (GDN_C, GDN_C), 1)

    for h in range(nh):
        qa = conv_act(h * GDN_DK)
        ka = conv_act(hk + h * GDN_DK)
        v = conv_act(2 * hk + h * GDN_DV)
        q = qa * lax.rsqrt(jnp.sum(qa * qa, axis=-1, keepdims=True) + EPS) * (GDN_DK ** -0.5)
        k = ka * lax.rsqrt(jnp.sum(ka * ka, axis=-1, keepdims=True) + EPS)
        gcol = cols[:, h:h + 1]
        bcol = cols[:, nh + h:nh + h + 1]
        gamcol = cols[:, 2 * nh + h:2 * nh + h + 1]
        kdcol = cols[:, 3 * nh + h:3 * nh + h + 1]
        bgcol = cols[:, 4 * nh + h:4 * nh + h + 1]
        decay = jnp.exp(jnp.where(ii >= jj, gcol - gc[h:h + 1, :], -jnp.inf))
        kb = k.astype(BF16)
        gram = lax.dot_general(jnp.concatenate([q.astype(BF16), kb], axis=0), kb, _NT,
                               preferred_element_type=F32)
        qk = gram[0:GDN_C, :] * decay
        low = jnp.where(ii > jj, bcol * gram[GDN_C:2 * GDN_C, :] * decay, 0.0)
        tinv = _unit_lower_inverse(low, ii, jj)
        wu = _mm(tinv, jnp.concatenate([bgcol * k, bcol * v], axis=1))
        s_old = s_ref[h]
        ws_qs = _mm(jnp.concatenate([wu[:, 0:GDN_DK], gamcol * q], axis=0), s_old)
        u = wu[:, GDN_DK:GDN_DK + GDN_DV] - ws_qs[0:GDN_C, :]
        ub = u.astype(BF16)
        o = ws_qs[GDN_C:2 * GDN_C, :] + jnp.dot(qk.astype(BF16), ub, preferred_element_type=F32)
        s_ref[h] = glast[h:h + 1, :] * s_old + lax.dot_general(
            (kdcol * k).astype(BF16), ub, _TN, preferred_element_type=F32)
        z = x_ref[:, GDN_QKV + h * GDN_DV:GDN_QKV + (h + 1) * GDN_DV]
        o_ref[:, h * GDN_DV:(h + 1) * GDN_DV] = (_rms(o, on_ref[...]) * (z * _sigmoid(z))).astype(BF16)


def gdn_prompt(proj, gates_t, conv_w, a_log, dt_bias, o_norm):
    b, t, _ = proj.shape
    nc = t // GDN_C
    nh = GDN_HEADS
    return pl.pallas_call(
        _gdn_prompt_body,
        grid=(b, nc),
        in_specs=[
            pl.BlockSpec((None, GDN_C, GDN_MAIN), lambda i, c: (i, c, 0)),
            pl.BlockSpec((None, 8, GDN_MAIN), lambda i, c: (i, jnp.maximum(c * (GDN_C // 8) - 1, 0), 0)),
            pl.BlockSpec((LANES, GDN_C), lambda i, c: (0, i * nc + c)),
            pl.BlockSpec((GDN_CONV, GDN_QKV), lambda i, c: (0, 0)),
            pl.BlockSpec((nh, 1), lambda i, c: (0, 0)),
            pl.BlockSpec((nh, 1), lambda i, c: (0, 0)),
            pl.BlockSpec((1, GDN_DV), lambda i, c: (0, 0)),
        ],
        out_specs=[
            pl.BlockSpec((None, GDN_C, nh * GDN_DV), lambda i, c: (i, c, 0)),
            pl.BlockSpec((None, nh, GDN_DK, GDN_DV), lambda i, c: (i, 0, 0, 0)),
        ],
        out_shape=[
            jax.ShapeDtypeStruct((b, t, nh * GDN_DV), BF16),
            jax.ShapeDtypeStruct((b, nh, GDN_DK, GDN_DV), F32),
        ],
        scratch_shapes=[pltpu.VMEM((8 + GDN_C, GDN_QKV), F32)],
        compiler_params=_cparams(("parallel", "arbitrary")),
        name="gdn_prompt",
    )(proj, proj, gates_t, conv_w, a_log.reshape(nh, 1), dt_bias.reshape(nh, 1), o_norm.reshape(1, GDN_DV))


def _gdn_decode_body(x_ref, cs_ref, ab_ref, cw_ref, alog_ref, dtb_ref, on_ref, s0_ref, o_ref, s_ref):
    nh = GDN_HEADS
    x = x_ref[0:3 * nh, :]
    conv = x * cw_ref[GDN_CONV - 1]
    for i in range(GDN_CONV - 1):
        conv = conv + cs_ref[i] * cw_ref[i]
    act = conv * _sigmoid(conv)
    qa, ka, v8 = act[0:nh], act[nh:2 * nh], act[2 * nh:3 * nh]
    q8 = qa * lax.rsqrt(jnp.sum(qa * qa, axis=-1, keepdims=True) + EPS) * (GDN_DK ** -0.5)
    k8 = ka * lax.rsqrt(jnp.sum(ka * ka, axis=-1, keepdims=True) + EPS)
    g = -jnp.exp(alog_ref[...]) * _softplus(ab_ref[0:nh, :] + dtb_ref[...])
    gam8 = jnp.broadcast_to(jnp.exp(g), (nh, GDN_DV))
    beta8 = jnp.broadcast_to(_sigmoid(ab_ref[nh:2 * nh, :]), (nh, GDN_DV))
    qk8 = jnp.broadcast_to(jnp.sum(q8 * k8, axis=-1, keepdims=True), (nh, GDN_DV))
    cols = jnp.transpose(jnp.concatenate([q8, k8, jnp.zeros((LANES - 2 * nh, GDN_DK), F32)], axis=0))
    outs = []
    for h in range(nh):
        s_old = s0_ref[h]
        qcol = cols[:, h:h + 1]
        kcol = cols[:, nh + h:nh + h + 1]
        k_s = jnp.sum(kcol * s_old, axis=0, keepdims=True)
        q_s = jnp.sum(qcol * s_old, axis=0, keepdims=True)
        gam = gam8[h:h + 1, :]
        u = beta8[h:h + 1, :] * (v8[h:h + 1, :] - gam * k_s)
        outs.append(gam * q_s + qk8[h:h + 1, :] * u)
        s_ref[h] = gam * s_old + kcol * u
    o8 = jnp.concatenate(outs, axis=0)
    z8 = x_ref[3 * nh:4 * nh, :]
    o_ref[...] = (_rms(o8, on_ref[...]) * (z8 * _sigmoid(z8))).astype(BF16)


def gdn_decode(proj, gates, conv_state, s0, conv_w, a_log, dt_bias, o_norm):
    nb = proj.shape[0]
    nh = GDN_HEADS
    nrow = GDN_QKV // LANES
    o, s = pl.pallas_call(
        _gdn_decode_body,
        grid=(nb,),
        in_specs=[
            pl.BlockSpec((None, GDN_MAIN // LANES, LANES), lambda i: (i, 0, 0)),
            pl.BlockSpec((None, GDN_CONV - 1, nrow, LANES), lambda i: (i, 0, 0, 0)),
            pl.BlockSpec((None, 2 * nh, 1), lambda i: (i, 0, 0)),
            pl.BlockSpec((GDN_CONV, nrow, LANES), lambda i: (0, 0, 0)),
            pl.BlockSpec((nh, 1), lambda i: (0, 0)),
            pl.BlockSpec((nh, 1), lambda i: (0, 0)),
            pl.BlockSpec((1, GDN_DV), lambda i: (0, 0)),
            pl.BlockSpec((None, nh, GDN_DK, GDN_DV), lambda i: (i, 0, 0, 0)),
        ],
        out_specs=[
            pl.BlockSpec((None, nh, GDN_DV), lambda i: (i, 0, 0)),
            pl.BlockSpec((None, nh, GDN_DK, GDN_DV), lambda i: (i, 0, 0, 0)),
        ],
        out_shape=[
            jax.ShapeDtypeStruct((nb, nh, GDN_DV), BF16),
            jax.ShapeDtypeStruct((nb, nh, GDN_DK, GDN_DV), F32),
        ],
        compiler_params=_cparams(("parallel",)),
        name="gdn_decode",
    )(proj.reshape(nb, GDN_MAIN // LANES, LANES),
      conv_state.reshape(nb, GDN_CONV - 1, nrow, LANES),
      gates.reshape(nb, 2 * nh, 1),
      conv_w.reshape(GDN_CONV, nrow, LANES),
      a_log.reshape(nh, 1), dt_bias.reshape(nh, 1), o_norm.reshape(1, GDN_DV), s0)
    return o.reshape(nb, nh * GDN_DV), s


def kernel(x_prompt, x_sample, cache_k, cache_v, page_table, state_pool, state_conv, state_delta,
           norm_mix, norm_ffn, norm_final, rel_bias,
           w_in_even, pool_w, pool_scale, lambda_q1, lambda_k1, lambda_q2, lambda_k2, subln_w, w_out_even,
           w_in_odd, conv_w, a_log, dt_bias, o_norm, w_out_odd,
           w_gate_up, w_down):
    bp, t, d = x_prompt.shape
    bs = x_sample.shape[0]
    mp = bp * t
    nh = GDN_HEADS
    past = page_table.shape[1] * PAGE_SIZE
    tm_p = 1024
    tm_s = bs

    hp = x_prompt.reshape(mp, d)
    hs = x_sample.reshape(bs, d)
    bias_p, bias_d, bias_n = rel_bias_tiles(rel_bias, past)

    k_p, v_p, k_s, v_s, pool_p, pool_s = [], [], [], [], [], []
    conv_p, conv_s, delta_p, delta_s = [], [], [], []
    for layer in range(DEPTH):
        j = layer // 2
        last = layer == DEPTH - 1
        if layer % 2 == 0:
            w_in = w_in_even[j].astype(BF16)
            w_out = w_out_even[j].astype(BF16)
            pw = pool_w[j].astype(BF16)
            lam_init = 0.8 - 0.6 * math.exp(-0.3 * layer)
            lam_params = jnp.stack([lambda_q1[j], lambda_k1[j], lambda_q2[j], lambda_k2[j]])

            proj_p = norm_matmul(hp, norm_mix[layer], w_in, tm=tm_p, tn=512).reshape(bp, t, EVEN_IN)
            proj_s = norm_matmul(hs, norm_mix[layer], w_in, tm=tm_s, tn=512)

            ypool_p = pool_prompt(proj_p, pw, pool_scale[j])
            oatt_p = attn_prompt(proj_p, bias_p, lam_params, subln_w[j], lam_init)
            ypool_s = pool_decode(jnp.transpose(state_pool[j], (1, 0, 2)), proj_s, pw, pool_scale[j])
            oatt_s = attn_decode(proj_s, cache_k, cache_v, page_table, j, bias_d, bias_n,
                                 lam_params, subln_w[j], lam_init)

            kv_shape = (DIFF_HEADS, DIFF_HEAD_DIM)
            k_p.append(proj_p[..., 2 * DIFF_WIDTH:3 * DIFF_WIDTH].reshape(bp, t, *kv_shape))
            v_p.append(proj_p[..., 3 * DIFF_WIDTH:4 * DIFF_WIDTH].reshape(bp, t, *kv_shape))
            k_s.append(proj_s[:, 2 * DIFF_WIDTH:3 * DIFF_WIDTH].reshape(bs, 1, *kv_shape))
            v_s.append(proj_s[:, 3 * DIFF_WIDTH:4 * DIFF_WIDTH].reshape(bs, 1, *kv_shape))
            pool_p.append(proj_p[:, t - POOL_BUF:, 0:POOL_WIDTH])
            pool_s.append(jnp.concatenate([state_pool[j][:, 1:], proj_s[:, None, 0:POOL_WIDTH]], axis=1))

            mix_p = ((ypool_p.reshape(mp, POOL_WIDTH), 0), (oatt_p.reshape(mp, DIFF_WIDTH), 0))
            mix_s = ((ypool_s, 0), (oatt_s, 0))
        else:
            w_in = w_in_odd[j][:, 0:GDN_MAIN].astype(BF16)
            w_gates_t = jnp.zeros((LANES, d), F32).at[0:2 * nh].set(w_in_odd[j][:, GDN_MAIN:].T).astype(BF16)
            w_out = w_out_odd[j].astype(BF16)

            proj_p, gates_p = norm_matmul(hp, norm_mix[layer], w_in, w_gates_t, tm=tm_p, tn=512)
            proj_s, gates_s = norm_matmul(hs, norm_mix[layer], w_in, w_gates_t, tm=tm_s, tn=512)
            proj_p = proj_p.reshape(bp, t, GDN_MAIN)

            o_p, s_p = gdn_prompt(proj_p, gates_p, conv_w[j], a_log[j], dt_bias[j], o_norm[j])
            o_s, s_s = gdn_decode(proj_s, gates_s[0:2 * nh].T, state_conv[j], state_delta[j],
                                  conv_w[j], a_log[j], dt_bias[j], o_norm[j])

            conv_p.append(proj_p[:, t - (GDN_CONV - 1):, 0:GDN_QKV])
            conv_s.append(jnp.concatenate([state_conv[j][:, 1:], proj_s[:, None, 0:GDN_QKV]], axis=1))
            delta_p.append(s_p)
            delta_s.append(s_s)

            o_p = o_p.reshape(mp, nh * GDN_DV)
            mix_p = ((o_p, 0), (o_p, 1))
            mix_s = ((o_s, 0), (o_s, 1))

        w_gu = w_gate_up[layer].astype(BF16)
        w_d = w_down[layer].astype(BF16)
        g_fin = norm_final if last else None
        hp = outproj_ffn(hp, mix_p[0], mix_p[1], w_out, norm_ffn[layer], w_gu, w_d, g_fin, tm=tm_p)
        hs = outproj_ffn(hs, mix_s[0], mix_s[1], w_out, norm_ffn[layer], w_gu, w_d, g_fin, tm=tm_s)

    return (hp.reshape(bp, t, d), hs.reshape(bs, 1, d),
            jnp.stack(k_p, axis=1), jnp.stack(v_p, axis=1), jnp.stack(k_s, axis=1), jnp.stack(v_s, axis=1),
            jnp.stack(pool_p), jnp.stack(pool_s), jnp.stack(conv_p), jnp.stack(conv_s),
            jnp.stack(delta_p), jnp.stack(delta_s))
```

```python
import functools
import math

import jax
import jax.numpy as jnp
from jax import lax
from jax.experimental import pallas as pl
from jax.experimental.pallas import tpu as pltpu

F32 = jnp.float32
BF16 = jnp.bfloat16

D_MODEL = 1024
DEPTH = 4
PAGE_SIZE = 128
POOL_WIDTH = 512
POOL_WINDOWS = (2, 4, 8, 16)
POOL_GROUP = 128
POOL_BUF = 15
DIFF_HEADS = 4
DIFF_HALF = 64
DIFF_HEAD_DIM = 128
DIFF_WIDTH = 512
EVEN_IN = 2048
REL_BUCKETS = 32
REL_MAX_DIST = 128
GDN_HEADS = 8
GDN_DK = 128
GDN_DV = 128
GDN_CONV = 4
GDN_QKV = 3072
GDN_MAIN = 4096
D_FF = 2816
EPS = 1e-6

LANES = 128
VMEM_LIMIT = 48 * 1024 * 1024
NEG_BIG = -1e30

ATT_TQ = 256
ATT_TK = 256
GDN_C = 128
INV_BLOCK = 16
FFN_TF = 256

_NT = (((1,), (1,)), ((), ()))
_TN = (((0,), (0,)), ((), ()))


def _cparams(sem):
    return pltpu.CompilerParams(dimension_semantics=sem, vmem_limit_bytes=VMEM_LIMIT)


def _sigmoid(x):
    return 1.0 / (1.0 + jnp.exp(-x))


def _softplus(x):
    return jnp.maximum(x, 0.0) + jnp.log1p(jnp.exp(-jnp.abs(x)))


def _rms(x, gain):
    return x * lax.rsqrt(jnp.mean(x * x, axis=-1, keepdims=True) + EPS) * gain


def _mm(a, b):
    return jnp.dot(a.astype(BF16), b.astype(BF16), preferred_element_type=F32)


def _norm_mm_body(x_ref, g_ref, w_ref, o_ref, xn_ref):
    @pl.when(pl.program_id(1) == 0)
    def _():
        xn_ref[...] = _rms(x_ref[...], g_ref[...]).astype(BF16)

    o_ref[...] = jnp.dot(xn_ref[...], w_ref[...], preferred_element_type=F32)


def _norm_mm_gate_body(x_ref, g_ref, w_ref, wst_ref, o_ref, ost_ref, xn_ref):
    @pl.when(pl.program_id(1) == 0)
    def _():
        xn = _rms(x_ref[...], g_ref[...]).astype(BF16)
        xn_ref[...] = xn
        ost_ref[...] = lax.dot_general(wst_ref[...], xn, _NT, preferred_element_type=F32)

    o_ref[...] = jnp.dot(xn_ref[...], w_ref[...], preferred_element_type=F32)


def norm_matmul(x, gain, w, idx, w_gates_t=None, *, tm, tn):
    m, d = x.shape
    n = w.shape[2]
    grid = (m // tm, n // tn)
    in_specs = [
        pl.BlockSpec((tm, d), lambda i, j: (i, 0)),
        pl.BlockSpec((1, d), lambda i, j: (0, 0)),
        pl.BlockSpec((None, d, tn), lambda i, j: (idx, 0, j)),
    ]
    out_specs = pl.BlockSpec((tm, tn), lambda i, j: (i, j))
    out_shape = jax.ShapeDtypeStruct((m, n), F32)
    args = [x, gain.reshape(1, d), w]
    body = _norm_mm_body
    if w_gates_t is not None:
        in_specs.append(pl.BlockSpec((None, LANES, d), lambda i, j: (idx, 0, 0)))
        out_specs = [out_specs, pl.BlockSpec((LANES, tm), lambda i, j: (0, i))]
        out_shape = [out_shape, jax.ShapeDtypeStruct((LANES, m), F32)]
        args.append(w_gates_t)
        body = _norm_mm_gate_body
    return pl.pallas_call(
        body,
        grid=grid,
        in_specs=in_specs,
        out_specs=out_specs,
        out_shape=out_shape,
        scratch_shapes=[pltpu.VMEM((tm, d), BF16)],
        compiler_params=_cparams(("parallel", "arbitrary")),
        name="norm_matmul",
    )(*args)


def _ffn_body(*refs, final):
    if final:
        (res_ref, a0_ref, a1_ref, wo0_ref, wo1_ref, gf_ref, wg_ref, wu_ref, wd_ref, gfin_ref,
         o_ref, h1_ref, xn_ref, acc_ref) = refs
    else:
        (res_ref, a0_ref, a1_ref, wo0_ref, wo1_ref, gf_ref, wg_ref, wu_ref, wd_ref,
         o_ref, h1_ref, xn_ref, acc_ref) = refs
    f = pl.program_id(1)

    @pl.when(f == 0)
    def _():
        h1 = (res_ref[...]
              + jnp.dot(a0_ref[...], wo0_ref[...], preferred_element_type=F32)
              + jnp.dot(a1_ref[...], wo1_ref[...], preferred_element_type=F32))
        h1_ref[...] = h1
        xn_ref[...] = _rms(h1, gf_ref[...]).astype(BF16)
        acc_ref[...] = jnp.zeros_like(acc_ref)

    xn = xn_ref[...]
    g = jnp.dot(xn, wg_ref[...], preferred_element_type=F32)
    u = jnp.dot(xn, wu_ref[...], preferred_element_type=F32)
    act = (g * _sigmoid(g) * u).astype(BF16)
    acc_ref[...] += jnp.dot(act, wd_ref[...], preferred_element_type=F32)

    @pl.when(f == pl.num_programs(1) - 1)
    def _():
        y = h1_ref[...] + acc_ref[...]
        if final:
            y = _rms(y, gfin_ref[...])
        o_ref[...] = y


def outproj_ffn(res, mix0, mix1, w_out, idx_out, g_ffn, w_gu, w_d, layer, g_final=None, *, tm):
    m, d = res.shape
    half = w_out.shape[1] // 2
    nf = D_FF // FFN_TF
    (m0, c0), (m1, c1) = mix0, mix1
    final = g_final is not None
    in_specs = [
        pl.BlockSpec((tm, d), lambda i, f: (i, 0)),
        pl.BlockSpec((tm, half), lambda i, f: (i, c0)),
        pl.BlockSpec((tm, half), lambda i, f: (i, c1)),
        pl.BlockSpec((None, half, d), lambda i, f: (idx_out, 0, 0)),
        pl.BlockSpec((None, half, d), lambda i, f: (idx_out, 1, 0)),
        pl.BlockSpec((1, d), lambda i, f: (0, 0)),
        pl.BlockSpec((None, d, FFN_TF), lambda i, f: (layer, 0, f)),
        pl.BlockSpec((None, d, FFN_TF), lambda i, f: (layer, 0, nf + f)),
        pl.BlockSpec((None, FFN_TF, d), lambda i, f: (layer, f, 0)),
    ]
    args = [res, m0, m1, w_out, w_out, g_ffn.reshape(1, d), w_gu, w_gu, w_d]
    if final:
        in_specs.append(pl.BlockSpec((1, d), lambda i, f: (0, 0)))
        args.append(g_final.reshape(1, d))
    return pl.pallas_call(
        functools.partial(_ffn_body, final=final),
        grid=(m // tm, nf),
        in_specs=in_specs,
        out_specs=pl.BlockSpec((tm, d), lambda i, f: (i, 0)),
        out_shape=jax.ShapeDtypeStruct((m, d), F32),
        scratch_shapes=[pltpu.VMEM((tm, d), F32), pltpu.VMEM((tm, d), BF16), pltpu.VMEM((tm, d), F32)],
        compiler_params=_cparams(("parallel", "arbitrary")),
        name="outproj_ffn",
    )(*args)


POOL_PAD = 16
POOL_ROWS = 512


def _pool_prompt_body(u_ref, pw_ref, ps_ref, o_ref, ext_ref):
    t_len = u_ref.shape[0]
    ext_ref[0:POOL_PAD, :] = jnp.zeros((POOL_PAD, POOL_WIDTH), F32)
    ext_ref[POOL_PAD:, :] = u_ref[...]
    for r0 in range(0, t_len, POOL_ROWS):
        pos = lax.broadcasted_iota(jnp.int32, (POOL_ROWS, 1), 0) + r0
        for g, w in enumerate(POOL_WINDOWS):
            sl = slice(g * POOL_GROUP, (g + 1) * POOL_GROUP)
            cur = ext_ref[POOL_PAD + r0:POOL_PAD + r0 + POOL_ROWS, sl]
            win = cur
            for i in range(1, w):
                win = win + ext_ref[POOL_PAD + r0 - i:POOL_PAD + r0 - i + POOL_ROWS, sl]
            cnt = jnp.minimum(pos + 1, w).astype(F32)
            dlt = win / cnt - cur
            y = jnp.dot(dlt.astype(BF16), pw_ref[g], preferred_element_type=F32) * ps_ref[:, sl]
            o_ref[r0:r0 + POOL_ROWS, sl] = y.astype(BF16)


def pool_prompt(proj, pool_w, pool_scale):
    b, t, _ = proj.shape
    return pl.pallas_call(
        _pool_prompt_body,
        grid=(b,),
        in_specs=[
            pl.BlockSpec((None, t, POOL_WIDTH), lambda i: (i, 0, 0)),
            pl.BlockSpec((len(POOL_WINDOWS), POOL_GROUP, POOL_GROUP), lambda i: (0, 0, 0)),
            pl.BlockSpec((1, POOL_WIDTH), lambda i: (0, 0)),
        ],
        out_specs=pl.BlockSpec((None, t, POOL_WIDTH), lambda i: (i, 0, 0)),
        out_shape=jax.ShapeDtypeStruct((b, t, POOL_WIDTH), BF16),
        scratch_shapes=[pltpu.VMEM((t + POOL_PAD, POOL_WIDTH), F32)],
        compiler_params=_cparams(("parallel",)),
        name="pool_prompt",
    )(proj, pool_w, pool_scale.reshape(1, POOL_WIDTH))


def _pool_decode_body(st_ref, u_ref, pw_ref, ps_ref, o_ref):
    for g, w in enumerate(POOL_WINDOWS):
        sl = slice(g * POOL_GROUP, (g + 1) * POOL_GROUP)
        cur = u_ref[:, sl]
        win = cur
        for i in range(1, w):
            win = win + st_ref[POOL_BUF - i, :, sl]
        dlt = win / float(w) - cur
        y = jnp.dot(dlt.astype(BF16), pw_ref[g], preferred_element_type=F32) * ps_ref[:, sl]
        o_ref[:, sl] = y.astype(BF16)


def pool_decode(state_t, proj, pool_w, pool_scale):
    nb = proj.shape[0]
    return pl.pallas_call(
        _pool_decode_body,
        grid=(1,),
        in_specs=[
            pl.BlockSpec((POOL_BUF, nb, POOL_WIDTH), lambda i: (0, 0, 0)),
            pl.BlockSpec((nb, POOL_WIDTH), lambda i: (0, 0)),
            pl.BlockSpec((len(POOL_WINDOWS), POOL_GROUP, POOL_GROUP), lambda i: (0, 0, 0)),
            pl.BlockSpec((1, POOL_WIDTH), lambda i: (0, 0)),
        ],
        out_specs=pl.BlockSpec((nb, POOL_WIDTH), lambda i: (0, 0)),
        out_shape=jax.ShapeDtypeStruct((nb, POOL_WIDTH), BF16),
        compiler_params=_cparams(("arbitrary",)),
        name="pool_decode",
    )(state_t, proj, pool_w, pool_scale.reshape(1, POOL_WIDTH))


def _rel_bucket(n):
    max_exact = REL_BUCKETS // 2
    nf = jnp.maximum(n, 1).astype(F32)
    large = max_exact + (jnp.log(nf / max_exact) / math.log(REL_MAX_DIST / max_exact)
                         * (REL_BUCKETS - max_exact)).astype(jnp.int32)
    large = jnp.minimum(large, REL_BUCKETS - 1)
    return jnp.where(n < max_exact, n, large)


def _table_lookup(tab_ref, bucket, h):
    out = jnp.zeros(bucket.shape, F32)
    for b in range(REL_BUCKETS):
        out = jnp.where(bucket == b, tab_ref[b, h], out)
    return out


def _rel_bias_body(tab_ref, bp_ref, bd_ref, bn_ref, *, past):
    ii = lax.broadcasted_iota(jnp.int32, (ATT_TQ, ATT_TK), 0)
    jj = lax.broadcasted_iota(jnp.int32, (ATT_TQ, ATT_TK), 1)
    for h in range(DIFF_HEADS):
        for t in range(3):
            dist = t * ATT_TK + ii - jj
            bias = _table_lookup(tab_ref, _rel_bucket(jnp.maximum(dist, 0)), h)
            bp_ref[h, t] = jnp.where(dist >= 0, bias, NEG_BIG)
    nrow = 2 * DIFF_HEADS
    row = lax.broadcasted_iota(jnp.int32, (nrow, past * DIFF_HEADS), 0)
    col = lax.broadcasted_iota(jnp.int32, (nrow, past * DIFF_HEADS), 1)
    bucket = _rel_bucket(past - col // DIFF_HEADS)
    rown = lax.broadcasted_iota(jnp.int32, (nrow, LANES), 0)
    bd = jnp.full((nrow, past * DIFF_HEADS), NEG_BIG, F32)
    bn = jnp.zeros((nrow, LANES), F32)
    for h in range(DIFF_HEADS):
        own = jnp.where(row // 2 == h, col % DIFF_HEADS, -1) == h
        bd = jnp.where(own, _table_lookup(tab_ref, bucket, h), bd)
        bn = jnp.where(rown // 2 == h, tab_ref[0, h], bn)
    bd_ref[...] = bd
    bn_ref[...] = bn


def rel_bias_tiles(rel_bias, past):
    return pl.pallas_call(
        functools.partial(_rel_bias_body, past=past),
        in_specs=[pl.BlockSpec(memory_space=pltpu.SMEM)],
        out_specs=[
            pl.BlockSpec(memory_space=pltpu.VMEM),
            pl.BlockSpec(memory_space=pltpu.VMEM),
            pl.BlockSpec(memory_space=pltpu.VMEM),
        ],
        out_shape=[
            jax.ShapeDtypeStruct((DIFF_HEADS, 3, ATT_TQ, ATT_TK), F32),
            jax.ShapeDtypeStruct((2 * DIFF_HEADS, past * DIFF_HEADS), F32),
            jax.ShapeDtypeStruct((2 * DIFF_HEADS, LANES), F32),
        ],
        compiler_params=pltpu.CompilerParams(vmem_limit_bytes=VMEM_LIMIT),
        name="rel_bias_tiles",
    )(rel_bias)


def _lambda(lam_ref, lam_init):
    lp = lam_ref[...]
    s1 = jnp.sum(lp[0:1, :] * lp[1:2, :], axis=-1, keepdims=True)
    s2 = jnp.sum(lp[2:3, :] * lp[3:4, :], axis=-1, keepdims=True)
    return jnp.exp(s1) - jnp.exp(s2) + lam_init


def _attn_prompt_body(q_ref, k_ref, v_ref, bias_ref, lam_ref, sub_ref, o_ref,
                      kb_ref, vb_ref, qq_ref, s_ref, m_ref, acc_ref, *, lam_init):
    qi = pl.program_id(2)

    @pl.when(qi == 0)
    def _():
        kb_ref[...] = k_ref[...].astype(BF16)
        vb_ref[:, 0:LANES] = v_ref[...].astype(BF16)
        vb_ref[:, LANES:2 * LANES] = jnp.ones((vb_ref.shape[0], LANES), BF16)

    q = q_ref[...] * (DIFF_HALF ** -0.5)
    lane = lax.broadcasted_iota(jnp.int32, q.shape, 1)
    qq_ref[...] = jnp.concatenate([jnp.where(lane < DIFF_HALF, q, 0.0),
                                   jnp.where(lane >= DIFF_HALF, q, 0.0)], axis=0).astype(BF16)
    m_ref[...] = jnp.full(m_ref.shape, -jnp.inf, F32)
    acc_ref[...] = jnp.zeros_like(acc_ref)

    def scores(j):
        off = pl.multiple_of(j * ATT_TK, ATT_TK)
        return lax.dot_general(qq_ref[...], kb_ref[pl.ds(off, ATT_TK), :], _NT, preferred_element_type=F32)

    s_ref[...] = scores(0)

    def step(j, carry):
        bias = bias_ref[jnp.minimum(qi - j, 2)]
        halves = [s_ref[r0:r0 + ATT_TQ, :] + bias for r0 in (0, ATT_TQ)]
        s_ref[...] = scores(jnp.minimum(j + 1, qi))
        vj = vb_ref[pl.ds(pl.multiple_of(j * ATT_TK, ATT_TK), ATT_TK), :]
        for idx, s in enumerate(halves):
            rows = slice(idx * ATT_TQ, (idx + 1) * ATT_TQ)
            m_prev = m_ref[rows, :]
            m_new = jnp.maximum(m_prev, jnp.max(s, axis=-1, keepdims=True))
            alpha = jnp.exp(m_prev - m_new)
            p = jnp.exp(s - jnp.concatenate([m_new] * (ATT_TK // LANES), axis=1))
            acc_ref[rows, :] = (jnp.concatenate([alpha, alpha], axis=1) * acc_ref[rows, :]
                                + jnp.dot(p.astype(BF16), vj, preferred_element_type=F32))
            m_ref[rows, :] = m_new
        return carry

    lax.fori_loop(0, qi + 1, step, 0)

    acc = acc_ref[...]
    o_all = acc[:, 0:LANES] / acc[:, LANES:2 * LANES]
    o = o_all[0:ATT_TQ, :] - _lambda(lam_ref, lam_init) * o_all[ATT_TQ:2 * ATT_TQ, :]
    o_ref[...] = (_rms(o, sub_ref[...]) * (1.0 - lam_init)).astype(BF16)


def attn_prompt(proj, bias_p, lam_params, subln, lam_init):
    b, t, _ = proj.shape
    nh = DIFF_HEADS
    return pl.pallas_call(
        functools.partial(_attn_prompt_body, lam_init=lam_init),
        grid=(b, nh, t // ATT_TQ),
        in_specs=[
            pl.BlockSpec((None, ATT_TQ, LANES), lambda i, h, q: (i, q, nh + h)),
            pl.BlockSpec((None, t, LANES), lambda i, h, q: (i, 0, 2 * nh + h)),
            pl.BlockSpec((None, t, LANES), lambda i, h, q: (i, 0, 3 * nh + h)),
            pl.BlockSpec((None, 3, ATT_TQ, ATT_TK), lambda i, h, q: (h, 0, 0, 0)),
            pl.BlockSpec((4, DIFF_HALF), lambda i, h, q: (0, 0)),
            pl.BlockSpec((1, LANES), lambda i, h, q: (0, 0)),
        ],
        out_specs=pl.BlockSpec((None, ATT_TQ, LANES), lambda i, h, q: (i, q, h)),
        out_shape=jax.ShapeDtypeStruct((b, t, DIFF_WIDTH), BF16),
        scratch_shapes=[
            pltpu.VMEM((t, LANES), BF16),
            pltpu.VMEM((t, 2 * LANES), BF16),
            pltpu.VMEM((2 * ATT_TQ, LANES), BF16),
            pltpu.VMEM((2 * ATT_TQ, ATT_TK), F32),
            pltpu.VMEM((2 * ATT_TQ, LANES), F32),
            pltpu.VMEM((2 * ATT_TQ, 2 * LANES), F32),
        ],
        compiler_params=_cparams(("parallel", "parallel", "arbitrary")),
        name="attn_prompt",
    )(proj, proj, proj, bias_p, lam_params, subln.reshape(1, LANES))


def _attn_decode_body(pt_ref, x_ref, *rest, n_pages, lam_init):
    k_refs = rest[:n_pages]
    v_refs = rest[n_pages:2 * n_pages]
    bd_ref, bn_ref, lam_ref, sub_ref, o_ref = rest[2 * n_pages:]
    page_rows = PAGE_SIZE * DIFF_HEADS
    lane = lax.broadcasted_iota(jnp.int32, (1, DIFF_HEAD_DIM), 1)
    rows_q, rows_k, rows_v = [], [], []
    for h in range(DIFF_HEADS):
        sl = slice(h * DIFF_HEAD_DIM, (h + 1) * DIFF_HEAD_DIM)
        qh = x_ref[:, DIFF_WIDTH:2 * DIFF_WIDTH][:, sl] * (DIFF_HALF ** -0.5)
        kh = x_ref[:, 2 * DIFF_WIDTH:3 * DIFF_WIDTH][:, sl]
        vh = x_ref[:, 3 * DIFF_WIDTH:4 * DIFF_WIDTH][:, sl]
        rows_q += [jnp.where(lane < DIFF_HALF, qh, 0.0), jnp.where(lane >= DIFF_HALF, qh, 0.0)]
        rows_k += [kh, kh]
        rows_v += [vh, vh]
    q8 = jnp.concatenate(rows_q, axis=0)
    k8 = jnp.concatenate(rows_k, axis=0)
    v8 = jnp.concatenate(rows_v, axis=0)
    q8b = q8.astype(BF16)
    s = jnp.concatenate(
        [lax.dot_general(q8b, k_refs[p][...].astype(BF16), _NT, preferred_element_type=F32)
         for p in range(n_pages)], axis=1) + bd_ref[...]
    s_new = jnp.sum(q8 * k8, axis=-1, keepdims=True) + bn_ref[:, 0:1]
    m = jnp.maximum(jnp.max(s, axis=-1, keepdims=True), s_new)
    p = jnp.exp(s - m)
    p_new = jnp.exp(s_new - m)
    inv_l = 1.0 / (jnp.sum(p, axis=-1, keepdims=True) + p_new)
    a = p * inv_l
    r = (p_new * inv_l) * v8
    for pg in range(n_pages):
        r = r + jnp.dot(a[:, pg * page_rows:(pg + 1) * page_rows].astype(BF16),
                        v_refs[pg][...].astype(BF16), preferred_element_type=F32)
    lam = _lambda(lam_ref, lam_init)
    for h in range(DIFF_HEADS):
        o = r[2 * h:2 * h + 1, :] - lam * r[2 * h + 1:2 * h + 2, :]
        o_ref[:, h * DIFF_HEAD_DIM:(h + 1) * DIFF_HEAD_DIM] = (
            _rms(o, sub_ref[...]) * (1.0 - lam_init)).astype(BF16)


def attn_decode(proj, cache_k, cache_v, page_table, layer_j, bias_d, bias_n, lam_params, subln, lam_init):
    nb = proj.shape[0]
    n_pages = page_table.shape[1]
    n_pool, n_even = cache_k.shape[:2]
    page_rows = PAGE_SIZE * DIFF_HEADS
    ck = cache_k.reshape(n_pool, n_even, page_rows, DIFF_HEAD_DIM)
    cv = cache_v.reshape(n_pool, n_even, page_rows, DIFF_HEAD_DIM)

    def page_spec(p):
        return pl.BlockSpec((None, None, page_rows, DIFF_HEAD_DIM),
                            lambda i, pt: (pt[i * n_pages + p], layer_j, 0, 0))

    full = lambda shape: pl.BlockSpec(shape, lambda i, pt: (0,) * len(shape))
    grid_spec = pltpu.PrefetchScalarGridSpec(
        num_scalar_prefetch=1,
        grid=(nb,),
        in_specs=([pl.BlockSpec((None, 1, EVEN_IN), lambda i, pt: (i, 0, 0))]
                  + [page_spec(p) for p in range(n_pages)] * 2
                  + [full(bias_d.shape), full(bias_n.shape), full((4, DIFF_HALF)), full((1, LANES))]),
        out_specs=pl.BlockSpec((None, 1, DIFF_WIDTH), lambda i, pt: (i, 0, 0)),
    )
    out = pl.pallas_call(
        functools.partial(_attn_decode_body, n_pages=n_pages, lam_init=lam_init),
        grid_spec=grid_spec,
        out_shape=jax.ShapeDtypeStruct((nb, 1, DIFF_WIDTH), BF16),
        compiler_params=_cparams(("arbitrary",)),
        name="attn_decode",
    )(page_table.reshape(-1), proj.reshape(nb, 1, EVEN_IN), *([ck] * n_pages), *([cv] * n_pages),
      bias_d, bias_n, lam_params, subln.reshape(1, LANES))
    return out.reshape(nb, DIFF_WIDTH)


def _blockdiag(x):
    lane = lax.broadcasted_iota(jnp.int32, x.shape, 1)
    zero = jnp.zeros_like(x)
    return jnp.concatenate([jnp.where(lane < GDN_C, x, zero), jnp.where(lane >= GDN_C, x, zero)], axis=0)


def _mmp(a, b):
    return jnp.dot(a.astype(BF16), _blockdiag(b.astype(BF16)), preferred_element_type=F32)


def _unit_lower_inverse(lows, ii, jl):
    eye = jnp.where(ii == jl, 1.0, 0.0)
    in_block = (ii // INV_BLOCK) == (jl // INV_BLOCK)
    ps = [jnp.where(in_block, -low, 0.0) for low in lows]
    offs = [jnp.where(in_block, 0.0, low) for low in lows]
    dinvs = [eye + p for p in ps]
    span = 2
    while span < INV_BLOCK:
        ps = [_mmp(p, p) for p in ps]
        dinvs = [d + _mmp(d, p) for d, p in zip(dinvs, ps)]
        span *= 2
    powers = [[-_mmp(d, off) for d, off in zip(dinvs, offs)]]
    span = 2
    while span < GDN_C // INV_BLOCK:
        powers.append([_mmp(n, n) for n in powers[-1]])
        span *= 2
    xs = dinvs
    for pw in reversed(powers):
        xs = [x + _mmp(n, x) for n, x in zip(pw, xs)]
    return xs


def _gdn_prompt_body(x_ref, xp_ref, abt_ref, cw_ref, alog_ref, dtb_ref, on_ref, o_ref, s_ref, ext_ref):
    c = pl.program_id(1)
    nh = GDN_HEADS
    npair = nh // 2
    hk = nh * GDN_DK
    cc = GDN_C

    @pl.when(c == 0)
    def _():
        s_ref[...] = jnp.zeros_like(s_ref)

    ext_ref[0:8, :] = jnp.where(c > 0, xp_ref[:, 0:GDN_QKV], 0.0)
    ext_ref[8:8 + cc, :] = x_ref[:, 0:GDN_QKV]

    def conv_act(col):
        sl = slice(col, col + LANES)
        acc = ext_ref[8:8 + cc, sl] * cw_ref[3:4, sl]
        for i in range(GDN_CONV - 1):
            acc = acc + ext_ref[5 + i:5 + i + cc, sl] * cw_ref[i:i + 1, sl]
        return acc * _sigmoid(acc)

    def l2n(x):
        return x * lax.rsqrt(jnp.sum(x * x, axis=-1, keepdims=True) + EPS)

    def pair(xs):
        return [jnp.concatenate([xs[2 * p], xs[2 * p + 1]], axis=1) for p in range(npair)]

    lane = lax.broadcasted_iota(jnp.int32, (nh, cc), 1)
    g = -jnp.exp(alog_ref[...]) * _softplus(abt_ref[0:nh, :] + dtb_ref[...])
    beta = _sigmoid(abt_ref[nh:2 * nh, :])
    gc = g
    shift = 1
    while shift < cc:
        gc = gc + jnp.where(lane >= shift, pltpu.roll(gc, shift, 1), 0.0)
        shift *= 2
    g_last = jnp.broadcast_to(gc[:, cc - 1:cc], (nh, cc))
    s_decay = jnp.exp(g_last)
    cols = jnp.transpose(jnp.concatenate([gc, beta, jnp.zeros((LANES - 2 * nh, cc), F32)], axis=0))

    def col_pair(n0):
        return pair([jnp.broadcast_to(cols[:, n0 + h:n0 + h + 1], (cc, cc)) for h in range(nh)])

    def row_pair(x):
        return pair([x[h:h + 1, :] for h in range(nh)])

    q_p = pair([l2n(conv_act(h * GDN_DK)) * (GDN_DK ** -0.5) for h in range(nh)])
    k_p = pair([l2n(conv_act(hk + h * GDN_DK)) for h in range(nh)])
    v_p = pair([conv_act(2 * hk + h * GDN_DV) for h in range(nh)])
    gc_b = col_pair(0)
    beta_b = col_pair(nh)
    gam_b = [jnp.exp(x) for x in gc_b]
    kdec_b = [jnp.exp(gl - x) for gl, x in zip(row_pair(g_last), gc_b)]

    ii = lax.broadcasted_iota(jnp.int32, (cc, 2 * cc), 0)
    jl = lax.broadcasted_iota(jnp.int32, (cc, 2 * cc), 1) % cc
    decay = [jnp.exp(jnp.where(ii >= jl, gb - gr, -jnp.inf)) for gb, gr in zip(gc_b, row_pair(gc))]
    gram = [lax.dot_general(jnp.concatenate([q, k], axis=0).astype(BF16), _blockdiag(k.astype(BF16)), _NT,
                            preferred_element_type=F32) for q, k in zip(q_p, k_p)]
    qk = [gm[0:cc, :] * d for gm, d in zip(gram, decay)]
    lows = [jnp.where(ii > jl, b * gm[cc:2 * cc, :] * d, 0.0) for b, gm, d in zip(beta_b, gram, decay)]
    tinv = _unit_lower_inverse(lows, ii, jl)
    w = [_mmp(t, b * gm * k) for t, b, gm, k in zip(tinv, beta_b, gam_b, k_p)]
    u0 = [_mmp(t, b * v) for t, b, v in zip(tinv, beta_b, v_p)]
    s_old = [s_ref[h] for h in range(nh)]
    zero = jnp.zeros((GDN_DK, GDN_DV), BF16)
    s_bd = [jnp.concatenate([jnp.concatenate([s_old[2 * p].astype(BF16), zero], axis=1),
                             jnp.concatenate([zero, s_old[2 * p + 1].astype(BF16)], axis=1)], axis=0)
            for p in range(npair)]
    ws_qs = [jnp.dot(jnp.concatenate([wp, gm * q], axis=0).astype(BF16), sb, preferred_element_type=F32)
             for wp, gm, q, sb in zip(w, gam_b, q_p, s_bd)]
    u = [a - x[0:cc, :] for a, x in zip(u0, ws_qs)]
    o = [x[cc:2 * cc, :] + _mmp(a, b) for x, a, b in zip(ws_qs, qk, u)]
    upd = [lax.dot_general((kd * k).astype(BF16), b.astype(BF16), _TN, preferred_element_type=F32)
           for kd, k, b in zip(kdec_b, k_p, u)]
    for h in range(nh):
        half = slice((h % 2) * cc, (h % 2 + 1) * cc)
        s_ref[h] = s_decay[h:h + 1, :] * s_old[h] + upd[h // 2][half, half]
        z = x_ref[:, GDN_QKV + h * GDN_DV:GDN_QKV + (h + 1) * GDN_DV]
        o_ref[:, h * GDN_DV:(h + 1) * GDN_DV] = (
            _rms(o[h // 2][:, half], on_ref[...]) * (z * _sigmoid(z))).astype(BF16)


def gdn_prompt(proj, gates_t, conv_w, a_log, dt_bias, o_norm):
    b, t, _ = proj.shape
    nc = t // GDN_C
    nh = GDN_HEADS
    return pl.pallas_call(
        _gdn_prompt_body,
        grid=(b, nc),
        in_specs=[
            pl.BlockSpec((None, GDN_C, GDN_MAIN), lambda i, c: (i, c, 0)),
            pl.BlockSpec((None, 8, GDN_MAIN), lambda i, c: (i, jnp.maximum(c * (GDN_C // 8) - 1, 0), 0)),
            pl.BlockSpec((LANES, GDN_C), lambda i, c: (0, i * nc + c)),
            pl.BlockSpec((GDN_CONV, GDN_QKV), lambda i, c: (0, 0)),
            pl.BlockSpec((nh, 1), lambda i, c: (0, 0)),
            pl.BlockSpec((nh, 1), lambda i, c: (0, 0)),
            pl.BlockSpec((1, GDN_DV), lambda i, c: (0, 0)),
        ],
        out_specs=[
            pl.BlockSpec((None, GDN_C, nh * GDN_DV), lambda i, c: (i, c, 0)),
            pl.BlockSpec((None, nh, GDN_DK, GDN_DV), lambda i, c: (i, 0, 0, 0)),
        ],
        out_shape=[
            jax.ShapeDtypeStruct((b, t, nh * GDN_DV), BF16),
            jax.ShapeDtypeStruct((b, nh, GDN_DK, GDN_DV), F32),
        ],
        scratch_shapes=[pltpu.VMEM((8 + GDN_C, GDN_QKV), F32)],
        compiler_params=_cparams(("parallel", "arbitrary")),
        name="gdn_prompt",
    )(proj, proj, gates_t, conv_w, a_log.reshape(nh, 1), dt_bias.reshape(nh, 1), o_norm.reshape(1, GDN_DV))


def _gdn_decode_body(x_ref, cs_ref, ab_ref, cw_ref, alog_ref, dtb_ref, on_ref, s0_ref, o_ref, s_ref):
    nh = GDN_HEADS
    x = x_ref[0:3 * nh, :]
    conv = x * cw_ref[GDN_CONV - 1]
    for i in range(GDN_CONV - 1):
        conv = conv + cs_ref[i] * cw_ref[i]
    act = conv * _sigmoid(conv)
    qa, ka, v8 = act[0:nh], act[nh:2 * nh], act[2 * nh:3 * nh]
    q8 = qa * lax.rsqrt(jnp.sum(qa * qa, axis=-1, keepdims=True) + EPS) * (GDN_DK ** -0.5)
    k8 = ka * lax.rsqrt(jnp.sum(ka * ka, axis=-1, keepdims=True) + EPS)
    g = -jnp.exp(alog_ref[...]) * _softplus(ab_ref[0:nh, :] + dtb_ref[...])
    gam8 = jnp.broadcast_to(jnp.exp(g), (nh, GDN_DV))
    beta8 = jnp.broadcast_to(_sigmoid(ab_ref[nh:2 * nh, :]), (nh, GDN_DV))
    qk8 = jnp.broadcast_to(jnp.sum(q8 * k8, axis=-1, keepdims=True), (nh, GDN_DV))
    cols = jnp.transpose(jnp.concatenate([q8, k8, jnp.zeros((LANES - 2 * nh, GDN_DK), F32)], axis=0))
    outs = []
    for h in range(nh):
        s_old = s0_ref[h]
        qcol = cols[:, h:h + 1]
        kcol = cols[:, nh + h:nh + h + 1]
        k_s = jnp.sum(kcol * s_old, axis=0, keepdims=True)
        q_s = jnp.sum(qcol * s_old, axis=0, keepdims=True)
        gam = gam8[h:h + 1, :]
        u = beta8[h:h + 1, :] * (v8[h:h + 1, :] - gam * k_s)
        outs.append(gam * q_s + qk8[h:h + 1, :] * u)
        s_ref[h] = gam * s_old + kcol * u
    o8 = jnp.concatenate(outs, axis=0)
    z8 = x_ref[3 * nh:4 * nh, :]
    o_ref[...] = (_rms(o8, on_ref[...]) * (z8 * _sigmoid(z8))).astype(BF16)


def gdn_decode(proj, gates, conv_state, s0_all, layer_j, conv_w, a_log, dt_bias, o_norm):
    nb = proj.shape[0]
    nh = GDN_HEADS
    nrow = GDN_QKV // LANES
    o, s = pl.pallas_call(
        _gdn_decode_body,
        grid=(nb,),
        in_specs=[
            pl.BlockSpec((None, GDN_MAIN // LANES, LANES), lambda i: (i, 0, 0)),
            pl.BlockSpec((None, GDN_CONV - 1, nrow, LANES), lambda i: (i, 0, 0, 0)),
            pl.BlockSpec((None, 2 * nh, 1), lambda i: (i, 0, 0)),
            pl.BlockSpec((GDN_CONV, nrow, LANES), lambda i: (0, 0, 0)),
            pl.BlockSpec((nh, 1), lambda i: (0, 0)),
            pl.BlockSpec((nh, 1), lambda i: (0, 0)),
            pl.BlockSpec((1, GDN_DV), lambda i: (0, 0)),
            pl.BlockSpec((None, None, nh, GDN_DK, GDN_DV), lambda i: (layer_j, i, 0, 0, 0)),
        ],
        out_specs=[
            pl.BlockSpec((None, nh, GDN_DV), lambda i: (i, 0, 0)),
            pl.BlockSpec((None, nh, GDN_DK, GDN_DV), lambda i: (i, 0, 0, 0)),
        ],
        out_shape=[
            jax.ShapeDtypeStruct((nb, nh, GDN_DV), BF16),
            jax.ShapeDtypeStruct((nb, nh, GDN_DK, GDN_DV), F32),
        ],
        compiler_params=_cparams(("parallel",)),
        name="gdn_decode",
    )(proj.reshape(nb, GDN_MAIN // LANES, LANES),
      conv_state.reshape(nb, GDN_CONV - 1, nrow, LANES),
      gates.reshape(nb, 2 * nh, 1),
      conv_w.reshape(GDN_CONV, nrow, LANES),
      a_log.reshape(nh, 1), dt_bias.reshape(nh, 1), o_norm.reshape(1, GDN_DV), s0_all)
    return o.reshape(nb, nh * GDN_DV), s


def kernel(x_prompt, x_sample, cache_k, cache_v, page_table, state_pool, state_conv, state_delta,
           norm_mix, norm_ffn, norm_final, rel_bias,
           w_in_even, pool_w, pool_scale, lambda_q1, lambda_k1, lambda_q2, lambda_k2, subln_w, w_out_even,
           w_in_odd, conv_w, a_log, dt_bias, o_norm, w_out_odd,
           w_gate_up, w_down):
    bp, t, d = x_prompt.shape
    bs = x_sample.shape[0]
    mp = bp * t
    nh = GDN_HEADS
    past = page_table.shape[1] * PAGE_SIZE
    tm_p = 1024
    tm_s = bs

    hp = x_prompt.reshape(mp, d)
    hs = x_sample.reshape(bs, d)
    bias_p, bias_d, bias_n = rel_bias_tiles(rel_bias, past)

    w_in_e = w_in_even.astype(BF16)
    w_in_o = w_in_odd[:, :, 0:GDN_MAIN].astype(BF16)
    w_gates_t = jnp.zeros((w_in_odd.shape[0], LANES, d), F32).at[:, 0:2 * nh, :].set(
        jnp.transpose(w_in_odd[:, :, GDN_MAIN:], (0, 2, 1))).astype(BF16)
    w_out_e = w_out_even.astype(BF16)
    w_out_o = w_out_odd.astype(BF16)
    w_gu = w_gate_up.astype(BF16)
    w_d = w_down.astype(BF16)
    pool_wb = pool_w.astype(BF16)

    k_p, v_p, k_s, v_s, pool_p, pool_s = [], [], [], [], [], []
    conv_p, conv_s, delta_p, delta_s = [], [], [], []
    for layer in range(DEPTH):
        j = layer // 2
        last = layer == DEPTH - 1
        if layer % 2 == 0:
            w_out = w_out_e
            pw = pool_wb[j]
            lam_init = 0.8 - 0.6 * math.exp(-0.3 * layer)
            lam_params = jnp.stack([lambda_q1[j], lambda_k1[j], lambda_q2[j], lambda_k2[j]])

            proj_p = norm_matmul(hp, norm_mix[layer], w_in_e, j, tm=tm_p, tn=512).reshape(bp, t, EVEN_IN)
            proj_s = norm_matmul(hs, norm_mix[layer], w_in_e, j, tm=tm_s, tn=512)

            ypool_p = pool_prompt(proj_p, pw, pool_scale[j])
            oatt_p = attn_prompt(proj_p, bias_p, lam_params, subln_w[j], lam_init)
            ypool_s = pool_decode(jnp.transpose(state_pool[j], (1, 0, 2)), proj_s, pw, pool_scale[j])
            oatt_s = attn_decode(proj_s, cache_k, cache_v, page_table, j, bias_d, bias_n,
                                 lam_params, subln_w[j], lam_init)

            kv_shape = (DIFF_HEADS, DIFF_HEAD_DIM)
            k_p.append(proj_p[..., 2 * DIFF_WIDTH:3 * DIFF_WIDTH].reshape(bp, t, *kv_shape))
            v_p.append(proj_p[..., 3 * DIFF_WIDTH:4 * DIFF_WIDTH].reshape(bp, t, *kv_shape))
            k_s.append(proj_s[:, 2 * DIFF_WIDTH:3 * DIFF_WIDTH].reshape(bs, 1, *kv_shape))
            v_s.append(proj_s[:, 3 * DIFF_WIDTH:4 * DIFF_WIDTH].reshape(bs, 1, *kv_shape))
            pool_p.append(proj_p[:, t - POOL_BUF:, 0:POOL_WIDTH])
            pool_s.append(jnp.concatenate([state_pool[j][:, 1:], proj_s[:, None, 0:POOL_WIDTH]], axis=1))

            mix_p = ((ypool_p.reshape(mp, POOL_WIDTH), 0), (oatt_p.reshape(mp, DIFF_WIDTH), 0))
            mix_s = ((ypool_s, 0), (oatt_s, 0))
        else:
            w_out = w_out_o

            proj_p, gates_p = norm_matmul(hp, norm_mix[layer], w_in_o, j, w_gates_t, tm=tm_p, tn=512)
            proj_s, gates_s = norm_matmul(hs, norm_mix[layer], w_in_o, j, w_gates_t, tm=tm_s, tn=512)
            proj_p = proj_p.reshape(bp, t, GDN_MAIN)

            o_p, s_p = gdn_prompt(proj_p, gates_p, conv_w[j], a_log[j], dt_bias[j], o_norm[j])
            o_s, s_s = gdn_decode(proj_s, gates_s[0:2 * nh].T, state_conv[j], state_delta, j,
                                  conv_w[j], a_log[j], dt_bias[j], o_norm[j])

            conv_p.append(proj_p[:, t - (GDN_CONV - 1):, 0:GDN_QKV])
            conv_s.append(jnp.concatenate([state_conv[j][:, 1:], proj_s[:, None, 0:GDN_QKV]], axis=1))
            delta_p.append(s_p)
            delta_s.append(s_s)

            o_p = o_p.reshape(mp, nh * GDN_DV)
            mix_p = ((o_p, 0), (o_p, 1))
            mix_s = ((o_s, 0), (o_s, 1))

        g_fin = norm_final if last else None
        hp = outproj_ffn(hp, mix_p[0], mix_p[1], w_out, j, norm_ffn[layer], w_gu, w_d, layer, g_fin, tm=tm_p)
        hs = outproj_ffn(hs, mix_s[0], mix_s[1], w_out, j, norm_ffn[layer], w_gu, w_d, layer, g_fin, tm=tm_s)

    return (hp.reshape(bp, t, d), hs.reshape(bs, 1, d),
            jnp.stack(k_p, axis=1), jnp.stack(v_p, axis=1), jnp.stack(k_s, axis=1), jnp.stack(v_s, axis=1),
            jnp.stack(pool_p), jnp.stack(pool_s), jnp.stack(conv_p), jnp.stack(conv_s),
            jnp.stack(delta_p), jnp.stack(delta_s))
```

```python
import functools
import math

import jax
import jax.numpy as jnp
from jax import lax
from jax.experimental import pallas as pl
from jax.experimental.pallas import tpu as pltpu

F32 = jnp.float32
BF16 = jnp.bfloat16

D_MODEL = 1024
DEPTH = 4
PAGE_SIZE = 128
POOL_WIDTH = 512
POOL_WINDOWS = (2, 4, 8, 16)
POOL_GROUP = 128
POOL_BUF = 15
DIFF_HEADS = 4
DIFF_HALF = 64
DIFF_HEAD_DIM = 128
DIFF_WIDTH = 512
EVEN_IN = 2048
REL_BUCKETS = 32
REL_MAX_DIST = 128
GDN_HEADS = 8
GDN_DK = 128
GDN_DV = 128
GDN_CONV = 4
GDN_QKV = 3072
GDN_MAIN = 4096
D_FF = 2816
EPS = 1e-6

LANES = 128
VMEM_LIMIT = 48 * 1024 * 1024
NEG_BIG = -1e30

ATT_TQ = 512
ATT_TK = 512
GDN_C = 128
GDN_STEP_CHUNKS = 2
GDN_DEC_ROWS = 4
INV_BLOCK = 16
FFN_TF = 256

_NT = (((1,), (1,)), ((), ()))
_TN = (((0,), (0,)), ((), ()))


def _cparams(sem):
    return pltpu.CompilerParams(dimension_semantics=sem, vmem_limit_bytes=VMEM_LIMIT)


def _sigmoid(x):
    return 1.0 / (1.0 + jnp.exp(-x))


def _softplus(x):
    return jnp.maximum(x, 0.0) + jnp.log1p(jnp.exp(-jnp.abs(x)))


def _rms(x, gain):
    return x * lax.rsqrt(jnp.mean(x * x, axis=-1, keepdims=True) + EPS) * gain


def _mm(a, b):
    return jnp.dot(a.astype(BF16), b.astype(BF16), preferred_element_type=F32)


def _norm_mm_kv_body(x_ref, g_ref, w_ref, o_ref, k_ref, v_ref, xn_ref):
    j = pl.program_id(1)

    @pl.when(j == 0)
    def _():
        xn_ref[...] = _rms(x_ref[...], g_ref[...]).astype(BF16)

    res = jnp.dot(xn_ref[...], w_ref[...], preferred_element_type=F32)
    o_ref[...] = res

    def head_rows(dst_ref):
        for h in range(DIFF_HEADS):
            dst_ref[pl.ds(h, res.shape[0], stride=DIFF_HEADS), :] = res[:, h * DIFF_HEAD_DIM:(h + 1) * DIFF_HEAD_DIM]

    @pl.when(j == 2)
    def _():
        head_rows(k_ref)

    @pl.when(j == 3)
    def _():
        head_rows(v_ref)


def _norm_mm_gate_body(x_ref, g_ref, w_ref, wst_ref, o_ref, ost_ref, xn_ref):
    @pl.when(pl.program_id(1) == 0)
    def _():
        xn = _rms(x_ref[...], g_ref[...]).astype(BF16)
        xn_ref[...] = xn
        ost_ref[...] = lax.dot_general(wst_ref[...], xn, _NT, preferred_element_type=F32)

    o_ref[...] = jnp.dot(xn_ref[...], w_ref[...], preferred_element_type=F32)


def norm_matmul(x, gain, w, idx, w_gates_t=None, *, tm, tn):
    m, d = x.shape
    n = w.shape[2]
    grid = (m // tm, n // tn)
    in_specs = [
        pl.BlockSpec((tm, d), lambda i, j: (i, 0)),
        pl.BlockSpec((1, d), lambda i, j: (0, 0)),
        pl.BlockSpec((None, d, tn), lambda i, j: (idx, 0, j)),
    ]
    out_specs = pl.BlockSpec((tm, tn), lambda i, j: (i, j))
    out_shape = jax.ShapeDtypeStruct((m, n), F32)
    args = [x, gain.reshape(1, d), w]
    if w_gates_t is None:
        assert tn == DIFF_WIDTH and n == EVEN_IN
        kv_spec = pl.BlockSpec((tm * DIFF_HEADS, DIFF_HEAD_DIM), lambda i, j: (i, 0))
        kv_shape = jax.ShapeDtypeStruct((m * DIFF_HEADS, DIFF_HEAD_DIM), F32)
        out_specs = [out_specs, kv_spec, kv_spec]
        out_shape = [out_shape, kv_shape, kv_shape]
        body = _norm_mm_kv_body
    else:
        in_specs.append(pl.BlockSpec((None, LANES, d), lambda i, j: (idx, 0, 0)))
        out_specs = [out_specs, pl.BlockSpec((LANES, tm), lambda i, j: (0, i))]
        out_shape = [out_shape, jax.ShapeDtypeStruct((LANES, m), F32)]
        args.append(w_gates_t)
        body = _norm_mm_gate_body
    return pl.pallas_call(
        body,
        grid=grid,
        in_specs=in_specs,
        out_specs=out_specs,
        out_shape=out_shape,
        scratch_shapes=[pltpu.VMEM((tm, d), BF16)],
        compiler_params=_cparams(("parallel", "arbitrary")),
        name="norm_matmul",
    )(*args)


def _ffn_body(*refs, final):
    if final:
        (res_ref, a0_ref, a1_ref, wo0_ref, wo1_ref, gf_ref, wg_ref, wu_ref, wd_ref, gfin_ref,
         o_ref, h1_ref, xn_ref, acc_ref) = refs
    else:
        (res_ref, a0_ref, a1_ref, wo0_ref, wo1_ref, gf_ref, wg_ref, wu_ref, wd_ref,
         o_ref, h1_ref, xn_ref, acc_ref) = refs
    f = pl.program_id(1)

    @pl.when(f == 0)
    def _():
        h1 = (res_ref[...]
              + jnp.dot(a0_ref[...], wo0_ref[...], preferred_element_type=F32)
              + jnp.dot(a1_ref[...], wo1_ref[...], preferred_element_type=F32))
        h1_ref[...] = h1
        xn_ref[...] = _rms(h1, gf_ref[...]).astype(BF16)
        acc_ref[...] = jnp.zeros_like(acc_ref)

    xn = xn_ref[...]
    g = jnp.dot(xn, wg_ref[...], preferred_element_type=F32)
    u = jnp.dot(xn, wu_ref[...], preferred_element_type=F32)
    act = (g * _sigmoid(g) * u).astype(BF16)
    acc_ref[...] += jnp.dot(act, wd_ref[...], preferred_element_type=F32)

    @pl.when(f == pl.num_programs(1) - 1)
    def _():
        y = h1_ref[...] + acc_ref[...]
        if final:
            y = _rms(y, gfin_ref[...])
        o_ref[...] = y


def outproj_ffn(res, mix0, mix1, w_out, idx_out, g_ffn, w_gu, w_d, layer, g_final=None, *, tm):
    m, d = res.shape
    half = w_out.shape[1] // 2
    nf = D_FF // FFN_TF
    (m0, c0), (m1, c1) = mix0, mix1
    final = g_final is not None
    in_specs = [
        pl.BlockSpec((tm, d), lambda i, f: (i, 0)),
        pl.BlockSpec((tm, half), lambda i, f: (i, c0)),
        pl.BlockSpec((tm, half), lambda i, f: (i, c1)),
        pl.BlockSpec((None, half, d), lambda i, f: (idx_out, 0, 0)),
        pl.BlockSpec((None, half, d), lambda i, f: (idx_out, 1, 0)),
        pl.BlockSpec((1, d), lambda i, f: (0, 0)),
        pl.BlockSpec((None, d, FFN_TF), lambda i, f: (layer, 0, f)),
        pl.BlockSpec((None, d, FFN_TF), lambda i, f: (layer, 0, nf + f)),
        pl.BlockSpec((None, FFN_TF, d), lambda i, f: (layer, f, 0)),
    ]
    args = [res, m0, m1, w_out, w_out, g_ffn.reshape(1, d), w_gu, w_gu, w_d]
    if final:
        in_specs.append(pl.BlockSpec((1, d), lambda i, f: (0, 0)))
        args.append(g_final.reshape(1, d))
    return pl.pallas_call(
        functools.partial(_ffn_body, final=final),
        grid=(m // tm, nf),
        in_specs=in_specs,
        out_specs=pl.BlockSpec((tm, d), lambda i, f: (i, 0)),
        out_shape=jax.ShapeDtypeStruct((m, d), F32),
        scratch_shapes=[pltpu.VMEM((tm, d), F32), pltpu.VMEM((tm, d), BF16), pltpu.VMEM((tm, d), F32)],
        compiler_params=_cparams(("parallel", "arbitrary")),
        name="outproj_ffn",
    )(*args)


POOL_PAD = 16
POOL_ROWS = 512


def _pool_prompt_body(u_ref, pw_ref, ps_ref, o_ref, ext_ref):
    t_len = u_ref.shape[0]
    ext_ref[0:POOL_PAD, :] = jnp.zeros((POOL_PAD, POOL_WIDTH), F32)
    ext_ref[POOL_PAD:, :] = u_ref[...]
    for r0 in range(0, t_len, POOL_ROWS):
        pos = lax.broadcasted_iota(jnp.int32, (POOL_ROWS, 1), 0) + r0
        for g, w in enumerate(POOL_WINDOWS):
            sl = slice(g * POOL_GROUP, (g + 1) * POOL_GROUP)
            cur = ext_ref[POOL_PAD + r0:POOL_PAD + r0 + POOL_ROWS, sl]
            win = cur
            for i in range(1, w):
                win = win + ext_ref[POOL_PAD + r0 - i:POOL_PAD + r0 - i + POOL_ROWS, sl]
            cnt = jnp.minimum(pos + 1, w).astype(F32)
            dlt = win / cnt - cur
            y = jnp.dot(dlt.astype(BF16), pw_ref[g], preferred_element_type=F32) * ps_ref[:, sl]
            o_ref[r0:r0 + POOL_ROWS, sl] = y.astype(BF16)


def pool_prompt(proj, pool_w, pool_scale):
    b, t, _ = proj.shape
    return pl.pallas_call(
        _pool_prompt_body,
        grid=(b,),
        in_specs=[
            pl.BlockSpec((None, t, POOL_WIDTH), lambda i: (i, 0, 0)),
            pl.BlockSpec((len(POOL_WINDOWS), POOL_GROUP, POOL_GROUP), lambda i: (0, 0, 0)),
            pl.BlockSpec((1, POOL_WIDTH), lambda i: (0, 0)),
        ],
        out_specs=pl.BlockSpec((None, t, POOL_WIDTH), lambda i: (i, 0, 0)),
        out_shape=jax.ShapeDtypeStruct((b, t, POOL_WIDTH), BF16),
        scratch_shapes=[pltpu.VMEM((t + POOL_PAD, POOL_WIDTH), F32)],
        compiler_params=_cparams(("parallel",)),
        name="pool_prompt",
    )(proj, pool_w, pool_scale.reshape(1, POOL_WIDTH))


def _pool_decode_body(st_ref, u_ref, pw_ref, ps_ref, o_ref):
    for g, w in enumerate(POOL_WINDOWS):
        sl = slice(g * POOL_GROUP, (g + 1) * POOL_GROUP)
        cur = u_ref[:, sl]
        win = cur
        for i in range(1, w):
            win = win + st_ref[POOL_BUF - i, :, sl]
        dlt = win / float(w) - cur
        y = jnp.dot(dlt.astype(BF16), pw_ref[g], preferred_element_type=F32) * ps_ref[:, sl]
        o_ref[:, sl] = y.astype(BF16)


def pool_decode(state_t, proj, pool_w, pool_scale):
    nb = proj.shape[0]
    return pl.pallas_call(
        _pool_decode_body,
        grid=(1,),
        in_specs=[
            pl.BlockSpec((POOL_BUF, nb, POOL_WIDTH), lambda i: (0, 0, 0)),
            pl.BlockSpec((nb, POOL_WIDTH), lambda i: (0, 0)),
            pl.BlockSpec((len(POOL_WINDOWS), POOL_GROUP, POOL_GROUP), lambda i: (0, 0, 0)),
            pl.BlockSpec((1, POOL_WIDTH), lambda i: (0, 0)),
        ],
        out_specs=pl.BlockSpec((nb, POOL_WIDTH), lambda i: (0, 0)),
        out_shape=jax.ShapeDtypeStruct((nb, POOL_WIDTH), BF16),
        compiler_params=_cparams(("arbitrary",)),
        name="pool_decode",
    )(state_t, proj, pool_w, pool_scale.reshape(1, POOL_WIDTH))


def _rel_bucket(n):
    max_exact = REL_BUCKETS // 2
    nf = jnp.maximum(n, 1).astype(F32)
    large = max_exact + (jnp.log(nf / max_exact) / math.log(REL_MAX_DIST / max_exact)
                         * (REL_BUCKETS - max_exact)).astype(jnp.int32)
    large = jnp.minimum(large, REL_BUCKETS - 1)
    return jnp.where(n < max_exact, n, large)


def _table_lookup(tab_ref, bucket, h):
    out = jnp.zeros(bucket.shape, F32)
    for b in range(REL_BUCKETS):
        out = jnp.where(bucket == b, tab_ref[b, h], out)
    return out


def _rel_bias_body(tab_ref, bp_ref, bd_ref, bn_ref, *, past):
    ii = lax.broadcasted_iota(jnp.int32, (ATT_TQ, ATT_TK), 0)
    jj = lax.broadcasted_iota(jnp.int32, (ATT_TQ, ATT_TK), 1)
    for h in range(DIFF_HEADS):
        for t in range(3):
            dist = t * ATT_TK + ii - jj
            bias = _table_lookup(tab_ref, _rel_bucket(jnp.maximum(dist, 0)), h)
            bp_ref[h, t] = jnp.where(dist >= 0, bias, NEG_BIG)
    nrow = 2 * DIFF_HEADS
    row = lax.broadcasted_iota(jnp.int32, (nrow, past * DIFF_HEADS), 0)
    col = lax.broadcasted_iota(jnp.int32, (nrow, past * DIFF_HEADS), 1)
    bucket = _rel_bucket(past - col // DIFF_HEADS)
    rown = lax.broadcasted_iota(jnp.int32, (nrow, LANES), 0)
    bd = jnp.full((nrow, past * DIFF_HEADS), NEG_BIG, F32)
    bn = jnp.zeros((nrow, LANES), F32)
    for h in range(DIFF_HEADS):
        own = jnp.where(row // 2 == h, col % DIFF_HEADS, -1) == h
        bd = jnp.where(own, _table_lookup(tab_ref, bucket, h), bd)
        bn = jnp.where(rown // 2 == h, tab_ref[0, h], bn)
    bd_ref[...] = bd
    bn_ref[...] = bn


def rel_bias_tiles(rel_bias, past):
    return pl.pallas_call(
        functools.partial(_rel_bias_body, past=past),
        in_specs=[pl.BlockSpec(memory_space=pltpu.SMEM)],
        out_specs=[
            pl.BlockSpec(memory_space=pltpu.VMEM),
            pl.BlockSpec(memory_space=pltpu.VMEM),
            pl.BlockSpec(memory_space=pltpu.VMEM),
        ],
        out_shape=[
            jax.ShapeDtypeStruct((DIFF_HEADS, 3, ATT_TQ, ATT_TK), F32),
            jax.ShapeDtypeStruct((2 * DIFF_HEADS, past * DIFF_HEADS), F32),
            jax.ShapeDtypeStruct((2 * DIFF_HEADS, LANES), F32),
        ],
        compiler_params=pltpu.CompilerParams(vmem_limit_bytes=VMEM_LIMIT),
        name="rel_bias_tiles",
    )(rel_bias)


def _lambda(lam_ref, lam_init):
    lp = lam_ref[...]
    s1 = jnp.sum(lp[0:1, :] * lp[1:2, :], axis=-1, keepdims=True)
    s2 = jnp.sum(lp[2:3, :] * lp[3:4, :], axis=-1, keepdims=True)
    return jnp.exp(s1) - jnp.exp(s2) + lam_init


def _attn_prompt_body(q_ref, k_ref, v_ref, bias_ref, lam_ref, sub_ref, o_ref,
                      kb_ref, vb_ref, qq_ref, s_ref, m_ref, acc_ref, *, lam_init):
    qi = pl.program_id(2)

    @pl.when(qi == 0)
    def _():
        kb_ref[...] = k_ref[...].astype(BF16)
        vb_ref[:, 0:LANES] = v_ref[...].astype(BF16)
        vb_ref[:, LANES:2 * LANES] = jnp.ones((vb_ref.shape[0], LANES), BF16)

    q = q_ref[...] * (DIFF_HALF ** -0.5)
    lane = lax.broadcasted_iota(jnp.int32, q.shape, 1)
    qq_ref[...] = jnp.concatenate([jnp.where(lane < DIFF_HALF, q, 0.0),
                                   jnp.where(lane >= DIFF_HALF, q, 0.0)], axis=0).astype(BF16)
    m_ref[...] = jnp.full(m_ref.shape, -jnp.inf, F32)
    acc_ref[...] = jnp.zeros_like(acc_ref)

    def scores(j):
        off = pl.multiple_of(j * ATT_TK, ATT_TK)
        return lax.dot_general(qq_ref[...], kb_ref[pl.ds(off, ATT_TK), :], _NT, preferred_element_type=F32)

    s_ref[...] = scores(0)

    def step(j, carry):
        bias = bias_ref[jnp.minimum(qi - j, 2)]
        halves = [s_ref[r0:r0 + ATT_TQ, :] + bias for r0 in (0, ATT_TQ)]
        s_ref[...] = scores(jnp.minimum(j + 1, qi))
        vj = vb_ref[pl.ds(pl.multiple_of(j * ATT_TK, ATT_TK), ATT_TK), :]
        for idx, s in enumerate(halves):
            rows = slice(idx * ATT_TQ, (idx + 1) * ATT_TQ)
            m_prev = m_ref[rows, :]
            m_new = jnp.maximum(m_prev, jnp.max(s, axis=-1, keepdims=True))
            alpha = jnp.exp(m_prev - m_new)
            p = jnp.exp(s - jnp.concatenate([m_new] * (ATT_TK // LANES), axis=1))
            acc_ref[rows, :] = (jnp.concatenate([alpha, alpha], axis=1) * acc_ref[rows, :]
                                + jnp.dot(p.astype(BF16), vj, preferred_element_type=F32))
            m_ref[rows, :] = m_new
        return carry

    lax.fori_loop(0, qi + 1, step, 0)

    acc = acc_ref[...]
    o_all = acc[:, 0:LANES] / acc[:, LANES:2 * LANES]
    o = o_all[0:ATT_TQ, :] - _lambda(lam_ref, lam_init) * o_all[ATT_TQ:2 * ATT_TQ, :]
    o_ref[...] = (_rms(o, sub_ref[...]) * (1.0 - lam_init)).astype(BF16)


def attn_prompt(proj, bias_p, lam_params, subln, lam_init):
    b, t, _ = proj.shape
    nh = DIFF_HEADS
    return pl.pallas_call(
        functools.partial(_attn_prompt_body, lam_init=lam_init),
        grid=(b, nh, t // ATT_TQ),
        in_specs=[
            pl.BlockSpec((None, ATT_TQ, LANES), lambda i, h, q: (i, q, nh + h)),
            pl.BlockSpec((None, t, LANES), lambda i, h, q: (i, 0, 2 * nh + h)),
            pl.BlockSpec((None, t, LANES), lambda i, h, q: (i, 0, 3 * nh + h)),
            pl.BlockSpec((None, 3, ATT_TQ, ATT_TK), lambda i, h, q: (h, 0, 0, 0)),
            pl.BlockSpec((4, DIFF_HALF), lambda i, h, q: (0, 0)),
            pl.BlockSpec((1, LANES), lambda i, h, q: (0, 0)),
        ],
        out_specs=pl.BlockSpec((None, ATT_TQ, LANES), lambda i, h, q: (i, q, h)),
        out_shape=jax.ShapeDtypeStruct((b, t, DIFF_WIDTH), BF16),
        scratch_shapes=[
            pltpu.VMEM((t, LANES), BF16),
            pltpu.VMEM((t, 2 * LANES), BF16),
            pltpu.VMEM((2 * ATT_TQ, LANES), BF16),
            pltpu.VMEM((2 * ATT_TQ, ATT_TK), F32),
            pltpu.VMEM((2 * ATT_TQ, LANES), F32),
            pltpu.VMEM((2 * ATT_TQ, 2 * LANES), F32),
        ],
        compiler_params=_cparams(("parallel", "parallel", "arbitrary")),
        name="attn_prompt",
    )(proj, proj, proj, bias_p, lam_params, subln.reshape(1, LANES))


def _attn_decode_body(pt_ref, x_ref, *rest, n_pages, lam_init):
    k_refs = rest[:n_pages]
    v_refs = rest[n_pages:2 * n_pages]
    bd_ref, bn_ref, lam_ref, sub_ref, o_ref = rest[2 * n_pages:]
    page_rows = PAGE_SIZE * DIFF_HEADS
    lane = lax.broadcasted_iota(jnp.int32, (1, DIFF_HEAD_DIM), 1)
    rows_q, rows_k, rows_v = [], [], []
    for h in range(DIFF_HEADS):
        sl = slice(h * DIFF_HEAD_DIM, (h + 1) * DIFF_HEAD_DIM)
        qh = x_ref[:, DIFF_WIDTH:2 * DIFF_WIDTH][:, sl] * (DIFF_HALF ** -0.5)
        kh = x_ref[:, 2 * DIFF_WIDTH:3 * DIFF_WIDTH][:, sl]
        vh = x_ref[:, 3 * DIFF_WIDTH:4 * DIFF_WIDTH][:, sl]
        rows_q += [jnp.where(lane < DIFF_HALF, qh, 0.0), jnp.where(lane >= DIFF_HALF, qh, 0.0)]
        rows_k += [kh, kh]
        rows_v += [vh, vh]
    q8 = jnp.concatenate(rows_q, axis=0)
    k8 = jnp.concatenate(rows_k, axis=0)
    v8 = jnp.concatenate(rows_v, axis=0)
    q8b = q8.astype(BF16)
    s = jnp.concatenate(
        [lax.dot_general(q8b, k_refs[p][...].astype(BF16), _NT, preferred_element_type=F32)
         for p in range(n_pages)], axis=1) + bd_ref[...]
    s_new = jnp.sum(q8 * k8, axis=-1, keepdims=True) + bn_ref[:, 0:1]
    m = jnp.maximum(jnp.max(s, axis=-1, keepdims=True), s_new)
    p = jnp.exp(s - m)
    p_new = jnp.exp(s_new - m)
    inv_l = 1.0 / (jnp.sum(p, axis=-1, keepdims=True) + p_new)
    a = p * inv_l
    r = (p_new * inv_l) * v8
    for pg in range(n_pages):
        r = r + jnp.dot(a[:, pg * page_rows:(pg + 1) * page_rows].astype(BF16),
                        v_refs[pg][...].astype(BF16), preferred_element_type=F32)
    lam = _lambda(lam_ref, lam_init)
    for h in range(DIFF_HEADS):
        o = r[2 * h:2 * h + 1, :] - lam * r[2 * h + 1:2 * h + 2, :]
        o_ref[:, h * DIFF_HEAD_DIM:(h + 1) * DIFF_HEAD_DIM] = (
            _rms(o, sub_ref[...]) * (1.0 - lam_init)).astype(BF16)


def attn_decode(proj, cache_k, cache_v, page_table, layer_j, bias_d, bias_n, lam_params, subln, lam_init):
    nb = proj.shape[0]
    n_pages = page_table.shape[1]
    n_pool, n_even = cache_k.shape[:2]
    page_rows = PAGE_SIZE * DIFF_HEADS
    ck = cache_k.reshape(n_pool, n_even, page_rows, DIFF_HEAD_DIM)
    cv = cache_v.reshape(n_pool, n_even, page_rows, DIFF_HEAD_DIM)

    def page_spec(p):
        return pl.BlockSpec((None, None, page_rows, DIFF_HEAD_DIM),
                            lambda i, pt: (pt[i * n_pages + p], layer_j, 0, 0))

    full = lambda shape: pl.BlockSpec(shape, lambda i, pt: (0,) * len(shape))
    grid_spec = pltpu.PrefetchScalarGridSpec(
        num_scalar_prefetch=1,
        grid=(nb,),
        in_specs=([pl.BlockSpec((None, 1, EVEN_IN), lambda i, pt: (i, 0, 0))]
                  + [page_spec(p) for p in range(n_pages)] * 2
                  + [full(bias_d.shape), full(bias_n.shape), full((4, DIFF_HALF)), full((1, LANES))]),
        out_specs=pl.BlockSpec((None, 1, DIFF_WIDTH), lambda i, pt: (i, 0, 0)),
    )
    out = pl.pallas_call(
        functools.partial(_attn_decode_body, n_pages=n_pages, lam_init=lam_init),
        grid_spec=grid_spec,
        out_shape=jax.ShapeDtypeStruct((nb, 1, DIFF_WIDTH), BF16),
        compiler_params=_cparams(("arbitrary",)),
        name="attn_decode",
    )(page_table.reshape(-1), proj.reshape(nb, 1, EVEN_IN), *([ck] * n_pages), *([cv] * n_pages),
      bias_d, bias_n, lam_params, subln.reshape(1, LANES))
    return out.reshape(nb, DIFF_WIDTH)


def _blockdiag(x):
    lane = lax.broadcasted_iota(jnp.int32, x.shape, 1)
    zero = jnp.zeros_like(x)
    return jnp.concatenate([jnp.where(lane < GDN_C, x, zero), jnp.where(lane >= GDN_C, x, zero)], axis=0)


def _mmp(a, b):
    return jnp.dot(a.astype(BF16), _blockdiag(b.astype(BF16)), preferred_element_type=F32)


def _unit_lower_inverse(lows, ii, jl):
    eye = jnp.where(ii == jl, 1.0, 0.0)
    in_block = (ii // INV_BLOCK) == (jl // INV_BLOCK)
    ps = [jnp.where(in_block, -low, 0.0) for low in lows]
    offs = [jnp.where(in_block, 0.0, low) for low in lows]
    dinvs = [eye + p for p in ps]
    span = 2
    while span < INV_BLOCK:
        ps = [_mmp(p, p) for p in ps]
        dinvs = [d + _mmp(d, p) for d, p in zip(dinvs, ps)]
        span *= 2
    powers = [[-_mmp(d, off) for d, off in zip(dinvs, offs)]]
    span = 2
    while span < GDN_C // INV_BLOCK:
        powers.append([_mmp(n, n) for n in powers[-1]])
        span *= 2
    xs = dinvs
    for pw in reversed(powers):
        xs = [x + _mmp(n, x) for n, x in zip(pw, xs)]
    return xs


def _gdn_prompt_body(x_ref, xp_ref, abt_ref, cw_ref, alog_ref, dtb_ref, on_ref, o_ref, s_ref, ext_ref):
    c = pl.program_id(1)
    nh = GDN_HEADS
    npair = nh // 2
    hk = nh * GDN_DK
    cc = GDN_C

    @pl.when(c == 0)
    def _():
        s_ref[...] = jnp.zeros_like(s_ref)

    rows_step = GDN_STEP_CHUNKS * cc
    ext_ref[0:8, :] = jnp.where(c > 0, xp_ref[:, 0:GDN_QKV], 0.0)
    ext_ref[8:8 + rows_step, :] = x_ref[:, 0:GDN_QKV]

    def conv_act(col, r0):
        sl = slice(col, col + LANES)
        acc = ext_ref[8 + r0:8 + r0 + cc, sl] * cw_ref[3:4, sl]
        for i in range(GDN_CONV - 1):
            acc = acc + ext_ref[5 + i + r0:5 + i + r0 + cc, sl] * cw_ref[i:i + 1, sl]
        return acc * _sigmoid(acc)

    def l2n(x):
        return x * lax.rsqrt(jnp.sum(x * x, axis=-1, keepdims=True) + EPS)

    def pair(xs):
        return [jnp.concatenate([xs[2 * p], xs[2 * p + 1]], axis=1) for p in range(npair)]

    def row_pair(x):
        return pair([x[h:h + 1, :] for h in range(nh)])

    lane = lax.broadcasted_iota(jnp.int32, (nh, cc), 1)
    q_p, k_p, v_p, gc_b, beta_b, kdec_b, gc_rows, s_decay = [], [], [], [], [], [], [], []
    for ck in range(GDN_STEP_CHUNKS):
        r0 = ck * cc
        g = -jnp.exp(alog_ref[...]) * _softplus(abt_ref[0:nh, r0:r0 + cc] + dtb_ref[...])
        beta = _sigmoid(abt_ref[nh:2 * nh, r0:r0 + cc])
        gc = g
        shift = 1
        while shift < cc:
            gc = gc + jnp.where(lane >= shift, pltpu.roll(gc, shift, 1), 0.0)
            shift *= 2
        g_last = jnp.broadcast_to(gc[:, cc - 1:cc], (nh, cc))
        s_decay.append(jnp.exp(g_last))
        cols = jnp.transpose(jnp.concatenate([gc, beta, jnp.zeros((LANES - 2 * nh, cc), F32)], axis=0))
        bcast = [jnp.broadcast_to(cols[:, n:n + 1], (cc, cc)) for n in range(2 * nh)]
        gcb = pair(bcast[0:nh])
        gc_b += gcb
        beta_b += pair(bcast[nh:2 * nh])
        kdec_b += [jnp.exp(gl - x) for gl, x in zip(row_pair(g_last), gcb)]
        gc_rows += row_pair(gc)
        q_p += pair([l2n(conv_act(h * GDN_DK, r0)) * (GDN_DK ** -0.5) for h in range(nh)])
        k_p += pair([l2n(conv_act(hk + h * GDN_DK, r0)) for h in range(nh)])
        v_p += pair([conv_act(2 * hk + h * GDN_DV, r0) for h in range(nh)])
    gam_b = [jnp.exp(x) for x in gc_b]

    ii = lax.broadcasted_iota(jnp.int32, (cc, 2 * cc), 0)
    jl = lax.broadcasted_iota(jnp.int32, (cc, 2 * cc), 1) % cc
    decay = [jnp.exp(jnp.where(ii >= jl, gb - gr, -jnp.inf)) for gb, gr in zip(gc_b, gc_rows)]
    gram = [lax.dot_general(jnp.concatenate([q, k], axis=0).astype(BF16), _blockdiag(k.astype(BF16)), _NT,
                            preferred_element_type=F32) for q, k in zip(q_p, k_p)]
    qk = [gm[0:cc, :] * d for gm, d in zip(gram, decay)]
    lows = [jnp.where(ii > jl, b * gm[cc:2 * cc, :] * d, 0.0) for b, gm, d in zip(beta_b, gram, decay)]
    tinv = _unit_lower_inverse(lows, ii, jl)
    w = [_mmp(t, b * gm * k) for t, b, gm, k in zip(tinv, beta_b, gam_b, k_p)]
    u0 = [_mmp(t, b * v) for t, b, v in zip(tinv, beta_b, v_p)]
    wq = [jnp.concatenate([wp, gm * q], axis=0).astype(BF16) for wp, gm, q in zip(w, gam_b, q_p)]
    kd = [(kdb * k).astype(BF16) for kdb, k in zip(kdec_b, k_p)]

    state = [s_ref[h] for h in range(nh)]
    zero = jnp.zeros((GDN_DK, GDN_DV), BF16)
    for ck in range(GDN_STEP_CHUNKS):
        r0 = ck * cc
        ent = range(ck * npair, (ck + 1) * npair)
        s_bd = [jnp.concatenate([jnp.concatenate([state[2 * p].astype(BF16), zero], axis=1),
                                 jnp.concatenate([zero, state[2 * p + 1].astype(BF16)], axis=1)], axis=0)
                for p in range(npair)]
        ws_qs = [jnp.dot(wq[e], sb, preferred_element_type=F32) for e, sb in zip(ent, s_bd)]
        u = [u0[e] - x[0:cc, :] for e, x in zip(ent, ws_qs)]
        o = [x[cc:2 * cc, :] + _mmp(qk[e], b) for e, x, b in zip(ent, ws_qs, u)]
        upd = [lax.dot_general(kd[e], b.astype(BF16), _TN, preferred_element_type=F32)
               for e, b in zip(ent, u)]
        new_state = []
        for h in range(nh):
            half = slice((h % 2) * cc, (h % 2 + 1) * cc)
            new_state.append(s_decay[ck][h:h + 1, :] * state[h] + upd[h // 2][half, half])
            z = x_ref[r0:r0 + cc, GDN_QKV + h * GDN_DV:GDN_QKV + (h + 1) * GDN_DV]
            o_ref[r0:r0 + cc, h * GDN_DV:(h + 1) * GDN_DV] = (
                _rms(o[h // 2][:, half], on_ref[...]) * (z * _sigmoid(z))).astype(BF16)
        state = new_state
    for h in range(nh):
        s_ref[h] = state[h]


def gdn_prompt(proj, gates_t, conv_w, a_log, dt_bias, o_norm):
    b, t, _ = proj.shape
    rows = GDN_STEP_CHUNKS * GDN_C
    nc = t // rows
    nh = GDN_HEADS
    return pl.pallas_call(
        _gdn_prompt_body,
        grid=(b, nc),
        in_specs=[
            pl.BlockSpec((None, rows, GDN_MAIN), lambda i, c: (i, c, 0)),
            pl.BlockSpec((None, 8, GDN_MAIN), lambda i, c: (i, jnp.maximum(c * (rows // 8) - 1, 0), 0)),
            pl.BlockSpec((LANES, rows), lambda i, c: (0, i * nc + c)),
            pl.BlockSpec((GDN_CONV, GDN_QKV), lambda i, c: (0, 0)),
            pl.BlockSpec((nh, 1), lambda i, c: (0, 0)),
            pl.BlockSpec((nh, 1), lambda i, c: (0, 0)),
            pl.BlockSpec((1, GDN_DV), lambda i, c: (0, 0)),
        ],
        out_specs=[
            pl.BlockSpec((None, rows, nh * GDN_DV), lambda i, c: (i, c, 0)),
            pl.BlockSpec((None, nh, GDN_DK, GDN_DV), lambda i, c: (i, 0, 0, 0)),
        ],
        out_shape=[
            jax.ShapeDtypeStruct((b, t, nh * GDN_DV), BF16),
            jax.ShapeDtypeStruct((b, nh, GDN_DK, GDN_DV), F32),
        ],
        scratch_shapes=[pltpu.VMEM((8 + rows, GDN_QKV), F32)],
        compiler_params=_cparams(("parallel", "arbitrary")),
        name="gdn_prompt",
    )(proj, proj, gates_t, conv_w, a_log.reshape(nh, 1), dt_bias.reshape(nh, 1), o_norm.reshape(1, GDN_DV))


def _gdn_decode_body(x_ref, cs_ref, ab_ref, cw_ref, alog_ref, dtb_ref, on_ref, s0_ref, o_ref, s_ref):
    nh = GDN_HEADS
    for r in range(GDN_DEC_ROWS):
        x = x_ref[r, 0:3 * nh, :]
        conv = x * cw_ref[GDN_CONV - 1]
        for i in range(GDN_CONV - 1):
            conv = conv + cs_ref[r, i] * cw_ref[i]
        act = conv * _sigmoid(conv)
        qa, ka, v8 = act[0:nh], act[nh:2 * nh], act[2 * nh:3 * nh]
        q8 = qa * lax.rsqrt(jnp.sum(qa * qa, axis=-1, keepdims=True) + EPS) * (GDN_DK ** -0.5)
        k8 = ka * lax.rsqrt(jnp.sum(ka * ka, axis=-1, keepdims=True) + EPS)
        g = -jnp.exp(alog_ref[...]) * _softplus(ab_ref[r, 0:nh, :] + dtb_ref[...])
        gam8 = jnp.broadcast_to(jnp.exp(g), (nh, GDN_DV))
        beta8 = jnp.broadcast_to(_sigmoid(ab_ref[r, nh:2 * nh, :]), (nh, GDN_DV))
        qk8 = jnp.broadcast_to(jnp.sum(q8 * k8, axis=-1, keepdims=True), (nh, GDN_DV))
        cols = jnp.transpose(jnp.concatenate([q8, k8, jnp.zeros((LANES - 2 * nh, GDN_DK), F32)], axis=0))
        outs = []
        for h in range(nh):
            s_old = s0_ref[r, h]
            qcol = cols[:, h:h + 1]
            kcol = cols[:, nh + h:nh + h + 1]
            k_s = jnp.sum(kcol * s_old, axis=0, keepdims=True)
            q_s = jnp.sum(qcol * s_old, axis=0, keepdims=True)
            gam = gam8[h:h + 1, :]
            u = beta8[h:h + 1, :] * (v8[h:h + 1, :] - gam * k_s)
            outs.append(gam * q_s + qk8[h:h + 1, :] * u)
            s_ref[r, h] = gam * s_old + kcol * u
        o8 = jnp.concatenate(outs, axis=0)
        z8 = x_ref[r, 3 * nh:4 * nh, :]
        o_ref[r] = (_rms(o8, on_ref[...]) * (z8 * _sigmoid(z8))).astype(BF16)


def gdn_decode(proj, gates, conv_state, s0_all, layer_j, conv_w, a_log, dt_bias, o_norm):
    nb = proj.shape[0]
    nh = GDN_HEADS
    nrow = GDN_QKV // LANES
    o, s = pl.pallas_call(
        _gdn_decode_body,
        grid=(nb // GDN_DEC_ROWS,),
        in_specs=[
            pl.BlockSpec((GDN_DEC_ROWS, GDN_MAIN // LANES, LANES), lambda i: (i, 0, 0)),
            pl.BlockSpec((GDN_DEC_ROWS, GDN_CONV - 1, nrow, LANES), lambda i: (i, 0, 0, 0)),
            pl.BlockSpec((GDN_DEC_ROWS, 2 * nh, 1), lambda i: (i, 0, 0)),
            pl.BlockSpec((GDN_CONV, nrow, LANES), lambda i: (0, 0, 0)),
            pl.BlockSpec((nh, 1), lambda i: (0, 0)),
            pl.BlockSpec((nh, 1), lambda i: (0, 0)),
            pl.BlockSpec((1, GDN_DV), lambda i: (0, 0)),
            pl.BlockSpec((None, GDN_DEC_ROWS, nh, GDN_DK, GDN_DV), lambda i: (layer_j, i, 0, 0, 0)),
        ],
        out_specs=[
            pl.BlockSpec((GDN_DEC_ROWS, nh, GDN_DV), lambda i: (i, 0, 0)),
            pl.BlockSpec((GDN_DEC_ROWS, nh, GDN_DK, GDN_DV), lambda i: (i, 0, 0, 0)),
        ],
        out_shape=[
            jax.ShapeDtypeStruct((nb, nh, GDN_DV), BF16),
            jax.ShapeDtypeStruct((nb, nh, GDN_DK, GDN_DV), F32),
        ],
        compiler_params=_cparams(("parallel",)),
        name="gdn_decode",
    )(proj.reshape(nb, GDN_MAIN // LANES, LANES),
      conv_state.reshape(nb, GDN_CONV - 1, nrow, LANES),
      gates.reshape(nb, 2 * nh, 1),
      conv_w.reshape(GDN_CONV, nrow, LANES),
      a_log.reshape(nh, 1), dt_bias.reshape(nh, 1), o_norm.reshape(1, GDN_DV), s0_all)
    return o.reshape(nb, nh * GDN_DV), s


def kernel(x_prompt, x_sample, cache_k, cache_v, page_table, state_pool, state_conv, state_delta,
           norm_mix, norm_ffn, norm_final, rel_bias,
           w_in_even, pool_w, pool_scale, lambda_q1, lambda_k1, lambda_q2, lambda_k2, subln_w, w_out_even,
           w_in_odd, conv_w, a_log, dt_bias, o_norm, w_out_odd,
           w_gate_up, w_down):
    bp, t, d = x_prompt.shape
    bs = x_sample.shape[0]
    mp = bp * t
    nh = GDN_HEADS
    past = page_table.shape[1] * PAGE_SIZE
    tm_p = 1024
    tm_s = bs

    hp = x_prompt.reshape(mp, d)
    hs = x_sample.reshape(bs, d)
    bias_p, bias_d, bias_n = rel_bias_tiles(rel_bias, past)

    w_in_e = w_in_even.astype(BF16)
    w_in_o = w_in_odd[:, :, 0:GDN_MAIN].astype(BF16)
    w_gates_t = jnp.zeros((w_in_odd.shape[0], LANES, d), F32).at[:, 0:2 * nh, :].set(
        jnp.transpose(w_in_odd[:, :, GDN_MAIN:], (0, 2, 1))).astype(BF16)
    w_out_e = w_out_even.astype(BF16)
    w_out_o = w_out_odd.astype(BF16)
    w_gu = w_gate_up.astype(BF16)
    w_d = w_down.astype(BF16)
    pool_wb = pool_w.astype(BF16)

    k_p, v_p, k_s, v_s, pool_p, pool_s = [], [], [], [], [], []
    conv_p, conv_s, delta_p, delta_s = [], [], [], []
    for layer in range(DEPTH):
        j = layer // 2
        last = layer == DEPTH - 1
        if layer % 2 == 0:
            w_out = w_out_e
            pw = pool_wb[j]
            lam_init = 0.8 - 0.6 * math.exp(-0.3 * layer)
            lam_params = jnp.stack([lambda_q1[j], lambda_k1[j], lambda_q2[j], lambda_k2[j]])

            proj_p, kp, vp = norm_matmul(hp, norm_mix[layer], w_in_e, j, tm=tm_p, tn=DIFF_WIDTH)
            proj_s, ks_, vs_ = norm_matmul(hs, norm_mix[layer], w_in_e, j, tm=tm_s, tn=DIFF_WIDTH)
            proj_p = proj_p.reshape(bp, t, EVEN_IN)

            ypool_p = pool_prompt(proj_p, pw, pool_scale[j])
            oatt_p = attn_prompt(proj_p, bias_p, lam_params, subln_w[j], lam_init)
            ypool_s = pool_decode(jnp.transpose(state_pool[j], (1, 0, 2)), proj_s, pw, pool_scale[j])
            oatt_s = attn_decode(proj_s, cache_k, cache_v, page_table, j, bias_d, bias_n,
                                 lam_params, subln_w[j], lam_init)

            kv_shape = (DIFF_HEADS, DIFF_HEAD_DIM)
            k_p.append(kp.reshape(bp, t, *kv_shape))
            v_p.append(vp.reshape(bp, t, *kv_shape))
            k_s.append(ks_.reshape(bs, 1, *kv_shape))
            v_s.append(vs_.reshape(bs, 1, *kv_shape))
            pool_p.append(proj_p[:, t - POOL_BUF:, 0:POOL_WIDTH])
            pool_s.append(jnp.concatenate([state_pool[j][:, 1:], proj_s[:, None, 0:POOL_WIDTH]], axis=1))

            mix_p = ((ypool_p.reshape(mp, POOL_WIDTH), 0), (oatt_p.reshape(mp, DIFF_WIDTH), 0))
            mix_s = ((ypool_s, 0), (oatt_s, 0))
        else:
            w_out = w_out_o

            proj_p, gates_p = norm_matmul(hp, norm_mix[layer], w_in_o, j, w_gates_t, tm=tm_p, tn=512)
            proj_s, gates_s = norm_matmul(hs, norm_mix[layer], w_in_o, j, w_gates_t, tm=tm_s, tn=512)
            proj_p = proj_p.reshape(bp, t, GDN_MAIN)

            o_p, s_p = gdn_prompt(proj_p, gates_p, conv_w[j], a_log[j], dt_bias[j], o_norm[j])
            o_s, s_s = gdn_decode(proj_s, gates_s[0:2 * nh].T, state_conv[j], state_delta, j,
                                  conv_w[j], a_log[j], dt_bias[j], o_norm[j])

            conv_p.append(proj_p[:, t - (GDN_CONV - 1):, 0:GDN_QKV])
            conv_s.append(jnp.concatenate([state_conv[j][:, 1:], proj_s[:, None, 0:GDN_QKV]], axis=1))
            delta_p.append(s_p)
            delta_s.append(s_s)

            o_p = o_p.reshape(mp, nh * GDN_DV)
            mix_p = ((o_p, 0), (o_p, 1))
            mix_s = ((o_s, 0), (o_s, 1))

        g_fin = norm_final if last else None
        hp = outproj_ffn(hp, mix_p[0], mix_p[1], w_out, j, norm_ffn[layer], w_gu, w_d, layer, g_fin, tm=tm_p)
        hs = outproj_ffn(hs, mix_s[0], mix_s[1], w_out, j, norm_ffn[layer], w_gu, w_d, layer, g_fin, tm=tm_s)

    return (hp.reshape(bp, t, d), hs.reshape(bs, 1, d),
            jnp.stack(k_p, axis=1), jnp.stack(v_p, axis=1), jnp.stack(k_s, axis=1), jnp.stack(v_s, axis=1),
            jnp.stack(pool_p), jnp.stack(pool_s), jnp.stack(conv_p), jnp.stack(conv_s),
            jnp.stack(delta_p), jnp.stack(delta_s))
```

```python
import functools
import math

import jax
import jax.numpy as jnp
from jax import lax
from jax.experimental import pallas as pl
from jax.experimental.pallas import tpu as pltpu

F32 = jnp.float32
BF16 = jnp.bfloat16

D_MODEL = 1024
DEPTH = 4
PAGE_SIZE = 128
POOL_WIDTH = 512
POOL_WINDOWS = (2, 4, 8, 16)
POOL_GROUP = 128
POOL_BUF = 15
DIFF_HEADS = 4
DIFF_HALF = 64
DIFF_HEAD_DIM = 128
DIFF_WIDTH = 512
EVEN_IN = 2048
REL_BUCKETS = 32
REL_MAX_DIST = 128
GDN_HEADS = 8
GDN_DK = 128
GDN_DV = 128
GDN_CONV = 4
GDN_QKV = 3072
GDN_MAIN = 4096
D_FF = 2816
EPS = 1e-6

LANES = 128
VMEM_LIMIT = 48 * 1024 * 1024
NEG_BIG = -1e30

ATT_TQ = 512
ATT_TK = 512
GDN_C = 128
GDN_STEP_CHUNKS = 2
GDN_DEC_ROWS = 4
GDN_PROJ_TN = 512
INV_BLOCK = 16
FFN_TF = 256

_NT = (((1,), (1,)), ((), ()))
_TN = (((0,), (0,)), ((), ()))


def _cparams(sem):
    return pltpu.CompilerParams(dimension_semantics=sem, vmem_limit_bytes=VMEM_LIMIT)


def _sigmoid(x):
    return 1.0 / (1.0 + jnp.exp(-x))


def _softplus(x):
    return jnp.maximum(x, 0.0) + jnp.log1p(jnp.exp(-jnp.abs(x)))


def _rms(x, gain):
    return x * lax.rsqrt(jnp.mean(x * x, axis=-1, keepdims=True) + EPS) * gain


def _mm(a, b):
    return jnp.dot(a.astype(BF16), b.astype(BF16), preferred_element_type=F32)


def _norm_mm_kv_body(x_ref, g_ref, w_ref, o_ref, k_ref, v_ref, xn_ref):
    j = pl.program_id(1)

    @pl.when(j == 0)
    def _():
        xn_ref[...] = _rms(x_ref[...], g_ref[...]).astype(BF16)

    res = jnp.dot(xn_ref[...], w_ref[...], preferred_element_type=F32)
    o_ref[...] = res

    def head_rows(dst_ref):
        for h in range(DIFF_HEADS):
            dst_ref[pl.ds(h, res.shape[0], stride=DIFF_HEADS), :] = res[:, h * DIFF_HEAD_DIM:(h + 1) * DIFF_HEAD_DIM]

    @pl.when(j == 2)
    def _():
        head_rows(k_ref)

    @pl.when(j == 3)
    def _():
        head_rows(v_ref)


def _norm_mm_gate_body(x_ref, g_ref, w_ref, wst_ref, o_ref, ost_ref, xn_ref):
    @pl.when(pl.program_id(1) == 0)
    def _():
        xn = _rms(x_ref[...], g_ref[...]).astype(BF16)
        xn_ref[...] = xn
        ost_ref[...] = lax.dot_general(wst_ref[...], xn, _NT, preferred_element_type=F32)

    o_ref[...] = jnp.dot(xn_ref[...], w_ref[...], preferred_element_type=F32)


def norm_matmul(x, gain, w, idx, w_gates_t=None, *, tm, tn):
    m, d = x.shape
    n = w.shape[2]
    grid = (m // tm, n // tn)
    in_specs = [
        pl.BlockSpec((tm, d), lambda i, j: (i, 0)),
        pl.BlockSpec((1, d), lambda i, j: (0, 0)),
        pl.BlockSpec((None, d, tn), lambda i, j: (idx, 0, j)),
    ]
    out_specs = pl.BlockSpec((tm, tn), lambda i, j: (i, j))
    out_shape = jax.ShapeDtypeStruct((m, n), F32)
    args = [x, gain.reshape(1, d), w]
    if w_gates_t is None:
        assert tn == DIFF_WIDTH and n == EVEN_IN
        kv_spec = pl.BlockSpec((tm * DIFF_HEADS, DIFF_HEAD_DIM), lambda i, j: (i, 0))
        kv_shape = jax.ShapeDtypeStruct((m * DIFF_HEADS, DIFF_HEAD_DIM), F32)
        out_specs = [out_specs, kv_spec, kv_spec]
        out_shape = [out_shape, kv_shape, kv_shape]
        body = _norm_mm_kv_body
    else:
        in_specs.append(pl.BlockSpec((None, LANES, d), lambda i, j: (idx, 0, 0)))
        out_specs = [out_specs, pl.BlockSpec((LANES, tm), lambda i, j: (0, i))]
        out_shape = [out_shape, jax.ShapeDtypeStruct((LANES, m), F32)]
        args.append(w_gates_t)
        body = _norm_mm_gate_body
    return pl.pallas_call(
        body,
        grid=grid,
        in_specs=in_specs,
        out_specs=out_specs,
        out_shape=out_shape,
        scratch_shapes=[pltpu.VMEM((tm, d), BF16)],
        compiler_params=_cparams(("parallel", "arbitrary")),
        name="norm_matmul",
    )(*args)


def _ffn_body(*refs, final):
    if final:
        (res_ref, a0_ref, a1_ref, wo_ref, gf_ref, wg_ref, wu_ref, wd_ref, gfin_ref,
         o_ref, h1_ref, xn_ref, acc_ref) = refs
    else:
        (res_ref, a0_ref, a1_ref, wo_ref, gf_ref, wg_ref, wu_ref, wd_ref,
         o_ref, h1_ref, xn_ref, acc_ref) = refs
    f = pl.program_id(1)

    @pl.when(f == 0)
    def _():
        mixed = jnp.concatenate([a0_ref[...], a1_ref[...]], axis=1)
        h1 = res_ref[...] + jnp.dot(mixed, wo_ref[...], preferred_element_type=F32)
        h1_ref[...] = h1
        xn_ref[...] = _rms(h1, gf_ref[...]).astype(BF16)
        acc_ref[...] = jnp.zeros_like(acc_ref)

    xn = xn_ref[...]
    g = jnp.dot(xn, wg_ref[...], preferred_element_type=F32)
    u = jnp.dot(xn, wu_ref[...], preferred_element_type=F32)
    act = (g * _sigmoid(g) * u).astype(BF16)
    acc_ref[...] += jnp.dot(act, wd_ref[...], preferred_element_type=F32)

    @pl.when(f == pl.num_programs(1) - 1)
    def _():
        y = h1_ref[...] + acc_ref[...]
        if final:
            y = _rms(y, gfin_ref[...])
        o_ref[...] = y


def outproj_ffn(res, mix0, mix1, w_out, idx_out, g_ffn, w_gu, w_d, layer, g_final=None, *, tm):
    m, d = res.shape
    half = w_out.shape[1] // 2
    nf = D_FF // FFN_TF
    (m0, c0), (m1, c1) = mix0, mix1
    final = g_final is not None
    in_specs = [
        pl.BlockSpec((tm, d), lambda i, f: (i, 0)),
        pl.BlockSpec((tm, half), lambda i, f: (i, c0)),
        pl.BlockSpec((tm, half), lambda i, f: (i, c1)),
        pl.BlockSpec((None, 2 * half, d), lambda i, f: (idx_out, 0, 0)),
        pl.BlockSpec((1, d), lambda i, f: (0, 0)),
        pl.BlockSpec((None, d, FFN_TF), lambda i, f: (layer, 0, f)),
        pl.BlockSpec((None, d, FFN_TF), lambda i, f: (layer, 0, nf + f)),
        pl.BlockSpec((None, FFN_TF, d), lambda i, f: (layer, f, 0)),
    ]
    args = [res, m0, m1, w_out, g_ffn.reshape(1, d), w_gu, w_gu, w_d]
    if final:
        in_specs.append(pl.BlockSpec((1, d), lambda i, f: (0, 0)))
        args.append(g_final.reshape(1, d))
    return pl.pallas_call(
        functools.partial(_ffn_body, final=final),
        grid=(m // tm, nf),
        in_specs=in_specs,
        out_specs=pl.BlockSpec((tm, d), lambda i, f: (i, 0)),
        out_shape=jax.ShapeDtypeStruct((m, d), F32),
        scratch_shapes=[pltpu.VMEM((tm, d), F32), pltpu.VMEM((tm, d), BF16), pltpu.VMEM((tm, d), F32)],
        compiler_params=_cparams(("parallel", "arbitrary")),
        name="outproj_ffn",
    )(*args)


POOL_PAD = 16
POOL_ROWS = 512


def _pool_prompt_body(u_ref, pw_ref, ps_ref, o_ref, ext_ref):
    t_len = u_ref.shape[0]
    ext_ref[0:POOL_PAD, :] = jnp.zeros((POOL_PAD, POOL_WIDTH), F32)
    ext_ref[POOL_PAD:, :] = u_ref[...]
    for r0 in range(0, t_len, POOL_ROWS):
        pos = lax.broadcasted_iota(jnp.int32, (POOL_ROWS, 1), 0) + r0
        for g, w in enumerate(POOL_WINDOWS):
            sl = slice(g * POOL_GROUP, (g + 1) * POOL_GROUP)
            cur = ext_ref[POOL_PAD + r0:POOL_PAD + r0 + POOL_ROWS, sl]
            win = cur
            for i in range(1, w):
                win = win + ext_ref[POOL_PAD + r0 - i:POOL_PAD + r0 - i + POOL_ROWS, sl]
            cnt = jnp.minimum(pos + 1, w).astype(F32)
            dlt = win / cnt - cur
            y = jnp.dot(dlt.astype(BF16), pw_ref[g], preferred_element_type=F32) * ps_ref[:, sl]
            o_ref[r0:r0 + POOL_ROWS, sl] = y.astype(BF16)


def pool_prompt(proj, pool_w, pool_scale):
    b, t, _ = proj.shape
    return pl.pallas_call(
        _pool_prompt_body,
        grid=(b,),
        in_specs=[
            pl.BlockSpec((None, t, POOL_WIDTH), lambda i: (i, 0, 0)),
            pl.BlockSpec((len(POOL_WINDOWS), POOL_GROUP, POOL_GROUP), lambda i: (0, 0, 0)),
            pl.BlockSpec((1, POOL_WIDTH), lambda i: (0, 0)),
        ],
        out_specs=pl.BlockSpec((None, t, POOL_WIDTH), lambda i: (i, 0, 0)),
        out_shape=jax.ShapeDtypeStruct((b, t, POOL_WIDTH), BF16),
        scratch_shapes=[pltpu.VMEM((t + POOL_PAD, POOL_WIDTH), F32)],
        compiler_params=_cparams(("parallel",)),
        name="pool_prompt",
    )(proj, pool_w, pool_scale.reshape(1, POOL_WIDTH))


def _pool_decode_body(st_ref, u_ref, pw_ref, ps_ref, o_ref):
    for g, w in enumerate(POOL_WINDOWS):
        sl = slice(g * POOL_GROUP, (g + 1) * POOL_GROUP)
        cur = u_ref[:, sl]
        win = cur
        for i in range(1, w):
            win = win + st_ref[POOL_BUF - i, :, sl]
        dlt = win / float(w) - cur
        y = jnp.dot(dlt.astype(BF16), pw_ref[g], preferred_element_type=F32) * ps_ref[:, sl]
        o_ref[:, sl] = y.astype(BF16)


def pool_decode(state_t, proj, pool_w, pool_scale):
    nb = proj.shape[0]
    return pl.pallas_call(
        _pool_decode_body,
        grid=(1,),
        in_specs=[
            pl.BlockSpec((POOL_BUF, nb, POOL_WIDTH), lambda i: (0, 0, 0)),
            pl.BlockSpec((nb, POOL_WIDTH), lambda i: (0, 0)),
            pl.BlockSpec((len(POOL_WINDOWS), POOL_GROUP, POOL_GROUP), lambda i: (0, 0, 0)),
            pl.BlockSpec((1, POOL_WIDTH), lambda i: (0, 0)),
        ],
        out_specs=pl.BlockSpec((nb, POOL_WIDTH), lambda i: (0, 0)),
        out_shape=jax.ShapeDtypeStruct((nb, POOL_WIDTH), BF16),
        compiler_params=_cparams(("arbitrary",)),
        name="pool_decode",
    )(state_t, proj, pool_w, pool_scale.reshape(1, POOL_WIDTH))


def _rel_bucket(n):
    max_exact = REL_BUCKETS // 2
    nf = jnp.maximum(n, 1).astype(F32)
    large = max_exact + (jnp.log(nf / max_exact) / math.log(REL_MAX_DIST / max_exact)
                         * (REL_BUCKETS - max_exact)).astype(jnp.int32)
    large = jnp.minimum(large, REL_BUCKETS - 1)
    return jnp.where(n < max_exact, n, large)


def _table_lookup(tab_ref, bucket, h):
    out = jnp.zeros(bucket.shape, F32)
    for b in range(REL_BUCKETS):
        out = jnp.where(bucket == b, tab_ref[b, h], out)
    return out


def _rel_bias_body(tab_ref, bp_ref, bd_ref, bn_ref, *, past):
    ii = lax.broadcasted_iota(jnp.int32, (ATT_TQ, ATT_TK), 0)
    jj = lax.broadcasted_iota(jnp.int32, (ATT_TQ, ATT_TK), 1)
    for h in range(DIFF_HEADS):
        for t in range(3):
            dist = t * ATT_TK + ii - jj
            bias = _table_lookup(tab_ref, _rel_bucket(jnp.maximum(dist, 0)), h)
            bp_ref[h, t] = jnp.where(dist >= 0, bias, NEG_BIG)
    nrow = 2 * DIFF_HEADS
    row = lax.broadcasted_iota(jnp.int32, (nrow, past * DIFF_HEADS), 0)
    col = lax.broadcasted_iota(jnp.int32, (nrow, past * DIFF_HEADS), 1)
    bucket = _rel_bucket(past - col // DIFF_HEADS)
    rown = lax.broadcasted_iota(jnp.int32, (nrow, LANES), 0)
    bd = jnp.full((nrow, past * DIFF_HEADS), NEG_BIG, F32)
    bn = jnp.zeros((nrow, LANES), F32)
    for h in range(DIFF_HEADS):
        own = jnp.where(row // 2 == h, col % DIFF_HEADS, -1) == h
        bd = jnp.where(own, _table_lookup(tab_ref, bucket, h), bd)
        bn = jnp.where(rown // 2 == h, tab_ref[0, h], bn)
    bd_ref[...] = bd
    bn_ref[...] = bn


def rel_bias_tiles(rel_bias, past):
    return pl.pallas_call(
        functools.partial(_rel_bias_body, past=past),
        in_specs=[pl.BlockSpec(memory_space=pltpu.SMEM)],
        out_specs=[
            pl.BlockSpec(memory_space=pltpu.VMEM),
            pl.BlockSpec(memory_space=pltpu.VMEM),
            pl.BlockSpec(memory_space=pltpu.VMEM),
        ],
        out_shape=[
            jax.ShapeDtypeStruct((DIFF_HEADS, 3, ATT_TQ, ATT_TK), F32),
            jax.ShapeDtypeStruct((2 * DIFF_HEADS, past * DIFF_HEADS), F32),
            jax.ShapeDtypeStruct((2 * DIFF_HEADS, LANES), F32),
        ],
        compiler_params=pltpu.CompilerParams(vmem_limit_bytes=VMEM_LIMIT),
        name="rel_bias_tiles",
    )(rel_bias)


def _lambda(lam_ref, lam_init):
    lp = lam_ref[...]
    s1 = jnp.sum(lp[0:1, :] * lp[1:2, :], axis=-1, keepdims=True)
    s2 = jnp.sum(lp[2:3, :] * lp[3:4, :], axis=-1, keepdims=True)
    return jnp.exp(s1) - jnp.exp(s2) + lam_init


def _attn_prompt_body(q_ref, k_ref, v_ref, bias_ref, lam_ref, sub_ref, o_ref,
                      kb_ref, vb_ref, qq_ref, s_ref, m_ref, acc_ref, *, lam_init):
    qi = pl.program_id(2)

    @pl.when(qi == 0)
    def _():
        kb_ref[...] = k_ref[...].astype(BF16)
        vb_ref[:, 0:LANES] = v_ref[...].astype(BF16)
        vb_ref[:, LANES:2 * LANES] = jnp.ones((vb_ref.shape[0], LANES), BF16)

    q = q_ref[...] * (DIFF_HALF ** -0.5)
    lane = lax.broadcasted_iota(jnp.int32, q.shape, 1)
    qq_ref[...] = jnp.concatenate([jnp.where(lane < DIFF_HALF, q, 0.0),
                                   jnp.where(lane >= DIFF_HALF, q, 0.0)], axis=0).astype(BF16)
    m_ref[...] = jnp.full(m_ref.shape, -jnp.inf, F32)
    acc_ref[...] = jnp.zeros_like(acc_ref)

    def scores(j):
        off = pl.multiple_of(j * ATT_TK, ATT_TK)
        return lax.dot_general(qq_ref[...], kb_ref[pl.ds(off, ATT_TK), :], _NT, preferred_element_type=F32)

    s_ref[...] = scores(0)

    def step(j, carry):
        bias = bias_ref[jnp.minimum(qi - j, 2)]
        halves = [s_ref[r0:r0 + ATT_TQ, :] + bias for r0 in (0, ATT_TQ)]
        s_ref[...] = scores(jnp.minimum(j + 1, qi))
        vj = vb_ref[pl.ds(pl.multiple_of(j * ATT_TK, ATT_TK), ATT_TK), :]
        for idx, s in enumerate(halves):
            rows = slice(idx * ATT_TQ, (idx + 1) * ATT_TQ)
            m_prev = m_ref[rows, :]
            m_new = jnp.maximum(m_prev, jnp.max(s, axis=-1, keepdims=True))
            alpha = jnp.exp(m_prev - m_new)
            p = jnp.exp(s - jnp.concatenate([m_new] * (ATT_TK // LANES), axis=1))
            acc_ref[rows, :] = (jnp.concatenate([alpha, alpha], axis=1) * acc_ref[rows, :]
                                + jnp.dot(p.astype(BF16), vj, preferred_element_type=F32))
            m_ref[rows, :] = m_new
        return carry

    lax.fori_loop(0, qi + 1, step, 0)

    acc = acc_ref[...]
    o_all = acc[:, 0:LANES] / acc[:, LANES:2 * LANES]
    o = o_all[0:ATT_TQ, :] - _lambda(lam_ref, lam_init) * o_all[ATT_TQ:2 * ATT_TQ, :]
    o_ref[...] = (_rms(o, sub_ref[...]) * (1.0 - lam_init)).astype(BF16)


def attn_prompt(proj, bias_p, lam_params, subln, lam_init):
    b, t, _ = proj.shape
    nh = DIFF_HEADS
    return pl.pallas_call(
        functools.partial(_attn_prompt_body, lam_init=lam_init),
        grid=(b, nh, t // ATT_TQ),
        in_specs=[
            pl.BlockSpec((None, ATT_TQ, LANES), lambda i, h, q: (i, q, nh + h)),
            pl.BlockSpec((None, t, LANES), lambda i, h, q: (i, 0, 2 * nh + h)),
            pl.BlockSpec((None, t, LANES), lambda i, h, q: (i, 0, 3 * nh + h)),
            pl.BlockSpec((None, 3, ATT_TQ, ATT_TK), lambda i, h, q: (h, 0, 0, 0)),
            pl.BlockSpec((4, DIFF_HALF), lambda i, h, q: (0, 0)),
            pl.BlockSpec((1, LANES), lambda i, h, q: (0, 0)),
        ],
        out_specs=pl.BlockSpec((None, ATT_TQ, LANES), lambda i, h, q: (i, q, h)),
        out_shape=jax.ShapeDtypeStruct((b, t, DIFF_WIDTH), BF16),
        scratch_shapes=[
            pltpu.VMEM((t, LANES), BF16),
            pltpu.VMEM((t, 2 * LANES), BF16),
            pltpu.VMEM((2 * ATT_TQ, LANES), BF16),
            pltpu.VMEM((2 * ATT_TQ, ATT_TK), F32),
            pltpu.VMEM((2 * ATT_TQ, LANES), F32),
            pltpu.VMEM((2 * ATT_TQ, 2 * LANES), F32),
        ],
        compiler_params=_cparams(("parallel", "parallel", "arbitrary")),
        name="attn_prompt",
    )(proj, proj, proj, bias_p, lam_params, subln.reshape(1, LANES))


def _attn_decode_body(pt_ref, x_ref, *rest, n_pages, lam_init):
    k_refs = rest[:n_pages]
    v_refs = rest[n_pages:2 * n_pages]
    bd_ref, bn_ref, lam_ref, sub_ref, o_ref = rest[2 * n_pages:]
    page_rows = PAGE_SIZE * DIFF_HEADS
    lane = lax.broadcasted_iota(jnp.int32, (1, DIFF_HEAD_DIM), 1)
    rows_q, rows_k, rows_v = [], [], []
    for h in range(DIFF_HEADS):
        sl = slice(h * DIFF_HEAD_DIM, (h + 1) * DIFF_HEAD_DIM)
        qh = x_ref[:, DIFF_WIDTH:2 * DIFF_WIDTH][:, sl] * (DIFF_HALF ** -0.5)
        kh = x_ref[:, 2 * DIFF_WIDTH:3 * DIFF_WIDTH][:, sl]
        vh = x_ref[:, 3 * DIFF_WIDTH:4 * DIFF_WIDTH][:, sl]
        rows_q += [jnp.where(lane < DIFF_HALF, qh, 0.0), jnp.where(lane >= DIFF_HALF, qh, 0.0)]
        rows_k += [kh, kh]
        rows_v += [vh, vh]
    q8 = jnp.concatenate(rows_q, axis=0)
    k8 = jnp.concatenate(rows_k, axis=0)
    v8 = jnp.concatenate(rows_v, axis=0)
    q8b = q8.astype(BF16)
    s = jnp.concatenate(
        [lax.dot_general(q8b, k_refs[p][...].astype(BF16), _NT, preferred_element_type=F32)
         for p in range(n_pages)], axis=1) + bd_ref[...]
    s_new = jnp.sum(q8 * k8, axis=-1, keepdims=True) + bn_ref[:, 0:1]
    m = jnp.maximum(jnp.max(s, axis=-1, keepdims=True), s_new)
    p = jnp.exp(s - m)
    p_new = jnp.exp(s_new - m)
    inv_l = 1.0 / (jnp.sum(p, axis=-1, keepdims=True) + p_new)
    a = p * inv_l
    r = (p_new * inv_l) * v8
    for pg in range(n_pages):
        r = r + jnp.dot(a[:, pg * page_rows:(pg + 1) * page_rows].astype(BF16),
                        v_refs[pg][...].astype(BF16), preferred_element_type=F32)
    lam = _lambda(lam_ref, lam_init)
    for h in range(DIFF_HEADS):
        o = r[2 * h:2 * h + 1, :] - lam * r[2 * h + 1:2 * h + 2, :]
        o_ref[:, h * DIFF_HEAD_DIM:(h + 1) * DIFF_HEAD_DIM] = (
            _rms(o, sub_ref[...]) * (1.0 - lam_init)).astype(BF16)


def attn_decode(proj, cache_k, cache_v, page_table, layer_j, bias_d, bias_n, lam_params, subln, lam_init):
    nb = proj.shape[0]
    n_pages = page_table.shape[1]
    n_pool, n_even = cache_k.shape[:2]
    page_rows = PAGE_SIZE * DIFF_HEADS
    ck = cache_k.reshape(n_pool, n_even, page_rows, DIFF_HEAD_DIM)
    cv = cache_v.reshape(n_pool, n_even, page_rows, DIFF_HEAD_DIM)

    def page_spec(p):
        return pl.BlockSpec((None, None, page_rows, DIFF_HEAD_DIM),
                            lambda i, pt: (pt[i * n_pages + p], layer_j, 0, 0))

    full = lambda shape: pl.BlockSpec(shape, lambda i, pt: (0,) * len(shape))
    grid_spec = pltpu.PrefetchScalarGridSpec(
        num_scalar_prefetch=1,
        grid=(nb,),
        in_specs=([pl.BlockSpec((None, 1, EVEN_IN), lambda i, pt: (i, 0, 0))]
                  + [page_spec(p) for p in range(n_pages)] * 2
                  + [full(bias_d.shape), full(bias_n.shape), full((4, DIFF_HALF)), full((1, LANES))]),
        out_specs=pl.BlockSpec((None, 1, DIFF_WIDTH), lambda i, pt: (i, 0, 0)),
    )
    out = pl.pallas_call(
        functools.partial(_attn_decode_body, n_pages=n_pages, lam_init=lam_init),
        grid_spec=grid_spec,
        out_shape=jax.ShapeDtypeStruct((nb, 1, DIFF_WIDTH), BF16),
        compiler_params=_cparams(("arbitrary",)),
        name="attn_decode",
    )(page_table.reshape(-1), proj.reshape(nb, 1, EVEN_IN), *([ck] * n_pages), *([cv] * n_pages),
      bias_d, bias_n, lam_params, subln.reshape(1, LANES))
    return out.reshape(nb, DIFF_WIDTH)


def _blockdiag(x):
    lane = lax.broadcasted_iota(jnp.int32, x.shape, 1)
    zero = jnp.zeros_like(x)
    return jnp.concatenate([jnp.where(lane < GDN_C, x, zero), jnp.where(lane >= GDN_C, x, zero)], axis=0)


def _mmp(a, b):
    return jnp.dot(a.astype(BF16), _blockdiag(b.astype(BF16)), preferred_element_type=F32)


def _unit_lower_inverse(lows, ii, jl):
    eye = jnp.where(ii == jl, 1.0, 0.0)
    in_block = (ii // INV_BLOCK) == (jl // INV_BLOCK)
    ps = [jnp.where(in_block, -low, 0.0) for low in lows]
    offs = [jnp.where(in_block, 0.0, low) for low in lows]
    dinvs = [eye + p for p in ps]
    span = 2
    while span < INV_BLOCK:
        ps = [_mmp(p, p) for p in ps]
        dinvs = [d + _mmp(d, p) for d, p in zip(dinvs, ps)]
        span *= 2
    powers = [[-_mmp(d, off) for d, off in zip(dinvs, offs)]]
    span = 2
    while span < GDN_C // INV_BLOCK:
        powers.append([_mmp(n, n) for n in powers[-1]])
        span *= 2
    xs = dinvs
    for pw in reversed(powers):
        xs = [x + _mmp(n, x) for n, x in zip(pw, xs)]
    return xs


def _gdn_prompt_body(h_ref, gn_ref, w_ref, wgt_ref, cw_ref, alog_ref, dtb_ref, on_ref,
                     o_ref, s_ref, cst_ref, ext_ref, z_ref):
    c = pl.program_id(1)
    nh = GDN_HEADS
    npair = nh // 2
    hk = nh * GDN_DK
    cc = GDN_C
    rows_step = GDN_STEP_CHUNKS * cc
    heads_per_tile = GDN_PROJ_TN // LANES

    @pl.when(c == 0)
    def _():
        s_ref[...] = jnp.zeros_like(s_ref)
        ext_ref[0:8, :] = jnp.zeros((8, GDN_QKV), F32)

    @pl.when(c > 0)
    def _():
        ext_ref[0:8, :] = ext_ref[rows_step:rows_step + 8, :]

    xn = _rms(h_ref[...], gn_ref[...]).astype(BF16)
    abt = lax.dot_general(wgt_ref[...], xn, _NT, preferred_element_type=F32)

    def project(n):
        cols = slice(n * GDN_PROJ_TN, (n + 1) * GDN_PROJ_TN)
        tile = jnp.dot(xn, w_ref[:, cols], preferred_element_type=F32)
        if n < GDN_QKV // GDN_PROJ_TN:
            ext_ref[8:8 + rows_step, cols] = tile
        else:
            z_ref[:, n * GDN_PROJ_TN - GDN_QKV:(n + 1) * GDN_PROJ_TN - GDN_QKV] = tile

    def conv_act(col, r0):
        sl = slice(col, col + LANES)
        acc = ext_ref[8 + r0:8 + r0 + cc, sl] * cw_ref[3:4, sl]
        for i in range(GDN_CONV - 1):
            acc = acc + ext_ref[5 + i + r0:5 + i + r0 + cc, sl] * cw_ref[i:i + 1, sl]
        return acc * _sigmoid(acc)

    def l2n(x):
        return x * lax.rsqrt(jnp.sum(x * x, axis=-1, keepdims=True) + EPS)

    def pair(xs):
        return [jnp.concatenate([xs[2 * p], xs[2 * p + 1]], axis=1) for p in range(npair)]

    def row_pair(x):
        return pair([x[h:h + 1, :] for h in range(nh)])

    project(0)
    lane = lax.broadcasted_iota(jnp.int32, (nh, cc), 1)
    q_p, k_p, v_p, gc_b, beta_b, kdec_b, gc_rows, s_decay = [], [], [], [], [], [], [], []
    for ck in range(GDN_STEP_CHUNKS):
        r0 = ck * cc
        g = -jnp.exp(alog_ref[...]) * _softplus(abt[0:nh, r0:r0 + cc] + dtb_ref[...])
        beta = _sigmoid(abt[nh:2 * nh, r0:r0 + cc])
        gc = g
        shift = 1
        while shift < cc:
            gc = gc + jnp.where(lane >= shift, pltpu.roll(gc, shift, 1), 0.0)
            shift *= 2
        g_last = jnp.broadcast_to(gc[:, cc - 1:cc], (nh, cc))
        s_decay.append(jnp.exp(g_last))
        cols = jnp.transpose(jnp.concatenate([gc, beta, jnp.zeros((LANES - 2 * nh, cc), F32)], axis=0))
        bcast = [jnp.broadcast_to(cols[:, n:n + 1], (cc, cc)) for n in range(2 * nh)]
        gcb = pair(bcast[0:nh])
        gc_b += gcb
        beta_b += pair(bcast[nh:2 * nh])
        kdec_b += [jnp.exp(gl - x) for gl, x in zip(row_pair(g_last), gcb)]
        gc_rows += row_pair(gc)
    gam_b = [jnp.exp(x) for x in gc_b]
    ii = lax.broadcasted_iota(jnp.int32, (cc, 2 * cc), 0)
    jl = lax.broadcasted_iota(jnp.int32, (cc, 2 * cc), 1) % cc
    decay = [jnp.exp(jnp.where(ii >= jl, gb - gr, -jnp.inf)) for gb, gr in zip(gc_b, gc_rows)]

    act = {}
    for n in range(1, GDN_MAIN // GDN_PROJ_TN + 1):
        if n < GDN_MAIN // GDN_PROJ_TN:
            project(n)
        if n <= GDN_QKV // GDN_PROJ_TN:
            for hh in range(heads_per_tile):
                col = (n - 1) * GDN_PROJ_TN + hh * LANES
                for ck in range(GDN_STEP_CHUNKS):
                    a = conv_act(col, ck * cc)
                    if col < hk:
                        a = l2n(a) * (GDN_DK ** -0.5)
                    elif col < 2 * hk:
                        a = l2n(a)
                    act[(col, ck)] = a
    for ck in range(GDN_STEP_CHUNKS):
        q_p += pair([act[(h * GDN_DK, ck)] for h in range(nh)])
        k_p += pair([act[(hk + h * GDN_DK, ck)] for h in range(nh)])
        v_p += pair([act[(2 * hk + h * GDN_DV, ck)] for h in range(nh)])

    gram = [lax.dot_general(jnp.concatenate([q, k], axis=0).astype(BF16), _blockdiag(k.astype(BF16)), _NT,
                            preferred_element_type=F32) for q, k in zip(q_p, k_p)]
    qk = [gm[0:cc, :] * d for gm, d in zip(gram, decay)]
    lows = [jnp.where(ii > jl, b * gm[cc:2 * cc, :] * d, 0.0) for b, gm, d in zip(beta_b, gram, decay)]
    tinv = _unit_lower_inverse(lows, ii, jl)
    w =[_mmp(t, b * gm * k) for t, b, gm, k in zip(tinv, beta_b, gam_b, k_p)]
    u0 = [_mmp(t, b * v) for t, b, v in zip(tinv, beta_b, v_p)]
    wq = [jnp.concatenate([wp, gm * q], axis=0).astype(BF16) for wp, gm, q in zip(w, gam_b, q_p)]
    kd = [(kdb * k).astype(BF16) for kdb, k in zip(kdec_b, k_p)]

    state = [s_ref[h] for h in range(nh)]
    zero = jnp.zeros((GDN_DK, GDN_DV), BF16)
    for ck in range(GDN_STEP_CHUNKS):
        r0 = ck * cc
        ent = range(ck * npair, (ck + 1) * npair)
        s_bd = [jnp.concatenate([jnp.concatenate([state[2 * p].astype(BF16), zero], axis=1),
                                 jnp.concatenate([zero, state[2 * p + 1].astype(BF16)], axis=1)], axis=0)
                for p in range(npair)]
        ws_qs = [jnp.dot(wq[e], sb, preferred_element_type=F32) for e, sb in zip(ent, s_bd)]
        u = [u0[e] - x[0:cc, :] for e, x in zip(ent, ws_qs)]
        o = [x[cc:2 * cc, :] + _mmp(qk[e], b) for e, x, b in zip(ent, ws_qs, u)]
        upd = [lax.dot_general(kd[e], b.astype(BF16), _TN, preferred_element_type=F32)
               for e, b in zip(ent, u)]
        new_state = []
        for h in range(nh):
            half = slice((h % 2) * cc, (h % 2 + 1) * cc)
            new_state.append(s_decay[ck][h:h + 1, :] * state[h] + upd[h // 2][half, half])
            z = z_ref[r0:r0 + cc, h * GDN_DV:(h + 1) * GDN_DV]
            o_ref[r0:r0 + cc, h * GDN_DV:(h + 1) * GDN_DV] = (
                _rms(o[h // 2][:, half], on_ref[...]) * (z * _sigmoid(z))).astype(BF16)
        state = new_state
    for h in range(nh):
        s_ref[h] = state[h]

    @pl.when(c == pl.num_programs(1) - 1)
    def _():
        cst_ref[...] = ext_ref[rows_step:rows_step + 8, :]


def gdn_prompt(h, gain, w_in, w_gates_t, idx, conv_w, a_log, dt_bias, o_norm):
    b, t, d = h.shape
    rows = GDN_STEP_CHUNKS * GDN_C
    nc = t // rows
    nh = GDN_HEADS
    return pl.pallas_call(
        _gdn_prompt_body,
        grid=(b, nc),
        in_specs=[
            pl.BlockSpec((None, rows, d), lambda i, c: (i, c, 0)),
            pl.BlockSpec((1, d), lambda i, c: (0, 0)),
            pl.BlockSpec((None, d, GDN_MAIN), lambda i, c: (idx, 0, 0)),
            pl.BlockSpec((None, LANES, d), lambda i, c: (idx, 0, 0)),
            pl.BlockSpec((GDN_CONV, GDN_QKV), lambda i, c: (0, 0)),
            pl.BlockSpec((nh, 1), lambda i, c: (0, 0)),
            pl.BlockSpec((nh, 1), lambda i, c: (0, 0)),
            pl.BlockSpec((1, GDN_DV), lambda i, c: (0, 0)),
        ],
        out_specs=[
            pl.BlockSpec((None, rows, nh * GDN_DV), lambda i, c: (i, c, 0)),
            pl.BlockSpec((None, nh, GDN_DK, GDN_DV), lambda i, c: (i, 0, 0, 0)),
            pl.BlockSpec((None, 8, GDN_QKV), lambda i, c: (i, 0, 0)),
        ],
        out_shape=[
            jax.ShapeDtypeStruct((b, t, nh * GDN_DV), BF16),
            jax.ShapeDtypeStruct((b, nh, GDN_DK, GDN_DV), F32),
            jax.ShapeDtypeStruct((b, 8, GDN_QKV), F32),
        ],
        scratch_shapes=[pltpu.VMEM((8 + rows, GDN_QKV), F32), pltpu.VMEM((rows, nh * GDN_DV), F32)],
        compiler_params=_cparams(("parallel", "arbitrary")),
        name="gdn_prompt",
    )(h, gain.reshape(1, d), w_in, w_gates_t, conv_w, a_log.reshape(nh, 1), dt_bias.reshape(nh, 1),
      o_norm.reshape(1, GDN_DV))


def _gdn_decode_body(x_ref, cs_ref, ab_ref, cw_ref, alog_ref, dtb_ref, on_ref, s0_ref, o_ref, s_ref):
    nh = GDN_HEADS
    for r in range(GDN_DEC_ROWS):
        x = x_ref[r, 0:3 * nh, :]
        conv = x * cw_ref[GDN_CONV - 1]
        for i in range(GDN_CONV - 1):
            conv = conv + cs_ref[r, i] * cw_ref[i]
        act = conv * _sigmoid(conv)
        qa, ka, v8 = act[0:nh], act[nh:2 * nh], act[2 * nh:3 * nh]
        q8 = qa * lax.rsqrt(jnp.sum(qa * qa, axis=-1, keepdims=True) + EPS) * (GDN_DK ** -0.5)
        k8 = ka * lax.rsqrt(jnp.sum(ka * ka, axis=-1, keepdims=True) + EPS)
        g = -jnp.exp(alog_ref[...]) * _softplus(ab_ref[r, 0:nh, :] + dtb_ref[...])
        gam8 = jnp.broadcast_to(jnp.exp(g), (nh, GDN_DV))
        beta8 = jnp.broadcast_to(_sigmoid(ab_ref[r, nh:2 * nh, :]), (nh, GDN_DV))
        qk8 = jnp.broadcast_to(jnp.sum(q8 * k8, axis=-1, keepdims=True), (nh, GDN_DV))
        cols = jnp.transpose(jnp.concatenate([q8, k8, jnp.zeros((LANES - 2 * nh, GDN_DK), F32)], axis=0))
        outs = []
        for h in range(nh):
            s_old = s0_ref[r, h]
            qcol = cols[:, h:h + 1]
            kcol = cols[:, nh + h:nh + h + 1]
            k_s = jnp.sum(kcol * s_old, axis=0, keepdims=True)
            q_s = jnp.sum(qcol * s_old, axis=0, keepdims=True)
            gam = gam8[h:h + 1, :]
            u = beta8[h:h + 1, :] * (v8[h:h + 1, :] - gam * k_s)
            outs.append(gam * q_s + qk8[h:h + 1, :] * u)
            s_ref[r, h] = gam * s_old + kcol * u
        o8 = jnp.concatenate(outs, axis=0)
        z8 = x_ref[r, 3 * nh:4 * nh, :]
        o_ref[r] = (_rms(o8, on_ref[...]) * (z8 * _sigmoid(z8))).astype(BF16)


def gdn_decode(proj, gates, conv_state, s0_all, layer_j, conv_w, a_log, dt_bias, o_norm):
    nb = proj.shape[0]
    nh = GDN_HEADS
    nrow = GDN_QKV // LANES
    o, s = pl.pallas_call(
        _gdn_decode_body,
        grid=(nb // GDN_DEC_ROWS,),
        in_specs=[
            pl.BlockSpec((GDN_DEC_ROWS, GDN_MAIN // LANES, LANES), lambda i: (i, 0, 0)),
            pl.BlockSpec((GDN_DEC_ROWS, GDN_CONV - 1, nrow, LANES), lambda i: (i, 0, 0, 0)),
            pl.BlockSpec((GDN_DEC_ROWS, 2 * nh, 1), lambda i: (i, 0, 0)),
            pl.BlockSpec((GDN_CONV, nrow, LANES), lambda i: (0, 0, 0)),
            pl.BlockSpec((nh, 1), lambda i: (0, 0)),
            pl.BlockSpec((nh, 1), lambda i: (0, 0)),
            pl.BlockSpec((1, GDN_DV), lambda i: (0, 0)),
            pl.BlockSpec((None, GDN_DEC_ROWS, nh, GDN_DK, GDN_DV), lambda i: (layer_j, i, 0, 0, 0)),
        ],
        out_specs=[
            pl.BlockSpec((GDN_DEC_ROWS, nh, GDN_DV), lambda i: (i, 0, 0)),
            pl.BlockSpec((GDN_DEC_ROWS, nh, GDN_DK, GDN_DV), lambda i: (i, 0, 0, 0)),
        ],
        out_shape=[
            jax.ShapeDtypeStruct((nb, nh, GDN_DV), BF16),
            jax.ShapeDtypeStruct((nb, nh, GDN_DK, GDN_DV), F32),
        ],
        compiler_params=_cparams(("parallel",)),
        name="gdn_decode",
    )(proj.reshape(nb, GDN_MAIN // LANES, LANES),
      conv_state.reshape(nb, GDN_CONV - 1, nrow, LANES),
      gates.reshape(nb, 2 * nh, 1),
      conv_w.reshape(GDN_CONV, nrow, LANES),
      a_log.reshape(nh, 1), dt_bias.reshape(nh, 1), o_norm.reshape(1, GDN_DV), s0_all)
    return o.reshape(nb, nh * GDN_DV), s


def kernel(x_prompt, x_sample, cache_k, cache_v, page_table, state_pool, state_conv, state_delta,
           norm_mix, norm_ffn, norm_final, rel_bias,
           w_in_even, pool_w, pool_scale, lambda_q1, lambda_k1, lambda_q2, lambda_k2, subln_w, w_out_even,
           w_in_odd, conv_w, a_log, dt_bias, o_norm, w_out_odd,
           w_gate_up, w_down):
    bp, t, d = x_prompt.shape
    bs = x_sample.shape[0]
    mp = bp * t
    nh = GDN_HEADS
    past = page_table.shape[1] * PAGE_SIZE
    tm_p = 1024
    tm_s = bs

    hp = x_prompt.reshape(mp, d)
    hs = x_sample.reshape(bs, d)
    bias_p, bias_d, bias_n = rel_bias_tiles(rel_bias, past)

    w_in_e = w_in_even.astype(BF16)
    w_in_o = w_in_odd[:, :, 0:GDN_MAIN].astype(BF16)
    w_gates_t = jnp.zeros((w_in_odd.shape[0], LANES, d), F32).at[:, 0:2 * nh, :].set(
        jnp.transpose(w_in_odd[:, :, GDN_MAIN:], (0, 2, 1))).astype(BF16)
    w_out_e = w_out_even.astype(BF16)
    w_out_o = w_out_odd.astype(BF16)
    w_gu = w_gate_up.astype(BF16)
    w_d = w_down.astype(BF16)
    pool_wb = pool_w.astype(BF16)

    k_p, v_p, k_s, v_s, pool_p, pool_s = [], [], [], [], [], []
    conv_p, conv_s, delta_p, delta_s = [], [], [], []
    for layer in range(DEPTH):
        j = layer // 2
        last = layer == DEPTH - 1
        if layer % 2 == 0:
            w_out = w_out_e
            pw = pool_wb[j]
            lam_init = 0.8 - 0.6 * math.exp(-0.3 * layer)
            lam_params = jnp.stack([lambda_q1[j], lambda_k1[j], lambda_q2[j], lambda_k2[j]])

            proj_p, kp, vp = norm_matmul(hp, norm_mix[layer], w_in_e, j, tm=tm_p, tn=DIFF_WIDTH)
            proj_s, ks_, vs_ = norm_matmul(hs, norm_mix[layer], w_in_e, j, tm=tm_s, tn=DIFF_WIDTH)
            proj_p = proj_p.reshape(bp, t, EVEN_IN)

            ypool_p = pool_prompt(proj_p, pw, pool_scale[j])
            oatt_p = attn_prompt(proj_p, bias_p, lam_params, subln_w[j], lam_init)
            ypool_s = pool_decode(jnp.transpose(state_pool[j], (1, 0, 2)), proj_s, pw, pool_scale[j])
            oatt_s = attn_decode(proj_s, cache_k, cache_v, page_table, j, bias_d, bias_n,
                                 lam_params, subln_w[j], lam_init)

            kv_shape = (DIFF_HEADS, DIFF_HEAD_DIM)
            k_p.append(kp.reshape(bp, t, *kv_shape))
            v_p.append(vp.reshape(bp, t, *kv_shape))
            k_s.append(ks_.reshape(bs, 1, *kv_shape))
            v_s.append(vs_.reshape(bs, 1, *kv_shape))
            pool_p.append(proj_p[:, t - POOL_BUF:, 0:POOL_WIDTH])
            pool_s.append(jnp.concatenate([state_pool[j][:, 1:], proj_s[:, None, 0:POOL_WIDTH]], axis=1))

            mix_p = ((ypool_p.reshape(mp, POOL_WIDTH), 0), (oatt_p.reshape(mp, DIFF_WIDTH), 0))
            mix_s = ((ypool_s, 0), (oatt_s, 0))
        else:
            w_out = w_out_o

            proj_s, gates_s = norm_matmul(hs, norm_mix[layer], w_in_o, j, w_gates_t, tm=tm_s, tn=512)

            o_p, s_p, tail_p = gdn_prompt(hp.reshape(bp, t, d), norm_mix[layer], w_in_o, w_gates_t, j,
                                          conv_w[j], a_log[j], dt_bias[j], o_norm[j])
            o_s, s_s = gdn_decode(proj_s, gates_s[0:2 * nh].T, state_conv[j], state_delta, j,
                                  conv_w[j], a_log[j], dt_bias[j], o_norm[j])

            conv_p.append(tail_p[:, 8 - (GDN_CONV - 1):, :])
            conv_s.append(jnp.concatenate([state_conv[j][:, 1:], proj_s[:, None, 0:GDN_QKV]], axis=1))
            delta_p.append(s_p)
            delta_s.append(s_s)

            o_p = o_p.reshape(mp, nh * GDN_DV)
            mix_p = ((o_p, 0), (o_p, 1))
            mix_s = ((o_s, 0), (o_s, 1))

        g_fin = norm_final if last else None
        hp = outproj_ffn(hp, mix_p[0], mix_p[1], w_out, j, norm_ffn[layer], w_gu, w_d, layer, g_fin, tm=tm_p)
        hs = outproj_ffn(hs, mix_s[0], mix_s[1], w_out, j, norm_ffn[layer], w_gu, w_d, layer, g_fin, tm=tm_s)

    return (hp.reshape(bp, t, d), hs.reshape(bs, 1, d),
            jnp.stack(k_p, axis=1), jnp.stack(v_p, axis=1), jnp.stack(k_s, axis=1), jnp.stack(v_s, axis=1),
            jnp.stack(pool_p), jnp.stack(pool_s), jnp.stack(conv_p), jnp.stack(conv_s),
            jnp.stack(delta_p), jnp.stack(delta_s))
```

```python
import functools
import math

import jax
import jax.numpy as jnp
from jax import lax
from jax.experimental import pallas as pl
from jax.experimental.pallas import tpu as pltpu

F32 = jnp.float32
BF16 = jnp.bfloat16

D_MODEL = 1024
DEPTH = 4
PAGE_SIZE = 128
POOL_WIDTH = 512
POOL_WINDOWS = (2, 4, 8, 16)
POOL_GROUP = 128
POOL_BUF = 15
DIFF_HEADS = 4
DIFF_HALF = 64
DIFF_HEAD_DIM = 128
DIFF_WIDTH = 512
EVEN_IN = 2048
REL_BUCKETS = 32
REL_MAX_DIST = 128
GDN_HEADS = 8
GDN_DK = 128
GDN_DV = 128
GDN_CONV = 4
GDN_QKV = 3072
GDN_MAIN = 4096
D_FF = 2816
EPS = 1e-6

LANES = 128
VMEM_LIMIT = 48 * 1024 * 1024
NEG_BIG = -1e30

ATT_TQ = 512
ATT_TK = 512
GDN_C = 128
GDN_STEP_CHUNKS = 2
GDN_DEC_ROWS = 4
GDN_PROJ_TN = 512
INV_BLOCK = 16
FFN_TF = 256
FFN_TM = 512

_NT = (((1,), (1,)), ((), ()))
_TN = (((0,), (0,)), ((), ()))


def _cparams(sem):
    return pltpu.CompilerParams(dimension_semantics=sem, vmem_limit_bytes=VMEM_LIMIT)


def _sigmoid(x):
    return 1.0 / (1.0 + jnp.exp(-x))


def _softplus(x):
    return jnp.maximum(x, 0.0) + jnp.log1p(jnp.exp(-jnp.abs(x)))


def _rms(x, gain):
    return x * lax.rsqrt(jnp.mean(x * x, axis=-1, keepdims=True) + EPS) * gain


def _mm(a, b):
    return jnp.dot(a.astype(BF16), b.astype(BF16), preferred_element_type=F32)


def _norm_mm_kv_body(x_ref, g_ref, w_ref, o_ref, k_ref, v_ref, xn_ref):
    j = pl.program_id(1)

    @pl.when(j == 0)
    def _():
        xn_ref[...] = _rms(x_ref[...], g_ref[...]).astype(BF16)

    res = jnp.dot(xn_ref[...], w_ref[...], preferred_element_type=F32)
    o_ref[...] = res

    def head_rows(dst_ref):
        for h in range(DIFF_HEADS):
            dst_ref[pl.ds(h, res.shape[0], stride=DIFF_HEADS), :] = res[:, h * DIFF_HEAD_DIM:(h + 1) * DIFF_HEAD_DIM]

    @pl.when(j == 2)
    def _():
        head_rows(k_ref)

    @pl.when(j == 3)
    def _():
        head_rows(v_ref)


def _norm_mm_gate_body(x_ref, g_ref, w_ref, wst_ref, o_ref, ost_ref, xn_ref):
    @pl.when(pl.program_id(1) == 0)
    def _():
        xn = _rms(x_ref[...], g_ref[...]).astype(BF16)
        xn_ref[...] = xn
        ost_ref[...] = lax.dot_general(wst_ref[...], xn, _NT, preferred_element_type=F32)

    o_ref[...] = jnp.dot(xn_ref[...], w_ref[...], preferred_element_type=F32)


def norm_matmul(x, gain, w, idx, w_gates_t=None, *, tm, tn):
    m, d = x.shape
    n = w.shape[2]
    grid = (m // tm, n // tn)
    in_specs = [
        pl.BlockSpec((tm, d), lambda i, j: (i, 0)),
        pl.BlockSpec((1, d), lambda i, j: (0, 0)),
        pl.BlockSpec((None, d, tn), lambda i, j: (idx, 0, j)),
    ]
    out_specs = pl.BlockSpec((tm, tn), lambda i, j: (i, j))
    out_shape = jax.ShapeDtypeStruct((m, n), F32)
    args = [x, gain.reshape(1, d), w]
    if w_gates_t is None:
        assert tn == DIFF_WIDTH and n == EVEN_IN
        kv_spec = pl.BlockSpec((tm * DIFF_HEADS, DIFF_HEAD_DIM), lambda i, j: (i, 0))
        kv_shape = jax.ShapeDtypeStruct((m * DIFF_HEADS, DIFF_HEAD_DIM), F32)
        out_specs = [out_specs, kv_spec, kv_spec]
        out_shape = [out_shape, kv_shape, kv_shape]
        body = _norm_mm_kv_body
    else:
        in_specs.append(pl.BlockSpec((None, LANES, d), lambda i, j: (idx, 0, 0)))
        out_specs = [out_specs, pl.BlockSpec((LANES, tm), lambda i, j: (0, i))]
        out_shape = [out_shape, jax.ShapeDtypeStruct((LANES, m), F32)]
        args.append(w_gates_t)
        body = _norm_mm_gate_body
    return pl.pallas_call(
        body,
        grid=grid,
        in_specs=in_specs,
        out_specs=out_specs,
        out_shape=out_shape,
        scratch_shapes=[pltpu.VMEM((tm, d), BF16)],
        compiler_params=_cparams(("parallel", "arbitrary")),
        name="norm_matmul",
    )(*args)


def _ffn_body(*refs, final):
    if final:
        res_ref, a0_ref, a1_ref, wo_ref, gf_ref, wgu_ref, wd_ref, gfin_ref, o_ref = refs
    else:
        res_ref, a0_ref, a1_ref, wo_ref, gf_ref, wgu_ref, wd_ref, o_ref = refs
    mixed = jnp.concatenate([a0_ref[...], a1_ref[...]], axis=1)
    h1 = res_ref[...] + jnp.dot(mixed, wo_ref[...], preferred_element_type=F32)
    xn = _rms(h1, gf_ref[...]).astype(BF16)
    acts = []
    for f in range(D_FF // FFN_TF):
        g = jnp.dot(xn, wgu_ref[:, f * FFN_TF:(f + 1) * FFN_TF], preferred_element_type=F32)
        u = jnp.dot(xn, wgu_ref[:, D_FF + f * FFN_TF:D_FF + (f + 1) * FFN_TF], preferred_element_type=F32)
        acts.append((g * _sigmoid(g) * u).astype(BF16))
    y = h1 + jnp.dot(jnp.concatenate(acts, axis=1), wd_ref[...], preferred_element_type=F32)
    if final:
        y = _rms(y, gfin_ref[...])
    o_ref[...] = y


def outproj_ffn(res, mix0, mix1, w_out, idx_out, g_ffn, w_gu, w_d, layer, g_final=None, *, tm):
    m, d = res.shape
    half = w_out.shape[1] // 2
    (m0, c0), (m1, c1) = mix0, mix1
    final = g_final is not None
    once = pl.Buffered(1)
    in_specs = [
        pl.BlockSpec((tm, d), lambda i: (i, 0)),
        pl.BlockSpec((tm, half), lambda i: (i, c0)),
        pl.BlockSpec((tm, half), lambda i: (i, c1)),
        pl.BlockSpec((None, 2 * half, d), lambda i: (idx_out, 0, 0), pipeline_mode=once),
        pl.BlockSpec((1, d), lambda i: (0, 0)),
        pl.BlockSpec((None, d, 2 * D_FF), lambda i: (layer, 0, 0), pipeline_mode=once),
        pl.BlockSpec((None, D_FF, d), lambda i: (layer, 0, 0), pipeline_mode=once),
    ]
    args = [res, m0, m1, w_out, g_ffn.reshape(1, d), w_gu, w_d]
    if final:
        in_specs.append(pl.BlockSpec((1, d), lambda i: (0, 0)))
        args.append(g_final.reshape(1, d))
    return pl.pallas_call(
        functools.partial(_ffn_body, final=final),
        grid=(m // tm,),
        in_specs=in_specs,
        out_specs=pl.BlockSpec((tm, d), lambda i: (i, 0)),
        out_shape=jax.ShapeDtypeStruct((m, d), F32),
        compiler_params=_cparams(("parallel",)),
        name="outproj_ffn",
    )(*args)


POOL_PAD = 16
POOL_ROWS = 512


def _pool_prompt_body(u_ref, pw_ref, ps_ref, o_ref, ext_ref):
    t_len = u_ref.shape[0]
    ext_ref[0:POOL_PAD, :] = jnp.zeros((POOL_PAD, POOL_WIDTH), F32)
    ext_ref[POOL_PAD:, :] = u_ref[...]
    for r0 in range(0, t_len, POOL_ROWS):
        pos = lax.broadcasted_iota(jnp.int32, (POOL_ROWS, 1), 0) + r0
        for g, w in enumerate(POOL_WINDOWS):
            sl = slice(g * POOL_GROUP, (g + 1) * POOL_GROUP)
            cur = ext_ref[POOL_PAD + r0:POOL_PAD + r0 + POOL_ROWS, sl]
            win = cur
            for i in range(1, w):
                win = win + ext_ref[POOL_PAD + r0 - i:POOL_PAD + r0 - i + POOL_ROWS, sl]
            cnt = jnp.minimum(pos + 1, w).astype(F32)
            dlt = win / cnt - cur
            y = jnp.dot(dlt.astype(BF16), pw_ref[g], preferred_element_type=F32) * ps_ref[:, sl]
            o_ref[r0:r0 + POOL_ROWS, sl] = y.astype(BF16)


def pool_prompt(proj, pool_w, pool_scale):
    b, t, _ = proj.shape
    return pl.pallas_call(
        _pool_prompt_body,
        grid=(b,),
        in_specs=[
            pl.BlockSpec((None, t, POOL_WIDTH), lambda i: (i, 0, 0)),
            pl.BlockSpec((len(POOL_WINDOWS), POOL_GROUP, POOL_GROUP), lambda i: (0, 0, 0)),
            pl.BlockSpec((1, POOL_WIDTH), lambda i: (0, 0)),
        ],
        out_specs=pl.BlockSpec((None, t, POOL_WIDTH), lambda i: (i, 0, 0)),
        out_shape=jax.ShapeDtypeStruct((b, t, POOL_WIDTH), BF16),
        scratch_shapes=[pltpu.VMEM((t + POOL_PAD, POOL_WIDTH), F32)],
        compiler_params=_cparams(("parallel",)),
        name="pool_prompt",
    )(proj, pool_w, pool_scale.reshape(1, POOL_WIDTH))


def _pool_decode_body(st_ref, u_ref, pw_ref, ps_ref, o_ref):
    for g, w in enumerate(POOL_WINDOWS):
        sl = slice(g * POOL_GROUP, (g + 1) * POOL_GROUP)
        cur = u_ref[:, sl]
        win = cur
        for i in range(1, w):
            win = win + st_ref[POOL_BUF - i, :, sl]
        dlt = win / float(w) - cur
        y = jnp.dot(dlt.astype(BF16), pw_ref[g], preferred_element_type=F32) * ps_ref[:, sl]
        o_ref[:, sl] = y.astype(BF16)


def pool_decode(state_t, proj, pool_w, pool_scale):
    nb = proj.shape[0]
    return pl.pallas_call(
        _pool_decode_body,
        grid=(1,),
        in_specs=[
            pl.BlockSpec((POOL_BUF, nb, POOL_WIDTH), lambda i: (0, 0, 0)),
            pl.BlockSpec((nb, POOL_WIDTH), lambda i: (0, 0)),
            pl.BlockSpec((len(POOL_WINDOWS), POOL_GROUP, POOL_GROUP), lambda i: (0, 0, 0)),
            pl.BlockSpec((1, POOL_WIDTH), lambda i: (0, 0)),
        ],
        out_specs=pl.BlockSpec((nb, POOL_WIDTH), lambda i: (0, 0)),
        out_shape=jax.ShapeDtypeStruct((nb, POOL_WIDTH), BF16),
        compiler_params=_cparams(("arbitrary",)),
        name="pool_decode",
    )(state_t, proj, pool_w, pool_scale.reshape(1, POOL_WIDTH))


def _rel_bucket(n):
    max_exact = REL_BUCKETS // 2
    nf = jnp.maximum(n, 1).astype(F32)
    large = max_exact + (jnp.log(nf / max_exact) / math.log(REL_MAX_DIST / max_exact)
                         * (REL_BUCKETS - max_exact)).astype(jnp.int32)
    large = jnp.minimum(large, REL_BUCKETS - 1)
    return jnp.where(n < max_exact, n, large)


def _table_lookup(tab_ref, bucket, h):
    out = jnp.zeros(bucket.shape, F32)
    for b in range(REL_BUCKETS):
        out = jnp.where(bucket == b, tab_ref[b, h], out)
    return out


def _rel_bias_body(tab_ref, bp_ref, bd_ref, bn_ref, *, past):
    ii = lax.broadcasted_iota(jnp.int32, (ATT_TQ, ATT_TK), 0)
    jj = lax.broadcasted_iota(jnp.int32, (ATT_TQ, ATT_TK), 1)
    for h in range(DIFF_HEADS):
        for t in range(3):
            dist = t * ATT_TK + ii - jj
            bias = _table_lookup(tab_ref, _rel_bucket(jnp.maximum(dist, 0)), h)
            bp_ref[h, t] = jnp.where(dist >= 0, bias, NEG_BIG)
    nrow = 2 * DIFF_HEADS
    row = lax.broadcasted_iota(jnp.int32, (nrow, past * DIFF_HEADS), 0)
    col = lax.broadcasted_iota(jnp.int32, (nrow, past * DIFF_HEADS), 1)
    bucket = _rel_bucket(past - col // DIFF_HEADS)
    rown = lax.broadcasted_iota(jnp.int32, (nrow, LANES), 0)
    bd = jnp.full((nrow, past * DIFF_HEADS), NEG_BIG, F32)
    bn = jnp.zeros((nrow, LANES), F32)
    for h in range(DIFF_HEADS):
        own = jnp.where(row // 2 == h, col % DIFF_HEADS, -1) == h
        bd = jnp.where(own, _table_lookup(tab_ref, bucket, h), bd)
        bn = jnp.where(rown // 2 == h, tab_ref[0, h], bn)
    bd_ref[...] = bd
    bn_ref[...] = bn


def rel_bias_tiles(rel_bias, past):
    return pl.pallas_call(
        functools.partial(_rel_bias_body, past=past),
        in_specs=[pl.BlockSpec(memory_space=pltpu.SMEM)],
        out_specs=[
            pl.BlockSpec(memory_space=pltpu.VMEM),
            pl.BlockSpec(memory_space=pltpu.VMEM),
            pl.BlockSpec(memory_space=pltpu.VMEM),
        ],
        out_shape=[
            jax.ShapeDtypeStruct((DIFF_HEADS, 3, ATT_TQ, ATT_TK), F32),
            jax.ShapeDtypeStruct((2 * DIFF_HEADS, past * DIFF_HEADS), F32),
            jax.ShapeDtypeStruct((2 * DIFF_HEADS, LANES), F32),
        ],
        compiler_params=pltpu.CompilerParams(vmem_limit_bytes=VMEM_LIMIT),
        name="rel_bias_tiles",
    )(rel_bias)


def _lambda(lam_ref, lam_init):
    lp = lam_ref[...]
    s1 = jnp.sum(lp[0:1, :] * lp[1:2, :], axis=-1, keepdims=True)
    s2 = jnp.sum(lp[2:3, :] * lp[3:4, :], axis=-1, keepdims=True)
    return jnp.exp(s1) - jnp.exp(s2) + lam_init


def _attn_prompt_body(q_ref, k_ref, v_ref, bias_ref, lam_ref, sub_ref, o_ref,
                      kb_ref, vb_ref, qq_ref, m_ref, acc_ref, *, lam_init):
    nq = k_ref.shape[0] // ATT_TQ
    kb_ref[...] = k_ref[...].astype(BF16)
    vb_ref[:, 0:LANES] = v_ref[...].astype(BF16)
    vb_ref[:, LANES:2 * LANES] = jnp.ones((vb_ref.shape[0], LANES), BF16)
    lane = lax.broadcasted_iota(jnp.int32, (ATT_TQ, LANES), 1)
    for qi in range(nq):
        q = q_ref[qi * ATT_TQ:(qi + 1) * ATT_TQ, :] * (DIFF_HALF ** -0.5)
        qq_ref[qi] = jnp.concatenate([jnp.where(lane < DIFF_HALF, q, 0.0),
                                      jnp.where(lane >= DIFF_HALF, q, 0.0)], axis=0).astype(BF16)
    m_ref[...] = jnp.full(m_ref.shape, -jnp.inf, F32)
    acc_ref[...] = jnp.zeros_like(acc_ref)

    for w in range(nq):
        kj = kb_ref[w * ATT_TK:(w + 1) * ATT_TK, :]
        vj = vb_ref[w * ATT_TK:(w + 1) * ATT_TK, :]
        for qi in range(w, nq):
            bias = bias_ref[min(qi - w, 2)]
            s_all = lax.dot_general(qq_ref[qi], kj, _NT, preferred_element_type=F32)
            for idx in range(2):
                rows = slice(idx * ATT_TQ, (idx + 1) * ATT_TQ)
                s = s_all[rows, :] + bias
                m_prev = m_ref[qi, rows, :]
                m_new = jnp.maximum(m_prev, jnp.max(s, axis=-1, keepdims=True))
                alpha = jnp.exp(m_prev - m_new)
                p = jnp.exp(s - jnp.concatenate([m_new] * (ATT_TK // LANES), axis=1))
                acc_ref[qi, rows, :] = (jnp.concatenate([alpha, alpha], axis=1) * acc_ref[qi, rows, :]
                                        + jnp.dot(p.astype(BF16), vj, preferred_element_type=F32))
                m_ref[qi, rows, :] = m_new

    lam = _lambda(lam_ref, lam_init)
    for qi in range(nq):
        acc = acc_ref[qi]
        o_all = acc[:, 0:LANES] / acc[:, LANES:2 * LANES]
        o = o_all[0:ATT_TQ, :] - lam * o_all[ATT_TQ:2 * ATT_TQ, :]
        o_ref[qi * ATT_TQ:(qi + 1) * ATT_TQ, :] = (_rms(o, sub_ref[...]) * (1.0 - lam_init)).astype(BF16)


def attn_prompt(proj, bias_p, lam_params, subln, lam_init):
    b, t, _ = proj.shape
    nh = DIFF_HEADS
    nq = t // ATT_TQ
    assert ATT_TQ == ATT_TK
    return pl.pallas_call(
        functools.partial(_attn_prompt_body, lam_init=lam_init),
        grid=(b, nh),
        in_specs=[
            pl.BlockSpec((None, t, LANES), lambda i, h: (i, 0, nh + h)),
            pl.BlockSpec((None, t, LANES), lambda i, h: (i, 0, 2 * nh + h)),
            pl.BlockSpec((None, t, LANES), lambda i, h: (i, 0, 3 * nh + h)),
            pl.BlockSpec((None, 3, ATT_TQ, ATT_TK), lambda i, h: (h, 0, 0, 0)),
            pl.BlockSpec((4, DIFF_HALF), lambda i, h: (0, 0)),
            pl.BlockSpec((1, LANES), lambda i, h: (0, 0)),
        ],
        out_specs=pl.BlockSpec((None, t, LANES), lambda i, h: (i, 0, h)),
        out_shape=jax.ShapeDtypeStruct((b, t, DIFF_WIDTH), BF16),
        scratch_shapes=[
            pltpu.VMEM((t, LANES), BF16),
            pltpu.VMEM((t, 2 * LANES), BF16),
            pltpu.VMEM((nq, 2 * ATT_TQ, LANES), BF16),
            pltpu.VMEM((nq, 2 * ATT_TQ, LANES), F32),
            pltpu.VMEM((nq, 2 * ATT_TQ, 2 * LANES), F32),
        ],
        compiler_params=_cparams(("parallel", "parallel")),
        name="attn_prompt",
    )(proj, proj, proj, bias_p, lam_params, subln.reshape(1, LANES))


def _attn_decode_body(pt_ref, x_ref, *rest, n_pages, lam_init):
    k_refs = rest[:n_pages]
    v_refs = rest[n_pages:2 * n_pages]
    bd_ref, bn_ref, lam_ref, sub_ref, o_ref = rest[2 * n_pages:]
    page_rows = PAGE_SIZE * DIFF_HEADS
    lane = lax.broadcasted_iota(jnp.int32, (1, DIFF_HEAD_DIM), 1)
    rows_q, rows_k, rows_v = [], [], []
    for h in range(DIFF_HEADS):
        sl = slice(h * DIFF_HEAD_DIM, (h + 1) * DIFF_HEAD_DIM)
        qh = x_ref[:, DIFF_WIDTH:2 * DIFF_WIDTH][:, sl] * (DIFF_HALF ** -0.5)
        kh = x_ref[:, 2 * DIFF_WIDTH:3 * DIFF_WIDTH][:, sl]
        vh = x_ref[:, 3 * DIFF_WIDTH:4 * DIFF_WIDTH][:, sl]
        rows_q += [jnp.where(lane < DIFF_HALF, qh, 0.0), jnp.where(lane >= DIFF_HALF, qh, 0.0)]
        rows_k += [kh, kh]
        rows_v += [vh, vh]
    q8 = jnp.concatenate(rows_q, axis=0)
    k8 = jnp.concatenate(rows_k, axis=0)
    v8 = jnp.concatenate(rows_v, axis=0)
    q8b = q8.astype(BF16)
    s = jnp.concatenate(
        [lax.dot_general(q8b, k_refs[p][...].astype(BF16), _NT, preferred_element_type=F32)
         for p in range(n_pages)], axis=1) + bd_ref[...]
    s_new = jnp.sum(q8 * k8, axis=-1, keepdims=True) + bn_ref[:, 0:1]
    m = jnp.maximum(jnp.max(s, axis=-1, keepdims=True), s_new)
    p = jnp.exp(s - m)
    p_new = jnp.exp(s_new - m)
    inv_l = 1.0 / (jnp.sum(p, axis=-1, keepdims=True) + p_new)
    a = p * inv_l
    r = (p_new * inv_l) * v8
    for pg in range(n_pages):
        r = r + jnp.dot(a[:, pg * page_rows:(pg + 1) * page_rows].astype(BF16),
                        v_refs[pg][...].astype(BF16), preferred_element_type=F32)
    lam = _lambda(lam_ref, lam_init)
    for h in range(DIFF_HEADS):
        o = r[2 * h:2 * h + 1, :] - lam * r[2 * h + 1:2 * h + 2, :]
        o_ref[:, h * DIFF_HEAD_DIM:(h + 1) * DIFF_HEAD_DIM] = (
            _rms(o, sub_ref[...]) * (1.0 - lam_init)).astype(BF16)


def attn_decode(proj, cache_k, cache_v, page_table, layer_j, bias_d, bias_n, lam_params, subln, lam_init):
    nb = proj.shape[0]
    n_pages = page_table.shape[1]
    n_pool, n_even = cache_k.shape[:2]
    page_rows = PAGE_SIZE * DIFF_HEADS
    ck = cache_k.reshape(n_pool, n_even, page_rows, DIFF_HEAD_DIM)
    cv = cache_v.reshape(n_pool, n_even, page_rows, DIFF_HEAD_DIM)

    def page_spec(p):
        return pl.BlockSpec((None, None, page_rows, DIFF_HEAD_DIM),
                            lambda i, pt: (pt[i * n_pages + p], layer_j, 0, 0))

    full = lambda shape: pl.BlockSpec(shape, lambda i, pt: (0,) * len(shape))
    grid_spec = pltpu.PrefetchScalarGridSpec(
        num_scalar_prefetch=1,
        grid=(nb,),
        in_specs=([pl.BlockSpec((None, 1, EVEN_IN), lambda i, pt: (i, 0, 0))]
                  + [page_spec(p) for p in range(n_pages)] * 2
                  + [full(bias_d.shape), full(bias_n.shape), full((4, DIFF_HALF)), full((1, LANES))]),
        out_specs=pl.BlockSpec((None, 1, DIFF_WIDTH), lambda i, pt: (i, 0, 0)),
    )
    out = pl.pallas_call(
        functools.partial(_attn_decode_body, n_pages=n_pages, lam_init=lam_init),
        grid_spec=grid_spec,
        out_shape=jax.ShapeDtypeStruct((nb, 1, DIFF_WIDTH), BF16),
        compiler_params=_cparams(("arbitrary",)),
        name="attn_decode",
    )(page_table.reshape(-1), proj.reshape(nb, 1, EVEN_IN), *([ck] * n_pages), *([cv] * n_pages),
      bias_d, bias_n, lam_params, subln.reshape(1, LANES))
    return out.reshape(nb, DIFF_WIDTH)


def _blockdiag(x):
    lane = lax.broadcasted_iota(jnp.int32, x.shape, 1)
    zero = jnp.zeros_like(x)
    return jnp.concatenate([jnp.where(lane < GDN_C, x, zero), jnp.where(lane >= GDN_C, x, zero)], axis=0)


def _mmp(a, b):
    return jnp.dot(a.astype(BF16), _blockdiag(b.astype(BF16)), preferred_element_type=F32)


def _unit_lower_inverse(lows, ii, jl):
    eye = jnp.where(ii == jl, 1.0, 0.0)
    in_block = (ii // INV_BLOCK) == (jl // INV_BLOCK)
    ps = [jnp.where(in_block, -low, 0.0) for low in lows]
    offs = [jnp.where(in_block, 0.0, low) for low in lows]
    dinvs = [eye + p for p in ps]
    span = 2
    while span < INV_BLOCK:
        ps = [_mmp(p, p) for p in ps]
        dinvs = [d + _mmp(d, p) for d, p in zip(dinvs, ps)]
        span *= 2
    powers = [[-_mmp(d, off) for d, off in zip(dinvs, offs)]]
    span = 2
    while span < GDN_C // INV_BLOCK:
        powers.append([_mmp(n, n) for n in powers[-1]])
        span *= 2
    xs = dinvs
    for pw in reversed(powers):
        xs = [x + _mmp(n, x) for n, x in zip(pw, xs)]
    return xs


def _gdn_prompt_body(h_ref, gn_ref, w_ref, wgt_ref, cw_ref, alog_ref, dtb_ref, on_ref,
                     o_ref, s_ref, cst_ref, ext_ref, z_ref):
    c = pl.program_id(1)
    nh = GDN_HEADS
    npair = nh // 2
    hk = nh * GDN_DK
    cc = GDN_C
    rows_step = GDN_STEP_CHUNKS * cc
    heads_per_tile = GDN_PROJ_TN // LANES

    @pl.when(c == 0)
    def _():
        s_ref[...] = jnp.zeros_like(s_ref)
        ext_ref[0:8, :] = jnp.zeros((8, GDN_QKV), F32)

    @pl.when(c > 0)
    def _():
        ext_ref[0:8, :] = ext_ref[rows_step:rows_step + 8, :]

    xn = _rms(h_ref[...], gn_ref[...]).astype(BF16)
    abt = lax.dot_general(wgt_ref[...], xn, _NT, preferred_element_type=F32)

    def project(n):
        cols = slice(n * GDN_PROJ_TN, (n + 1) * GDN_PROJ_TN)
        tile = jnp.dot(xn, w_ref[:, cols], preferred_element_type=F32)
        if n < GDN_QKV // GDN_PROJ_TN:
            ext_ref[8:8 + rows_step, cols] = tile
        else:
            z_ref[:, n * GDN_PROJ_TN - GDN_QKV:(n + 1) * GDN_PROJ_TN - GDN_QKV] = tile

    def conv_act(col, r0):
        sl = slice(col, col + LANES)
        acc = ext_ref[8 + r0:8 + r0 + cc, sl] * cw_ref[3:4, sl]
        for i in range(GDN_CONV - 1):
            acc = acc + ext_ref[5 + i + r0:5 + i + r0 + cc, sl] * cw_ref[i:i + 1, sl]
        return acc * _sigmoid(acc)

    def l2n(x):
        return x * lax.rsqrt(jnp.sum(x * x, axis=-1, keepdims=True) + EPS)

    def pair(xs):
        return [jnp.concatenate([xs[2 * p], xs[2 * p + 1]], axis=1) for p in range(npair)]

    def row_pair(x):
        return pair([x[h:h + 1, :] for h in range(nh)])

    project(0)
    lane = lax.broadcasted_iota(jnp.int32, (nh, cc), 1)
    q_p, k_p, v_p, gc_b, beta_b, kdec_b, gc_rows, s_decay = [], [], [], [], [], [], [], []
    for ck in range(GDN_STEP_CHUNKS):
        r0 = ck * cc
        g = -jnp.exp(alog_ref[...]) * _softplus(abt[0:nh, r0:r0 + cc] + dtb_ref[...])
        beta = _sigmoid(abt[nh:2 * nh, r0:r0 + cc])
        gc = g
        shift = 1
        while shift < cc:
            gc = gc + jnp.where(lane >= shift, pltpu.roll(gc, shift, 1), 0.0)
            shift *= 2
        g_last = jnp.broadcast_to(gc[:, cc - 1:cc], (nh, cc))
        s_decay.append(jnp.exp(g_last))
        cols = jnp.transpose(jnp.concatenate([gc, beta, jnp.zeros((LANES - 2 * nh, cc), F32)], axis=0))
        bcast = [jnp.broadcast_to(cols[:, n:n + 1], (cc, cc)) for n in range(2 * nh)]
        gcb = pair(bcast[0:nh])
        gc_b += gcb
        beta_b += pair(bcast[nh:2 * nh])
        kdec_b += [jnp.exp(gl - x) for gl, x in zip(row_pair(g_last), gcb)]
        gc_rows += row_pair(gc)
    gam_b = [jnp.exp(x) for x in gc_b]
    ii = lax.broadcasted_iota(jnp.int32, (cc, 2 * cc), 0)
    jl = lax.broadcasted_iota(jnp.int32, (cc, 2 * cc), 1) % cc
    decay = [jnp.exp(jnp.where(ii >= jl, gb - gr, -jnp.inf)) for gb, gr in zip(gc_b, gc_rows)]

    act = {}
    for n in range(1, GDN_MAIN // GDN_PROJ_TN + 1):
        if n < GDN_MAIN // GDN_PROJ_TN:
            project(n)
        if n <= GDN_QKV // GDN_PROJ_TN:
            for hh in range(heads_per_tile):
                col = (n - 1) * GDN_PROJ_TN + hh * LANES
                for ck in range(GDN_STEP_CHUNKS):
                    a = conv_act(col, ck * cc)
                    if col < hk:
                        a = l2n(a) * (GDN_DK ** -0.5)
                    elif col < 2 * hk:
                        a = l2n(a)
                    act[(col, ck)] = a
    for ck in range(GDN_STEP_CHUNKS):
        q_p += pair([act[(h * GDN_DK, ck)] for h in range(nh)])
        k_p += pair([act[(hk + h * GDN_DK, ck)] for h in range(nh)])
        v_p += pair([act[(2 * hk + h * GDN_DV, ck)] for h in range(nh)])

    gram = [lax.dot_general(jnp.concatenate([q, k], axis=0).astype(BF16), _blockdiag(k.astype(BF16)), _NT,
                            preferred_element_type=F32) for q, k in zip(q_p, k_p)]
    qk = [gm[0:cc, :] * d for gm, d in zip(gram, decay)]
    lows = [jnp.where(ii > jl, b * gm[cc:2 * cc, :] * d, 0.0) for b, gm, d in zip(beta_b, gram, decay)]
    tinv = _unit_lower_inverse(lows, ii, jl)
    w =[_mmp(t, b * gm * k) for t, b, gm, k in zip(tinv, beta_b, gam_b, k_p)]
    u0 = [_mmp(t, b * v) for t, b, v in zip(tinv, beta_b, v_p)]
    wq = [jnp.concatenate([wp, gm * q], axis=0).astype(BF16) for wp, gm, q in zip(w, gam_b, q_p)]
    kd = [(kdb * k).astype(BF16) for kdb, k in zip(kdec_b, k_p)]

    state = [s_ref[h] for h in range(nh)]
    zero = jnp.zeros((GDN_DK, GDN_DV), BF16)
    for ck in range(GDN_STEP_CHUNKS):
        r0 = ck * cc
        ent = range(ck * npair, (ck + 1) * npair)
        s_bd = [jnp.concatenate([jnp.concatenate([state[2 * p].astype(BF16), zero], axis=1),
                                 jnp.concatenate([zero, state[2 * p + 1].astype(BF16)], axis=1)], axis=0)
                for p in range(npair)]
        ws_qs = [jnp.dot(wq[e], sb, preferred_element_type=F32) for e, sb in zip(ent, s_bd)]
        u = [u0[e] - x[0:cc, :] for e, x in zip(ent, ws_qs)]
        o = [x[cc:2 * cc, :] + _mmp(qk[e], b) for e, x, b in zip(ent, ws_qs, u)]
        upd = [lax.dot_general(kd[e], b.astype(BF16), _TN, preferred_element_type=F32)
               for e, b in zip(ent, u)]
        new_state = []
        for h in range(nh):
            half = slice((h % 2) * cc, (h % 2 + 1) * cc)
            new_state.append(s_decay[ck][h:h + 1, :] * state[h] + upd[h // 2][half, half])
            z = z_ref[r0:r0 + cc, h * GDN_DV:(h + 1) * GDN_DV]
            o_ref[r0:r0 + cc, h * GDN_DV:(h + 1) * GDN_DV] = (
                _rms(o[h // 2][:, half], on_ref[...]) * (z * _sigmoid(z))).astype(BF16)
        state = new_state
    for h in range(nh):
        s_ref[h] = state[h]

    @pl.when(c == pl.num_programs(1) - 1)
    def _():
        cst_ref[...] = ext_ref[rows_step:rows_step + 8, :]


def gdn_prompt(h, gain, w_in, w_gates_t, idx, conv_w, a_log, dt_bias, o_norm):
    b, t, d = h.shape
    rows = GDN_STEP_CHUNKS * GDN_C
    nc = t // rows
    nh = GDN_HEADS
    return pl.pallas_call(
        _gdn_prompt_body,
        grid=(b, nc),
        in_specs=[
            pl.BlockSpec((None, rows, d), lambda i, c: (i, c, 0)),
            pl.BlockSpec((1, d), lambda i, c: (0, 0)),
            pl.BlockSpec((None, d, GDN_MAIN), lambda i, c: (idx, 0, 0)),
            pl.BlockSpec((None, LANES, d), lambda i, c: (idx, 0, 0)),
            pl.BlockSpec((GDN_CONV, GDN_QKV), lambda i, c: (0, 0)),
            pl.BlockSpec((nh, 1), lambda i, c: (0, 0)),
            pl.BlockSpec((nh, 1), lambda i, c: (0, 0)),
            pl.BlockSpec((1, GDN_DV), lambda i, c: (0, 0)),
        ],
        out_specs=[
            pl.BlockSpec((None, rows, nh * GDN_DV), lambda i, c: (i, c, 0)),
            pl.BlockSpec((None, nh, GDN_DK, GDN_DV), lambda i, c: (i, 0, 0, 0)),
            pl.BlockSpec((None, 8, GDN_QKV), lambda i, c: (i, 0, 0)),
        ],
        out_shape=[
            jax.ShapeDtypeStruct((b, t, nh * GDN_DV), BF16),
            jax.ShapeDtypeStruct((b, nh, GDN_DK, GDN_DV), F32),
            jax.ShapeDtypeStruct((b, 8, GDN_QKV), F32),
        ],
        scratch_shapes=[pltpu.VMEM((8 + rows, GDN_QKV), F32), pltpu.VMEM((rows, nh * GDN_DV), F32)],
        compiler_params=_cparams(("parallel", "arbitrary")),
        name="gdn_prompt",
    )(h, gain.reshape(1, d), w_in, w_gates_t, conv_w, a_log.reshape(nh, 1), dt_bias.reshape(nh, 1),
      o_norm.reshape(1, GDN_DV))


def _gdn_decode_body(x_ref, cs_ref, ab_ref, cw_ref, alog_ref, dtb_ref, on_ref, s0_ref, o_ref, s_ref):
    nh = GDN_HEADS
    for r in range(GDN_DEC_ROWS):
        x = x_ref[r, 0:3 * nh, :]
        conv = x * cw_ref[GDN_CONV - 1]
        for i in range(GDN_CONV - 1):
            conv = conv + cs_ref[r, i] * cw_ref[i]
        act = conv * _sigmoid(conv)
        qa, ka, v8 = act[0:nh], act[nh:2 * nh], act[2 * nh:3 * nh]
        q8 = qa * lax.rsqrt(jnp.sum(qa * qa, axis=-1, keepdims=True) + EPS) * (GDN_DK ** -0.5)
        k8 = ka * lax.rsqrt(jnp.sum(ka * ka, axis=-1, keepdims=True) + EPS)
        g = -jnp.exp(alog_ref[...]) * _softplus(ab_ref[r, 0:nh, :] + dtb_ref[...])
        gam8 = jnp.broadcast_to(jnp.exp(g), (nh, GDN_DV))
        beta8 = jnp.broadcast_to(_sigmoid(ab_ref[r, nh:2 * nh, :]), (nh, GDN_DV))
        qk8 = jnp.broadcast_to(jnp.sum(q8 * k8, axis=-1, keepdims=True), (nh, GDN_DV))
        cols = jnp.transpose(jnp.concatenate([q8, k8, jnp.zeros((LANES - 2 * nh, GDN_DK), F32)], axis=0))
        outs = []
        for h in range(nh):
            s_old = s0_ref[r, h]
            qcol = cols[:, h:h + 1]
            kcol = cols[:, nh + h:nh + h + 1]
            k_s = jnp.sum(kcol * s_old, axis=0, keepdims=True)
            q_s = jnp.sum(qcol * s_old, axis=0, keepdims=True)
            gam = gam8[h:h + 1, :]
            u = beta8[h:h + 1, :] * (v8[h:h + 1, :] - gam * k_s)
            outs.append(gam * q_s + qk8[h:h + 1, :] * u)
            s_ref[r, h] = gam * s_old + kcol * u
        o8 = jnp.concatenate(outs, axis=0)
        z8 = x_ref[r, 3 * nh:4 * nh, :]
        o_ref[r] = (_rms(o8, on_ref[...]) * (z8 * _sigmoid(z8))).astype(BF16)


def gdn_decode(proj, gates, conv_state, s0_all, layer_j, conv_w, a_log, dt_bias, o_norm):
    nb = proj.shape[0]
    nh = GDN_HEADS
    nrow = GDN_QKV // LANES
    o, s = pl.pallas_call(
        _gdn_decode_body,
        grid=(nb // GDN_DEC_ROWS,),
        in_specs=[
            pl.BlockSpec((GDN_DEC_ROWS, GDN_MAIN // LANES, LANES), lambda i: (i, 0, 0)),
            pl.BlockSpec((GDN_DEC_ROWS, GDN_CONV - 1, nrow, LANES), lambda i: (i, 0, 0, 0)),
            pl.BlockSpec((GDN_DEC_ROWS, 2 * nh, 1), lambda i: (i, 0, 0)),
            pl.BlockSpec((GDN_CONV, nrow, LANES), lambda i: (0, 0, 0)),
            pl.BlockSpec((nh, 1), lambda i: (0, 0)),
            pl.BlockSpec((nh, 1), lambda i: (0, 0)),
            pl.BlockSpec((1, GDN_DV), lambda i: (0, 0)),
            pl.BlockSpec((None, GDN_DEC_ROWS, nh, GDN_DK, GDN_DV), lambda i: (layer_j, i, 0, 0, 0)),
        ],
        out_specs=[
            pl.BlockSpec((GDN_DEC_ROWS, nh, GDN_DV), lambda i: (i, 0, 0)),
            pl.BlockSpec((GDN_DEC_ROWS, nh, GDN_DK, GDN_DV), lambda i: (i, 0, 0, 0)),
        ],
        out_shape=[
            jax.ShapeDtypeStruct((nb, nh, GDN_DV), BF16),
            jax.ShapeDtypeStruct((nb, nh, GDN_DK, GDN_DV), F32),
        ],
        compiler_params=_cparams(("parallel",)),
        name="gdn_decode",
    )(proj.reshape(nb, GDN_MAIN // LANES, LANES),
      conv_state.reshape(nb, GDN_CONV - 1, nrow, LANES),
      gates.reshape(nb, 2 * nh, 1),
      conv_w.reshape(GDN_CONV, nrow, LANES),
      a_log.reshape(nh, 1), dt_bias.reshape(nh, 1), o_norm.reshape(1, GDN_DV), s0_all)
    return o.reshape(nb, nh * GDN_DV), s


def kernel(x_prompt, x_sample, cache_k, cache_v, page_table, state_pool, state_conv, state_delta,
           norm_mix, norm_ffn, norm_final, rel_bias,
           w_in_even, pool_w, pool_scale, lambda_q1, lambda_k1, lambda_q2, lambda_k2, subln_w, w_out_even,
           w_in_odd, conv_w, a_log, dt_bias, o_norm, w_out_odd,
           w_gate_up, w_down):
    bp, t, d = x_prompt.shape
    bs = x_sample.shape[0]
    mp = bp * t
    nh = GDN_HEADS
    past = page_table.shape[1] * PAGE_SIZE
    tm_p = 1024
    tm_s = bs

    hp = x_prompt.reshape(mp, d)
    hs = x_sample.reshape(bs, d)
    bias_p, bias_d, bias_n = rel_bias_tiles(rel_bias, past)

    w_in_e = w_in_even.astype(BF16)
    w_in_o = w_in_odd[:, :, 0:GDN_MAIN].astype(BF16)
    w_gates_t = jnp.zeros((w_in_odd.shape[0], LANES, d), F32).at[:, 0:2 * nh, :].set(
        jnp.transpose(w_in_odd[:, :, GDN_MAIN:], (0, 2, 1))).astype(BF16)
    w_out_e = w_out_even.astype(BF16)
    w_out_o = w_out_odd.astype(BF16)
    w_gu = w_gate_up.astype(BF16)
    w_d = w_down.astype(BF16)
    pool_wb = pool_w.astype(BF16)

    k_p, v_p, k_s, v_s, pool_p, pool_s = [], [], [], [], [], []
    conv_p, conv_s, delta_p, delta_s = [], [], [], []
    for layer in range(DEPTH):
        j = layer // 2
        last = layer == DEPTH - 1
        if layer % 2 == 0:
            w_out = w_out_e
            pw = pool_wb[j]
            lam_init = 0.8 - 0.6 * math.exp(-0.3 * layer)
            lam_params = jnp.stack([lambda_q1[j], lambda_k1[j], lambda_q2[j], lambda_k2[j]])

            proj_p, kp, vp = norm_matmul(hp, norm_mix[layer], w_in_e, j, tm=tm_p, tn=DIFF_WIDTH)
            proj_s, ks_, vs_ = norm_matmul(hs, norm_mix[layer], w_in_e, j, tm=tm_s, tn=DIFF_WIDTH)
            proj_p = proj_p.reshape(bp, t, EVEN_IN)

            ypool_p = pool_prompt(proj_p, pw, pool_scale[j])
            oatt_p = attn_prompt(proj_p, bias_p, lam_params, subln_w[j], lam_init)
            ypool_s = pool_decode(jnp.transpose(state_pool[j], (1, 0, 2)), proj_s, pw, pool_scale[j])
            oatt_s = attn_decode(proj_s, cache_k, cache_v, page_table, j, bias_d, bias_n,
                                 lam_params, subln_w[j], lam_init)

            kv_shape = (DIFF_HEADS, DIFF_HEAD_DIM)
            k_p.append(kp.reshape(bp, t, *kv_shape))
            v_p.append(vp.reshape(bp, t, *kv_shape))
            k_s.append(ks_.reshape(bs, 1, *kv_shape))
            v_s.append(vs_.reshape(bs, 1, *kv_shape))
            pool_p.append(proj_p[:, t - POOL_BUF:, 0:POOL_WIDTH])
            pool_s.append(jnp.concatenate([state_pool[j][:, 1:], proj_s[:, None, 0:POOL_WIDTH]], axis=1))

            mix_p = ((ypool_p.reshape(mp, POOL_WIDTH), 0), (oatt_p.reshape(mp, DIFF_WIDTH), 0))
            mix_s = ((ypool_s, 0), (oatt_s, 0))
        else:
            w_out = w_out_o

            proj_s, gates_s = norm_matmul(hs, norm_mix[layer], w_in_o, j, w_gates_t, tm=tm_s, tn=512)

            o_p, s_p, tail_p = gdn_prompt(hp.reshape(bp, t, d), norm_mix[layer], w_in_o, w_gates_t, j,
                                          conv_w[j], a_log[j], dt_bias[j], o_norm[j])
            o_s, s_s = gdn_decode(proj_s, gates_s[0:2 * nh].T, state_conv[j], state_delta, j,
                                  conv_w[j], a_log[j], dt_bias[j], o_norm[j])

            conv_p.append(tail_p[:, 8 - (GDN_CONV - 1):, :])
            conv_s.append(jnp.concatenate([state_conv[j][:, 1:], proj_s[:, None, 0:GDN_QKV]], axis=1))
            delta_p.append(s_p)
            delta_s.append(s_s)

            o_p = o_p.reshape(mp, nh * GDN_DV)
            mix_p = ((o_p, 0), (o_p, 1))
            mix_s = ((o_s, 0), (o_s, 1))

        g_fin = norm_final if last else None
        hp = outproj_ffn(hp, mix_p[0], mix_p[1], w_out, j, norm_ffn[layer], w_gu, w_d, layer, g_fin, tm=FFN_TM)
        hs = outproj_ffn(hs, mix_s[0], mix_s[1], w_out, j, norm_ffn[layer], w_gu, w_d, layer, g_fin, tm=tm_s)

    return (hp.reshape(bp, t, d), hs.reshape(bs, 1, d),
            jnp.stack(k_p, axis=1), jnp.stack(v_p, axis=1), jnp.stack(k_s, axis=1), jnp.stack(v_s, axis=1),
            jnp.stack(pool_p), jnp.stack(pool_s), jnp.stack(conv_p), jnp.stack(conv_s),
            jnp.stack(delta_p), jnp.stack(delta_s))
```

```python
import functools
import math

import jax
import jax.numpy as jnp
from jax import lax
from jax.experimental import pallas as pl
from jax.experimental.pallas import tpu as pltpu

F32 = jnp.float32
BF16 = jnp.bfloat16

D_MODEL = 1024
DEPTH = 4
PAGE_SIZE = 128
POOL_WIDTH = 512
POOL_WINDOWS = (2, 4, 8, 16)
POOL_GROUP = 128
POOL_BUF = 15
DIFF_HEADS = 4
DIFF_HALF = 64
DIFF_HEAD_DIM = 128
DIFF_WIDTH = 512
EVEN_IN = 2048
REL_BUCKETS = 32
REL_MAX_DIST = 128
GDN_HEADS = 8
GDN_DK = 128
GDN_DV = 128
GDN_CONV = 4
GDN_QKV = 3072
GDN_MAIN = 4096
D_FF = 2816
EPS = 1e-6

LANES = 128
VMEM_LIMIT = 48 * 1024 * 1024
NEG_BIG = -1e30

ATT_TQ = 512
ATT_TK = 512
GDN_C = 128
GDN_STEP_CHUNKS = 2
GDN_DEC_ROWS = 4
GDN_PROJ_TN = 512
INV_BLOCK = 16
FFN_TF = 256
FFN_TM = 512

_NT = (((1,), (1,)), ((), ()))
_TN = (((0,), (0,)), ((), ()))


def _cparams(sem):
    return pltpu.CompilerParams(dimension_semantics=sem, vmem_limit_bytes=VMEM_LIMIT)


def _sigmoid(x):
    return 1.0 / (1.0 + jnp.exp(-x))


def _silu(x):
    h = 0.5 * x
    return h + h * jnp.tanh(h)


def _softplus(x):
    return jnp.maximum(x, 0.0) + jnp.log1p(jnp.exp(-jnp.abs(x)))


def _rms(x, gain):
    return x * lax.rsqrt(jnp.mean(x * x, axis=-1, keepdims=True) + EPS) * gain


def _mm(a, b):
    return jnp.dot(a.astype(BF16), b.astype(BF16), preferred_element_type=F32)


def _norm_mm_kv_body(x_ref, g_ref, w_ref, o_ref, k_ref, v_ref, xn_ref):
    j = pl.program_id(1)

    @pl.when(j == 0)
    def _():
        xn_ref[...] = _rms(x_ref[...], g_ref[...]).astype(BF16)

    res = jnp.dot(xn_ref[...], w_ref[...], preferred_element_type=F32)
    o_ref[...] = res

    def head_rows(dst_ref):
        for h in range(DIFF_HEADS):
            dst_ref[pl.ds(h, res.shape[0], stride=DIFF_HEADS), :] = res[:, h * DIFF_HEAD_DIM:(h + 1) * DIFF_HEAD_DIM]

    @pl.when(j == 2)
    def _():
        head_rows(k_ref)

    @pl.when(j == 3)
    def _():
        head_rows(v_ref)


def _norm_mm_gate_body(x_ref, g_ref, w_ref, wst_ref, o_ref, ost_ref, xn_ref):
    @pl.when(pl.program_id(1) == 0)
    def _():
        xn = _rms(x_ref[...], g_ref[...]).astype(BF16)
        xn_ref[...] = xn
        ost_ref[...] = lax.dot_general(wst_ref[...], xn, _NT, preferred_element_type=F32)

    o_ref[...] = jnp.dot(xn_ref[...], w_ref[...], preferred_element_type=F32)


def norm_matmul(x, gain, w, idx, w_gates_t=None, *, tm, tn):
    m, d = x.shape
    n = w.shape[2]
    grid = (m // tm, n // tn)
    in_specs = [
        pl.BlockSpec((tm, d), lambda i, j: (i, 0)),
        pl.BlockSpec((1, d), lambda i, j: (0, 0)),
        pl.BlockSpec((None, d, tn), lambda i, j: (idx, 0, j)),
    ]
    out_specs = pl.BlockSpec((tm, tn), lambda i, j: (i, j))
    out_shape = jax.ShapeDtypeStruct((m, n), F32)
    args = [x, gain.reshape(1, d), w]
    if w_gates_t is None:
        assert tn == DIFF_WIDTH and n == EVEN_IN
        kv_spec = pl.BlockSpec((tm * DIFF_HEADS, DIFF_HEAD_DIM), lambda i, j: (i, 0))
        kv_shape = jax.ShapeDtypeStruct((m * DIFF_HEADS, DIFF_HEAD_DIM), F32)
        out_specs = [out_specs, kv_spec, kv_spec]
        out_shape = [out_shape, kv_shape, kv_shape]
        body = _norm_mm_kv_body
    else:
        in_specs.append(pl.BlockSpec((None, LANES, d), lambda i, j: (idx, 0, 0)))
        out_specs = [out_specs, pl.BlockSpec((LANES, tm), lambda i, j: (0, i))]
        out_shape = [out_shape, jax.ShapeDtypeStruct((LANES, m), F32)]
        args.append(w_gates_t)
        body = _norm_mm_gate_body
    return pl.pallas_call(
        body,
        grid=grid,
        in_specs=in_specs,
        out_specs=out_specs,
        out_shape=out_shape,
        scratch_shapes=[pltpu.VMEM((tm, d), BF16)],
        compiler_params=_cparams(("parallel", "arbitrary")),
        name="norm_matmul",
    )(*args)


def _ffn_body(*refs, final):
    if final:
        res_ref, a0_ref, a1_ref, wo_ref, gf_ref, wgu_ref, wd_ref, gfin_ref, o_ref = refs
    else:
        res_ref, a0_ref, a1_ref, wo_ref, gf_ref, wgu_ref, wd_ref, o_ref = refs
    mixed = jnp.concatenate([a0_ref[...], a1_ref[...]], axis=1)
    h1 = res_ref[...] + jnp.dot(mixed, wo_ref[...], preferred_element_type=F32)
    xn = _rms(h1, gf_ref[...]).astype(BF16)
    acts = []
    for f in range(D_FF // FFN_TF):
        g = jnp.dot(xn, wgu_ref[:, f * FFN_TF:(f + 1) * FFN_TF], preferred_element_type=F32)
        u = jnp.dot(xn, wgu_ref[:, D_FF + f * FFN_TF:D_FF + (f + 1) * FFN_TF], preferred_element_type=F32)
        acts.append((g * _sigmoid(g) * u).astype(BF16))
    y = h1 + jnp.dot(jnp.concatenate(acts, axis=1), wd_ref[...], preferred_element_type=F32)
    if final:
        y = _rms(y, gfin_ref[...])
    o_ref[...] = y


def outproj_ffn(res, mix0, mix1, w_out, idx_out, g_ffn, w_gu, w_d, layer, g_final=None, *, tm):
    m, d = res.shape
    half = w_out.shape[1] // 2
    (m0, c0), (m1, c1) = mix0, mix1
    final = g_final is not None
    once = pl.Buffered(1)
    in_specs = [
        pl.BlockSpec((tm, d), lambda i: (i, 0)),
        pl.BlockSpec((tm, half), lambda i: (i, c0)),
        pl.BlockSpec((tm, half), lambda i: (i, c1)),
        pl.BlockSpec((None, 2 * half, d), lambda i: (idx_out, 0, 0), pipeline_mode=once),
        pl.BlockSpec((1, d), lambda i: (0, 0)),
        pl.BlockSpec((None, d, 2 * D_FF), lambda i: (layer, 0, 0), pipeline_mode=once),
        pl.BlockSpec((None, D_FF, d), lambda i: (layer, 0, 0), pipeline_mode=once),
    ]
    args = [res, m0, m1, w_out, g_ffn.reshape(1, d), w_gu, w_d]
    if final:
        in_specs.append(pl.BlockSpec((1, d), lambda i: (0, 0)))
        args.append(g_final.reshape(1, d))
    return pl.pallas_call(
        functools.partial(_ffn_body, final=final),
        grid=(m // tm,),
        in_specs=in_specs,
        out_specs=pl.BlockSpec((tm, d), lambda i: (i, 0)),
        out_shape=jax.ShapeDtypeStruct((m, d), F32),
        compiler_params=_cparams(("parallel",)),
        name="outproj_ffn",
    )(*args)


POOL_PAD = 16
POOL_ROWS = 512


def _pool_prompt_body(u_ref, pw_ref, ps_ref, o_ref, ext_ref):
    t_len = u_ref.shape[0]
    ext_ref[0:POOL_PAD, :] = jnp.zeros((POOL_PAD, POOL_WIDTH), F32)
    ext_ref[POOL_PAD:, :] = u_ref[...]
    for r0 in range(0, t_len, POOL_ROWS):
        pos = lax.broadcasted_iota(jnp.int32, (POOL_ROWS, 1), 0) + r0
        for g, w in enumerate(POOL_WINDOWS):
            sl = slice(g * POOL_GROUP, (g + 1) * POOL_GROUP)
            cur = ext_ref[POOL_PAD + r0:POOL_PAD + r0 + POOL_ROWS, sl]
            win = cur
            for i in range(1, w):
                win = win + ext_ref[POOL_PAD + r0 - i:POOL_PAD + r0 - i + POOL_ROWS, sl]
            cnt = jnp.minimum(pos + 1, w).astype(F32)
            dlt = win / cnt - cur
            y = jnp.dot(dlt.astype(BF16), pw_ref[g], preferred_element_type=F32) * ps_ref[:, sl]
            o_ref[r0:r0 + POOL_ROWS, sl] = y.astype(BF16)


def pool_prompt(proj, pool_w, pool_scale):
    b, t, _ = proj.shape
    return pl.pallas_call(
        _pool_prompt_body,
        grid=(b,),
        in_specs=[
            pl.BlockSpec((None, t, POOL_WIDTH), lambda i: (i, 0, 0)),
            pl.BlockSpec((len(POOL_WINDOWS), POOL_GROUP, POOL_GROUP), lambda i: (0, 0, 0)),
            pl.BlockSpec((1, POOL_WIDTH), lambda i: (0, 0)),
        ],
        out_specs=pl.BlockSpec((None, t, POOL_WIDTH), lambda i: (i, 0, 0)),
        out_shape=jax.ShapeDtypeStruct((b, t, POOL_WIDTH), BF16),
        scratch_shapes=[pltpu.VMEM((t + POOL_PAD, POOL_WIDTH), F32)],
        compiler_params=_cparams(("parallel",)),
        name="pool_prompt",
    )(proj, pool_w, pool_scale.reshape(1, POOL_WIDTH))


def _pool_decode_body(st_ref, u_ref, pw_ref, ps_ref, o_ref):
    for g, w in enumerate(POOL_WINDOWS):
        sl = slice(g * POOL_GROUP, (g + 1) * POOL_GROUP)
        cur = u_ref[:, sl]
        win = cur
        for i in range(1, w):
            win = win + st_ref[POOL_BUF - i, :, sl]
        dlt = win / float(w) - cur
        y = jnp.dot(dlt.astype(BF16), pw_ref[g], preferred_element_type=F32) * ps_ref[:, sl]
        o_ref[:, sl] = y.astype(BF16)


def pool_decode(state_t, proj, pool_w, pool_scale):
    nb = proj.shape[0]
    return pl.pallas_call(
        _pool_decode_body,
        grid=(1,),
        in_specs=[
            pl.BlockSpec((POOL_BUF, nb, POOL_WIDTH), lambda i: (0, 0, 0)),
            pl.BlockSpec((nb, POOL_WIDTH), lambda i: (0, 0)),
            pl.BlockSpec((len(POOL_WINDOWS), POOL_GROUP, POOL_GROUP), lambda i: (0, 0, 0)),
            pl.BlockSpec((1, POOL_WIDTH), lambda i: (0, 0)),
        ],
        out_specs=pl.BlockSpec((nb, POOL_WIDTH), lambda i: (0, 0)),
        out_shape=jax.ShapeDtypeStruct((nb, POOL_WIDTH), BF16),
        compiler_params=_cparams(("arbitrary",)),
        name="pool_decode",
    )(state_t, proj, pool_w, pool_scale.reshape(1, POOL_WIDTH))


def _rel_bucket(n):
    max_exact = REL_BUCKETS // 2
    nf = jnp.maximum(n, 1).astype(F32)
    large = max_exact + (jnp.log(nf / max_exact) / math.log(REL_MAX_DIST / max_exact)
                         * (REL_BUCKETS - max_exact)).astype(jnp.int32)
    large = jnp.minimum(large, REL_BUCKETS - 1)
    return jnp.where(n < max_exact, n, large)


def _table_lookup(tab_ref, bucket, h):
    out = jnp.zeros(bucket.shape, F32)
    for b in range(REL_BUCKETS):
        out = jnp.where(bucket == b, tab_ref[b, h], out)
    return out


def _rel_bias_body(tab_ref, bp_ref, bd_ref, bn_ref, *, past):
    ii = lax.broadcasted_iota(jnp.int32, (ATT_TQ, ATT_TK), 0)
    jj = lax.broadcasted_iota(jnp.int32, (ATT_TQ, ATT_TK), 1)
    for h in range(DIFF_HEADS):
        for t in range(3):
            dist = t * ATT_TK + ii - jj
            bias = _table_lookup(tab_ref, _rel_bucket(jnp.maximum(dist, 0)), h)
            bp_ref[h, t] = jnp.where(dist >= 0, bias, NEG_BIG)
    nrow = 2 * DIFF_HEADS
    row = lax.broadcasted_iota(jnp.int32, (nrow, past * DIFF_HEADS), 0)
    col = lax.broadcasted_iota(jnp.int32, (nrow, past * DIFF_HEADS), 1)
    bucket = _rel_bucket(past - col // DIFF_HEADS)
    rown = lax.broadcasted_iota(jnp.int32, (nrow, LANES), 0)
    bd = jnp.full((nrow, past * DIFF_HEADS), NEG_BIG, F32)
    bn = jnp.zeros((nrow, LANES), F32)
    for h in range(DIFF_HEADS):
        own = jnp.where(row // 2 == h, col % DIFF_HEADS, -1) == h
        bd = jnp.where(own, _table_lookup(tab_ref, bucket, h), bd)
        bn = jnp.where(rown // 2 == h, tab_ref[0, h], bn)
    bd_ref[...] = bd
    bn_ref[...] = bn


def rel_bias_tiles(rel_bias, past):
    return pl.pallas_call(
        functools.partial(_rel_bias_body, past=past),
        in_specs=[pl.BlockSpec(memory_space=pltpu.SMEM)],
        out_specs=[
            pl.BlockSpec(memory_space=pltpu.VMEM),
            pl.BlockSpec(memory_space=pltpu.VMEM),
            pl.BlockSpec(memory_space=pltpu.VMEM),
        ],
        out_shape=[
            jax.ShapeDtypeStruct((DIFF_HEADS, 3, ATT_TQ, ATT_TK), F32),
            jax.ShapeDtypeStruct((2 * DIFF_HEADS, past * DIFF_HEADS), F32),
            jax.ShapeDtypeStruct((2 * DIFF_HEADS, LANES), F32),
        ],
        compiler_params=pltpu.CompilerParams(vmem_limit_bytes=VMEM_LIMIT),
        name="rel_bias_tiles",
    )(rel_bias)


def _lambda(lam_ref, lam_init):
    lp = lam_ref[...]
    s1 = jnp.sum(lp[0:1, :] * lp[1:2, :], axis=-1, keepdims=True)
    s2 = jnp.sum(lp[2:3, :] * lp[3:4, :], axis=-1, keepdims=True)
    return jnp.exp(s1) - jnp.exp(s2) + lam_init


def _attn_prompt_body(q_ref, k_ref, v_ref, bias_ref, lam_ref, sub_ref, o_ref,
                      kb_ref, vb_ref, qq_ref, m_ref, acc_ref, *, lam_init):
    nq = k_ref.shape[0] // ATT_TQ
    kb_ref[...] = k_ref[...].astype(BF16)
    vb_ref[:, 0:LANES] = v_ref[...].astype(BF16)
    vb_ref[:, LANES:2 * LANES] = jnp.ones((vb_ref.shape[0], LANES), BF16)
    lane = lax.broadcasted_iota(jnp.int32, (ATT_TQ, LANES), 1)
    for qi in range(nq):
        q = q_ref[qi * ATT_TQ:(qi + 1) * ATT_TQ, :] * (DIFF_HALF ** -0.5)
        qq_ref[qi] = jnp.concatenate([jnp.where(lane < DIFF_HALF, q, 0.0),
                                      jnp.where(lane >= DIFF_HALF, q, 0.0)], axis=0).astype(BF16)
    m_ref[...] = jnp.full(m_ref.shape, -jnp.inf, F32)
    acc_ref[...] = jnp.zeros_like(acc_ref)

    half = ATT_TQ // 2
    for w in range(nq):
        for qi in range(w, nq):
            tile = min(qi - w, 2)
            parts = [(0, half, half), (half, half, ATT_TK)] if tile == 0 else [(0, ATT_TQ, ATT_TK)]
            for idx in range(2):
                for r0, nr, nk in parts:
                    rows = slice(idx * ATT_TQ + r0, idx * ATT_TQ + r0 + nr)
                    kj = kb_ref[w * ATT_TK:w * ATT_TK + nk, :]
                    vj = vb_ref[w * ATT_TK:w * ATT_TK + nk, :]
                    s = (lax.dot_general(qq_ref[qi, rows, :], kj, _NT, preferred_element_type=F32)
                         + bias_ref[tile, r0:r0 + nr, 0:nk])
                    m_prev = m_ref[qi, rows, :]
                    m_new = jnp.maximum(m_prev, jnp.max(s, axis=-1, keepdims=True))
                    alpha = jnp.exp(m_prev - m_new)
                    p = jnp.exp(s - jnp.concatenate([m_new] * (nk // LANES), axis=1))
                    acc_ref[qi, rows, :] = (jnp.concatenate([alpha, alpha], axis=1) * acc_ref[qi, rows, :]
                                            + jnp.dot(p.astype(BF16), vj, preferred_element_type=F32))
                    m_ref[qi, rows, :] = m_new

    lam = _lambda(lam_ref, lam_init)
    for qi in range(nq):
        acc = acc_ref[qi]
        o_all = acc[:, 0:LANES] / acc[:, LANES:2 * LANES]
        o = o_all[0:ATT_TQ, :] - lam * o_all[ATT_TQ:2 * ATT_TQ, :]
        o_ref[qi * ATT_TQ:(qi + 1) * ATT_TQ, :] = (_rms(o, sub_ref[...]) * (1.0 - lam_init)).astype(BF16)


def attn_prompt(proj, bias_p, lam_params, subln, lam_init):
    b, t, _ = proj.shape
    nh = DIFF_HEADS
    nq = t // ATT_TQ
    assert ATT_TQ == ATT_TK
    return pl.pallas_call(
        functools.partial(_attn_prompt_body, lam_init=lam_init),
        grid=(b, nh),
        in_specs=[
            pl.BlockSpec((None, t, LANES), lambda i, h: (i, 0, nh + h)),
            pl.BlockSpec((None, t, LANES), lambda i, h: (i, 0, 2 * nh + h)),
            pl.BlockSpec((None, t, LANES), lambda i, h: (i, 0, 3 * nh + h)),
            pl.BlockSpec((None, 3, ATT_TQ, ATT_TK), lambda i, h: (h, 0, 0, 0)),
            pl.BlockSpec((4, DIFF_HALF), lambda i, h: (0, 0)),
            pl.BlockSpec((1, LANES), lambda i, h: (0, 0)),
        ],
        out_specs=pl.BlockSpec((None, t, LANES), lambda i, h: (i, 0, h)),
        out_shape=jax.ShapeDtypeStruct((b, t, DIFF_WIDTH), BF16),
        scratch_shapes=[
            pltpu.VMEM((t, LANES), BF16),
            pltpu.VMEM((t, 2 * LANES), BF16),
            pltpu.VMEM((nq, 2 * ATT_TQ, LANES), BF16),
            pltpu.VMEM((nq, 2 * ATT_TQ, LANES), F32),
            pltpu.VMEM((nq, 2 * ATT_TQ, 2 * LANES), F32),
        ],
        compiler_params=_cparams(("parallel", "parallel")),
        name="attn_prompt",
    )(proj, proj, proj, bias_p, lam_params, subln.reshape(1, LANES))


def _attn_decode_body(pt_ref, x_ref, *rest, n_pages, lam_init):
    k_refs = rest[:n_pages]
    v_refs = rest[n_pages:2 * n_pages]
    bd_ref, bn_ref, lam_ref, sub_ref, o_ref = rest[2 * n_pages:]
    page_rows = PAGE_SIZE * DIFF_HEADS
    lane = lax.broadcasted_iota(jnp.int32, (1, DIFF_HEAD_DIM), 1)
    rows_q, rows_k, rows_v = [], [], []
    for h in range(DIFF_HEADS):
        sl = slice(h * DIFF_HEAD_DIM, (h + 1) * DIFF_HEAD_DIM)
        qh = x_ref[:, DIFF_WIDTH:2 * DIFF_WIDTH][:, sl] * (DIFF_HALF ** -0.5)
        kh = x_ref[:, 2 * DIFF_WIDTH:3 * DIFF_WIDTH][:, sl]
        vh = x_ref[:, 3 * DIFF_WIDTH:4 * DIFF_WIDTH][:, sl]
        rows_q += [jnp.where(lane < DIFF_HALF, qh, 0.0), jnp.where(lane >= DIFF_HALF, qh, 0.0)]
        rows_k += [kh, kh]
        rows_v += [vh, vh]
    q8 = jnp.concatenate(rows_q, axis=0)
    k8 = jnp.concatenate(rows_k, axis=0)
    v8 = jnp.concatenate(rows_v, axis=0)
    q8b = q8.astype(BF16)
    s = jnp.concatenate(
        [lax.dot_general(q8b, k_refs[p][...].astype(BF16), _NT, preferred_element_type=F32)
         for p in range(n_pages)], axis=1) + bd_ref[...]
    s_new = jnp.sum(q8 * k8, axis=-1, keepdims=True) + bn_ref[:, 0:1]
    m = jnp.maximum(jnp.max(s, axis=-1, keepdims=True), s_new)
    p = jnp.exp(s - m)
    p_new = jnp.exp(s_new - m)
    inv_l = 1.0 / (jnp.sum(p, axis=-1, keepdims=True) + p_new)
    a = p * inv_l
    r = (p_new * inv_l) * v8
    for pg in range(n_pages):
        r = r + jnp.dot(a[:, pg * page_rows:(pg + 1) * page_rows].astype(BF16),
                        v_refs[pg][...].astype(BF16), preferred_element_type=F32)
    lam = _lambda(lam_ref, lam_init)
    for h in range(DIFF_HEADS):
        o = r[2 * h:2 * h + 1, :] - lam * r[2 * h + 1:2 * h + 2, :]
        o_ref[:, h * DIFF_HEAD_DIM:(h + 1) * DIFF_HEAD_DIM] = (
            _rms(o, sub_ref[...]) * (1.0 - lam_init)).astype(BF16)


def attn_decode(proj, cache_k, cache_v, page_table, layer_j, bias_d, bias_n, lam_params, subln, lam_init):
    nb = proj.shape[0]
    n_pages = page_table.shape[1]
    n_pool, n_even = cache_k.shape[:2]
    page_rows = PAGE_SIZE * DIFF_HEADS
    ck = cache_k.reshape(n_pool, n_even, page_rows, DIFF_HEAD_DIM)
    cv = cache_v.reshape(n_pool, n_even, page_rows, DIFF_HEAD_DIM)

    def page_spec(p):
        return pl.BlockSpec((None, None, page_rows, DIFF_HEAD_DIM),
                            lambda i, pt: (pt[i * n_pages + p], layer_j, 0, 0))

    full = lambda shape: pl.BlockSpec(shape, lambda i, pt: (0,) * len(shape))
    grid_spec = pltpu.PrefetchScalarGridSpec(
        num_scalar_prefetch=1,
        grid=(nb,),
        in_specs=([pl.BlockSpec((None, 1, EVEN_IN), lambda i, pt: (i, 0, 0))]
                  + [page_spec(p) for p in range(n_pages)] * 2
                  + [full(bias_d.shape), full(bias_n.shape), full((4, DIFF_HALF)), full((1, LANES))]),
        out_specs=pl.BlockSpec((None, 1, DIFF_WIDTH), lambda i, pt: (i, 0, 0)),
    )
    out = pl.pallas_call(
        functools.partial(_attn_decode_body, n_pages=n_pages, lam_init=lam_init),
        grid_spec=grid_spec,
        out_shape=jax.ShapeDtypeStruct((nb, 1, DIFF_WIDTH), BF16),
        compiler_params=_cparams(("arbitrary",)),
        name="attn_decode",
    )(page_table.reshape(-1), proj.reshape(nb, 1, EVEN_IN), *([ck] * n_pages), *([cv] * n_pages),
      bias_d, bias_n, lam_params, subln.reshape(1, LANES))
    return out.reshape(nb, DIFF_WIDTH)


def _blockdiag(x):
    lane = lax.broadcasted_iota(jnp.int32, x.shape, 1)
    zero = jnp.zeros_like(x)
    return jnp.concatenate([jnp.where(lane < GDN_C, x, zero), jnp.where(lane >= GDN_C, x, zero)], axis=0)


def _mmp(a, b):
    return jnp.dot(a.astype(BF16), _blockdiag(b.astype(BF16)), preferred_element_type=F32)


def _unit_lower_inverse(lows, ii, jl):
    eye = jnp.where(ii == jl, 1.0, 0.0)
    in_block = (ii // INV_BLOCK) == (jl // INV_BLOCK)
    ps = [jnp.where(in_block, -low, 0.0) for low in lows]
    offs = [jnp.where(in_block, 0.0, low) for low in lows]
    dinvs = [eye + p for p in ps]
    span = 2
    while span < INV_BLOCK:
        ps = [_mmp(p, p) for p in ps]
        dinvs = [d + _mmp(d, p) for d, p in zip(dinvs, ps)]
        span *= 2
    powers = [[-_mmp(d, off) for d, off in zip(dinvs, offs)]]
    span = 2
    while span < GDN_C // INV_BLOCK:
        powers.append([_mmp(n, n) for n in powers[-1]])
        span *= 2
    xs = dinvs
    for pw in reversed(powers):
        xs = [x + _mmp(n, x) for n, x in zip(pw, xs)]
    return xs


def _gdn_prompt_body(h_ref, gn_ref, w_ref, wgt_ref, cw_ref, alog_ref, dtb_ref, on_ref,
                     o_ref, s_ref, cst_ref, ext_ref, z_ref):
    c = pl.program_id(1)
    nh = GDN_HEADS
    npair = nh // 2
    hk = nh * GDN_DK
    cc = GDN_C
    rows_step = GDN_STEP_CHUNKS * cc
    heads_per_tile = GDN_PROJ_TN // LANES

    @pl.when(c == 0)
    def _():
        s_ref[...] = jnp.zeros_like(s_ref)
        ext_ref[0:8, :] = jnp.zeros((8, GDN_QKV), F32)

    @pl.when(c > 0)
    def _():
        ext_ref[0:8, :] = ext_ref[rows_step:rows_step + 8, :]

    xn = _rms(h_ref[...], gn_ref[...]).astype(BF16)
    abt = lax.dot_general(wgt_ref[...], xn, _NT, preferred_element_type=F32)

    def project(n):
        cols = slice(n * GDN_PROJ_TN, (n + 1) * GDN_PROJ_TN)
        tile = jnp.dot(xn, w_ref[:, cols], preferred_element_type=F32)
        if n < GDN_QKV // GDN_PROJ_TN:
            ext_ref[8:8 + rows_step, cols] = tile
        else:
            z_ref[:, n * GDN_PROJ_TN - GDN_QKV:(n + 1) * GDN_PROJ_TN - GDN_QKV] = tile

    def conv_act(col, r0):
        sl = slice(col, col + LANES)
        acc = ext_ref[8 + r0:8 + r0 + cc, sl] * cw_ref[3:4, sl]
        for i in range(GDN_CONV - 1):
            acc = acc + ext_ref[5 + i + r0:5 + i + r0 + cc, sl] * cw_ref[i:i + 1, sl]
        return _silu(acc)

    def l2n(x):
        return x * lax.rsqrt(jnp.sum(x * x, axis=-1, keepdims=True) + EPS)

    def pair(xs):
        return [jnp.concatenate([xs[2 * p], xs[2 * p + 1]], axis=1) for p in range(npair)]

    def row_pair(x):
        return pair([x[h:h + 1, :] for h in range(nh)])

    project(0)
    lane = lax.broadcasted_iota(jnp.int32, (nh, cc), 1)
    q_p, k_p, v_p, gc_b, beta_b, kdec_b, gc_rows, s_decay = [], [], [], [], [], [], [], []
    for ck in range(GDN_STEP_CHUNKS):
        r0 = ck * cc
        g = -jnp.exp(alog_ref[...]) * _softplus(abt[0:nh, r0:r0 + cc] + dtb_ref[...])
        beta = _sigmoid(abt[nh:2 * nh, r0:r0 + cc])
        gc = g
        shift = 1
        while shift < cc:
            gc = gc + jnp.where(lane >= shift, pltpu.roll(gc, shift, 1), 0.0)
            shift *= 2
        g_last = jnp.broadcast_to(gc[:, cc - 1:cc], (nh, cc))
        s_decay.append(jnp.exp(g_last))
        cols = jnp.transpose(jnp.concatenate([gc, beta, jnp.zeros((LANES - 2 * nh, cc), F32)], axis=0))
        bcast = [jnp.broadcast_to(cols[:, n:n + 1], (cc, cc)) for n in range(2 * nh)]
        gcb = pair(bcast[0:nh])
        gc_b += gcb
        beta_b += pair(bcast[nh:2 * nh])
        kdec_b += [jnp.exp(gl - x) for gl, x in zip(row_pair(g_last), gcb)]
        gc_rows += row_pair(gc)
    gam_b = [jnp.exp(x) for x in gc_b]
    ii = lax.broadcasted_iota(jnp.int32, (cc, 2 * cc), 0)
    jl = lax.broadcasted_iota(jnp.int32, (cc, 2 * cc), 1) % cc
    decay = [jnp.exp(jnp.where(ii >= jl, gb - gr, -jnp.inf)) for gb, gr in zip(gc_b, gc_rows)]

    act = {}
    for n in range(1, GDN_MAIN // GDN_PROJ_TN + 1):
        if n < GDN_MAIN // GDN_PROJ_TN:
            project(n)
        if n <= GDN_QKV // GDN_PROJ_TN:
            for hh in range(heads_per_tile):
                col = (n - 1) * GDN_PROJ_TN + hh * LANES
                for ck in range(GDN_STEP_CHUNKS):
                    a = conv_act(col, ck * cc)
                    if col < hk:
                        a = l2n(a) * (GDN_DK ** -0.5)
                    elif col < 2 * hk:
                        a = l2n(a)
                    act[(col, ck)] = a
    for ck in range(GDN_STEP_CHUNKS):
        q_p += pair([act[(h * GDN_DK, ck)] for h in range(nh)])
        k_p += pair([act[(hk + h * GDN_DK, ck)] for h in range(nh)])
        v_p += pair([act[(2 * hk + h * GDN_DV, ck)] for h in range(nh)])

    gram = [lax.dot_general(jnp.concatenate([q, k], axis=0).astype(BF16), _blockdiag(k.astype(BF16)), _NT,
                            preferred_element_type=F32) for q, k in zip(q_p, k_p)]
    qk = [gm[0:cc, :] * d for gm, d in zip(gram, decay)]
    lows = [jnp.where(ii > jl, b * gm[cc:2 * cc, :] * d, 0.0) for b, gm, d in zip(beta_b, gram, decay)]
    tinv = _unit_lower_inverse(lows, ii, jl)
    w =[_mmp(t, b * gm * k) for t, b, gm, k in zip(tinv, beta_b, gam_b, k_p)]
    u0 = [_mmp(t, b * v) for t, b, v in zip(tinv, beta_b, v_p)]
    wq = [jnp.concatenate([wp, gm * q], axis=0).astype(BF16) for wp, gm, q in zip(w, gam_b, q_p)]
    kd = [(kdb * k).astype(BF16) for kdb, k in zip(kdec_b, k_p)]

    state = [s_ref[h] for h in range(nh)]
    zero = jnp.zeros((GDN_DK, GDN_DV), BF16)
    for ck in range(GDN_STEP_CHUNKS):
        r0 = ck * cc
        ent = range(ck * npair, (ck + 1) * npair)
        s_bd = [jnp.concatenate([jnp.concatenate([state[2 * p].astype(BF16), zero], axis=1),
                                 jnp.concatenate([zero, state[2 * p + 1].astype(BF16)], axis=1)], axis=0)
                for p in range(npair)]
        ws_qs = [jnp.dot(wq[e], sb, preferred_element_type=F32) for e, sb in zip(ent, s_bd)]
        u = [u0[e] - x[0:cc, :] for e, x in zip(ent, ws_qs)]
        o = [x[cc:2 * cc, :] + _mmp(qk[e], b) for e, x, b in zip(ent, ws_qs, u)]
        upd = [lax.dot_general(kd[e], b.astype(BF16), _TN, preferred_element_type=F32)
               for e, b in zip(ent, u)]
        new_state = []
        for h in range(nh):
            half = slice((h % 2) * cc, (h % 2 + 1) * cc)
            new_state.append(s_decay[ck][h:h + 1, :] * state[h] + upd[h // 2][half, half])
            z = z_ref[r0:r0 + cc, h * GDN_DV:(h + 1) * GDN_DV]
            o_ref[r0:r0 + cc, h * GDN_DV:(h + 1) * GDN_DV] = (
                _rms(o[h // 2][:, half], on_ref[...]) * _silu(z)).astype(BF16)
        state = new_state
    for h in range(nh):
        s_ref[h] = state[h]

    @pl.when(c == pl.num_programs(1) - 1)
    def _():
        cst_ref[...] = ext_ref[rows_step:rows_step + 8, :]


def gdn_prompt(h, gain, w_in, w_gates_t, idx, conv_w, a_log, dt_bias, o_norm):
    b, t, d = h.shape
    rows = GDN_STEP_CHUNKS * GDN_C
    nc = t // rows
    nh = GDN_HEADS
    return pl.pallas_call(
        _gdn_prompt_body,
        grid=(b, nc),
        in_specs=[
            pl.BlockSpec((None, rows, d), lambda i, c: (i, c, 0)),
            pl.BlockSpec((1, d), lambda i, c: (0, 0)),
            pl.BlockSpec((None, d, GDN_MAIN), lambda i, c: (idx, 0, 0), pipeline_mode=pl.Buffered(1)),
            pl.BlockSpec((None, LANES, d), lambda i, c: (idx, 0, 0), pipeline_mode=pl.Buffered(1)),
            pl.BlockSpec((GDN_CONV, GDN_QKV), lambda i, c: (0, 0)),
            pl.BlockSpec((nh, 1), lambda i, c: (0, 0)),
            pl.BlockSpec((nh, 1), lambda i, c: (0, 0)),
            pl.BlockSpec((1, GDN_DV), lambda i, c: (0, 0)),
        ],
        out_specs=[
            pl.BlockSpec((None, rows, nh * GDN_DV), lambda i, c: (i, c, 0)),
            pl.BlockSpec((None, nh, GDN_DK, GDN_DV), lambda i, c: (i, 0, 0, 0)),
            pl.BlockSpec((None, 8, GDN_QKV), lambda i, c: (i, 0, 0)),
        ],
        out_shape=[
            jax.ShapeDtypeStruct((b, t, nh * GDN_DV), BF16),
            jax.ShapeDtypeStruct((b, nh, GDN_DK, GDN_DV), F32),
            jax.ShapeDtypeStruct((b, 8, GDN_QKV), F32),
        ],
        scratch_shapes=[pltpu.VMEM((8 + rows, GDN_QKV), F32), pltpu.VMEM((rows, nh * GDN_DV), F32)],
        compiler_params=_cparams(("parallel", "arbitrary")),
        name="gdn_prompt",
    )(h, gain.reshape(1, d), w_in, w_gates_t, conv_w, a_log.reshape(nh, 1), dt_bias.reshape(nh, 1),
      o_norm.reshape(1, GDN_DV))


def _gdn_decode_body(x_ref, cs_ref, ab_ref, cw_ref, alog_ref, dtb_ref, on_ref, s0_ref, *rest, first, layer_j):
    o_ref, s_ref = rest[-2:]
    nh = GDN_HEADS
    pad = jnp.zeros((8 - 2, GDN_DK), F32)
    if first:
        for slot in range(s_ref.shape[0]):
            if slot != layer_j:
                s_ref[slot] = jnp.zeros(s_ref.shape[1:], F32)
    pad7 = jnp.zeros((8 - 1, GDN_DK), F32)
    seqs = range(GDN_DEC_ROWS)
    q8, k8, v8, gam8, beta8, qk8 = [], [], [], [], [], []
    for r in seqs:
        x = x_ref[r, 0:3 * nh, :]
        conv = x * cw_ref[GDN_CONV - 1]
        for i in range(GDN_CONV - 1):
            conv = conv + cs_ref[r, i] * cw_ref[i]
        act = conv * _sigmoid(conv)
        qa, ka = act[0:nh], act[nh:2 * nh]
        q8.append(qa * lax.rsqrt(jnp.sum(qa * qa, axis=-1, keepdims=True) + EPS) * (GDN_DK ** -0.5))
        k8.append(ka * lax.rsqrt(jnp.sum(ka * ka, axis=-1, keepdims=True) + EPS))
        v8.append(act[2 * nh:3 * nh])
        g = -jnp.exp(alog_ref[...]) * _softplus(ab_ref[r, 0:nh, :] + dtb_ref[...])
        gam8.append(jnp.broadcast_to(jnp.exp(g), (nh, GDN_DV)))
        beta8.append(jnp.broadcast_to(_sigmoid(ab_ref[r, nh:2 * nh, :]), (nh, GDN_DV)))
        qk8.append(jnp.broadcast_to(jnp.sum(q8[r] * k8[r], axis=-1, keepdims=True), (nh, GDN_DV)))
    pairs = [(r, h) for r in seqs for h in range(nh)]
    s_old = [s0_ref[r, h] for r, h in pairs]
    qk_s = [jnp.dot(jnp.concatenate([q8[r][h:h + 1, :], k8[r][h:h + 1, :], pad], axis=0).astype(BF16),
                    s.astype(BF16), preferred_element_type=F32) for (r, h), s in zip(pairs, s_old)]
    u = [beta8[r][h:h + 1, :] * (v8[r][h:h + 1, :] - gam8[r][h:h + 1, :] * x[1:2, :])
         for (r, h), x in zip(pairs, qk_s)]
    outs = [gam8[r][h:h + 1, :] * x[0:1, :] + qk8[r][h:h + 1, :] * b for (r, h), x, b in zip(pairs, qk_s, u)]
    outer = [lax.dot_general(jnp.concatenate([k8[r][h:h + 1, :], pad7], axis=0).astype(BF16),
                             jnp.concatenate([b, pad7], axis=0).astype(BF16), _TN, preferred_element_type=F32)
             for (r, h), b in zip(pairs, u)]
    for (r, h), s, x in zip(pairs, s_old, outer):
        new = gam8[r][h:h + 1, :] * s + x
        if first:
            s_ref[layer_j, r, h] = new
        else:
            s_ref[r, h] = new
    for r in seqs:
        o8 = jnp.concatenate(outs[r * nh:(r + 1) * nh], axis=0)
        z8 = x_ref[r, 3 * nh:4 * nh, :]
        o_ref[r] = (_rms(o8, on_ref[...]) * (z8 * _sigmoid(z8))).astype(BF16)


def gdn_decode(proj, gates, conv_state, s0_all, layer_j, s_new_all, conv_w, a_log, dt_bias, o_norm):
    nb = proj.shape[0]
    nh = GDN_HEADS
    nrow = GDN_QKV // LANES
    first = s_new_all is None
    in_specs = [
        pl.BlockSpec((GDN_DEC_ROWS, GDN_MAIN // LANES, LANES), lambda i: (i, 0, 0)),
        pl.BlockSpec((GDN_DEC_ROWS, GDN_CONV - 1, nrow, LANES), lambda i: (i, 0, 0, 0)),
        pl.BlockSpec((GDN_DEC_ROWS, 2 * nh, 1), lambda i: (i, 0, 0)),
        pl.BlockSpec((GDN_CONV, nrow, LANES), lambda i: (0, 0, 0)),
        pl.BlockSpec((nh, 1), lambda i: (0, 0)),
        pl.BlockSpec((nh, 1), lambda i: (0, 0)),
        pl.BlockSpec((1, GDN_DV), lambda i: (0, 0)),
        pl.BlockSpec((None, GDN_DEC_ROWS, nh, GDN_DK, GDN_DV), lambda i: (layer_j, i, 0, 0, 0)),
    ]
    args = [proj.reshape(nb, GDN_MAIN // LANES, LANES),
            conv_state.reshape(nb, GDN_CONV - 1, nrow, LANES),
            gates.reshape(nb, 2 * nh, 1),
            conv_w.reshape(GDN_CONV, nrow, LANES),
            a_log.reshape(nh, 1), dt_bias.reshape(nh, 1), o_norm.reshape(1, GDN_DV), s0_all]
    if first:
        s_spec = pl.BlockSpec((s0_all.shape[0], GDN_DEC_ROWS, nh, GDN_DK, GDN_DV), lambda i: (0, i, 0, 0, 0))
        aliases = {}
    else:
        in_specs.append(pl.BlockSpec(memory_space=pl.ANY))
        args.append(s_new_all)
        s_spec = pl.BlockSpec((None, GDN_DEC_ROWS, nh, GDN_DK, GDN_DV), lambda i: (layer_j, i, 0, 0, 0))
        aliases = {len(args) - 1: 1}
    o, s = pl.pallas_call(
        functools.partial(_gdn_decode_body, first=first, layer_j=layer_j),
        grid=(nb // GDN_DEC_ROWS,),
        in_specs=in_specs,
        out_specs=[pl.BlockSpec((GDN_DEC_ROWS, nh, GDN_DV), lambda i: (i, 0, 0)), s_spec],
        out_shape=[
            jax.ShapeDtypeStruct((nb, nh, GDN_DV), BF16),
            jax.ShapeDtypeStruct(s0_all.shape, F32),
        ],
        input_output_aliases=aliases,
        compiler_params=_cparams(("parallel",)),
        name="gdn_decode",
    )(*args)
    return o.reshape(nb, nh * GDN_DV), s


def kernel(x_prompt, x_sample, cache_k, cache_v, page_table, state_pool, state_conv, state_delta,
           norm_mix, norm_ffn, norm_final, rel_bias,
           w_in_even, pool_w, pool_scale, lambda_q1, lambda_k1, lambda_q2, lambda_k2, subln_w, w_out_even,
           w_in_odd, conv_w, a_log, dt_bias, o_norm, w_out_odd,
           w_gate_up, w_down):
    bp, t, d = x_prompt.shape
    bs = x_sample.shape[0]
    mp = bp * t
    nh = GDN_HEADS
    past = page_table.shape[1] * PAGE_SIZE
    tm_p = 1024
    tm_s = bs

    hp = x_prompt.reshape(mp, d)
    hs = x_sample.reshape(bs, d)
    bias_p, bias_d, bias_n = rel_bias_tiles(rel_bias, past)

    w_in_e = w_in_even.astype(BF16)
    w_in_o = w_in_odd[:, :, 0:GDN_MAIN].astype(BF16)
    w_gates_t = jnp.zeros((w_in_odd.shape[0], LANES, d), F32).at[:, 0:2 * nh, :].set(
        jnp.transpose(w_in_odd[:, :, GDN_MAIN:], (0, 2, 1))).astype(BF16)
    w_out_e = w_out_even.astype(BF16)
    w_out_o = w_out_odd.astype(BF16)
    w_gu = w_gate_up.astype(BF16)
    w_d = w_down.astype(BF16)
    pool_wb = pool_w.astype(BF16)

    k_p, v_p, k_s, v_s, pool_p, pool_s = [], [], [], [], [], []
    conv_p, conv_s, delta_p = [], [], []
    delta_s = None
    for layer in range(DEPTH):
        j = layer // 2
        last = layer == DEPTH - 1
        if layer % 2 == 0:
            w_out = w_out_e
            pw = pool_wb[j]
            lam_init = 0.8 - 0.6 * math.exp(-0.3 * layer)
            lam_params = jnp.stack([lambda_q1[j], lambda_k1[j], lambda_q2[j], lambda_k2[j]])

            proj_p, kp, vp = norm_matmul(hp, norm_mix[layer], w_in_e, j, tm=tm_p, tn=DIFF_WIDTH)
            proj_s, ks_, vs_ = norm_matmul(hs, norm_mix[layer], w_in_e, j, tm=tm_s, tn=DIFF_WIDTH)
            proj_p = proj_p.reshape(bp, t, EVEN_IN)

            ypool_p = pool_prompt(proj_p, pw, pool_scale[j])
            oatt_p = attn_prompt(proj_p, bias_p, lam_params, subln_w[j], lam_init)
            ypool_s = pool_decode(jnp.transpose(state_pool[j], (1, 0, 2)), proj_s, pw, pool_scale[j])
            oatt_s = attn_decode(proj_s, cache_k, cache_v, page_table, j, bias_d, bias_n,
                                 lam_params, subln_w[j], lam_init)

            kv_shape = (DIFF_HEADS, DIFF_HEAD_DIM)
            k_p.append(kp.reshape(bp, t, *kv_shape))
            v_p.append(vp.reshape(bp, t, *kv_shape))
            k_s.append(ks_.reshape(bs, 1, *kv_shape))
            v_s.append(vs_.reshape(bs, 1, *kv_shape))
            pool_p.append(proj_p[:, t - POOL_BUF:, 0:POOL_WIDTH])
            pool_s.append(jnp.concatenate([state_pool[j][:, 1:], proj_s[:, None, 0:POOL_WIDTH]], axis=1))

            mix_p = ((ypool_p.reshape(mp, POOL_WIDTH), 0), (oatt_p.reshape(mp, DIFF_WIDTH), 0))
            mix_s = ((ypool_s, 0), (oatt_s, 0))
        else:
            w_out = w_out_o

            proj_s, gates_s = norm_matmul(hs, norm_mix[layer], w_in_o, j, w_gates_t, tm=tm_s, tn=512)

            o_p, s_p, tail_p = gdn_prompt(hp.reshape(bp, t, d), norm_mix[layer], w_in_o, w_gates_t, j,
                                          conv_w[j], a_log[j], dt_bias[j], o_norm[j])
            o_s, delta_s = gdn_decode(proj_s, gates_s[0:2 * nh].T, state_conv[j], state_delta, j, delta_s,
                                      conv_w[j], a_log[j], dt_bias[j], o_norm[j])

            conv_p.append(tail_p[:, 8 - (GDN_CONV - 1):, :])
            conv_s.append(jnp.concatenate([state_conv[j][:, 1:], proj_s[:, None, 0:GDN_QKV]], axis=1))
            delta_p.append(s_p)

            o_p = o_p.reshape(mp, nh * GDN_DV)
            mix_p = ((o_p, 0), (o_p, 1))
            mix_s = ((o_s, 0), (o_s, 1))

        g_fin = norm_final if last else None
        hp = outproj_ffn(hp, mix_p[0], mix_p[1], w_out, j, norm_ffn[layer], w_gu, w_d, layer, g_fin, tm=FFN_TM)
        hs = outproj_ffn(hs, mix_s[0], mix_s[1], w_out, j, norm_ffn[layer], w_gu, w_d, layer, g_fin, tm=tm_s)

    return (hp.reshape(bp, t, d), hs.reshape(bs, 1, d),
            jnp.stack(k_p, axis=1), jnp.stack(v_p, axis=1), jnp.stack(k_s, axis=1), jnp.stack(v_s, axis=1),
            jnp.stack(pool_p), jnp.stack(pool_s), jnp.stack(conv_p), jnp.stack(conv_s),
            jnp.stack(delta_p), delta_s)
```

```python
import functools
import math

import jax
import jax.numpy as jnp
from jax import lax
from jax.experimental import pallas as pl
from jax.experimental.pallas import tpu as pltpu

F32 = jnp.float32
BF16 = jnp.bfloat16

D_MODEL = 1024
DEPTH = 4
PAGE_SIZE = 128
POOL_WIDTH = 512
POOL_WINDOWS = (2, 4, 8, 16)
POOL_GROUP = 128
POOL_BUF = 15
DIFF_HEADS = 4
DIFF_HALF = 64
DIFF_HEAD_DIM = 128
DIFF_WIDTH = 512
EVEN_IN = 2048
REL_BUCKETS = 32
REL_MAX_DIST = 128
GDN_HEADS = 8
GDN_DK = 128
GDN_DV = 128
GDN_CONV = 4
GDN_QKV = 3072
GDN_MAIN = 4096
D_FF = 2816
EPS = 1e-6

LANES = 128
VMEM_LIMIT = 48 * 1024 * 1024
NEG_BIG = -1e30

ATT_TQ = 512
ATT_TK = 512
GDN_C = 128
GDN_STEP_CHUNKS = 2
GDN_DEC_ROWS = 4
GDN_PROJ_TN = 512
INV_BLOCK = 16
FFN_TF = 256
FFN_TM = 512

_NT = (((1,), (1,)), ((), ()))
_TN = (((0,), (0,)), ((), ()))


def _cparams(sem):
    return pltpu.CompilerParams(dimension_semantics=sem, vmem_limit_bytes=VMEM_LIMIT)


def _sigmoid(x):
    return 1.0 / (1.0 + jnp.exp(-x))


def _silu(x):
    h = 0.5 * x
    return h + h * jnp.tanh(h)


def _softplus(x):
    return jnp.maximum(x, 0.0) + jnp.log1p(jnp.exp(-jnp.abs(x)))


def _rms(x, gain):
    return x * lax.rsqrt(jnp.mean(x * x, axis=-1, keepdims=True) + EPS) * gain


def _mm(a, b):
    return jnp.dot(a.astype(BF16), b.astype(BF16), preferred_element_type=F32)


def _norm_mm_kv_body(x_ref, g_ref, w_ref, o_ref, k_ref, v_ref, xn_ref):
    j = pl.program_id(1)

    @pl.when(j == 0)
    def _():
        xn_ref[...] = _rms(x_ref[...], g_ref[...]).astype(BF16)

    res = jnp.dot(xn_ref[...], w_ref[...], preferred_element_type=F32)
    o_ref[...] = res

    def head_rows(dst_ref):
        for h in range(DIFF_HEADS):
            dst_ref[pl.ds(h, res.shape[0], stride=DIFF_HEADS), :] = res[:, h * DIFF_HEAD_DIM:(h + 1) * DIFF_HEAD_DIM]

    @pl.when(j == 2)
    def _():
        head_rows(k_ref)

    @pl.when(j == 3)
    def _():
        head_rows(v_ref)


def _norm_mm_gate_body(x_ref, g_ref, w_ref, wst_ref, o_ref, ost_ref, xn_ref):
    @pl.when(pl.program_id(1) == 0)
    def _():
        xn = _rms(x_ref[...], g_ref[...]).astype(BF16)
        xn_ref[...] = xn
        ost_ref[...] = lax.dot_general(wst_ref[...], xn, _NT, preferred_element_type=F32)

    o_ref[...] = jnp.dot(xn_ref[...], w_ref[...], preferred_element_type=F32)


def norm_matmul(x, gain, w, idx, w_gates_t=None, *, tm, tn):
    m, d = x.shape
    n = w.shape[2]
    grid = (m // tm, n // tn)
    in_specs = [
        pl.BlockSpec((tm, d), lambda i, j: (i, 0)),
        pl.BlockSpec((1, d), lambda i, j: (0, 0)),
        pl.BlockSpec((None, d, tn), lambda i, j: (idx, 0, j)),
    ]
    out_specs = pl.BlockSpec((tm, tn), lambda i, j: (i, j))
    out_shape = jax.ShapeDtypeStruct((m, n), F32)
    args = [x, gain.reshape(1, d), w]
    if w_gates_t is None:
        assert tn == DIFF_WIDTH and n == EVEN_IN
        kv_spec = pl.BlockSpec((tm * DIFF_HEADS, DIFF_HEAD_DIM), lambda i, j: (i, 0))
        kv_shape = jax.ShapeDtypeStruct((m * DIFF_HEADS, DIFF_HEAD_DIM), F32)
        out_specs = [out_specs, kv_spec, kv_spec]
        out_shape = [out_shape, kv_shape, kv_shape]
        body = _norm_mm_kv_body
    else:
        in_specs.append(pl.BlockSpec((None, LANES, d), lambda i, j: (idx, 0, 0)))
        out_specs = [out_specs, pl.BlockSpec((LANES, tm), lambda i, j: (0, i))]
        out_shape = [out_shape, jax.ShapeDtypeStruct((LANES, m), F32)]
        args.append(w_gates_t)
        body = _norm_mm_gate_body
    return pl.pallas_call(
        body,
        grid=grid,
        in_specs=in_specs,
        out_specs=out_specs,
        out_shape=out_shape,
        scratch_shapes=[pltpu.VMEM((tm, d), BF16)],
        compiler_params=_cparams(("parallel", "arbitrary")),
        name="norm_matmul",
    )(*args)


def _ffn_body(*refs, final):
    if final:
        res_ref, a0_ref, a1_ref, wo_ref, gf_ref, wgu_ref, wd_ref, gfin_ref, o_ref = refs
    else:
        res_ref, a0_ref, a1_ref, wo_ref, gf_ref, wgu_ref, wd_ref, o_ref = refs
    mixed = jnp.concatenate([a0_ref[...], a1_ref[...]], axis=1)
    h1 = res_ref[...] + jnp.dot(mixed, wo_ref[...], preferred_element_type=F32)
    xn = _rms(h1, gf_ref[...]).astype(BF16)
    acts = []
    for f in range(D_FF // FFN_TF):
        g = jnp.dot(xn, wgu_ref[:, f * FFN_TF:(f + 1) * FFN_TF], preferred_element_type=F32)
        u = jnp.dot(xn, wgu_ref[:, D_FF + f * FFN_TF:D_FF + (f + 1) * FFN_TF], preferred_element_type=F32)
        acts.append((g * _sigmoid(g) * u).astype(BF16))
    y = h1 + jnp.dot(jnp.concatenate(acts, axis=1), wd_ref[...], preferred_element_type=F32)
    if final:
        y = _rms(y, gfin_ref[...])
    o_ref[...] = y


def outproj_ffn(res, mix0, mix1, w_out, idx_out, g_ffn, w_gu, w_d, layer, g_final=None, *, tm):
    m, d = res.shape
    half = w_out.shape[1] // 2
    (m0, c0), (m1, c1) = mix0, mix1
    final = g_final is not None
    once = pl.Buffered(1)
    in_specs = [
        pl.BlockSpec((tm, d), lambda i: (i, 0)),
        pl.BlockSpec((tm, half), lambda i: (i, c0)),
        pl.BlockSpec((tm, half), lambda i: (i, c1)),
        pl.BlockSpec((None, 2 * half, d), lambda i: (idx_out, 0, 0), pipeline_mode=once),
        pl.BlockSpec((1, d), lambda i: (0, 0)),
        pl.BlockSpec((None, d, 2 * D_FF), lambda i: (layer, 0, 0), pipeline_mode=once),
        pl.BlockSpec((None, D_FF, d), lambda i: (layer, 0, 0), pipeline_mode=once),
    ]
    args = [res, m0, m1, w_out, g_ffn.reshape(1, d), w_gu, w_d]
    if final:
        in_specs.append(pl.BlockSpec((1, d), lambda i: (0, 0)))
        args.append(g_final.reshape(1, d))
    return pl.pallas_call(
        functools.partial(_ffn_body, final=final),
        grid=(m // tm,),
        in_specs=in_specs,
        out_specs=pl.BlockSpec((tm, d), lambda i: (i, 0)),
        out_shape=jax.ShapeDtypeStruct((m, d), F32),
        compiler_params=_cparams(("parallel",)),
        name="outproj_ffn",
    )(*args)


POOL_PAD = 16
POOL_ROWS = 512


def _even_proj_body(x_ref, g_ref, w_ref, pw_ref, ps_ref, qkv_ref, yp_ref, k_ref, v_ref, tail_ref,
                    xn_ref, ext_ref, *, blocks_per_seq):
    i = pl.program_id(0)
    j = pl.program_id(1)
    tm = x_ref.shape[0]

    @pl.when(j == 0)
    def _():
        xn_ref[...] = _rms(x_ref[...], g_ref[...]).astype(BF16)

    res = jnp.dot(xn_ref[...], w_ref[...], preferred_element_type=F32)

    blk = i % blocks_per_seq

    @pl.when((j == 0) & (blk == 0))
    def _():
        ext_ref[0:POOL_PAD, :] = jnp.zeros((POOL_PAD, POOL_WIDTH), F32)

    @pl.when((j == 0) & (blk > 0))
    def _():
        ext_ref[0:POOL_PAD, :] = ext_ref[tm:tm + POOL_PAD, :]

    @pl.when(j == 0)
    def _():
        ext_ref[POOL_PAD:, :] = res
        tail_ref[...] = res[tm - POOL_PAD:tm, :]
        for r0 in range(0, tm, POOL_ROWS):
            pos = lax.broadcasted_iota(jnp.int32, (POOL_ROWS, 1), 0) + (blk * tm + r0)
            for g, w in enumerate(POOL_WINDOWS):
                sl = slice(g * POOL_GROUP, (g + 1) * POOL_GROUP)
                cur = ext_ref[POOL_PAD + r0:POOL_PAD + r0 + POOL_ROWS, sl]
                win = cur
                for s in range(1, w):
                    win = win + ext_ref[POOL_PAD + r0 - s:POOL_PAD + r0 - s + POOL_ROWS, sl]
                cnt = jnp.minimum(pos + 1, w).astype(F32)
                dlt = win / cnt - cur
                y = jnp.dot(dlt.astype(BF16), pw_ref[g], preferred_element_type=F32) * ps_ref[:, sl]
                yp_ref[r0:r0 + POOL_ROWS, sl] = y.astype(BF16)

    @pl.when(j >= 1)
    def _():
        qkv_ref[...] = res.astype(BF16)

    def head_rows(dst_ref):
        for h in range(DIFF_HEADS):
            dst_ref[pl.ds(h, tm, stride=DIFF_HEADS), :] = res[:, h * DIFF_HEAD_DIM:(h + 1) * DIFF_HEAD_DIM]

    @pl.when(j == 2)
    def _():
        head_rows(k_ref)

    @pl.when(j == 3)
    def _():
        head_rows(v_ref)


def even_proj_prompt(x, gain, w, idx, pool_w, pool_scale, seq_len, *, tm):
    m, d = x.shape
    tn = DIFF_WIDTH
    blocks_per_seq = seq_len // tm
    nseq = m // seq_len
    kv_spec = pl.BlockSpec((tm * DIFF_HEADS, DIFF_HEAD_DIM), lambda i, j: (i, 0))
    kv_shape = jax.ShapeDtypeStruct((m * DIFF_HEADS, DIFF_HEAD_DIM), F32)
    return pl.pallas_call(
        functools.partial(_even_proj_body, blocks_per_seq=blocks_per_seq),
        grid=(m // tm, EVEN_IN // tn),
        in_specs=[
            pl.BlockSpec((tm, d), lambda i, j: (i, 0)),
            pl.BlockSpec((1, d), lambda i, j: (0, 0)),
            pl.BlockSpec((None, d, tn), lambda i, j: (idx, 0, j)),
            pl.BlockSpec((len(POOL_WINDOWS), POOL_GROUP, POOL_GROUP), lambda i, j: (0, 0, 0)),
            pl.BlockSpec((1, POOL_WIDTH), lambda i, j: (0, 0)),
        ],
        out_specs=[
            pl.BlockSpec((tm, tn), lambda i, j: (i, jnp.maximum(j - 1, 0))),
            pl.BlockSpec((tm, POOL_WIDTH), lambda i, j: (i, 0)),
            kv_spec,
            kv_spec,
            pl.BlockSpec((None, POOL_PAD, POOL_WIDTH), lambda i, j: (i // blocks_per_seq, 0, 0)),
        ],
        out_shape=[
            jax.ShapeDtypeStruct((m, 3 * DIFF_WIDTH), BF16),
            jax.ShapeDtypeStruct((m, POOL_WIDTH), BF16),
            kv_shape,
            kv_shape,
            jax.ShapeDtypeStruct((nseq, POOL_PAD, POOL_WIDTH), F32),
        ],
        scratch_shapes=[pltpu.VMEM((tm, d), BF16), pltpu.VMEM((POOL_PAD + tm, POOL_WIDTH), F32)],
        compiler_params=_cparams(("arbitrary", "arbitrary")),
        name="even_proj",
    )(x, gain.reshape(1, d), w, pool_w, pool_scale.reshape(1, POOL_WIDTH))


def _pool_decode_body(st_ref, u_ref, pw_ref, ps_ref, o_ref):
    for g, w in enumerate(POOL_WINDOWS):
        sl = slice(g * POOL_GROUP, (g + 1) * POOL_GROUP)
        cur = u_ref[:, sl]
        win = cur
        for i in range(1, w):
            win = win + st_ref[POOL_BUF - i, :, sl]
        dlt = win / float(w) - cur
        y = jnp.dot(dlt.astype(BF16), pw_ref[g], preferred_element_type=F32) * ps_ref[:, sl]
        o_ref[:, sl] = y.astype(BF16)


def pool_decode(state_t, proj, pool_w, pool_scale):
    nb = proj.shape[0]
    return pl.pallas_call(
        _pool_decode_body,
        grid=(1,),
        in_specs=[
            pl.BlockSpec((POOL_BUF, nb, POOL_WIDTH), lambda i: (0, 0, 0)),
            pl.BlockSpec((nb, POOL_WIDTH), lambda i: (0, 0)),
            pl.BlockSpec((len(POOL_WINDOWS), POOL_GROUP, POOL_GROUP), lambda i: (0, 0, 0)),
            pl.BlockSpec((1, POOL_WIDTH), lambda i: (0, 0)),
        ],
        out_specs=pl.BlockSpec((nb, POOL_WIDTH), lambda i: (0, 0)),
        out_shape=jax.ShapeDtypeStruct((nb, POOL_WIDTH), BF16),
        compiler_params=_cparams(("arbitrary",)),
        name="pool_decode",
    )(state_t, proj, pool_w, pool_scale.reshape(1, POOL_WIDTH))


def _rel_bucket(n):
    max_exact = REL_BUCKETS // 2
    nf = jnp.maximum(n, 1).astype(F32)
    large = max_exact + (jnp.log(nf / max_exact) / math.log(REL_MAX_DIST / max_exact)
                         * (REL_BUCKETS - max_exact)).astype(jnp.int32)
    large = jnp.minimum(large, REL_BUCKETS - 1)
    return jnp.where(n < max_exact, n, large)


def _table_lookup(tab_ref, bucket, h):
    out = jnp.zeros(bucket.shape, F32)
    for b in range(REL_BUCKETS):
        out = jnp.where(bucket == b, tab_ref[b, h], out)
    return out


def _rel_bias_body(tab_ref, bp_ref, bd_ref, bn_ref, *, past):
    ii = lax.broadcasted_iota(jnp.int32, (ATT_TQ, ATT_TK), 0)
    jj = lax.broadcasted_iota(jnp.int32, (ATT_TQ, ATT_TK), 1)
    for h in range(DIFF_HEADS):
        for t in range(3):
            dist = t * ATT_TK + ii - jj
            bias = _table_lookup(tab_ref, _rel_bucket(jnp.maximum(dist, 0)), h)
            bp_ref[h, t] = jnp.where(dist >= 0, bias, NEG_BIG)
    nrow = 2 * DIFF_HEADS
    row = lax.broadcasted_iota(jnp.int32, (nrow, past * DIFF_HEADS), 0)
    col = lax.broadcasted_iota(jnp.int32, (nrow, past * DIFF_HEADS), 1)
    bucket = _rel_bucket(past - col // DIFF_HEADS)
    rown = lax.broadcasted_iota(jnp.int32, (nrow, LANES), 0)
    bd = jnp.full((nrow, past * DIFF_HEADS), NEG_BIG, F32)
    bn = jnp.zeros((nrow, LANES), F32)
    for h in range(DIFF_HEADS):
        own = jnp.where(row // 2 == h, col % DIFF_HEADS, -1) == h
        bd = jnp.where(own, _table_lookup(tab_ref, bucket, h), bd)
        bn = jnp.where(rown // 2 == h, tab_ref[0, h], bn)
    bd_ref[...] = bd
    bn_ref[...] = bn


def rel_bias_tiles(rel_bias, past):
    return pl.pallas_call(
        functools.partial(_rel_bias_body, past=past),
        in_specs=[pl.BlockSpec(memory_space=pltpu.SMEM)],
        out_specs=[
            pl.BlockSpec(memory_space=pltpu.VMEM),
            pl.BlockSpec(memory_space=pltpu.VMEM),
            pl.BlockSpec(memory_space=pltpu.VMEM),
        ],
        out_shape=[
            jax.ShapeDtypeStruct((DIFF_HEADS, 3, ATT_TQ, ATT_TK), F32),
            jax.ShapeDtypeStruct((2 * DIFF_HEADS, past * DIFF_HEADS), F32),
            jax.ShapeDtypeStruct((2 * DIFF_HEADS, LANES), F32),
        ],
        compiler_params=pltpu.CompilerParams(vmem_limit_bytes=VMEM_LIMIT),
        name="rel_bias_tiles",
    )(rel_bias)


def _lambda(lam_ref, lam_init):
    lp = lam_ref[...]
    s1 = jnp.sum(lp[0:1, :] * lp[1:2, :], axis=-1, keepdims=True)
    s2 = jnp.sum(lp[2:3, :] * lp[3:4, :], axis=-1, keepdims=True)
    return jnp.exp(s1) - jnp.exp(s2) + lam_init


def _attn_prompt_body(q_ref, k_ref, v_ref, bias_ref, lam_ref, sub_ref, o_ref,
                      vb_ref, qq_ref, m_ref, acc_ref, *, lam_init):
    kb_ref = k_ref
    nq = k_ref.shape[0] // ATT_TQ
    vb_ref[:, 0:LANES] = v_ref[...]
    vb_ref[:, LANES:2 * LANES] = jnp.ones((vb_ref.shape[0], LANES), BF16)
    lane = lax.broadcasted_iota(jnp.int32, (ATT_TQ, LANES), 1)
    for qi in range(nq):
        q = q_ref[qi * ATT_TQ:(qi + 1) * ATT_TQ, :].astype(F32) * (DIFF_HALF ** -0.5)
        qq_ref[qi] = jnp.concatenate([jnp.where(lane < DIFF_HALF, q, 0.0),
                                      jnp.where(lane >= DIFF_HALF, q, 0.0)], axis=0).astype(BF16)
    m_ref[...] = jnp.full(m_ref.shape, -jnp.inf, F32)
    acc_ref[...] = jnp.zeros_like(acc_ref)

    half = ATT_TQ // 2
    for w in range(nq):
        for qi in range(w, nq):
            tile = min(qi - w, 2)
            parts = [(0, half, half), (half, half, ATT_TK)] if tile == 0 else [(0, ATT_TQ, ATT_TK)]
            for idx in range(2):
                for r0, nr, nk in parts:
                    rows = slice(idx * ATT_TQ + r0, idx * ATT_TQ + r0 + nr)
                    kj = kb_ref[w * ATT_TK:w * ATT_TK + nk, :]
                    vj = vb_ref[w * ATT_TK:w * ATT_TK + nk, :]
                    s = (lax.dot_general(qq_ref[qi, rows, :], kj, _NT, preferred_element_type=F32)
                         + bias_ref[tile, r0:r0 + nr, 0:nk])
                    m_prev = m_ref[qi, rows, :]
                    m_new = jnp.maximum(m_prev, jnp.max(s, axis=-1, keepdims=True))
                    alpha = jnp.exp(m_prev - m_new)
                    p = jnp.exp(s - jnp.concatenate([m_new] * (nk // LANES), axis=1))
                    acc_ref[qi, rows, :] = (jnp.concatenate([alpha, alpha], axis=1) * acc_ref[qi, rows, :]
                                            + jnp.dot(p.astype(BF16), vj, preferred_element_type=F32))
                    m_ref[qi, rows, :] = m_new

    lam = _lambda(lam_ref, lam_init)
    for qi in range(nq):
        acc = acc_ref[qi]
        o_all = acc[:, 0:LANES] / acc[:, LANES:2 * LANES]
        o = o_all[0:ATT_TQ, :] - lam * o_all[ATT_TQ:2 * ATT_TQ, :]
        o_ref[qi * ATT_TQ:(qi + 1) * ATT_TQ, :] = (_rms(o, sub_ref[...]) * (1.0 - lam_init)).astype(BF16)


def attn_prompt(qkv, bias_p, lam_params, subln, lam_init):
    b, t, _ = qkv.shape
    nh = DIFF_HEADS
    nq = t // ATT_TQ
    assert ATT_TQ == ATT_TK
    proj = qkv
    return pl.pallas_call(
        functools.partial(_attn_prompt_body, lam_init=lam_init),
        grid=(b, nh),
        in_specs=[
            pl.BlockSpec((None, t, LANES), lambda i, h: (i, 0, h)),
            pl.BlockSpec((None, t, LANES), lambda i, h: (i, 0, nh + h)),
            pl.BlockSpec((None, t, LANES), lambda i, h: (i, 0, 2 * nh + h)),
            pl.BlockSpec((None, 3, ATT_TQ, ATT_TK), lambda i, h: (h, 0, 0, 0)),
            pl.BlockSpec((4, DIFF_HALF), lambda i, h: (0, 0)),
            pl.BlockSpec((1, LANES), lambda i, h: (0, 0)),
        ],
        out_specs=pl.BlockSpec((None, t, LANES), lambda i, h: (i, 0, h)),
        out_shape=jax.ShapeDtypeStruct((b, t, DIFF_WIDTH), BF16),
        scratch_shapes=[
            pltpu.VMEM((t, 2 * LANES), BF16),
            pltpu.VMEM((nq, 2 * ATT_TQ, LANES), BF16),
            pltpu.VMEM((nq, 2 * ATT_TQ, LANES), F32),
            pltpu.VMEM((nq, 2 * ATT_TQ, 2 * LANES), F32),
        ],
        compiler_params=_cparams(("parallel", "parallel")),
        name="attn_prompt",
    )(proj, proj, proj, bias_p, lam_params, subln.reshape(1, LANES))


def _attn_decode_body(pt_ref, x_ref, *rest, n_pages, lam_init):
    k_refs = rest[:n_pages]
    v_refs = rest[n_pages:2 * n_pages]
    bd_ref, bn_ref, lam_ref, sub_ref, o_ref = rest[2 * n_pages:]
    page_rows = PAGE_SIZE * DIFF_HEADS
    lane = lax.broadcasted_iota(jnp.int32, (1, DIFF_HEAD_DIM), 1)
    rows_q, rows_k, rows_v = [], [], []
    for h in range(DIFF_HEADS):
        sl = slice(h * DIFF_HEAD_DIM, (h + 1) * DIFF_HEAD_DIM)
        qh = x_ref[:, DIFF_WIDTH:2 * DIFF_WIDTH][:, sl] * (DIFF_HALF ** -0.5)
        kh = x_ref[:, 2 * DIFF_WIDTH:3 * DIFF_WIDTH][:, sl]
        vh = x_ref[:, 3 * DIFF_WIDTH:4 * DIFF_WIDTH][:, sl]
        rows_q += [jnp.where(lane < DIFF_HALF, qh, 0.0), jnp.where(lane >= DIFF_HALF, qh, 0.0)]
        rows_k += [kh, kh]
        rows_v += [vh, vh]
    q8 = jnp.concatenate(rows_q, axis=0)
    k8 = jnp.concatenate(rows_k, axis=0)
    v8 = jnp.concatenate(rows_v, axis=0)
    q8b = q8.astype(BF16)
    s = jnp.concatenate(
        [lax.dot_general(q8b, k_refs[p][...].astype(BF16), _NT, preferred_element_type=F32)
         for p in range(n_pages)], axis=1) + bd_ref[...]
    s_new = jnp.sum(q8 * k8, axis=-1, keepdims=True) + bn_ref[:, 0:1]
    m = jnp.maximum(jnp.max(s, axis=-1, keepdims=True), s_new)
    p = jnp.exp(s - m)
    p_new = jnp.exp(s_new - m)
    inv_l = 1.0 / (jnp.sum(p, axis=-1, keepdims=True) + p_new)
    a = p * inv_l
    r = (p_new * inv_l) * v8
    for pg in range(n_pages):
        r = r + jnp.dot(a[:, pg * page_rows:(pg + 1) * page_rows].astype(BF16),
                        v_refs[pg][...].astype(BF16), preferred_element_type=F32)
    lam = _lambda(lam_ref, lam_init)
    for h in range(DIFF_HEADS):
        o = r[2 * h:2 * h + 1, :] - lam * r[2 * h + 1:2 * h + 2, :]
        o_ref[:, h * DIFF_HEAD_DIM:(h + 1) * DIFF_HEAD_DIM] = (
            _rms(o, sub_ref[...]) * (1.0 - lam_init)).astype(BF16)


def attn_decode(proj, cache_k, cache_v, page_table, layer_j, bias_d, bias_n, lam_params, subln, lam_init):
    nb = proj.shape[0]
    n_pages = page_table.shape[1]
    n_pool, n_even = cache_k.shape[:2]
    page_rows = PAGE_SIZE * DIFF_HEADS
    ck = cache_k.reshape(n_pool, n_even, page_rows, DIFF_HEAD_DIM)
    cv = cache_v.reshape(n_pool, n_even, page_rows, DIFF_HEAD_DIM)

    def page_spec(p):
        return pl.BlockSpec((None, None, page_rows, DIFF_HEAD_DIM),
                            lambda i, pt: (pt[i * n_pages + p], layer_j, 0, 0))

    full = lambda shape: pl.BlockSpec(shape, lambda i, pt: (0,) * len(shape))
    grid_spec = pltpu.PrefetchScalarGridSpec(
        num_scalar_prefetch=1,
        grid=(nb,),
        in_specs=([pl.BlockSpec((None, 1, EVEN_IN), lambda i, pt: (i, 0, 0))]
                  + [page_spec(p) for p in range(n_pages)] * 2
                  + [full(bias_d.shape), full(bias_n.shape), full((4, DIFF_HALF)), full((1, LANES))]),
        out_specs=pl.BlockSpec((None, 1, DIFF_WIDTH), lambda i, pt: (i, 0, 0)),
    )
    out = pl.pallas_call(
        functools.partial(_attn_decode_body, n_pages=n_pages, lam_init=lam_init),
        grid_spec=grid_spec,
        out_shape=jax.ShapeDtypeStruct((nb, 1, DIFF_WIDTH), BF16),
        compiler_params=_cparams(("arbitrary",)),
        name="attn_decode",
    )(page_table.reshape(-1), proj.reshape(nb, 1, EVEN_IN), *([ck] * n_pages), *([cv] * n_pages),
      bias_d, bias_n, lam_params, subln.reshape(1, LANES))
    return out.reshape(nb, DIFF_WIDTH)


def _blockdiag(x):
    lane = lax.broadcasted_iota(jnp.int32, x.shape, 1)
    zero = jnp.zeros_like(x)
    return jnp.concatenate([jnp.where(lane < GDN_C, x, zero), jnp.where(lane >= GDN_C, x, zero)], axis=0)


def _mmp(a, b):
    return jnp.dot(a.astype(BF16), _blockdiag(b.astype(BF16)), preferred_element_type=F32)


def _unit_lower_inverse(lows, ii, jl):
    eye = jnp.where(ii == jl, 1.0, 0.0)
    in_block = (ii // INV_BLOCK) == (jl // INV_BLOCK)
    ps = [jnp.where(in_block, -low, 0.0) for low in lows]
    offs = [jnp.where(in_block, 0.0, low) for low in lows]
    dinvs = [eye + p for p in ps]
    span = 2
    while span < INV_BLOCK:
        ps = [_mmp(p, p) for p in ps]
        dinvs = [d + _mmp(d, p) for d, p in zip(dinvs, ps)]
        span *= 2
    powers = [[-_mmp(d, off) for d, off in zip(dinvs, offs)]]
    span = 2
    while span < GDN_C // INV_BLOCK:
        powers.append([_mmp(n, n) for n in powers[-1]])
        span *= 2
    xs = dinvs
    for pw in reversed(powers):
        xs = [x + _mmp(n, x) for n, x in zip(pw, xs)]
    return xs


def _gdn_prompt_body(h_ref, gn_ref, w_ref, wgt_ref, cw_ref, alog_ref, dtb_ref, on_ref,
                     o_ref, s_ref, cst_ref, ext_ref, z_ref):
    c = pl.program_id(1)
    nh = GDN_HEADS
    npair = nh // 2
    hk = nh * GDN_DK
    cc = GDN_C
    rows_step = GDN_STEP_CHUNKS * cc
    heads_per_tile = GDN_PROJ_TN // LANES

    @pl.when(c == 0)
    def _():
        s_ref[...] = jnp.zeros_like(s_ref)
        ext_ref[0:8, :] = jnp.zeros((8, GDN_QKV), F32)

    @pl.when(c > 0)
    def _():
        ext_ref[0:8, :] = ext_ref[rows_step:rows_step + 8, :]

    xn = _rms(h_ref[...], gn_ref[...]).astype(BF16)
    abt = lax.dot_general(wgt_ref[...], xn, _NT, preferred_element_type=F32)

    def project(n):
        cols = slice(n * GDN_PROJ_TN, (n + 1) * GDN_PROJ_TN)
        tile = jnp.dot(xn, w_ref[:, cols], preferred_element_type=F32)
        if n < GDN_QKV // GDN_PROJ_TN:
            ext_ref[8:8 + rows_step, cols] = tile
        else:
            z_ref[:, n * GDN_PROJ_TN - GDN_QKV:(n + 1) * GDN_PROJ_TN - GDN_QKV] = tile

    def conv_act(col, r0):
        sl = slice(col, col + LANES)
        acc = ext_ref[8 + r0:8 + r0 + cc, sl] * cw_ref[3:4, sl]
        for i in range(GDN_CONV - 1):
            acc = acc + ext_ref[5 + i + r0:5 + i + r0 + cc, sl] * cw_ref[i:i + 1, sl]
        return _silu(acc)

    def l2n(x):
        return x * lax.rsqrt(jnp.sum(x * x, axis=-1, keepdims=True) + EPS)

    def pair(xs):
        return [jnp.concatenate([xs[2 * p], xs[2 * p + 1]], axis=1) for p in range(npair)]

    def row_pair(x):
        return pair([x[h:h + 1, :] for h in range(nh)])

    project(0)
    lane = lax.broadcasted_iota(jnp.int32, (nh, cc), 1)
    q_p, k_p, v_p, gc_b, beta_b, kdec_b, gc_rows, s_decay = [], [], [], [], [], [], [], []
    for ck in range(GDN_STEP_CHUNKS):
        r0 = ck * cc
        g = -jnp.exp(alog_ref[...]) * _softplus(abt[0:nh, r0:r0 + cc] + dtb_ref[...])
        beta = _sigmoid(abt[nh:2 * nh, r0:r0 + cc])
        gc = g
        shift = 1
        while shift < cc:
            gc = gc + jnp.where(lane >= shift, pltpu.roll(gc, shift, 1), 0.0)
            shift *= 2
        g_last = jnp.broadcast_to(gc[:, cc - 1:cc], (nh, cc))
        s_decay.append(jnp.exp(g_last))
        cols = jnp.transpose(jnp.concatenate([gc, beta, jnp.zeros((LANES - 2 * nh, cc), F32)], axis=0))
        bcast = [jnp.broadcast_to(cols[:, n:n + 1], (cc, cc)) for n in range(2 * nh)]
        gcb = pair(bcast[0:nh])
        gc_b += gcb
        beta_b += pair(bcast[nh:2 * nh])
        kdec_b += [jnp.exp(gl - x) for gl, x in zip(row_pair(g_last), gcb)]
        gc_rows += row_pair(gc)
    gam_b = [jnp.exp(x) for x in gc_b]
    ii = lax.broadcasted_iota(jnp.int32, (cc, 2 * cc), 0)
    jl = lax.broadcasted_iota(jnp.int32, (cc, 2 * cc), 1) % cc
    decay = [jnp.exp(jnp.where(ii >= jl, gb - gr, -jnp.inf)) for gb, gr in zip(gc_b, gc_rows)]

    act = {}
    for n in range(1, GDN_MAIN // GDN_PROJ_TN + 1):
        if n < GDN_MAIN // GDN_PROJ_TN:
            project(n)
        if n <= GDN_QKV // GDN_PROJ_TN:
            for hh in range(heads_per_tile):
                col = (n - 1) * GDN_PROJ_TN + hh * LANES
                for ck in range(GDN_STEP_CHUNKS):
                    a = conv_act(col, ck * cc)
                    if col < hk:
                        a = l2n(a) * (GDN_DK ** -0.5)
                    elif col < 2 * hk:
                        a = l2n(a)
                    act[(col, ck)] = a
    for ck in range(GDN_STEP_CHUNKS):
        q_p += pair([act[(h * GDN_DK, ck)] for h in range(nh)])
        k_p += pair([act[(hk + h * GDN_DK, ck)] for h in range(nh)])
        v_p += pair([act[(2 * hk + h * GDN_DV, ck)] for h in range(nh)])

    gram = [lax.dot_general(jnp.concatenate([q, k], axis=0).astype(BF16), _blockdiag(k.astype(BF16)), _NT,
                            preferred_element_type=F32) for q, k in zip(q_p, k_p)]
    qk = [gm[0:cc, :] * d for gm, d in zip(gram, decay)]
    lows = [jnp.where(ii > jl, b * gm[cc:2 * cc, :] * d, 0.0) for b, gm, d in zip(beta_b, gram, decay)]
    tinv = _unit_lower_inverse(lows, ii, jl)
    w =[_mmp(t, b * gm * k) for t, b, gm, k in zip(tinv, beta_b, gam_b, k_p)]
    u0 = [_mmp(t, b * v) for t, b, v in zip(tinv, beta_b, v_p)]
    wq = [jnp.concatenate([wp, gm * q], axis=0).astype(BF16) for wp, gm, q in zip(w, gam_b, q_p)]
    kd = [(kdb * k).astype(BF16) for kdb, k in zip(kdec_b, k_p)]

    state = [s_ref[h] for h in range(nh)]
    zero = jnp.zeros((GDN_DK, GDN_DV), BF16)
    for ck in range(GDN_STEP_CHUNKS):
        r0 = ck * cc
        ent = range(ck * npair, (ck + 1) * npair)
        s_bd = [jnp.concatenate([jnp.concatenate([state[2 * p].astype(BF16), zero], axis=1),
                                 jnp.concatenate([zero, state[2 * p + 1].astype(BF16)], axis=1)], axis=0)
                for p in range(npair)]
        ws_qs = [jnp.dot(wq[e], sb, preferred_element_type=F32) for e, sb in zip(ent, s_bd)]
        u = [u0[e] - x[0:cc, :] for e, x in zip(ent, ws_qs)]
        o = [x[cc:2 * cc, :] + _mmp(qk[e], b) for e, x, b in zip(ent, ws_qs, u)]
        upd = [lax.dot_general(kd[e], b.astype(BF16), _TN, preferred_element_type=F32)
               for e, b in zip(ent, u)]
        new_state = []
        for h in range(nh):
            half = slice((h % 2) * cc, (h % 2 + 1) * cc)
            new_state.append(s_decay[ck][h:h + 1, :] * state[h] + upd[h // 2][half, half])
            z = z_ref[r0:r0 + cc, h * GDN_DV:(h + 1) * GDN_DV]
            o_ref[r0:r0 + cc, h * GDN_DV:(h + 1) * GDN_DV] = (
                _rms(o[h // 2][:, half], on_ref[...]) * _silu(z)).astype(BF16)
        state = new_state
    for h in range(nh):
        s_ref[h] = state[h]

    @pl.when(c == pl.num_programs(1) - 1)
    def _():
        cst_ref[...] = ext_ref[rows_step:rows_step + 8, :]


def gdn_prompt(h, gain, w_in, w_gates_t, idx, conv_w, a_log, dt_bias, o_norm):
    b, t, d = h.shape
    rows = GDN_STEP_CHUNKS * GDN_C
    nc = t // rows
    nh = GDN_HEADS
    return pl.pallas_call(
        _gdn_prompt_body,
        grid=(b, nc),
        in_specs=[
            pl.BlockSpec((None, rows, d), lambda i, c: (i, c, 0)),
            pl.BlockSpec((1, d), lambda i, c: (0, 0)),
            pl.BlockSpec((None, d, GDN_MAIN), lambda i, c: (idx, 0, 0), pipeline_mode=pl.Buffered(1)),
            pl.BlockSpec((None, LANES, d), lambda i, c: (idx, 0, 0), pipeline_mode=pl.Buffered(1)),
            pl.BlockSpec((GDN_CONV, GDN_QKV), lambda i, c: (0, 0)),
            pl.BlockSpec((nh, 1), lambda i, c: (0, 0)),
            pl.BlockSpec((nh, 1), lambda i, c: (0, 0)),
            pl.BlockSpec((1, GDN_DV), lambda i, c: (0, 0)),
        ],
        out_specs=[
            pl.BlockSpec((None, rows, nh * GDN_DV), lambda i, c: (i, c, 0)),
            pl.BlockSpec((None, nh, GDN_DK, GDN_DV), lambda i, c: (i, 0, 0, 0)),
            pl.BlockSpec((None, 8, GDN_QKV), lambda i, c: (i, 0, 0)),
        ],
        out_shape=[
            jax.ShapeDtypeStruct((b, t, nh * GDN_DV), BF16),
            jax.ShapeDtypeStruct((b, nh, GDN_DK, GDN_DV), F32),
            jax.ShapeDtypeStruct((b, 8, GDN_QKV), F32),
        ],
        scratch_shapes=[pltpu.VMEM((8 + rows, GDN_QKV), F32), pltpu.VMEM((rows, nh * GDN_DV), F32)],
        compiler_params=_cparams(("parallel", "arbitrary")),
        name="gdn_prompt",
    )(h, gain.reshape(1, d), w_in, w_gates_t, conv_w, a_log.reshape(nh, 1), dt_bias.reshape(nh, 1),
      o_norm.reshape(1, GDN_DV))


def _gdn_decode_body(x_ref, cs_ref, ab_ref, cw_ref, alog_ref, dtb_ref, on_ref, s0_ref, *rest, first, layer_j):
    o_ref, s_ref = rest[-2:]
    nh = GDN_HEADS
    pad = jnp.zeros((8 - 2, GDN_DK), F32)
    if first:
        for slot in range(s_ref.shape[0]):
            if slot != layer_j:
                s_ref[slot] = jnp.zeros(s_ref.shape[1:], F32)
    pad7 = jnp.zeros((8 - 1, GDN_DK), F32)
    seqs = range(GDN_DEC_ROWS)
    q8, k8, v8, gam8, beta8, qk8 = [], [], [], [], [], []
    for r in seqs:
        x = x_ref[r, 0:3 * nh, :]
        conv = x * cw_ref[GDN_CONV - 1]
        for i in range(GDN_CONV - 1):
            conv = conv + cs_ref[r, i] * cw_ref[i]
        act = conv * _sigmoid(conv)
        qa, ka = act[0:nh], act[nh:2 * nh]
        q8.append(qa * lax.rsqrt(jnp.sum(qa * qa, axis=-1, keepdims=True) + EPS) * (GDN_DK ** -0.5))
        k8.append(ka * lax.rsqrt(jnp.sum(ka * ka, axis=-1, keepdims=True) + EPS))
        v8.append(act[2 * nh:3 * nh])
        g = -jnp.exp(alog_ref[...]) * _softplus(ab_ref[r, 0:nh, :] + dtb_ref[...])
        gam8.append(jnp.broadcast_to(jnp.exp(g), (nh, GDN_DV)))
        beta8.append(jnp.broadcast_to(_sigmoid(ab_ref[r, nh:2 * nh, :]), (nh, GDN_DV)))
        qk8.append(jnp.broadcast_to(jnp.sum(q8[r] * k8[r], axis=-1, keepdims=True), (nh, GDN_DV)))
    pairs = [(r, h) for r in seqs for h in range(nh)]
    s_old = [s0_ref[r, h] for r, h in pairs]
    qk_s = [jnp.dot(jnp.concatenate([q8[r][h:h + 1, :], k8[r][h:h + 1, :], pad], axis=0).astype(BF16),
                    s.astype(BF16), preferred_element_type=F32) for (r, h), s in zip(pairs, s_old)]
    u = [beta8[r][h:h + 1, :] * (v8[r][h:h + 1, :] - gam8[r][h:h + 1, :] * x[1:2, :])
         for (r, h), x in zip(pairs, qk_s)]
    outs = [gam8[r][h:h + 1, :] * x[0:1, :] + qk8[r][h:h + 1, :] * b for (r, h), x, b in zip(pairs, qk_s, u)]
    outer = [lax.dot_general(jnp.concatenate([k8[r][h:h + 1, :], pad7], axis=0).astype(BF16),
                             jnp.concatenate([b, pad7], axis=0).astype(BF16), _TN, preferred_element_type=F32)
             for (r, h), b in zip(pairs, u)]
    for (r, h), s, x in zip(pairs, s_old, outer):
        new = gam8[r][h:h + 1, :] * s + x
        if first:
            s_ref[layer_j, r, h] = new
        else:
            s_ref[r, h] = new
    for r in seqs:
        o8 = jnp.concatenate(outs[r * nh:(r + 1) * nh], axis=0)
        z8 = x_ref[r, 3 * nh:4 * nh, :]
        o_ref[r] = (_rms(o8, on_ref[...]) * (z8 * _sigmoid(z8))).astype(BF16)


def gdn_decode(proj, gates, conv_state, s0_all, layer_j, s_new_all, conv_w, a_log, dt_bias, o_norm):
    nb = proj.shape[0]
    nh = GDN_HEADS
    nrow = GDN_QKV // LANES
    first = s_new_all is None
    in_specs = [
        pl.BlockSpec((GDN_DEC_ROWS, GDN_MAIN // LANES, LANES), lambda i: (i, 0, 0)),
        pl.BlockSpec((GDN_DEC_ROWS, GDN_CONV - 1, nrow, LANES), lambda i: (i, 0, 0, 0)),
        pl.BlockSpec((GDN_DEC_ROWS, 2 * nh, 1), lambda i: (i, 0, 0)),
        pl.BlockSpec((GDN_CONV, nrow, LANES), lambda i: (0, 0, 0)),
        pl.BlockSpec((nh, 1), lambda i: (0, 0)),
        pl.BlockSpec((nh, 1), lambda i: (0, 0)),
        pl.BlockSpec((1, GDN_DV), lambda i: (0, 0)),
        pl.BlockSpec((None, GDN_DEC_ROWS, nh, GDN_DK, GDN_DV), lambda i: (layer_j, i, 0, 0, 0)),
    ]
    args = [proj.reshape(nb, GDN_MAIN // LANES, LANES),
            conv_state.reshape(nb, GDN_CONV - 1, nrow, LANES),
            gates.reshape(nb, 2 * nh, 1),
            conv_w.reshape(GDN_CONV, nrow, LANES),
            a_log.reshape(nh, 1), dt_bias.reshape(nh, 1), o_norm.reshape(1, GDN_DV), s0_all]
    if first:
        s_spec = pl.BlockSpec((s0_all.shape[0], GDN_DEC_ROWS, nh, GDN_DK, GDN_DV), lambda i: (0, i, 0, 0, 0))
        aliases = {}
    else:
        in_specs.append(pl.BlockSpec(memory_space=pl.ANY))
        args.append(s_new_all)
        s_spec = pl.BlockSpec((None, GDN_DEC_ROWS, nh, GDN_DK, GDN_DV), lambda i: (layer_j, i, 0, 0, 0))
        aliases = {len(args) - 1: 1}
    o, s = pl.pallas_call(
        functools.partial(_gdn_decode_body, first=first, layer_j=layer_j),
        grid=(nb // GDN_DEC_ROWS,),
        in_specs=in_specs,
        out_specs=[pl.BlockSpec((GDN_DEC_ROWS, nh, GDN_DV), lambda i: (i, 0, 0)), s_spec],
        out_shape=[
            jax.ShapeDtypeStruct((nb, nh, GDN_DV), BF16),
            jax.ShapeDtypeStruct(s0_all.shape, F32),
        ],
        input_output_aliases=aliases,
        compiler_params=_cparams(("parallel",)),
        name="gdn_decode",
    )(*args)
    return o.reshape(nb, nh * GDN_DV), s


def kernel(x_prompt, x_sample, cache_k, cache_v, page_table, state_pool, state_conv, state_delta,
           norm_mix, norm_ffn, norm_final, rel_bias,
           w_in_even, pool_w, pool_scale, lambda_q1, lambda_k1, lambda_q2, lambda_k2, subln_w, w_out_even,
           w_in_odd, conv_w, a_log, dt_bias, o_norm, w_out_odd,
           w_gate_up, w_down):
    bp, t, d = x_prompt.shape
    bs = x_sample.shape[0]
    mp = bp * t
    nh = GDN_HEADS
    past = page_table.shape[1] * PAGE_SIZE
    tm_p = 1024
    tm_s = bs

    hp = x_prompt.reshape(mp, d)
    hs = x_sample.reshape(bs, d)
    bias_p, bias_d, bias_n = rel_bias_tiles(rel_bias, past)

    w_in_e = w_in_even.astype(BF16)
    w_in_o = w_in_odd[:, :, 0:GDN_MAIN].astype(BF16)
    w_gates_t = jnp.zeros((w_in_odd.shape[0], LANES, d), F32).at[:, 0:2 * nh, :].set(
        jnp.transpose(w_in_odd[:, :, GDN_MAIN:], (0, 2, 1))).astype(BF16)
    w_out_e = w_out_even.astype(BF16)
    w_out_o = w_out_odd.astype(BF16)
    w_gu = w_gate_up.astype(BF16)
    w_d = w_down.astype(BF16)
    pool_wb = pool_w.astype(BF16)

    k_p, v_p, k_s, v_s, pool_p, pool_s = [], [], [], [], [], []
    conv_p, conv_s, delta_p = [], [], []
    delta_s = None
    for layer in range(DEPTH):
        j = layer // 2
        last = layer == DEPTH - 1
        if layer % 2 == 0:
            w_out = w_out_e
            pw = pool_wb[j]
            lam_init = 0.8 - 0.6 * math.exp(-0.3 * layer)
            lam_params = jnp.stack([lambda_q1[j], lambda_k1[j], lambda_q2[j], lambda_k2[j]])

            qkv_p, ypool_p, kp, vp, tail_p = even_proj_prompt(hp, norm_mix[layer], w_in_e, j, pw, pool_scale[j],
                                                              t, tm=tm_p)
            proj_s, ks_, vs_ = norm_matmul(hs, norm_mix[layer], w_in_e, j, tm=tm_s, tn=DIFF_WIDTH)

            oatt_p = attn_prompt(qkv_p.reshape(bp, t, 3 * DIFF_WIDTH), bias_p, lam_params, subln_w[j], lam_init)
            ypool_s = pool_decode(jnp.transpose(state_pool[j], (1, 0, 2)), proj_s, pw, pool_scale[j])
            oatt_s = attn_decode(proj_s, cache_k, cache_v, page_table, j, bias_d, bias_n,
                                 lam_params, subln_w[j], lam_init)

            kv_shape = (DIFF_HEADS, DIFF_HEAD_DIM)
            k_p.append(kp.reshape(bp, t, *kv_shape))
            v_p.append(vp.reshape(bp, t, *kv_shape))
            k_s.append(ks_.reshape(bs, 1, *kv_shape))
            v_s.append(vs_.reshape(bs, 1, *kv_shape))
            pool_p.append(tail_p[:, POOL_PAD - POOL_BUF:, :])
            pool_s.append(jnp.concatenate([state_pool[j][:, 1:], proj_s[:, None, 0:POOL_WIDTH]], axis=1))

            mix_p = ((ypool_p, 0), (oatt_p.reshape(mp, DIFF_WIDTH), 0))
            mix_s = ((ypool_s, 0), (oatt_s, 0))
        else:
            w_out = w_out_o

            proj_s, gates_s = norm_matmul(hs, norm_mix[layer], w_in_o, j, w_gates_t, tm=tm_s, tn=512)

            o_p, s_p, tail_p = gdn_prompt(hp.reshape(bp, t, d), norm_mix[layer], w_in_o, w_gates_t, j,
                                          conv_w[j], a_log[j], dt_bias[j], o_norm[j])
            o_s, delta_s = gdn_decode(proj_s, gates_s[0:2 * nh].T, state_conv[j], state_delta, j, delta_s,
                                      conv_w[j], a_log[j], dt_bias[j], o_norm[j])

            conv_p.append(tail_p[:, 8 - (GDN_CONV - 1):, :])
            conv_s.append(jnp.concatenate([state_conv[j][:, 1:], proj_s[:, None, 0:GDN_QKV]], axis=1))
            delta_p.append(s_p)

            o_p = o_p.reshape(mp, nh * GDN_DV)
            mix_p = ((o_p, 0), (o_p, 1))
            mix_s = ((o_s, 0), (o_s, 1))

        g_fin = norm_final if last else None
        hp = outproj_ffn(hp, mix_p[0], mix_p[1], w_out, j, norm_ffn[layer], w_gu, w_d, layer, g_fin, tm=FFN_TM)
        hs = outproj_ffn(hs, mix_s[0], mix_s[1], w_out, j, norm_ffn[layer], w_gu, w_d, layer, g_fin, tm=tm_s)

    return (hp.reshape(bp, t, d), hs.reshape(bs, 1, d),
            jnp.stack(k_p, axis=1), jnp.stack(v_p, axis=1), jnp.stack(k_s, axis=1), jnp.stack(v_s, axis=1),
            jnp.stack(pool_p), jnp.stack(pool_s), jnp.stack(conv_p), jnp.stack(conv_s),
            jnp.stack(delta_p), delta_s)
```

```python
import functools
import math

import jax
import jax.numpy as jnp
from jax import lax
from jax.experimental import pallas as pl
from jax.experimental.pallas import tpu as pltpu

F32 = jnp.float32
BF16 = jnp.bfloat16

D_MODEL = 1024
DEPTH = 4
PAGE_SIZE = 128
POOL_WIDTH = 512
POOL_WINDOWS = (2, 4, 8, 16)
POOL_GROUP = 128
POOL_BUF = 15
DIFF_HEADS = 4
DIFF_HALF = 64
DIFF_HEAD_DIM = 128
DIFF_WIDTH = 512
EVEN_IN = 2048
REL_BUCKETS = 32
REL_MAX_DIST = 128
GDN_HEADS = 8
GDN_DK = 128
GDN_DV = 128
GDN_CONV = 4
GDN_QKV = 3072
GDN_MAIN = 4096
D_FF = 2816
EPS = 1e-6

LANES = 128
VMEM_LIMIT = 48 * 1024 * 1024
NEG_BIG = -1e30

ATT_TQ = 512
ATT_TK = 512
GDN_C = 128
GDN_STEP_CHUNKS = 2
GDN_DEC_ROWS = 4
GDN_PROJ_TN = 512
INV_BLOCK = 16
FFN_TF = 256
FFN_TM = 512

_NT = (((1,), (1,)), ((), ()))
_TN = (((0,), (0,)), ((), ()))


def _cparams(sem):
    return pltpu.CompilerParams(dimension_semantics=sem, vmem_limit_bytes=VMEM_LIMIT)


def _sigmoid(x):
    return 1.0 / (1.0 + jnp.exp(-x))


def _silu(x):
    h = 0.5 * x
    return h + h * jnp.tanh(h)


def _softplus(x):
    return jnp.maximum(x, 0.0) + jnp.log1p(jnp.exp(-jnp.abs(x)))


def _rms(x, gain):
    return x * lax.rsqrt(jnp.mean(x * x, axis=-1, keepdims=True) + EPS) * gain


def _mm(a, b):
    return jnp.dot(a.astype(BF16), b.astype(BF16), preferred_element_type=F32)


def _norm_mm_kv_body(x_ref, g_ref, w_ref, o_ref, k_ref, v_ref, xn_ref):
    j = pl.program_id(1)

    @pl.when(j == 0)
    def _():
        xn_ref[...] = _rms(x_ref[...], g_ref[...]).astype(BF16)

    res = jnp.dot(xn_ref[...], w_ref[...], preferred_element_type=F32)
    o_ref[...] = res

    def head_rows(dst_ref):
        for h in range(DIFF_HEADS):
            dst_ref[pl.ds(h, res.shape[0], stride=DIFF_HEADS), :] = res[:, h * DIFF_HEAD_DIM:(h + 1) * DIFF_HEAD_DIM]

    @pl.when(j == 2)
    def _():
        head_rows(k_ref)

    @pl.when(j == 3)
    def _():
        head_rows(v_ref)


def _norm_mm_gate_body(x_ref, g_ref, w_ref, wst_ref, o_ref, ost_ref, xn_ref):
    @pl.when(pl.program_id(1) == 0)
    def _():
        xn = _rms(x_ref[...], g_ref[...]).astype(BF16)
        xn_ref[...] = xn
        ost_ref[...] = lax.dot_general(wst_ref[...], xn, _NT, preferred_element_type=F32)

    o_ref[...] = jnp.dot(xn_ref[...], w_ref[...], preferred_element_type=F32)


def norm_matmul(x, gain, w, idx, w_gates_t=None, *, tm, tn):
    m, d = x.shape
    n = EVEN_IN if w_gates_t is None else GDN_MAIN
    grid = (m // tm, n // tn)
    in_specs = [
        pl.BlockSpec((tm, d), lambda i, j: (i, 0)),
        pl.BlockSpec((1, d), lambda i, j: (0, 0)),
        pl.BlockSpec((None, d, tn), lambda i, j: (idx, 0, j)),
    ]
    out_specs = pl.BlockSpec((tm, tn), lambda i, j: (i, j))
    out_shape = jax.ShapeDtypeStruct((m, n), F32)
    args = [x, gain.reshape(1, d), w]
    if w_gates_t is None:
        assert tn == DIFF_WIDTH and n == EVEN_IN
        kv_spec = pl.BlockSpec((tm * DIFF_HEADS, DIFF_HEAD_DIM), lambda i, j: (i, 0))
        kv_shape = jax.ShapeDtypeStruct((m * DIFF_HEADS, DIFF_HEAD_DIM), F32)
        out_specs = [out_specs, kv_spec, kv_spec]
        out_shape = [out_shape, kv_shape, kv_shape]
        body = _norm_mm_kv_body
    else:
        in_specs.append(pl.BlockSpec((None, LANES, d), lambda i, j: (idx, 0, 0)))
        out_specs = [out_specs, pl.BlockSpec((LANES, tm), lambda i, j: (0, i))]
        out_shape = [out_shape, jax.ShapeDtypeStruct((LANES, m), F32)]
        args.append(w_gates_t)
        body = _norm_mm_gate_body
    return pl.pallas_call(
        body,
        grid=grid,
        in_specs=in_specs,
        out_specs=out_specs,
        out_shape=out_shape,
        scratch_shapes=[pltpu.VMEM((tm, d), BF16)],
        compiler_params=_cparams(("parallel", "arbitrary")),
        name="norm_matmul",
    )(*args)


def _ffn_body(*refs, final):
    if final:
        res_ref, a0_ref, a1_ref, wo_ref, gf_ref, wgu_ref, wd_ref, gfin_ref, o_ref = refs
    else:
        res_ref, a0_ref, a1_ref, wo_ref, gf_ref, wgu_ref, wd_ref, o_ref = refs
    mixed = jnp.concatenate([a0_ref[...], a1_ref[...]], axis=1)
    h1 = res_ref[...] + jnp.dot(mixed, wo_ref[...], preferred_element_type=F32)
    xn = _rms(h1, gf_ref[...]).astype(BF16)
    acts = []
    for f in range(D_FF // FFN_TF):
        g = jnp.dot(xn, wgu_ref[:, f * FFN_TF:(f + 1) * FFN_TF], preferred_element_type=F32)
        u = jnp.dot(xn, wgu_ref[:, D_FF + f * FFN_TF:D_FF + (f + 1) * FFN_TF], preferred_element_type=F32)
        acts.append((g * _sigmoid(g) * u).astype(BF16))
    y = h1 + jnp.dot(jnp.concatenate(acts, axis=1), wd_ref[...], preferred_element_type=F32)
    if final:
        y = _rms(y, gfin_ref[...])
    o_ref[...] = y


def outproj_ffn(res, mix0, mix1, w_out, idx_out, g_ffn, w_gu, w_d, layer, g_final=None, *, tm):
    m, d = res.shape
    half = w_out.shape[1] // 2
    (m0, c0), (m1, c1) = mix0, mix1
    final = g_final is not None
    once = pl.Buffered(1)
    in_specs = [
        pl.BlockSpec((tm, d), lambda i: (i, 0)),
        pl.BlockSpec((tm, half), lambda i: (i, c0)),
        pl.BlockSpec((tm, half), lambda i: (i, c1)),
        pl.BlockSpec((None, 2 * half, d), lambda i: (idx_out, 0, 0), pipeline_mode=once),
        pl.BlockSpec((1, d), lambda i: (0, 0)),
        pl.BlockSpec((None, d, 2 * D_FF), lambda i: (layer, 0, 0), pipeline_mode=once),
        pl.BlockSpec((None, D_FF, d), lambda i: (layer, 0, 0), pipeline_mode=once),
    ]
    args = [res, m0, m1, w_out, g_ffn.reshape(1, d), w_gu, w_d]
    if final:
        in_specs.append(pl.BlockSpec((1, d), lambda i: (0, 0)))
        args.append(g_final.reshape(1, d))
    return pl.pallas_call(
        functools.partial(_ffn_body, final=final),
        grid=(m // tm,),
        in_specs=in_specs,
        out_specs=pl.BlockSpec((tm, d), lambda i: (i, 0)),
        out_shape=jax.ShapeDtypeStruct((m, d), F32),
        compiler_params=_cparams(("parallel",)),
        name="outproj_ffn",
    )(*args)


POOL_PAD = 16
POOL_ROWS = 512


def _even_proj_body(x_ref, g_ref, w_ref, pw_ref, ps_ref, qkv_ref, yp_ref, k_ref, v_ref, tail_ref,
                    xn_ref, ext_ref, *, blocks_per_seq):
    i = pl.program_id(0)
    j = pl.program_id(1)
    tm = x_ref.shape[0]

    @pl.when(j == 0)
    def _():
        xn_ref[...] = _rms(x_ref[...], g_ref[...]).astype(BF16)

    res = jnp.dot(xn_ref[...], w_ref[...], preferred_element_type=F32)

    blk = i % blocks_per_seq

    @pl.when((j == 0) & (blk == 0))
    def _():
        ext_ref[0:POOL_PAD, :] = jnp.zeros((POOL_PAD, POOL_WIDTH), F32)

    @pl.when((j == 0) & (blk > 0))
    def _():
        ext_ref[0:POOL_PAD, :] = ext_ref[tm:tm + POOL_PAD, :]

    @pl.when(j == 0)
    def _():
        ext_ref[POOL_PAD:, :] = res
        tail_ref[...] = res[tm - POOL_PAD:tm, :]
        for r0 in range(0, tm, POOL_ROWS):
            pos = lax.broadcasted_iota(jnp.int32, (POOL_ROWS, 1), 0) + (blk * tm + r0)
            for g, w in enumerate(POOL_WINDOWS):
                sl = slice(g * POOL_GROUP, (g + 1) * POOL_GROUP)
                cur = ext_ref[POOL_PAD + r0:POOL_PAD + r0 + POOL_ROWS, sl]
                win = cur
                for s in range(1, w):
                    win = win + ext_ref[POOL_PAD + r0 - s:POOL_PAD + r0 - s + POOL_ROWS, sl]
                cnt = jnp.minimum(pos + 1, w).astype(F32)
                dlt = win / cnt - cur
                y = jnp.dot(dlt.astype(BF16), pw_ref[g], preferred_element_type=F32) * ps_ref[:, sl]
                yp_ref[r0:r0 + POOL_ROWS, sl] = y.astype(BF16)

    @pl.when(j >= 1)
    def _():
        qkv_ref[...] = res.astype(BF16)

    def head_rows(dst_ref):
        for h in range(DIFF_HEADS):
            dst_ref[pl.ds(h, tm, stride=DIFF_HEADS), :] = res[:, h * DIFF_HEAD_DIM:(h + 1) * DIFF_HEAD_DIM]

    @pl.when(j == 2)
    def _():
        head_rows(k_ref)

    @pl.when(j == 3)
    def _():
        head_rows(v_ref)


def even_proj_prompt(x, gain, w, idx, pool_w, pool_scale, seq_len, *, tm):
    m, d = x.shape
    tn = DIFF_WIDTH
    blocks_per_seq = seq_len // tm
    nseq = m // seq_len
    kv_spec = pl.BlockSpec((tm * DIFF_HEADS, DIFF_HEAD_DIM), lambda i, j: (i, 0))
    kv_shape = jax.ShapeDtypeStruct((m * DIFF_HEADS, DIFF_HEAD_DIM), F32)
    return pl.pallas_call(
        functools.partial(_even_proj_body, blocks_per_seq=blocks_per_seq),
        grid=(m // tm, EVEN_IN // tn),
        in_specs=[
            pl.BlockSpec((tm, d), lambda i, j: (i, 0)),
            pl.BlockSpec((1, d), lambda i, j: (0, 0)),
            pl.BlockSpec((None, d, tn), lambda i, j: (idx, 0, j)),
            pl.BlockSpec((len(POOL_WINDOWS), POOL_GROUP, POOL_GROUP), lambda i, j: (0, 0, 0)),
            pl.BlockSpec((1, POOL_WIDTH), lambda i, j: (0, 0)),
        ],
        out_specs=[
            pl.BlockSpec((tm, tn), lambda i, j: (i, jnp.maximum(j - 1, 0))),
            pl.BlockSpec((tm, POOL_WIDTH), lambda i, j: (i, 0)),
            kv_spec,
            kv_spec,
            pl.BlockSpec((None, POOL_PAD, POOL_WIDTH), lambda i, j: (i // blocks_per_seq, 0, 0)),
        ],
        out_shape=[
            jax.ShapeDtypeStruct((m, 3 * DIFF_WIDTH), BF16),
            jax.ShapeDtypeStruct((m, POOL_WIDTH), BF16),
            kv_shape,
            kv_shape,
            jax.ShapeDtypeStruct((nseq, POOL_PAD, POOL_WIDTH), F32),
        ],
        scratch_shapes=[pltpu.VMEM((tm, d), BF16), pltpu.VMEM((POOL_PAD + tm, POOL_WIDTH), F32)],
        compiler_params=_cparams(("arbitrary", "arbitrary")),
        name="even_proj",
    )(x, gain.reshape(1, d), w, pool_w, pool_scale.reshape(1, POOL_WIDTH))


def _pool_decode_body(st_ref, u_ref, pw_ref, ps_ref, o_ref):
    for g, w in enumerate(POOL_WINDOWS):
        sl = slice(g * POOL_GROUP, (g + 1) * POOL_GROUP)
        cur = u_ref[:, sl]
        win = cur
        for i in range(1, w):
            win = win + st_ref[POOL_BUF - i, :, sl]
        dlt = win / float(w) - cur
        y = jnp.dot(dlt.astype(BF16), pw_ref[g], preferred_element_type=F32) * ps_ref[:, sl]
        o_ref[:, sl] = y.astype(BF16)


def pool_decode(state_t, proj, pool_w, pool_scale):
    nb = proj.shape[0]
    return pl.pallas_call(
        _pool_decode_body,
        grid=(1,),
        in_specs=[
            pl.BlockSpec((POOL_BUF, nb, POOL_WIDTH), lambda i: (0, 0, 0)),
            pl.BlockSpec((nb, POOL_WIDTH), lambda i: (0, 0)),
            pl.BlockSpec((len(POOL_WINDOWS), POOL_GROUP, POOL_GROUP), lambda i: (0, 0, 0)),
            pl.BlockSpec((1, POOL_WIDTH), lambda i: (0, 0)),
        ],
        out_specs=pl.BlockSpec((nb, POOL_WIDTH), lambda i: (0, 0)),
        out_shape=jax.ShapeDtypeStruct((nb, POOL_WIDTH), BF16),
        compiler_params=_cparams(("arbitrary",)),
        name="pool_decode",
    )(state_t, proj, pool_w, pool_scale.reshape(1, POOL_WIDTH))


def _rel_bucket(n):
    max_exact = REL_BUCKETS // 2
    nf = jnp.maximum(n, 1).astype(F32)
    large = max_exact + (jnp.log(nf / max_exact) / math.log(REL_MAX_DIST / max_exact)
                         * (REL_BUCKETS - max_exact)).astype(jnp.int32)
    large = jnp.minimum(large, REL_BUCKETS - 1)
    return jnp.where(n < max_exact, n, large)


def _table_lookup(tab_ref, bucket, h):
    out = jnp.zeros(bucket.shape, F32)
    for b in range(REL_BUCKETS):
        out = jnp.where(bucket == b, tab_ref[b, h], out)
    return out


def _rel_bias_body(tab_ref, bp_ref, bd_ref, bn_ref, *, past):
    ii = lax.broadcasted_iota(jnp.int32, (ATT_TQ, ATT_TK), 0)
    jj = lax.broadcasted_iota(jnp.int32, (ATT_TQ, ATT_TK), 1)
    for h in range(DIFF_HEADS):
        for t in range(3):
            dist = t * ATT_TK + ii - jj
            bias = _table_lookup(tab_ref, _rel_bucket(jnp.maximum(dist, 0)), h)
            bp_ref[h, t] = jnp.where(dist >= 0, bias, NEG_BIG)
    nrow = 2 * DIFF_HEADS
    row = lax.broadcasted_iota(jnp.int32, (nrow, past * DIFF_HEADS), 0)
    col = lax.broadcasted_iota(jnp.int32, (nrow, past * DIFF_HEADS), 1)
    bucket = _rel_bucket(past - col // DIFF_HEADS)
    rown = lax.broadcasted_iota(jnp.int32, (nrow, LANES), 0)
    bd = jnp.full((nrow, past * DIFF_HEADS), NEG_BIG, F32)
    bn = jnp.zeros((nrow, LANES), F32)
    for h in range(DIFF_HEADS):
        own = jnp.where(row // 2 == h, col % DIFF_HEADS, -1) == h
        bd = jnp.where(own, _table_lookup(tab_ref, bucket, h), bd)
        bn = jnp.where(rown // 2 == h, tab_ref[0, h], bn)
    bd_ref[...] = bd
    bn_ref[...] = bn


def rel_bias_tiles(rel_bias, past):
    return pl.pallas_call(
        functools.partial(_rel_bias_body, past=past),
        in_specs=[pl.BlockSpec(memory_space=pltpu.SMEM)],
        out_specs=[
            pl.BlockSpec(memory_space=pltpu.VMEM),
            pl.BlockSpec(memory_space=pltpu.VMEM),
            pl.BlockSpec(memory_space=pltpu.VMEM),
        ],
        out_shape=[
            jax.ShapeDtypeStruct((DIFF_HEADS, 3, ATT_TQ, ATT_TK), F32),
            jax.ShapeDtypeStruct((2 * DIFF_HEADS, past * DIFF_HEADS), F32),
            jax.ShapeDtypeStruct((2 * DIFF_HEADS, LANES), F32),
        ],
        compiler_params=pltpu.CompilerParams(vmem_limit_bytes=VMEM_LIMIT),
        name="rel_bias_tiles",
    )(rel_bias)


def _lambda(lam_ref, lam_init):
    lp = lam_ref[...]
    s1 = jnp.sum(lp[0:1, :] * lp[1:2, :], axis=-1, keepdims=True)
    s2 = jnp.sum(lp[2:3, :] * lp[3:4, :], axis=-1, keepdims=True)
    return jnp.exp(s1) - jnp.exp(s2) + lam_init


def _attn_prompt_body(q_ref, k_ref, v_ref, bias_ref, lam_ref, sub_ref, o_ref,
                      vb_ref, qq_ref, m_ref, acc_ref, *, lam_init):
    kb_ref = k_ref
    nq = k_ref.shape[0] // ATT_TQ
    vb_ref[:, 0:LANES] = v_ref[...]
    vb_ref[:, LANES:2 * LANES] = jnp.ones((vb_ref.shape[0], LANES), BF16)
    lane = lax.broadcasted_iota(jnp.int32, (ATT_TQ, LANES), 1)
    for qi in range(nq):
        q = q_ref[qi * ATT_TQ:(qi + 1) * ATT_TQ, :].astype(F32) * (DIFF_HALF ** -0.5)
        qq_ref[qi] = jnp.concatenate([jnp.where(lane < DIFF_HALF, q, 0.0),
                                      jnp.where(lane >= DIFF_HALF, q, 0.0)], axis=0).astype(BF16)
    m_ref[...] = jnp.full(m_ref.shape, -jnp.inf, F32)
    acc_ref[...] = jnp.zeros_like(acc_ref)

    half = ATT_TQ // 2
    for w in range(nq):
        pieces = []
        for qi in range(w, nq):
            tile = min(qi - w, 2)
            parts = [(0, half, half), (half, half, ATT_TK)] if tile == 0 else [(0, ATT_TQ, ATT_TK)]
            for idx in range(2):
                for r0, nr, nk in parts:
                    pieces.append((qi, slice(idx * ATT_TQ + r0, idx * ATT_TQ + r0 + nr), nk, tile, r0))
        s_l = [lax.dot_general(qq_ref[qi, rows, :], kb_ref[w * ATT_TK:w * ATT_TK + nk, :], _NT,
                               preferred_element_type=F32) + bias_ref[tile, r0:r0 + rows.stop - rows.start, 0:nk]
               for qi, rows, nk, tile, r0 in pieces]
        m_prev = [m_ref[qi, rows, :] for qi, rows, _, _, _ in pieces]
        m_new = [jnp.maximum(mp, jnp.max(s, axis=-1, keepdims=True)) for mp, s in zip(m_prev, s_l)]
        p_l = [jnp.exp(s - jnp.concatenate([mn] * (pc[2] // LANES), axis=1)).astype(BF16)
               for s, mn, pc in zip(s_l, m_new, pieces)]
        for (qi, rows, nk, _, _), mp, mn, p in zip(pieces, m_prev, m_new, p_l):
            alpha = jnp.exp(mp - mn)
            acc_ref[qi, rows, :] = (jnp.concatenate([alpha, alpha], axis=1) * acc_ref[qi, rows, :]
                                    + jnp.dot(p, vb_ref[w * ATT_TK:w * ATT_TK + nk, :],
                                              preferred_element_type=F32))
            m_ref[qi, rows, :] = mn

    lam = _lambda(lam_ref, lam_init)
    for qi in range(nq):
        acc = acc_ref[qi]
        o_all = acc[:, 0:LANES] / acc[:, LANES:2 * LANES]
        o = o_all[0:ATT_TQ, :] - lam * o_all[ATT_TQ:2 * ATT_TQ, :]
        o_ref[qi * ATT_TQ:(qi + 1) * ATT_TQ, :] = (_rms(o, sub_ref[...]) * (1.0 - lam_init)).astype(BF16)


def attn_prompt(qkv, bias_p, lam_params, subln, lam_init):
    b, t, _ = qkv.shape
    nh = DIFF_HEADS
    nq = t // ATT_TQ
    assert ATT_TQ == ATT_TK
    proj = qkv
    return pl.pallas_call(
        functools.partial(_attn_prompt_body, lam_init=lam_init),
        grid=(b, nh),
        in_specs=[
            pl.BlockSpec((None, t, LANES), lambda i, h: (i, 0, h)),
            pl.BlockSpec((None, t, LANES), lambda i, h: (i, 0, nh + h)),
            pl.BlockSpec((None, t, LANES), lambda i, h: (i, 0, 2 * nh + h)),
            pl.BlockSpec((None, 3, ATT_TQ, ATT_TK), lambda i, h: (h, 0, 0, 0)),
            pl.BlockSpec((4, DIFF_HALF), lambda i, h: (0, 0)),
            pl.BlockSpec((1, LANES), lambda i, h: (0, 0)),
        ],
        out_specs=pl.BlockSpec((None, t, LANES), lambda i, h: (i, 0, h)),
        out_shape=jax.ShapeDtypeStruct((b, t, DIFF_WIDTH), BF16),
        scratch_shapes=[
            pltpu.VMEM((t, 2 * LANES), BF16),
            pltpu.VMEM((nq, 2 * ATT_TQ, LANES), BF16),
            pltpu.VMEM((nq, 2 * ATT_TQ, LANES), F32),
            pltpu.VMEM((nq, 2 * ATT_TQ, 2 * LANES), F32),
        ],
        compiler_params=_cparams(("parallel", "parallel")),
        name="attn_prompt",
    )(proj, proj, proj, bias_p, lam_params, subln.reshape(1, LANES))


def _attn_decode_body(pt_ref, x_ref, *rest, n_pages, lam_init):
    k_refs = rest[:n_pages]
    v_refs = rest[n_pages:2 * n_pages]
    bd_ref, bn_ref, lam_ref, sub_ref, o_ref = rest[2 * n_pages:]
    page_rows = PAGE_SIZE * DIFF_HEADS
    lane = lax.broadcasted_iota(jnp.int32, (1, DIFF_HEAD_DIM), 1)
    rows_q, rows_k, rows_v = [], [], []
    for h in range(DIFF_HEADS):
        sl = slice(h * DIFF_HEAD_DIM, (h + 1) * DIFF_HEAD_DIM)
        qh = x_ref[:, DIFF_WIDTH:2 * DIFF_WIDTH][:, sl] * (DIFF_HALF ** -0.5)
        kh = x_ref[:, 2 * DIFF_WIDTH:3 * DIFF_WIDTH][:, sl]
        vh = x_ref[:, 3 * DIFF_WIDTH:4 * DIFF_WIDTH][:, sl]
        rows_q += [jnp.where(lane < DIFF_HALF, qh, 0.0), jnp.where(lane >= DIFF_HALF, qh, 0.0)]
        rows_k += [kh, kh]
        rows_v += [vh, vh]
    q8 = jnp.concatenate(rows_q, axis=0)
    k8 = jnp.concatenate(rows_k, axis=0)
    v8 = jnp.concatenate(rows_v, axis=0)
    q8b = q8.astype(BF16)
    s = jnp.concatenate(
        [lax.dot_general(q8b, k_refs[p][...].astype(BF16), _NT, preferred_element_type=F32)
         for p in range(n_pages)], axis=1) + bd_ref[...]
    s_new = jnp.sum(q8 * k8, axis=-1, keepdims=True) + bn_ref[:, 0:1]
    m = jnp.maximum(jnp.max(s, axis=-1, keepdims=True), s_new)
    p = jnp.exp(s - m)
    p_new = jnp.exp(s_new - m)
    inv_l = 1.0 / (jnp.sum(p, axis=-1, keepdims=True) + p_new)
    a = p * inv_l
    r = (p_new * inv_l) * v8
    for pg in range(n_pages):
        r = r + jnp.dot(a[:, pg * page_rows:(pg + 1) * page_rows].astype(BF16),
                        v_refs[pg][...].astype(BF16), preferred_element_type=F32)
    lam = _lambda(lam_ref, lam_init)
    for h in range(DIFF_HEADS):
        o = r[2 * h:2 * h + 1, :] - lam * r[2 * h + 1:2 * h + 2, :]
        o_ref[:, h * DIFF_HEAD_DIM:(h + 1) * DIFF_HEAD_DIM] = (
            _rms(o, sub_ref[...]) * (1.0 - lam_init)).astype(BF16)


def attn_decode(proj, cache_k, cache_v, page_table, layer_j, bias_d, bias_n, lam_params, subln, lam_init):
    nb = proj.shape[0]
    n_pages = page_table.shape[1]
    n_pool, n_even = cache_k.shape[:2]
    page_rows = PAGE_SIZE * DIFF_HEADS
    ck = cache_k.reshape(n_pool, n_even, page_rows, DIFF_HEAD_DIM)
    cv = cache_v.reshape(n_pool, n_even, page_rows, DIFF_HEAD_DIM)

    def page_spec(p):
        return pl.BlockSpec((None, None, page_rows, DIFF_HEAD_DIM),
                            lambda i, pt: (pt[i * n_pages + p], layer_j, 0, 0))

    full = lambda shape: pl.BlockSpec(shape, lambda i, pt: (0,) * len(shape))
    grid_spec = pltpu.PrefetchScalarGridSpec(
        num_scalar_prefetch=1,
        grid=(nb,),
        in_specs=([pl.BlockSpec((None, 1, EVEN_IN), lambda i, pt: (i, 0, 0))]
                  + [page_spec(p) for p in range(n_pages)] * 2
                  + [full(bias_d.shape), full(bias_n.shape), full((4, DIFF_HALF)), full((1, LANES))]),
        out_specs=pl.BlockSpec((None, 1, DIFF_WIDTH), lambda i, pt: (i, 0, 0)),
    )
    out = pl.pallas_call(
        functools.partial(_attn_decode_body, n_pages=n_pages, lam_init=lam_init),
        grid_spec=grid_spec,
        out_shape=jax.ShapeDtypeStruct((nb, 1, DIFF_WIDTH), BF16),
        compiler_params=_cparams(("arbitrary",)),
        name="attn_decode",
    )(page_table.reshape(-1), proj.reshape(nb, 1, EVEN_IN), *([ck] * n_pages), *([cv] * n_pages),
      bias_d, bias_n, lam_params, subln.reshape(1, LANES))
    return out.reshape(nb, DIFF_WIDTH)


def _blockdiag(x):
    lane = lax.broadcasted_iota(jnp.int32, x.shape, 1)
    zero = jnp.zeros_like(x)
    return jnp.concatenate([jnp.where(lane < GDN_C, x, zero), jnp.where(lane >= GDN_C, x, zero)], axis=0)


def _mmp(a, b):
    return jnp.dot(a.astype(BF16), _blockdiag(b.astype(BF16)), preferred_element_type=F32)


def _unit_lower_inverse(lows, ii, jl):
    eye = jnp.where(ii == jl, 1.0, 0.0)
    in_block = (ii // INV_BLOCK) == (jl // INV_BLOCK)
    ps = [jnp.where(in_block, -low, 0.0) for low in lows]
    offs = [jnp.where(in_block, 0.0, low) for low in lows]
    dinvs = [eye + p for p in ps]
    span = 2
    while span < INV_BLOCK:
        ps = [_mmp(p, p) for p in ps]
        dinvs = [d + _mmp(d, p) for d, p in zip(dinvs, ps)]
        span *= 2
    powers = [[-_mmp(d, off) for d, off in zip(dinvs, offs)]]
    span = 2
    while span < GDN_C // INV_BLOCK:
        powers.append([_mmp(n, n) for n in powers[-1]])
        span *= 2
    xs = dinvs
    for pw in reversed(powers):
        xs = [x + _mmp(n, x) for n, x in zip(pw, xs)]
    return xs


def _gdn_prompt_body(h_ref, gn_ref, w_ref, wgt_ref, cw_ref, alog_ref, dtb_ref, on_ref,
                     o_ref, s_ref, cst_ref, ext_ref, z_ref):
    c = pl.program_id(1)
    nh = GDN_HEADS
    npair = nh // 2
    hk = nh * GDN_DK
    cc = GDN_C
    rows_step = GDN_STEP_CHUNKS * cc
    heads_per_tile = GDN_PROJ_TN // LANES

    @pl.when(c == 0)
    def _():
        s_ref[...] = jnp.zeros_like(s_ref)
        ext_ref[0:8, :] = jnp.zeros((8, GDN_QKV), F32)

    @pl.when(c > 0)
    def _():
        ext_ref[0:8, :] = ext_ref[rows_step:rows_step + 8, :]

    xn = _rms(h_ref[...], gn_ref[...]).astype(BF16)
    abt = lax.dot_general(wgt_ref[...], xn, _NT, preferred_element_type=F32)

    def project(n):
        cols = slice(n * GDN_PROJ_TN, (n + 1) * GDN_PROJ_TN)
        tile = jnp.dot(xn, w_ref[:, cols], preferred_element_type=F32)
        if n < GDN_QKV // GDN_PROJ_TN:
            ext_ref[8:8 + rows_step, cols] = tile
        else:
            z_ref[:, n * GDN_PROJ_TN - GDN_QKV:(n + 1) * GDN_PROJ_TN - GDN_QKV] = tile

    def conv_act(col, r0):
        sl = slice(col, col + LANES)
        acc = ext_ref[8 + r0:8 + r0 + cc, sl] * cw_ref[3:4, sl]
        for i in range(GDN_CONV - 1):
            acc = acc + ext_ref[5 + i + r0:5 + i + r0 + cc, sl] * cw_ref[i:i + 1, sl]
        return _silu(acc)

    def l2n(x):
        return x * lax.rsqrt(jnp.sum(x * x, axis=-1, keepdims=True) + EPS)

    def pair(xs):
        return [jnp.concatenate([xs[2 * p], xs[2 * p + 1]], axis=1) for p in range(npair)]

    def row_pair(x):
        return pair([x[h:h + 1, :] for h in range(nh)])

    project(0)
    lane = lax.broadcasted_iota(jnp.int32, (nh, cc), 1)
    q_p, k_p, v_p, gc_b, beta_b, kdec_b, gc_rows, s_decay = [], [], [], [], [], [], [], []
    for ck in range(GDN_STEP_CHUNKS):
        r0 = ck * cc
        g = -jnp.exp(alog_ref[...]) * _softplus(abt[0:nh, r0:r0 + cc] + dtb_ref[...])
        beta = _sigmoid(abt[nh:2 * nh, r0:r0 + cc])
        gc = g
        shift = 1
        while shift < cc:
            gc = gc + jnp.where(lane >= shift, pltpu.roll(gc, shift, 1), 0.0)
            shift *= 2
        g_last = jnp.broadcast_to(gc[:, cc - 1:cc], (nh, cc))
        s_decay.append(jnp.exp(g_last))
        cols = jnp.transpose(jnp.concatenate([gc, beta, jnp.zeros((LANES - 2 * nh, cc), F32)], axis=0))
        bcast = [jnp.broadcast_to(cols[:, n:n + 1], (cc, cc)) for n in range(2 * nh)]
        gcb = pair(bcast[0:nh])
        gc_b += gcb
        beta_b += pair(bcast[nh:2 * nh])
        kdec_b += [jnp.exp(gl - x) for gl, x in zip(row_pair(g_last), gcb)]
        gc_rows += row_pair(gc)
    gam_b = [jnp.exp(x) for x in gc_b]
    ii = lax.broadcasted_iota(jnp.int32, (cc, 2 * cc), 0)
    jl = lax.broadcasted_iota(jnp.int32, (cc, 2 * cc), 1) % cc
    decay = [jnp.exp(jnp.where(ii >= jl, gb - gr, -jnp.inf)) for gb, gr in zip(gc_b, gc_rows)]

    act = {}
    for n in range(1, GDN_MAIN // GDN_PROJ_TN + 1):
        if n < GDN_MAIN // GDN_PROJ_TN:
            project(n)
        if n <= GDN_QKV // GDN_PROJ_TN:
            for hh in range(heads_per_tile):
                col = (n - 1) * GDN_PROJ_TN + hh * LANES
                for ck in range(GDN_STEP_CHUNKS):
                    a = conv_act(col, ck * cc)
                    if col < hk:
                        a = l2n(a) * (GDN_DK ** -0.5)
                    elif col < 2 * hk:
                        a = l2n(a)
                    act[(col, ck)] = a
    for ck in range(GDN_STEP_CHUNKS):
        q_p += pair([act[(h * GDN_DK, ck)] for h in range(nh)])
        k_p += pair([act[(hk + h * GDN_DK, ck)] for h in range(nh)])
        v_p += pair([act[(2 * hk + h * GDN_DV, ck)] for h in range(nh)])

    gram = [lax.dot_general(jnp.concatenate([q, k], axis=0).astype(BF16), _blockdiag(k.astype(BF16)), _NT,
                            preferred_element_type=F32) for q, k in zip(q_p, k_p)]
    qk = [gm[0:cc, :] * d for gm, d in zip(gram, decay)]
    lows = [jnp.where(ii > jl, b * gm[cc:2 * cc, :] * d, 0.0) for b, gm, d in zip(beta_b, gram, decay)]
    tinv = _unit_lower_inverse(lows, ii, jl)
    w =[_mmp(t, b * gm * k) for t, b, gm, k in zip(tinv, beta_b, gam_b, k_p)]
    u0 = [_mmp(t, b * v) for t, b, v in zip(tinv, beta_b, v_p)]
    wq = [jnp.concatenate([wp, gm * q], axis=0).astype(BF16) for wp, gm, q in zip(w, gam_b, q_p)]
    kd = [(kdb * k).astype(BF16) for kdb, k in zip(kdec_b, k_p)]

    state = [s_ref[h] for h in range(nh)]
    zero = jnp.zeros((GDN_DK, GDN_DV), BF16)
    for ck in range(GDN_STEP_CHUNKS):
        r0 = ck * cc
        ent = range(ck * npair, (ck + 1) * npair)
        s_bd = [jnp.concatenate([jnp.concatenate([state[2 * p].astype(BF16), zero], axis=1),
                                 jnp.concatenate([zero, state[2 * p + 1].astype(BF16)], axis=1)], axis=0)
                for p in range(npair)]
        ws_qs = [jnp.dot(wq[e], sb, preferred_element_type=F32) for e, sb in zip(ent, s_bd)]
        u = [u0[e] - x[0:cc, :] for e, x in zip(ent, ws_qs)]
        o = [x[cc:2 * cc, :] + _mmp(qk[e], b) for e, x, b in zip(ent, ws_qs, u)]
        upd = [lax.dot_general(kd[e], b.astype(BF16), _TN, preferred_element_type=F32)
               for e, b in zip(ent, u)]
        new_state = []
        for h in range(nh):
            half = slice((h % 2) * cc, (h % 2 + 1) * cc)
            new_state.append(s_decay[ck][h:h + 1, :] * state[h] + upd[h // 2][half, half])
            z = z_ref[r0:r0 + cc, h * GDN_DV:(h + 1) * GDN_DV]
            o_ref[r0:r0 + cc, h * GDN_DV:(h + 1) * GDN_DV] = (
                _rms(o[h // 2][:, half], on_ref[...]) * _silu(z)).astype(BF16)
        state = new_state
    for h in range(nh):
        s_ref[h] = state[h]

    @pl.when(c == pl.num_programs(1) - 1)
    def _():
        cst_ref[...] = ext_ref[rows_step:rows_step + 8, :]


def gdn_prompt(h, gain, w_in, w_gates_t, idx, conv_w, a_log, dt_bias, o_norm):
    b, t, d = h.shape
    rows = GDN_STEP_CHUNKS * GDN_C
    nc = t // rows
    nh = GDN_HEADS
    return pl.pallas_call(
        _gdn_prompt_body,
        grid=(b, nc),
        in_specs=[
            pl.BlockSpec((None, rows, d), lambda i, c: (i, c, 0)),
            pl.BlockSpec((1, d), lambda i, c: (0, 0)),
            pl.BlockSpec((None, d, GDN_MAIN), lambda i, c: (idx, 0, 0), pipeline_mode=pl.Buffered(1)),
            pl.BlockSpec((None, LANES, d), lambda i, c: (idx, 0, 0), pipeline_mode=pl.Buffered(1)),
            pl.BlockSpec((GDN_CONV, GDN_QKV), lambda i, c: (0, 0)),
            pl.BlockSpec((nh, 1), lambda i, c: (0, 0)),
            pl.BlockSpec((nh, 1), lambda i, c: (0, 0)),
            pl.BlockSpec((1, GDN_DV), lambda i, c: (0, 0)),
        ],
        out_specs=[
            pl.BlockSpec((None, rows, nh * GDN_DV), lambda i, c: (i, c, 0)),
            pl.BlockSpec((None, nh, GDN_DK, GDN_DV), lambda i, c: (i, 0, 0, 0)),
            pl.BlockSpec((None, 8, GDN_QKV), lambda i, c: (i, 0, 0)),
        ],
        out_shape=[
            jax.ShapeDtypeStruct((b, t, nh * GDN_DV), BF16),
            jax.ShapeDtypeStruct((b, nh, GDN_DK, GDN_DV), F32),
            jax.ShapeDtypeStruct((b, 8, GDN_QKV), F32),
        ],
        scratch_shapes=[pltpu.VMEM((8 + rows, GDN_QKV), F32), pltpu.VMEM((rows, nh * GDN_DV), F32)],
        compiler_params=_cparams(("parallel", "arbitrary")),
        name="gdn_prompt",
    )(h, gain.reshape(1, d), w_in, w_gates_t, conv_w, a_log.reshape(nh, 1), dt_bias.reshape(nh, 1),
      o_norm.reshape(1, GDN_DV))


def _gdn_decode_body(x_ref, cs_ref, ab_ref, cw_ref, alog_ref, dtb_ref, on_ref, s0_ref, *rest, first, layer_j):
    o_ref, s_ref = rest[-2:]
    nh = GDN_HEADS
    pad = jnp.zeros((8 - 2, GDN_DK), F32)
    if first:
        for slot in range(s_ref.shape[0]):
            if slot != layer_j:
                s_ref[slot] = jnp.zeros(s_ref.shape[1:], F32)
    pad7 = jnp.zeros((8 - 1, GDN_DK), F32)
    seqs = range(GDN_DEC_ROWS)
    q8, k8, v8, gam8, beta8, qk8 = [], [], [], [], [], []
    for r in seqs:
        x = x_ref[r, 0:3 * nh, :]
        conv = x * cw_ref[GDN_CONV - 1]
        for i in range(GDN_CONV - 1):
            conv = conv + cs_ref[r, i] * cw_ref[i]
        act = conv * _sigmoid(conv)
        qa, ka = act[0:nh], act[nh:2 * nh]
        q8.append(qa * lax.rsqrt(jnp.sum(qa * qa, axis=-1, keepdims=True) + EPS) * (GDN_DK ** -0.5))
        k8.append(ka * lax.rsqrt(jnp.sum(ka * ka, axis=-1, keepdims=True) + EPS))
        v8.append(act[2 * nh:3 * nh])
        g = -jnp.exp(alog_ref[...]) * _softplus(ab_ref[r, 0:nh, :] + dtb_ref[...])
        gam8.append(jnp.broadcast_to(jnp.exp(g), (nh, GDN_DV)))
        beta8.append(jnp.broadcast_to(_sigmoid(ab_ref[r, nh:2 * nh, :]), (nh, GDN_DV)))
        qk8.append(jnp.broadcast_to(jnp.sum(q8[r] * k8[r], axis=-1, keepdims=True), (nh, GDN_DV)))
    pairs = [(r, h) for r in seqs for h in range(nh)]
    s_old = [s0_ref[r, h] for r, h in pairs]
    qk_s = [jnp.dot(jnp.concatenate([q8[r][h:h + 1, :], k8[r][h:h + 1, :], pad], axis=0).astype(BF16),
                    s.astype(BF16), preferred_element_type=F32) for (r, h), s in zip(pairs, s_old)]
    u = [beta8[r][h:h + 1, :] * (v8[r][h:h + 1, :] - gam8[r][h:h + 1, :] * x[1:2, :])
         for (r, h), x in zip(pairs, qk_s)]
    outs = [gam8[r][h:h + 1, :] * x[0:1, :] + qk8[r][h:h + 1, :] * b for (r, h), x, b in zip(pairs, qk_s, u)]
    outer = [lax.dot_general(jnp.concatenate([k8[r][h:h + 1, :], pad7], axis=0).astype(BF16),
                             jnp.concatenate([b, pad7], axis=0).astype(BF16), _TN, preferred_element_type=F32)
             for (r, h), b in zip(pairs, u)]
    for (r, h), s, x in zip(pairs, s_old, outer):
        new = gam8[r][h:h + 1, :] * s + x
        if first:
            s_ref[layer_j, r, h] = new
        else:
            s_ref[r, h] = new
    for r in seqs:
        o8 = jnp.concatenate(outs[r * nh:(r + 1) * nh], axis=0)
        z8 = x_ref[r, 3 * nh:4 * nh, :]
        o_ref[r] = (_rms(o8, on_ref[...]) * (z8 * _sigmoid(z8))).astype(BF16)


def gdn_decode(proj, gates, conv_state, s0_all, layer_j, s_new_all, conv_w, a_log, dt_bias, o_norm):
    nb = proj.shape[0]
    nh = GDN_HEADS
    nrow = GDN_QKV // LANES
    first = s_new_all is None
    in_specs = [
        pl.BlockSpec((GDN_DEC_ROWS, GDN_MAIN // LANES, LANES), lambda i: (i, 0, 0)),
        pl.BlockSpec((GDN_DEC_ROWS, GDN_CONV - 1, nrow, LANES), lambda i: (i, 0, 0, 0)),
        pl.BlockSpec((GDN_DEC_ROWS, 2 * nh, 1), lambda i: (i, 0, 0)),
        pl.BlockSpec((GDN_CONV, nrow, LANES), lambda i: (0, 0, 0)),
        pl.BlockSpec((nh, 1), lambda i: (0, 0)),
        pl.BlockSpec((nh, 1), lambda i: (0, 0)),
        pl.BlockSpec((1, GDN_DV), lambda i: (0, 0)),
        pl.BlockSpec((None, GDN_DEC_ROWS, nh, GDN_DK, GDN_DV), lambda i: (layer_j, i, 0, 0, 0)),
    ]
    args = [proj.reshape(nb, GDN_MAIN // LANES, LANES),
            conv_state.reshape(nb, GDN_CONV - 1, nrow, LANES),
            gates.reshape(nb, 2 * nh, 1),
            conv_w.reshape(GDN_CONV, nrow, LANES),
            a_log.reshape(nh, 1), dt_bias.reshape(nh, 1), o_norm.reshape(1, GDN_DV), s0_all]
    if first:
        s_spec = pl.BlockSpec((s0_all.shape[0], GDN_DEC_ROWS, nh, GDN_DK, GDN_DV), lambda i: (0, i, 0, 0, 0))
        aliases = {}
    else:
        in_specs.append(pl.BlockSpec(memory_space=pl.ANY))
        args.append(s_new_all)
        s_spec = pl.BlockSpec((None, GDN_DEC_ROWS, nh, GDN_DK, GDN_DV), lambda i: (layer_j, i, 0, 0, 0))
        aliases = {len(args) - 1: 1}
    o, s = pl.pallas_call(
        functools.partial(_gdn_decode_body, first=first, layer_j=layer_j),
        grid=(nb // GDN_DEC_ROWS,),
        in_specs=in_specs,
        out_specs=[pl.BlockSpec((GDN_DEC_ROWS, nh, GDN_DV), lambda i: (i, 0, 0)), s_spec],
        out_shape=[
            jax.ShapeDtypeStruct((nb, nh, GDN_DV), BF16),
            jax.ShapeDtypeStruct(s0_all.shape, F32),
        ],
        input_output_aliases=aliases,
        compiler_params=_cparams(("parallel",)),
        name="gdn_decode",
    )(*args)
    return o.reshape(nb, nh * GDN_DV), s


def kernel(x_prompt, x_sample, cache_k, cache_v, page_table, state_pool, state_conv, state_delta,
           norm_mix, norm_ffn, norm_final, rel_bias,
           w_in_even, pool_w, pool_scale, lambda_q1, lambda_k1, lambda_q2, lambda_k2, subln_w, w_out_even,
           w_in_odd, conv_w, a_log, dt_bias, o_norm, w_out_odd,
           w_gate_up, w_down):
    bp, t, d = x_prompt.shape
    bs = x_sample.shape[0]
    mp = bp * t
    nh = GDN_HEADS
    past = page_table.shape[1] * PAGE_SIZE
    tm_p = 1024
    tm_s = bs

    hp = x_prompt.reshape(mp, d)
    hs = x_sample.reshape(bs, d)
    bias_p, bias_d, bias_n = rel_bias_tiles(rel_bias, past)

    w_in_e = w_in_even.astype(BF16)
    w_in_o = w_in_odd.astype(BF16)
    w_gates_t = jnp.zeros((w_in_odd.shape[0], LANES, d), F32).at[:, 0:2 * nh, :].set(
        jnp.transpose(w_in_odd[:, :, GDN_MAIN:], (0, 2, 1))).astype(BF16)
    w_out_e = w_out_even.astype(BF16)
    w_out_o = w_out_odd.astype(BF16)
    w_gu = w_gate_up.astype(BF16)
    w_d = w_down.astype(BF16)
    pool_wb = pool_w.astype(BF16)

    k_p, v_p, k_s, v_s, pool_p, pool_s = [], [], [], [], [], []
    conv_p, conv_s, delta_p = [], [], []
    delta_s = None
    for layer in range(DEPTH):
        j = layer // 2
        last = layer == DEPTH - 1
        if layer % 2 == 0:
            w_out = w_out_e
            pw = pool_wb[j]
            lam_init = 0.8 - 0.6 * math.exp(-0.3 * layer)
            lam_params = jnp.stack([lambda_q1[j], lambda_k1[j], lambda_q2[j], lambda_k2[j]])

            qkv_p, ypool_p, kp, vp, tail_p = even_proj_prompt(hp, norm_mix[layer], w_in_e, j, pw, pool_scale[j],
                                                              t, tm=tm_p)
            proj_s, ks_, vs_ = norm_matmul(hs, norm_mix[layer], w_in_e, j, tm=tm_s, tn=DIFF_WIDTH)

            oatt_p = attn_prompt(qkv_p.reshape(bp, t, 3 * DIFF_WIDTH), bias_p, lam_params, subln_w[j], lam_init)
            ypool_s = pool_decode(jnp.transpose(state_pool[j], (1, 0, 2)), proj_s, pw, pool_scale[j])
            oatt_s = attn_decode(proj_s, cache_k, cache_v, page_table, j, bias_d, bias_n,
                                 lam_params, subln_w[j], lam_init)

            kv_shape = (DIFF_HEADS, DIFF_HEAD_DIM)
            k_p.append(kp.reshape(bp, t, *kv_shape))
            v_p.append(vp.reshape(bp, t, *kv_shape))
            k_s.append(ks_.reshape(bs, 1, *kv_shape))
            v_s.append(vs_.reshape(bs, 1, *kv_shape))
            pool_p.append(tail_p[:, POOL_PAD - POOL_BUF:, :])
            pool_s.append(jnp.concatenate([state_pool[j][:, 1:], proj_s[:, None, 0:POOL_WIDTH]], axis=1))

            mix_p = ((ypool_p, 0), (oatt_p.reshape(mp, DIFF_WIDTH), 0))
            mix_s = ((ypool_s, 0), (oatt_s, 0))
        else:
            w_out = w_out_o

            proj_s, gates_s = norm_matmul(hs, norm_mix[layer], w_in_o, j, w_gates_t, tm=tm_s, tn=512)

            o_p, s_p, tail_p = gdn_prompt(hp.reshape(bp, t, d), norm_mix[layer], w_in_o, w_gates_t, j,
                                          conv_w[j], a_log[j], dt_bias[j], o_norm[j])
            o_s, delta_s = gdn_decode(proj_s, gates_s[0:2 * nh].T, state_conv[j], state_delta, j, delta_s,
                                      conv_w[j], a_log[j], dt_bias[j], o_norm[j])

            conv_p.append(tail_p[:, 8 - (GDN_CONV - 1):, :])
            conv_s.append(jnp.concatenate([state_conv[j][:, 1:], proj_s[:, None, 0:GDN_QKV]], axis=1))
            delta_p.append(s_p)

            o_p = o_p.reshape(mp, nh * GDN_DV)
            mix_p = ((o_p, 0), (o_p, 1))
            mix_s = ((o_s, 0), (o_s, 1))

        g_fin = norm_final if last else None
        hp = outproj_ffn(hp, mix_p[0], mix_p[1], w_out, j, norm_ffn[layer], w_gu, w_d, layer, g_fin, tm=FFN_TM)
        hs = outproj_ffn(hs, mix_s[0], mix_s[1], w_out, j, norm_ffn[layer], w_gu, w_d, layer, g_fin, tm=tm_s)

    return (hp.reshape(bp, t, d), hs.reshape(bs, 1, d),
            jnp.stack(k_p, axis=1), jnp.stack(v_p, axis=1), jnp.stack(k_s, axis=1), jnp.stack(v_s, axis=1),
            jnp.stack(pool_p), jnp.stack(pool_s), jnp.stack(conv_p), jnp.stack(conv_s),
            jnp.stack(delta_p), delta_s)
```

```python
import functools
import math

import jax
import jax.numpy as jnp
from jax import lax
from jax.experimental import pallas as pl
from jax.experimental.pallas import tpu as pltpu

F32 = jnp.float32
BF16 = jnp.bfloat16

D_MODEL = 1024
DEPTH = 4
PAGE_SIZE = 128
POOL_WIDTH = 512
POOL_WINDOWS = (2, 4, 8, 16)
POOL_GROUP = 128
POOL_BUF = 15
DIFF_HEADS = 4
DIFF_HALF = 64
DIFF_HEAD_DIM = 128
DIFF_WIDTH = 512
EVEN_IN = 2048
REL_BUCKETS = 32
REL_MAX_DIST = 128
GDN_HEADS = 8
GDN_DK = 128
GDN_DV = 128
GDN_CONV = 4
GDN_QKV = 3072
GDN_MAIN = 4096
D_FF = 2816
EPS = 1e-6

LANES = 128
VMEM_LIMIT = 48 * 1024 * 1024
NEG_BIG = -1e30

ATT_TQ = 512
ATT_TK = 512
GDN_C = 128
GDN_STEP_CHUNKS = 2
GDN_DEC_ROWS = 4
ATT_DEC_BUFS = 3
GDN_PROJ_TN = 512
INV_BLOCK = 16
FFN_TF = 256
FFN_TM = 512

_NT = (((1,), (1,)), ((), ()))
_TN = (((0,), (0,)), ((), ()))


def _cparams(sem):
    return pltpu.CompilerParams(dimension_semantics=sem, vmem_limit_bytes=VMEM_LIMIT)


def _sigmoid(x):
    return 1.0 / (1.0 + jnp.exp(-x))


def _silu(x):
    h = 0.5 * x
    return h + h * jnp.tanh(h)


def _softplus(x):
    return jnp.maximum(x, 0.0) + jnp.log1p(jnp.exp(-jnp.abs(x)))


def _rms(x, gain):
    return x * lax.rsqrt(jnp.mean(x * x, axis=-1, keepdims=True) + EPS) * gain


def _mm(a, b):
    return jnp.dot(a.astype(BF16), b.astype(BF16), preferred_element_type=F32)


def _norm_mm_kv_body(x_ref, g_ref, w_ref, o_ref, k_ref, v_ref, xn_ref):
    j = pl.program_id(1)

    @pl.when(j == 0)
    def _():
        xn_ref[...] = _rms(x_ref[...], g_ref[...]).astype(BF16)

    res = jnp.dot(xn_ref[...], w_ref[...], preferred_element_type=F32)
    o_ref[...] = res

    def head_rows(dst_ref):
        for h in range(DIFF_HEADS):
            dst_ref[pl.ds(h, res.shape[0], stride=DIFF_HEADS), :] = res[:, h * DIFF_HEAD_DIM:(h + 1) * DIFF_HEAD_DIM]

    @pl.when(j == 2)
    def _():
        head_rows(k_ref)

    @pl.when(j == 3)
    def _():
        head_rows(v_ref)


def _norm_mm_gate_body(x_ref, g_ref, w_ref, wst_ref, o_ref, ost_ref, xn_ref):
    @pl.when(pl.program_id(1) == 0)
    def _():
        xn = _rms(x_ref[...], g_ref[...]).astype(BF16)
        xn_ref[...] = xn
        ost_ref[...] = jnp.transpose(jnp.dot(xn, wst_ref[...], preferred_element_type=F32))

    o_ref[...] = jnp.dot(xn_ref[...], w_ref[...], preferred_element_type=F32)


def norm_matmul(x, gain, w, idx, w_gates=None, *, tm, tn):
    m, d = x.shape
    n = EVEN_IN if w_gates is None else GDN_MAIN
    grid = (m // tm, n // tn)
    in_specs = [
        pl.BlockSpec((tm, d), lambda i, j: (i, 0)),
        pl.BlockSpec((1, d), lambda i, j: (0, 0)),
        pl.BlockSpec((None, d, tn), lambda i, j: (idx, 0, j)),
    ]
    out_specs = pl.BlockSpec((tm, tn), lambda i, j: (i, j))
    out_shape = jax.ShapeDtypeStruct((m, n), F32)
    args = [x, gain.reshape(1, d), w]
    if w_gates is None:
        assert tn == DIFF_WIDTH and n == EVEN_IN
        kv_spec = pl.BlockSpec((tm * DIFF_HEADS, DIFF_HEAD_DIM), lambda i, j: (i, 0))
        kv_shape = jax.ShapeDtypeStruct((m * DIFF_HEADS, DIFF_HEAD_DIM), F32)
        out_specs = [out_specs, kv_spec, kv_spec]
        out_shape = [out_shape, kv_shape, kv_shape]
        body = _norm_mm_kv_body
    else:
        in_specs.append(pl.BlockSpec((None, d, LANES), lambda i, j: (idx, 0, 0)))
        out_specs = [out_specs, pl.BlockSpec((LANES, tm), lambda i, j: (0, i))]
        out_shape = [out_shape, jax.ShapeDtypeStruct((LANES, m), F32)]
        args.append(w_gates)
        body = _norm_mm_gate_body
    return pl.pallas_call(
        body,
        grid=grid,
        in_specs=in_specs,
        out_specs=out_specs,
        out_shape=out_shape,
        scratch_shapes=[pltpu.VMEM((tm, d), BF16)],
        compiler_params=_cparams(("parallel", "arbitrary")),
        name="norm_matmul",
    )(*args)


def _ffn_body(*refs, final):
    if final:
        res_ref, a0_ref, a1_ref, wo_ref, gf_ref, wgu_ref, wd_ref, gfin_ref, o_ref = refs
    else:
        res_ref, a0_ref, a1_ref, wo_ref, gf_ref, wgu_ref, wd_ref, o_ref = refs
    mixed = jnp.concatenate([a0_ref[...], a1_ref[...]], axis=1)
    h1 = res_ref[...] + jnp.dot(mixed, wo_ref[...], preferred_element_type=F32)
    xn = _rms(h1, gf_ref[...]).astype(BF16)
    acts = []
    for f in range(D_FF // FFN_TF):
        g = jnp.dot(xn, wgu_ref[:, f * FFN_TF:(f + 1) * FFN_TF], preferred_element_type=F32)
        u = jnp.dot(xn, wgu_ref[:, D_FF + f * FFN_TF:D_FF + (f + 1) * FFN_TF], preferred_element_type=F32)
        acts.append((g * _sigmoid(g) * u).astype(BF16))
    y = h1 + jnp.dot(jnp.concatenate(acts, axis=1), wd_ref[...], preferred_element_type=F32)
    if final:
        y = _rms(y, gfin_ref[...])
    o_ref[...] = y


def outproj_ffn(res, mix0, mix1, w_out, idx_out, g_ffn, w_gu, w_d, layer, g_final=None, *, tm):
    m, d = res.shape
    half = w_out.shape[1] // 2
    (m0, c0), (m1, c1) = mix0, mix1
    final = g_final is not None
    once = pl.Buffered(1)
    in_specs = [
        pl.BlockSpec((tm, d), lambda i: (i, 0)),
        pl.BlockSpec((tm, half), lambda i: (i, c0)),
        pl.BlockSpec((tm, half), lambda i: (i, c1)),
        pl.BlockSpec((None, 2 * half, d), lambda i: (idx_out, 0, 0), pipeline_mode=once),
        pl.BlockSpec((1, d), lambda i: (0, 0)),
        pl.BlockSpec((None, d, 2 * D_FF), lambda i: (layer, 0, 0), pipeline_mode=once),
        pl.BlockSpec((None, D_FF, d), lambda i: (layer, 0, 0), pipeline_mode=once),
    ]
    args = [res, m0, m1, w_out, g_ffn.reshape(1, d), w_gu, w_d]
    if final:
        in_specs.append(pl.BlockSpec((1, d), lambda i: (0, 0)))
        args.append(g_final.reshape(1, d))
    return pl.pallas_call(
        functools.partial(_ffn_body, final=final),
        grid=(m // tm,),
        in_specs=in_specs,
        out_specs=pl.BlockSpec((tm, d), lambda i: (i, 0)),
        out_shape=jax.ShapeDtypeStruct((m, d), F32),
        compiler_params=_cparams(("parallel",)),
        name="outproj_ffn",
    )(*args)


POOL_PAD = 16
POOL_ROWS = 512


def _even_proj_body(x_ref, g_ref, w_ref, pw_ref, ps_ref, qkv_ref, yp_ref, k_ref, v_ref, tail_ref,
                    xn_ref, ext_ref, *, blocks_per_seq):
    i = pl.program_id(0)
    j = pl.program_id(1)
    tm = x_ref.shape[0]

    @pl.when(j == 0)
    def _():
        xn_ref[...] = _rms(x_ref[...], g_ref[...]).astype(BF16)

    res = jnp.dot(xn_ref[...], w_ref[...], preferred_element_type=F32)

    blk = i % blocks_per_seq

    @pl.when((j == 0) & (blk == 0))
    def _():
        ext_ref[0:POOL_PAD, :] = jnp.zeros((POOL_PAD, POOL_WIDTH), F32)

    @pl.when((j == 0) & (blk > 0))
    def _():
        ext_ref[0:POOL_PAD, :] = ext_ref[tm:tm + POOL_PAD, :]

    @pl.when(j == 0)
    def _():
        ext_ref[POOL_PAD:, :] = res
        tail_ref[...] = res[tm - POOL_PAD:tm, :]
        for r0 in range(0, tm, POOL_ROWS):
            pos = lax.broadcasted_iota(jnp.int32, (POOL_ROWS, 1), 0) + (blk * tm + r0)
            for g, w in enumerate(POOL_WINDOWS):
                sl = slice(g * POOL_GROUP, (g + 1) * POOL_GROUP)
                cur = ext_ref[POOL_PAD + r0:POOL_PAD + r0 + POOL_ROWS, sl]
                win = cur
                for s in range(1, w):
                    win = win + ext_ref[POOL_PAD + r0 - s:POOL_PAD + r0 - s + POOL_ROWS, sl]
                cnt = jnp.minimum(pos + 1, w).astype(F32)
                dlt = win / cnt - cur
                y = jnp.dot(dlt.astype(BF16), pw_ref[g], preferred_element_type=F32) * ps_ref[:, sl]
                yp_ref[r0:r0 + POOL_ROWS, sl] = y.astype(BF16)

    @pl.when(j >= 1)
    def _():
        qkv_ref[...] = res.astype(BF16)

    def head_rows(dst_ref):
        for h in range(DIFF_HEADS):
            dst_ref[pl.ds(h, tm, stride=DIFF_HEADS), :] = res[:, h * DIFF_HEAD_DIM:(h + 1) * DIFF_HEAD_DIM]

    @pl.when(j == 2)
    def _():
        head_rows(k_ref)

    @pl.when(j == 3)
    def _():
        head_rows(v_ref)


def even_proj_prompt(x, gain, w, idx, pool_w, pool_scale, seq_len, *, tm):
    m, d = x.shape
    tn = DIFF_WIDTH
    blocks_per_seq = seq_len // tm
    nseq = m // seq_len
    kv_spec = pl.BlockSpec((tm * DIFF_HEADS, DIFF_HEAD_DIM), lambda i, j: (i, 0))
    kv_shape = jax.ShapeDtypeStruct((m * DIFF_HEADS, DIFF_HEAD_DIM), F32)
    return pl.pallas_call(
        functools.partial(_even_proj_body, blocks_per_seq=blocks_per_seq),
        grid=(m // tm, EVEN_IN // tn),
        in_specs=[
            pl.BlockSpec((tm, d), lambda i, j: (i, 0)),
            pl.BlockSpec((1, d), lambda i, j: (0, 0)),
            pl.BlockSpec((None, d, tn), lambda i, j: (idx, 0, j)),
            pl.BlockSpec((len(POOL_WINDOWS), POOL_GROUP, POOL_GROUP), lambda i, j: (0, 0, 0)),
            pl.BlockSpec((1, POOL_WIDTH), lambda i, j: (0, 0)),
        ],
        out_specs=[
            pl.BlockSpec((tm, tn), lambda i, j: (i, jnp.maximum(j - 1, 0))),
            pl.BlockSpec((tm, POOL_WIDTH), lambda i, j: (i, 0)),
            kv_spec,
            kv_spec,
            pl.BlockSpec((None, POOL_PAD, POOL_WIDTH), lambda i, j: (i // blocks_per_seq, 0, 0)),
        ],
        out_shape=[
            jax.ShapeDtypeStruct((m, 3 * DIFF_WIDTH), BF16),
            jax.ShapeDtypeStruct((m, POOL_WIDTH), BF16),
            kv_shape,
            kv_shape,
            jax.ShapeDtypeStruct((nseq, POOL_PAD, POOL_WIDTH), F32),
        ],
        scratch_shapes=[pltpu.VMEM((tm, d), BF16), pltpu.VMEM((POOL_PAD + tm, POOL_WIDTH), F32)],
        compiler_params=_cparams(("arbitrary", "arbitrary")),
        name="even_proj",
    )(x, gain.reshape(1, d), w, pool_w, pool_scale.reshape(1, POOL_WIDTH))


def _pool_decode_body(st_ref, u_ref, pw_ref, ps_ref, o_ref):
    for g, w in enumerate(POOL_WINDOWS):
        sl = slice(g * POOL_GROUP, (g + 1) * POOL_GROUP)
        cur = u_ref[:, sl]
        win = cur
        for i in range(1, w):
            win = win + st_ref[POOL_BUF - i, :, sl]
        dlt = win / float(w) - cur
        y = jnp.dot(dlt.astype(BF16), pw_ref[g], preferred_element_type=F32) * ps_ref[:, sl]
        o_ref[:, sl] = y.astype(BF16)


def pool_decode(state_t, proj, pool_w, pool_scale):
    nb = proj.shape[0]
    return pl.pallas_call(
        _pool_decode_body,
        grid=(1,),
        in_specs=[
            pl.BlockSpec((POOL_BUF, nb, POOL_WIDTH), lambda i: (0, 0, 0)),
            pl.BlockSpec((nb, POOL_WIDTH), lambda i: (0, 0)),
            pl.BlockSpec((len(POOL_WINDOWS), POOL_GROUP, POOL_GROUP), lambda i: (0, 0, 0)),
            pl.BlockSpec((1, POOL_WIDTH), lambda i: (0, 0)),
        ],
        out_specs=pl.BlockSpec((nb, POOL_WIDTH), lambda i: (0, 0)),
        out_shape=jax.ShapeDtypeStruct((nb, POOL_WIDTH), BF16),
        compiler_params=_cparams(("arbitrary",)),
        name="pool_decode",
    )(state_t, proj, pool_w, pool_scale.reshape(1, POOL_WIDTH))


def _rel_bucket(n):
    max_exact = REL_BUCKETS // 2
    nf = jnp.maximum(n, 1).astype(F32)
    large = max_exact + (jnp.log(nf / max_exact) / math.log(REL_MAX_DIST / max_exact)
                         * (REL_BUCKETS - max_exact)).astype(jnp.int32)
    large = jnp.minimum(large, REL_BUCKETS - 1)
    return jnp.where(n < max_exact, n, large)


def _table_lookup(tab_ref, bucket, h):
    out = jnp.zeros(bucket.shape, F32)
    for b in range(REL_BUCKETS):
        out = jnp.where(bucket == b, tab_ref[b, h], out)
    return out


def _rel_bias_body(tab_ref, bp_ref, bd_ref, bn_ref, *, past):
    ii = lax.broadcasted_iota(jnp.int32, (ATT_TQ, ATT_TK), 0)
    jj = lax.broadcasted_iota(jnp.int32, (ATT_TQ, ATT_TK), 1)
    for h in range(DIFF_HEADS):
        for t in range(3):
            dist = t * ATT_TK + ii - jj
            bias = _table_lookup(tab_ref, _rel_bucket(jnp.maximum(dist, 0)), h)
            bp_ref[h, t] = jnp.where(dist >= 0, bias, NEG_BIG)
    nrow = 2 * DIFF_HEADS
    row = lax.broadcasted_iota(jnp.int32, (nrow, past * DIFF_HEADS), 0)
    col = lax.broadcasted_iota(jnp.int32, (nrow, past * DIFF_HEADS), 1)
    bucket = _rel_bucket(past - col // DIFF_HEADS)
    rown = lax.broadcasted_iota(jnp.int32, (nrow, LANES), 0)
    bd = jnp.full((nrow, past * DIFF_HEADS), NEG_BIG, F32)
    bn = jnp.zeros((nrow, LANES), F32)
    for h in range(DIFF_HEADS):
        own = jnp.where(row // 2 == h, col % DIFF_HEADS, -1) == h
        bd = jnp.where(own, _table_lookup(tab_ref, bucket, h), bd)
        bn = jnp.where(rown // 2 == h, tab_ref[0, h], bn)
    bd_ref[...] = bd
    bn_ref[...] = bn


def rel_bias_tiles(rel_bias, past):
    return pl.pallas_call(
        functools.partial(_rel_bias_body, past=past),
        in_specs=[pl.BlockSpec(memory_space=pltpu.SMEM)],
        out_specs=[
            pl.BlockSpec(memory_space=pltpu.VMEM),
            pl.BlockSpec(memory_space=pltpu.VMEM),
            pl.BlockSpec(memory_space=pltpu.VMEM),
        ],
        out_shape=[
            jax.ShapeDtypeStruct((DIFF_HEADS, 3, ATT_TQ, ATT_TK), F32),
            jax.ShapeDtypeStruct((2 * DIFF_HEADS, past * DIFF_HEADS), F32),
            jax.ShapeDtypeStruct((2 * DIFF_HEADS, LANES), F32),
        ],
        compiler_params=pltpu.CompilerParams(vmem_limit_bytes=VMEM_LIMIT),
        name="rel_bias_tiles",
    )(rel_bias)


def _lambda(lam_ref, lam_init):
    lp = lam_ref[...]
    s1 = jnp.sum(lp[0:1, :] * lp[1:2, :], axis=-1, keepdims=True)
    s2 = jnp.sum(lp[2:3, :] * lp[3:4, :], axis=-1, keepdims=True)
    return jnp.exp(s1) - jnp.exp(s2) + lam_init


def _attn_prompt_body(q_ref, k_ref, v_ref, bias_ref, lam_ref, sub_ref, o_ref,
                      vb_ref, qq_ref, m_ref, acc_ref, *, lam_init):
    kb_ref = k_ref
    nq = k_ref.shape[0] // ATT_TQ
    vb_ref[:, 0:LANES] = v_ref[...]
    vb_ref[:, LANES:2 * LANES] = jnp.ones((vb_ref.shape[0], LANES), BF16)
    lane = lax.broadcasted_iota(jnp.int32, (ATT_TQ, LANES), 1)
    for qi in range(nq):
        q = q_ref[qi * ATT_TQ:(qi + 1) * ATT_TQ, :].astype(F32) * (DIFF_HALF ** -0.5)
        qq_ref[qi] = jnp.concatenate([jnp.where(lane < DIFF_HALF, q, 0.0),
                                      jnp.where(lane >= DIFF_HALF, q, 0.0)], axis=0).astype(BF16)
    m_ref[...] = jnp.full(m_ref.shape, -jnp.inf, F32)
    acc_ref[...] = jnp.zeros_like(acc_ref)

    half = ATT_TQ // 2
    for w in range(nq):
        pieces = []
        for qi in range(w, nq):
            tile = min(qi - w, 2)
            parts = [(0, half, half), (half, half, ATT_TK)] if tile == 0 else [(0, ATT_TQ, ATT_TK)]
            for idx in range(2):
                for r0, nr, nk in parts:
                    pieces.append((qi, slice(idx * ATT_TQ + r0, idx * ATT_TQ + r0 + nr), nk, tile, r0))
        s_l = [lax.dot_general(qq_ref[qi, rows, :], kb_ref[w * ATT_TK:w * ATT_TK + nk, :], _NT,
                               preferred_element_type=F32) + bias_ref[tile, r0:r0 + rows.stop - rows.start, 0:nk]
               for qi, rows, nk, tile, r0 in pieces]
        m_prev = [m_ref[qi, rows, :] for qi, rows, _, _, _ in pieces]
        m_new = [jnp.maximum(mp, jnp.max(s, axis=-1, keepdims=True)) for mp, s in zip(m_prev, s_l)]
        p_l = [jnp.exp(s - jnp.concatenate([mn] * (pc[2] // LANES), axis=1)).astype(BF16)
               for s, mn, pc in zip(s_l, m_new, pieces)]
        for (qi, rows, nk, _, _), mp, mn, p in zip(pieces, m_prev, m_new, p_l):
            alpha = jnp.exp(mp - mn)
            acc_ref[qi, rows, :] = (jnp.concatenate([alpha, alpha], axis=1) * acc_ref[qi, rows, :]
                                    + jnp.dot(p, vb_ref[w * ATT_TK:w * ATT_TK + nk, :],
                                              preferred_element_type=F32))
            m_ref[qi, rows, :] = mn

    lam = _lambda(lam_ref, lam_init)
    for qi in range(nq):
        acc = acc_ref[qi]
        o_all = acc[:, 0:LANES] / acc[:, LANES:2 * LANES]
        o = o_all[0:ATT_TQ, :] - lam * o_all[ATT_TQ:2 * ATT_TQ, :]
        o_ref[qi * ATT_TQ:(qi + 1) * ATT_TQ, :] = (_rms(o, sub_ref[...]) * (1.0 - lam_init)).astype(BF16)


def attn_prompt(qkv, bias_p, lam_params, subln, lam_init):
    b, t, _ = qkv.shape
    nh = DIFF_HEADS
    nq = t // ATT_TQ
    assert ATT_TQ == ATT_TK
    proj = qkv
    return pl.pallas_call(
        functools.partial(_attn_prompt_body, lam_init=lam_init),
        grid=(b, nh),
        in_specs=[
            pl.BlockSpec((None, t, LANES), lambda i, h: (i, 0, h)),
            pl.BlockSpec((None, t, LANES), lambda i, h: (i, 0, nh + h)),
            pl.BlockSpec((None, t, LANES), lambda i, h: (i, 0, 2 * nh + h)),
            pl.BlockSpec((None, 3, ATT_TQ, ATT_TK), lambda i, h: (h, 0, 0, 0)),
            pl.BlockSpec((4, DIFF_HALF), lambda i, h: (0, 0)),
            pl.BlockSpec((1, LANES), lambda i, h: (0, 0)),
        ],
        out_specs=pl.BlockSpec((None, t, LANES), lambda i, h: (i, 0, h)),
        out_shape=jax.ShapeDtypeStruct((b, t, DIFF_WIDTH), BF16),
        scratch_shapes=[
            pltpu.VMEM((t, 2 * LANES), BF16),
            pltpu.VMEM((nq, 2 * ATT_TQ, LANES), BF16),
            pltpu.VMEM((nq, 2 * ATT_TQ, LANES), F32),
            pltpu.VMEM((nq, 2 * ATT_TQ, 2 * LANES), F32),
        ],
        compiler_params=_cparams(("parallel", "parallel")),
        name="attn_prompt",
    )(proj, proj, proj, bias_p, lam_params, subln.reshape(1, LANES))


def _attn_decode_body(pt_ref, x_ref, ck_ref, cv_ref, bd_ref, bn_ref, lam_ref, sub_ref, o_ref,
                      kbuf_ref, vbuf_ref, sem_ref, *, n_pages, layer_j, lam_init):
    i = pl.program_id(0)
    page_rows = PAGE_SIZE * DIFF_HEADS

    def page_copies(b, slot):
        copies = []
        for p in range(n_pages):
            page = pt_ref[b * n_pages + p]
            copies.append(pltpu.make_async_copy(ck_ref.at[page, layer_j], kbuf_ref.at[slot, p], sem_ref.at[slot, 0]))
            copies.append(pltpu.make_async_copy(cv_ref.at[page, layer_j], vbuf_ref.at[slot, p], sem_ref.at[slot, 1]))
        return copies

    @pl.when(i == 0)
    def _():
        for b in range(ATT_DEC_BUFS - 1):
            for cp in page_copies(b, b):
                cp.start()

    ahead = i + (ATT_DEC_BUFS - 1)

    @pl.when(ahead < pl.num_programs(0))
    def _():
        for cp in page_copies(ahead, ahead % ATT_DEC_BUFS):
            cp.start()

    slot = i % ATT_DEC_BUFS
    for cp in page_copies(i, slot):
        cp.wait()
    k_refs = [kbuf_ref.at[slot, p] for p in range(n_pages)]
    v_refs = [vbuf_ref.at[slot, p] for p in range(n_pages)]
    lane = lax.broadcasted_iota(jnp.int32, (1, DIFF_HEAD_DIM), 1)
    rows_q, rows_k, rows_v = [], [], []
    for h in range(DIFF_HEADS):
        sl = slice(h * DIFF_HEAD_DIM, (h + 1) * DIFF_HEAD_DIM)
        qh = x_ref[:, DIFF_WIDTH:2 * DIFF_WIDTH][:, sl] * (DIFF_HALF ** -0.5)
        kh = x_ref[:, 2 * DIFF_WIDTH:3 * DIFF_WIDTH][:, sl]
        vh = x_ref[:, 3 * DIFF_WIDTH:4 * DIFF_WIDTH][:, sl]
        rows_q += [jnp.where(lane < DIFF_HALF, qh, 0.0), jnp.where(lane >= DIFF_HALF, qh, 0.0)]
        rows_k += [kh, kh]
        rows_v += [vh, vh]
    q8 = jnp.concatenate(rows_q, axis=0)
    k8 = jnp.concatenate(rows_k, axis=0)
    v8 = jnp.concatenate(rows_v, axis=0)
    q8b = q8.astype(BF16)
    s = jnp.concatenate(
        [lax.dot_general(q8b, k_refs[p][...].astype(BF16), _NT, preferred_element_type=F32)
         for p in range(n_pages)], axis=1) + bd_ref[...]
    s_new = jnp.sum(q8 * k8, axis=-1, keepdims=True) + bn_ref[:, 0:1]
    m = jnp.maximum(jnp.max(s, axis=-1, keepdims=True), s_new)
    p = jnp.exp(s - m)
    p_new = jnp.exp(s_new - m)
    inv_l = 1.0 / (jnp.sum(p, axis=-1, keepdims=True) + p_new)
    a = p * inv_l
    r = (p_new * inv_l) * v8
    for pg in range(n_pages):
        r = r + jnp.dot(a[:, pg * page_rows:(pg + 1) * page_rows].astype(BF16),
                        v_refs[pg][...].astype(BF16), preferred_element_type=F32)
    lam = _lambda(lam_ref, lam_init)
    for h in range(DIFF_HEADS):
        o = r[2 * h:2 * h + 1, :] - lam * r[2 * h + 1:2 * h + 2, :]
        o_ref[:, h * DIFF_HEAD_DIM:(h + 1) * DIFF_HEAD_DIM] = (
            _rms(o, sub_ref[...]) * (1.0 - lam_init)).astype(BF16)


def attn_decode(proj, cache_k, cache_v, page_table, layer_j, bias_d, bias_n, lam_params, subln, lam_init):
    nb = proj.shape[0]
    n_pages = page_table.shape[1]
    n_pool, n_even = cache_k.shape[:2]
    page_rows = PAGE_SIZE * DIFF_HEADS
    ck = cache_k.reshape(n_pool, n_even, page_rows, DIFF_HEAD_DIM)
    cv = cache_v.reshape(n_pool, n_even, page_rows, DIFF_HEAD_DIM)

    assert nb >= ATT_DEC_BUFS - 1
    full = lambda shape: pl.BlockSpec(shape, lambda i, pt: (0,) * len(shape))
    grid_spec = pltpu.PrefetchScalarGridSpec(
        num_scalar_prefetch=1,
        grid=(nb,),
        in_specs=[pl.BlockSpec((None, 1, EVEN_IN), lambda i, pt: (i, 0, 0)),
                  pl.BlockSpec(memory_space=pl.ANY), pl.BlockSpec(memory_space=pl.ANY),
                  full(bias_d.shape), full(bias_n.shape), full((4, DIFF_HALF)), full((1, LANES))],
        out_specs=pl.BlockSpec((None, 1, DIFF_WIDTH), lambda i, pt: (i, 0, 0)),
        scratch_shapes=[
            pltpu.VMEM((ATT_DEC_BUFS, n_pages, page_rows, DIFF_HEAD_DIM), F32),
            pltpu.VMEM((ATT_DEC_BUFS, n_pages, page_rows, DIFF_HEAD_DIM), F32),
            pltpu.SemaphoreType.DMA((ATT_DEC_BUFS, 2)),
        ],
    )
    out = pl.pallas_call(
        functools.partial(_attn_decode_body, n_pages=n_pages, layer_j=layer_j, lam_init=lam_init),
        grid_spec=grid_spec,
        out_shape=jax.ShapeDtypeStruct((nb, 1, DIFF_WIDTH), BF16),
        compiler_params=_cparams(("arbitrary",)),
        name="attn_decode",
    )(page_table.reshape(-1), proj.reshape(nb, 1, EVEN_IN), ck, cv,
      bias_d, bias_n, lam_params, subln.reshape(1, LANES))
    return out.reshape(nb, DIFF_WIDTH)


def _blockdiag(x):
    lane = lax.broadcasted_iota(jnp.int32, x.shape, 1)
    zero = jnp.zeros_like(x)
    return jnp.concatenate([jnp.where(lane < GDN_C, x, zero), jnp.where(lane >= GDN_C, x, zero)], axis=0)


def _mmp(a, b):
    return jnp.dot(a.astype(BF16), _blockdiag(b.astype(BF16)), preferred_element_type=F32)


def _unit_lower_inverse(lows, ii, jl):
    eye = jnp.where(ii == jl, 1.0, 0.0)
    in_block = (ii // INV_BLOCK) == (jl // INV_BLOCK)
    ps = [jnp.where(in_block, -low, 0.0) for low in lows]
    offs = [jnp.where(in_block, 0.0, low) for low in lows]
    dinvs = [eye + p for p in ps]
    span = 2
    while span < INV_BLOCK:
        ps = [_mmp(p, p) for p in ps]
        dinvs = [d + _mmp(d, p) for d, p in zip(dinvs, ps)]
        span *= 2
    powers = [[-_mmp(d, off) for d, off in zip(dinvs, offs)]]
    span = 2
    while span < GDN_C // INV_BLOCK:
        powers.append([_mmp(n, n) for n in powers[-1]])
        span *= 2
    xs = dinvs
    for pw in reversed(powers):
        xs = [x + _mmp(n, x) for n, x in zip(pw, xs)]
    return xs


def _gdn_prompt_body(h_ref, gn_ref, w_ref, wgt_ref, cw_ref, alog_ref, dtb_ref, on_ref,
                     o_ref, s_ref, cst_ref, ext_ref, z_ref):
    c = pl.program_id(1)
    nh = GDN_HEADS
    npair = nh // 2
    hk = nh * GDN_DK
    cc = GDN_C
    rows_step = GDN_STEP_CHUNKS * cc
    heads_per_tile = GDN_PROJ_TN // LANES

    @pl.when(c == 0)
    def _():
        s_ref[...] = jnp.zeros_like(s_ref)
        ext_ref[0:8, :] = jnp.zeros((8, GDN_QKV), F32)

    @pl.when(c > 0)
    def _():
        ext_ref[0:8, :] = ext_ref[rows_step:rows_step + 8, :]

    xn = _rms(h_ref[...], gn_ref[...]).astype(BF16)
    abt = jnp.transpose(jnp.dot(xn, wgt_ref[...], preferred_element_type=F32))

    def project(n):
        cols = slice(n * GDN_PROJ_TN, (n + 1) * GDN_PROJ_TN)
        tile = jnp.dot(xn, w_ref[:, cols], preferred_element_type=F32)
        if n < GDN_QKV // GDN_PROJ_TN:
            ext_ref[8:8 + rows_step, cols] = tile
        else:
            z_ref[:, n * GDN_PROJ_TN - GDN_QKV:(n + 1) * GDN_PROJ_TN - GDN_QKV] = tile

    def conv_act(col, r0):
        sl = slice(col, col + LANES)
        acc = ext_ref[8 + r0:8 + r0 + cc, sl] * cw_ref[3:4, sl]
        for i in range(GDN_CONV - 1):
            acc = acc + ext_ref[5 + i + r0:5 + i + r0 + cc, sl] * cw_ref[i:i + 1, sl]
        return _silu(acc)

    def l2n(x):
        return x * lax.rsqrt(jnp.sum(x * x, axis=-1, keepdims=True) + EPS)

    def pair(xs):
        return [jnp.concatenate([xs[2 * p], xs[2 * p + 1]], axis=1) for p in range(npair)]

    def row_pair(x):
        return pair([x[h:h + 1, :] for h in range(nh)])

    project(0)
    lane = lax.broadcasted_iota(jnp.int32, (nh, cc), 1)
    q_p, k_p, v_p, gc_b, beta_b, kdec_b, gc_rows, s_decay = [], [], [], [], [], [], [], []
    for ck in range(GDN_STEP_CHUNKS):
        r0 = ck * cc
        g = -jnp.exp(alog_ref[...]) * _softplus(abt[0:nh, r0:r0 + cc] + dtb_ref[...])
        beta = _sigmoid(abt[nh:2 * nh, r0:r0 + cc])
        gc = g
        shift = 1
        while shift < cc:
            gc = gc + jnp.where(lane >= shift, pltpu.roll(gc, shift, 1), 0.0)
            shift *= 2
        g_last = jnp.broadcast_to(gc[:, cc - 1:cc], (nh, cc))
        s_decay.append(jnp.exp(g_last))
        cols = jnp.transpose(jnp.concatenate([gc, beta, jnp.zeros((LANES - 2 * nh, cc), F32)], axis=0))
        bcast = [jnp.broadcast_to(cols[:, n:n + 1], (cc, cc)) for n in range(2 * nh)]
        gcb = pair(bcast[0:nh])
        gc_b += gcb
        beta_b += pair(bcast[nh:2 * nh])
        kdec_b += [jnp.exp(gl - x) for gl, x in zip(row_pair(g_last), gcb)]
        gc_rows += row_pair(gc)
    gam_b = [jnp.exp(x) for x in gc_b]
    ii = lax.broadcasted_iota(jnp.int32, (cc, 2 * cc), 0)
    jl = lax.broadcasted_iota(jnp.int32, (cc, 2 * cc), 1) % cc
    decay = [jnp.exp(jnp.where(ii >= jl, gb - gr, -jnp.inf)) for gb, gr in zip(gc_b, gc_rows)]

    act = {}
    for n in range(1, GDN_MAIN // GDN_PROJ_TN + 1):
        if n < GDN_MAIN // GDN_PROJ_TN:
            project(n)
        if n <= GDN_QKV // GDN_PROJ_TN:
            for hh in range(heads_per_tile):
                col = (n - 1) * GDN_PROJ_TN + hh * LANES
                for ck in range(GDN_STEP_CHUNKS):
                    a = conv_act(col, ck * cc)
                    if col < hk:
                        a = l2n(a) * (GDN_DK ** -0.5)
                    elif col < 2 * hk:
                        a = l2n(a)
                    act[(col, ck)] = a
    for ck in range(GDN_STEP_CHUNKS):
        q_p += pair([act[(h * GDN_DK, ck)] for h in range(nh)])
        k_p += pair([act[(hk + h * GDN_DK, ck)] for h in range(nh)])
        v_p += pair([act[(2 * hk + h * GDN_DV, ck)] for h in range(nh)])

    gram = [lax.dot_general(jnp.concatenate([q, k], axis=0).astype(BF16), _blockdiag(k.astype(BF16)), _NT,
                            preferred_element_type=F32) for q, k in zip(q_p, k_p)]
    qk = [gm[0:cc, :] * d for gm, d in zip(gram, decay)]
    lows = [jnp.where(ii > jl, b * gm[cc:2 * cc, :] * d, 0.0) for b, gm, d in zip(beta_b, gram, decay)]
    tinv = _unit_lower_inverse(lows, ii, jl)
    w =[_mmp(t, b * gm * k) for t, b, gm, k in zip(tinv, beta_b, gam_b, k_p)]
    u0 = [_mmp(t, b * v) for t, b, v in zip(tinv, beta_b, v_p)]
    wq = [jnp.concatenate([wp, gm * q], axis=0).astype(BF16) for wp, gm, q in zip(w, gam_b, q_p)]
    kd = [(kdb * k).astype(BF16) for kdb, k in zip(kdec_b, k_p)]

    state = [s_ref[h] for h in range(nh)]
    zero = jnp.zeros((GDN_DK, GDN_DV), BF16)
    for ck in range(GDN_STEP_CHUNKS):
        r0 = ck * cc
        ent = range(ck * npair, (ck + 1) * npair)
        s_bd = [jnp.concatenate([jnp.concatenate([state[2 * p].astype(BF16), zero], axis=1),
                                 jnp.concatenate([zero, state[2 * p + 1].astype(BF16)], axis=1)], axis=0)
                for p in range(npair)]
        ws_qs = [jnp.dot(wq[e], sb, preferred_element_type=F32) for e, sb in zip(ent, s_bd)]
        u = [u0[e] - x[0:cc, :] for e, x in zip(ent, ws_qs)]
        o = [x[cc:2 * cc, :] + _mmp(qk[e], b) for e, x, b in zip(ent, ws_qs, u)]
        upd = [lax.dot_general(kd[e], b.astype(BF16), _TN, preferred_element_type=F32)
               for e, b in zip(ent, u)]
        new_state = []
        for h in range(nh):
            half = slice((h % 2) * cc, (h % 2 + 1) * cc)
            new_state.append(s_decay[ck][h:h + 1, :] * state[h] + upd[h // 2][half, half])
            z = z_ref[r0:r0 + cc, h * GDN_DV:(h + 1) * GDN_DV]
            o_ref[r0:r0 + cc, h * GDN_DV:(h + 1) * GDN_DV] = (
                _rms(o[h // 2][:, half], on_ref[...]) * _silu(z)).astype(BF16)
        state = new_state
    for h in range(nh):
        s_ref[h] = state[h]

    @pl.when(c == pl.num_programs(1) - 1)
    def _():
        cst_ref[...] = ext_ref[rows_step:rows_step + 8, :]


def gdn_prompt(h, gain, w_in, w_gates, idx, conv_w, a_log, dt_bias, o_norm):
    b, t, d = h.shape
    rows = GDN_STEP_CHUNKS * GDN_C
    nc = t // rows
    nh = GDN_HEADS
    return pl.pallas_call(
        _gdn_prompt_body,
        grid=(b, nc),
        in_specs=[
            pl.BlockSpec((None, rows, d), lambda i, c: (i, c, 0)),
            pl.BlockSpec((1, d), lambda i, c: (0, 0)),
            pl.BlockSpec((None, d, GDN_MAIN), lambda i, c: (idx, 0, 0), pipeline_mode=pl.Buffered(1)),
            pl.BlockSpec((None, d, LANES), lambda i, c: (idx, 0, 0), pipeline_mode=pl.Buffered(1)),
            pl.BlockSpec((GDN_CONV, GDN_QKV), lambda i, c: (0, 0)),
            pl.BlockSpec((nh, 1), lambda i, c: (0, 0)),
            pl.BlockSpec((nh, 1), lambda i, c: (0, 0)),
            pl.BlockSpec((1, GDN_DV), lambda i, c: (0, 0)),
        ],
        out_specs=[
            pl.BlockSpec((None, rows, nh * GDN_DV), lambda i, c: (i, c, 0)),
            pl.BlockSpec((None, nh, GDN_DK, GDN_DV), lambda i, c: (i, 0, 0, 0)),
            pl.BlockSpec((None, 8, GDN_QKV), lambda i, c: (i, 0, 0)),
        ],
        out_shape=[
            jax.ShapeDtypeStruct((b, t, nh * GDN_DV), BF16),
            jax.ShapeDtypeStruct((b, nh, GDN_DK, GDN_DV), F32),
            jax.ShapeDtypeStruct((b, 8, GDN_QKV), F32),
        ],
        scratch_shapes=[pltpu.VMEM((8 + rows, GDN_QKV), F32), pltpu.VMEM((rows, nh * GDN_DV), F32)],
        compiler_params=_cparams(("parallel", "arbitrary")),
        name="gdn_prompt",
    )(h, gain.reshape(1, d), w_in, w_gates, conv_w, a_log.reshape(nh, 1), dt_bias.reshape(nh, 1),
      o_norm.reshape(1, GDN_DV))


def _gdn_decode_body(x_ref, cs_ref, ab_ref, cw_ref, alog_ref, dtb_ref, on_ref, s0_ref, *rest, first, layer_j):
    o_ref, s_ref = rest[-2:]
    nh = GDN_HEADS
    pad = jnp.zeros((8 - 2, GDN_DK), F32)
    if first:
        for slot in range(s_ref.shape[0]):
            if slot != layer_j:
                s_ref[slot] = jnp.zeros(s_ref.shape[1:], F32)
    pad7 = jnp.zeros((8 - 1, GDN_DK), F32)
    seqs = range(GDN_DEC_ROWS)
    q8, k8, v8, gam8, beta8, qk8 = [], [], [], [], [], []
    for r in seqs:
        x = x_ref[r, 0:3 * nh, :]
        conv = x * cw_ref[GDN_CONV - 1]
        for i in range(GDN_CONV - 1):
            conv = conv + cs_ref[r, i] * cw_ref[i]
        act = conv * _sigmoid(conv)
        qa, ka = act[0:nh], act[nh:2 * nh]
        q8.append(qa * lax.rsqrt(jnp.sum(qa * qa, axis=-1, keepdims=True) + EPS) * (GDN_DK ** -0.5))
        k8.append(ka * lax.rsqrt(jnp.sum(ka * ka, axis=-1, keepdims=True) + EPS))
        v8.append(act[2 * nh:3 * nh])
        g = -jnp.exp(alog_ref[...]) * _softplus(ab_ref[r, 0:nh, :] + dtb_ref[...])
        gam8.append(jnp.broadcast_to(jnp.exp(g), (nh, GDN_DV)))
        beta8.append(jnp.broadcast_to(_sigmoid(ab_ref[r, nh:2 * nh, :]), (nh, GDN_DV)))
        qk8.append(jnp.broadcast_to(jnp.sum(q8[r] * k8[r], axis=-1, keepdims=True), (nh, GDN_DV)))
    pairs = [(r, h) for r in seqs for h in range(nh)]
    s_old = [s0_ref[r, h] for r, h in pairs]
    qk_s = [jnp.dot(jnp.concatenate([q8[r][h:h + 1, :], k8[r][h:h + 1, :], pad], axis=0).astype(BF16),
                    s.astype(BF16), preferred_element_type=F32) for (r, h), s in zip(pairs, s_old)]
    u = [beta8[r][h:h + 1, :] * (v8[r][h:h + 1, :] - gam8[r][h:h + 1, :] * x[1:2, :])
         for (r, h), x in zip(pairs, qk_s)]
    outs = [gam8[r][h:h + 1, :] * x[0:1, :] + qk8[r][h:h + 1, :] * b for (r, h), x, b in zip(pairs, qk_s, u)]
    outer = [lax.dot_general(jnp.concatenate([k8[r][h:h + 1, :], pad7], axis=0).astype(BF16),
                             jnp.concatenate([b, pad7], axis=0).astype(BF16), _TN, preferred_element_type=F32)
             for (r, h), b in zip(pairs, u)]
    for (r, h), s, x in zip(pairs, s_old, outer):
        new = gam8[r][h:h + 1, :] * s + x
        if first:
            s_ref[layer_j, r, h] = new
        else:
            s_ref[r, h] = new
    for r in seqs:
        o8 = jnp.concatenate(outs[r * nh:(r + 1) * nh], axis=0)
        z8 = x_ref[r, 3 * nh:4 * nh, :]
        o_ref[r] = (_rms(o8, on_ref[...]) * (z8 * _sigmoid(z8))).astype(BF16)


def gdn_decode(proj, gates, conv_state, s0_all, layer_j, s_new_all, conv_w, a_log, dt_bias, o_norm):
    nb = proj.shape[0]
    nh = GDN_HEADS
    nrow = GDN_QKV // LANES
    first = s_new_all is None
    in_specs = [
        pl.BlockSpec((GDN_DEC_ROWS, GDN_MAIN // LANES, LANES), lambda i: (i, 0, 0)),
        pl.BlockSpec((GDN_DEC_ROWS, GDN_CONV - 1, nrow, LANES), lambda i: (i, 0, 0, 0)),
        pl.BlockSpec((GDN_DEC_ROWS, 2 * nh, 1), lambda i: (i, 0, 0)),
        pl.BlockSpec((GDN_CONV, nrow, LANES), lambda i: (0, 0, 0)),
        pl.BlockSpec((nh, 1), lambda i: (0, 0)),
        pl.BlockSpec((nh, 1), lambda i: (0, 0)),
        pl.BlockSpec((1, GDN_DV), lambda i: (0, 0)),
        pl.BlockSpec((None, GDN_DEC_ROWS, nh, GDN_DK, GDN_DV), lambda i: (layer_j, i, 0, 0, 0)),
    ]
    args = [proj.reshape(nb, GDN_MAIN // LANES, LANES),
            conv_state.reshape(nb, GDN_CONV - 1, nrow, LANES),
            gates.reshape(nb, 2 * nh, 1),
            conv_w.reshape(GDN_CONV, nrow, LANES),
            a_log.reshape(nh, 1), dt_bias.reshape(nh, 1), o_norm.reshape(1, GDN_DV), s0_all]
    if first:
        s_spec = pl.BlockSpec((s0_all.shape[0], GDN_DEC_ROWS, nh, GDN_DK, GDN_DV), lambda i: (0, i, 0, 0, 0))
        aliases = {}
    else:
        in_specs.append(pl.BlockSpec(memory_space=pl.ANY))
        args.append(s_new_all)
        s_spec = pl.BlockSpec((None, GDN_DEC_ROWS, nh, GDN_DK, GDN_DV), lambda i: (layer_j, i, 0, 0, 0))
        aliases = {len(args) - 1: 1}
    o, s = pl.pallas_call(
        functools.partial(_gdn_decode_body, first=first, layer_j=layer_j),
        grid=(nb // GDN_DEC_ROWS,),
        in_specs=in_specs,
        out_specs=[pl.BlockSpec((GDN_DEC_ROWS, nh, GDN_DV), lambda i: (i, 0, 0)), s_spec],
        out_shape=[
            jax.ShapeDtypeStruct((nb, nh, GDN_DV), BF16),
            jax.ShapeDtypeStruct(s0_all.shape, F32),
        ],
        input_output_aliases=aliases,
        compiler_params=_cparams(("parallel",)),
        name="gdn_decode",
    )(*args)
    return o.reshape(nb, nh * GDN_DV), s


def kernel(x_prompt, x_sample, cache_k, cache_v, page_table, state_pool, state_conv, state_delta,
           norm_mix, norm_ffn, norm_final, rel_bias,
           w_in_even, pool_w, pool_scale, lambda_q1, lambda_k1, lambda_q2, lambda_k2, subln_w, w_out_even,
           w_in_odd, conv_w, a_log, dt_bias, o_norm, w_out_odd,
           w_gate_up, w_down):
    bp, t, d = x_prompt.shape
    bs = x_sample.shape[0]
    mp = bp * t
    nh = GDN_HEADS
    past = page_table.shape[1] * PAGE_SIZE
    tm_p = 1024
    tm_s = bs

    hp = x_prompt.reshape(mp, d)
    hs = x_sample.reshape(bs, d)
    bias_p, bias_d, bias_n = rel_bias_tiles(rel_bias, past)

    w_in_e = w_in_even.astype(BF16)
    w_in_o = w_in_odd.astype(BF16)
    w_gates = jnp.pad(w_in_odd[:, :, GDN_MAIN:], ((0, 0), (0, 0), (0, LANES - 2 * nh))).astype(BF16)
    w_out_e = w_out_even.astype(BF16)
    w_out_o = w_out_odd.astype(BF16)
    w_gu = w_gate_up.astype(BF16)
    w_d = w_down.astype(BF16)
    pool_wb = pool_w.astype(BF16)

    k_p, v_p, k_s, v_s, pool_p, pool_s = [], [], [], [], [], []
    conv_p, conv_s, delta_p = [], [], []
    delta_s = None
    for layer in range(DEPTH):
        j = layer // 2
        last = layer == DEPTH - 1
        if layer % 2 == 0:
            w_out = w_out_e
            pw = pool_wb[j]
            lam_init = 0.8 - 0.6 * math.exp(-0.3 * layer)
            lam_params = jnp.stack([lambda_q1[j], lambda_k1[j], lambda_q2[j], lambda_k2[j]])

            qkv_p, ypool_p, kp, vp, tail_p = even_proj_prompt(hp, norm_mix[layer], w_in_e, j, pw, pool_scale[j],
                                                              t, tm=tm_p)
            proj_s, ks_, vs_ = norm_matmul(hs, norm_mix[layer], w_in_e, j, tm=tm_s, tn=DIFF_WIDTH)

            oatt_p = attn_prompt(qkv_p.reshape(bp, t, 3 * DIFF_WIDTH), bias_p, lam_params, subln_w[j], lam_init)
            ypool_s = pool_decode(jnp.transpose(state_pool[j], (1, 0, 2)), proj_s, pw, pool_scale[j])
            oatt_s = attn_decode(proj_s, cache_k, cache_v, page_table, j, bias_d, bias_n,
                                 lam_params, subln_w[j], lam_init)

            kv_shape = (DIFF_HEADS, DIFF_HEAD_DIM)
            k_p.append(kp.reshape(bp, t, *kv_shape))
            v_p.append(vp.reshape(bp, t, *kv_shape))
            k_s.append(ks_.reshape(bs, 1, *kv_shape))
            v_s.append(vs_.reshape(bs, 1, *kv_shape))
            pool_p.append(tail_p[:, POOL_PAD - POOL_BUF:, :])
            pool_s.append(jnp.concatenate([state_pool[j][:, 1:], proj_s[:, None, 0:POOL_WIDTH]], axis=1))

            mix_p = ((ypool_p, 0), (oatt_p.reshape(mp, DIFF_WIDTH), 0))
            mix_s = ((ypool_s, 0), (oatt_s, 0))
        else:
            w_out = w_out_o

            proj_s, gates_s = norm_matmul(hs, norm_mix[layer], w_in_o, j, w_gates, tm=tm_s, tn=512)

            o_p, s_p, tail_p = gdn_prompt(hp.reshape(bp, t, d), norm_mix[layer], w_in_o, w_gates, j,
                                          conv_w[j], a_log[j], dt_bias[j], o_norm[j])
            o_s, delta_s = gdn_decode(proj_s, gates_s[0:2 * nh].T, state_conv[j], state_delta, j, delta_s,
                                      conv_w[j], a_log[j], dt_bias[j], o_norm[j])

            conv_p.append(tail_p[:, 8 - (GDN_CONV - 1):, :])
            conv_s.append(jnp.concatenate([state_conv[j][:, 1:], proj_s[:, None, 0:GDN_QKV]], axis=1))
            delta_p.append(s_p)

            o_p = o_p.reshape(mp, nh * GDN_DV)
            mix_p = ((o_p, 0), (o_p, 1))
            mix_s = ((o_s, 0), (o_s, 1))

        g_fin = norm_final if last else None
        hp = outproj_ffn(hp, mix_p[0], mix_p[1], w_out, j, norm_ffn[layer], w_gu, w_d, layer, g_fin, tm=FFN_TM)
        hs = outproj_ffn(hs, mix_s[0], mix_s[1], w_out, j, norm_ffn[layer], w_gu, w_d, layer, g_fin, tm=tm_s)

    return (hp.reshape(bp, t, d), hs.reshape(bs, 1, d),
            jnp.stack(k_p, axis=1), jnp.stack(v_p, axis=1), jnp.stack(k_s, axis=1), jnp.stack(v_s, axis=1),
            jnp.stack(pool_p), jnp.stack(pool_s), jnp.stack(conv_p), jnp.stack(conv_s),
            jnp.stack(delta_p), delta_s)
```

```python
import functools
import math

import jax
import jax.numpy as jnp
from jax import lax
from jax.experimental import pallas as pl
from jax.experimental.pallas import tpu as pltpu

F32 = jnp.float32
BF16 = jnp.bfloat16

D_MODEL = 1024
DEPTH = 4
PAGE_SIZE = 128
POOL_WIDTH = 512
POOL_WINDOWS = (2, 4, 8, 16)
POOL_GROUP = 128
POOL_BUF = 15
DIFF_HEADS = 4
DIFF_HALF = 64
DIFF_HEAD_DIM = 128
DIFF_WIDTH = 512
EVEN_IN = 2048
REL_BUCKETS = 32
REL_MAX_DIST = 128
GDN_HEADS = 8
GDN_DK = 128
GDN_DV = 128
GDN_CONV = 4
GDN_QKV = 3072
GDN_MAIN = 4096
D_FF = 2816
EPS = 1e-6

LANES = 128
VMEM_LIMIT = 48 * 1024 * 1024
NEG_BIG = -1e30

ATT_TQ = 512
ATT_TK = 512
GDN_C = 128
GDN_STEP_CHUNKS = 2
GDN_DEC_ROWS = 4
ATT_DEC_BUFS = 3
GDN_PROJ_TN = 512
INV_BLOCK = 16
FFN_TF = 256
FFN_TM = 512

_NT = (((1,), (1,)), ((), ()))
_TN = (((0,), (0,)), ((), ()))


def _cparams(sem):
    return pltpu.CompilerParams(dimension_semantics=sem, vmem_limit_bytes=VMEM_LIMIT)


def _sigmoid(x):
    return 1.0 / (1.0 + jnp.exp(-x))


def _silu(x):
    h = 0.5 * x
    return h + h * jnp.tanh(h)


def _softplus(x):
    return jnp.maximum(x, 0.0) + jnp.log1p(jnp.exp(-jnp.abs(x)))


def _rms(x, gain):
    return x * lax.rsqrt(jnp.mean(x * x, axis=-1, keepdims=True) + EPS) * gain


def _mm(a, b):
    return jnp.dot(a.astype(BF16), b.astype(BF16), preferred_element_type=F32)


def _norm_mm_kv_body(x_ref, g_ref, w_ref, o_ref, k_ref, v_ref, xn_ref):
    j = pl.program_id(1)

    @pl.when(j == 0)
    def _():
        xn_ref[...] = _rms(x_ref[...], g_ref[...]).astype(BF16)

    res = jnp.dot(xn_ref[...], w_ref[...], preferred_element_type=F32)
    o_ref[...] = res

    def head_rows(dst_ref):
        for h in range(DIFF_HEADS):
            dst_ref[pl.ds(h, res.shape[0], stride=DIFF_HEADS), :] = res[:, h * DIFF_HEAD_DIM:(h + 1) * DIFF_HEAD_DIM]

    @pl.when(j == 2)
    def _():
        head_rows(k_ref)

    @pl.when(j == 3)
    def _():
        head_rows(v_ref)


def _norm_mm_gate_body(x_ref, g_ref, w_ref, wst_ref, o_ref, ost_ref, xn_ref):
    @pl.when(pl.program_id(1) == 0)
    def _():
        xn = _rms(x_ref[...], g_ref[...]).astype(BF16)
        xn_ref[...] = xn
        ost_ref[...] = lax.dot_general(wst_ref[...], xn, _NT, preferred_element_type=F32)

    o_ref[...] = jnp.dot(xn_ref[...], w_ref[...], preferred_element_type=F32)


def norm_matmul(x, gain, w, idx, w_gates=None, *, tm, tn):
    m, d = x.shape
    n = EVEN_IN if w_gates is None else GDN_MAIN
    grid = (m // tm, n // tn)
    in_specs = [
        pl.BlockSpec((tm, d), lambda i, j: (i, 0)),
        pl.BlockSpec((1, d), lambda i, j: (0, 0)),
        pl.BlockSpec((None, d, tn), lambda i, j: (idx, 0, j)),
    ]
    out_specs = pl.BlockSpec((tm, tn), lambda i, j: (i, j))
    out_shape = jax.ShapeDtypeStruct((m, n), F32)
    args = [x, gain.reshape(1, d), w]
    if w_gates is None:
        assert tn == DIFF_WIDTH and n == EVEN_IN
        kv_spec = pl.BlockSpec((tm * DIFF_HEADS, DIFF_HEAD_DIM), lambda i, j: (i, 0))
        kv_shape = jax.ShapeDtypeStruct((m * DIFF_HEADS, DIFF_HEAD_DIM), F32)
        out_specs = [out_specs, kv_spec, kv_spec]
        out_shape = [out_shape, kv_shape, kv_shape]
        body = _norm_mm_kv_body
    else:
        in_specs.append(pl.BlockSpec((None, LANES, d), lambda i, j: (idx, 0, 0)))
        out_specs = [out_specs, pl.BlockSpec((LANES, tm), lambda i, j: (0, i))]
        out_shape = [out_shape, jax.ShapeDtypeStruct((LANES, m), F32)]
        args.append(w_gates)
        body = _norm_mm_gate_body
    return pl.pallas_call(
        body,
        grid=grid,
        in_specs=in_specs,
        out_specs=out_specs,
        out_shape=out_shape,
        scratch_shapes=[pltpu.VMEM((tm, d), BF16)],
        compiler_params=_cparams(("parallel", "arbitrary")),
        name="norm_matmul",
    )(*args)


def _ffn_body(*refs, final):
    if final:
        res_ref, a0_ref, a1_ref, wo_ref, gf_ref, wgu_ref, wd_ref, gfin_ref, o_ref = refs
    else:
        res_ref, a0_ref, a1_ref, wo_ref, gf_ref, wgu_ref, wd_ref, o_ref = refs
    mixed = jnp.concatenate([a0_ref[...], a1_ref[...]], axis=1)
    h1 = res_ref[...] + jnp.dot(mixed, wo_ref[...], preferred_element_type=F32)
    xn = _rms(h1, gf_ref[...]).astype(BF16)
    acts = []
    for f in range(D_FF // FFN_TF):
        g = jnp.dot(xn, wgu_ref[:, f * FFN_TF:(f + 1) * FFN_TF], preferred_element_type=F32)
        u = jnp.dot(xn, wgu_ref[:, D_FF + f * FFN_TF:D_FF + (f + 1) * FFN_TF], preferred_element_type=F32)
        acts.append((g * _sigmoid(g) * u).astype(BF16))
    y = h1 + jnp.dot(jnp.concatenate(acts, axis=1), wd_ref[...], preferred_element_type=F32)
    if final:
        y = _rms(y, gfin_ref[...])
    o_ref[...] = y


def outproj_ffn(res, mix0, mix1, w_out, idx_out, g_ffn, w_gu, w_d, layer, g_final=None, *, tm):
    m, d = res.shape
    half = w_out.shape[1] // 2
    (m0, c0), (m1, c1) = mix0, mix1
    final = g_final is not None
    once = pl.Buffered(1)
    in_specs = [
        pl.BlockSpec((tm, d), lambda i: (i, 0)),
        pl.BlockSpec((tm, half), lambda i: (i, c0)),
        pl.BlockSpec((tm, half), lambda i: (i, c1)),
        pl.BlockSpec((None, 2 * half, d), lambda i: (idx_out, 0, 0), pipeline_mode=once),
        pl.BlockSpec((1, d), lambda i: (0, 0)),
        pl.BlockSpec((None, d, 2 * D_FF), lambda i: (layer, 0, 0), pipeline_mode=once),
        pl.BlockSpec((None, D_FF, d), lambda i: (layer, 0, 0), pipeline_mode=once),
    ]
    args = [res, m0, m1, w_out, g_ffn.reshape(1, d), w_gu, w_d]
    if final:
        in_specs.append(pl.BlockSpec((1, d), lambda i: (0, 0)))
        args.append(g_final.reshape(1, d))
    return pl.pallas_call(
        functools.partial(_ffn_body, final=final),
        grid=(m // tm,),
        in_specs=in_specs,
        out_specs=pl.BlockSpec((tm, d), lambda i: (i, 0)),
        out_shape=jax.ShapeDtypeStruct((m, d), F32),
        compiler_params=_cparams(("parallel",)),
        name="outproj_ffn",
    )(*args)


POOL_PAD = 16
POOL_ROWS = 512


def _even_proj_body(x_ref, g_ref, w_ref, pw_ref, ps_ref, *rest, blocks_per_seq, first, layer_j):
    qkv_ref, yp_ref, k_ref, v_ref, tail_ref, xn_ref, ext_ref = rest[-7:]
    i = pl.program_id(0)
    j = pl.program_id(1)
    tm = x_ref.shape[0]

    @pl.when(j == 0)
    def _():
        xn_ref[...] = _rms(x_ref[...], g_ref[...]).astype(BF16)

    res = jnp.dot(xn_ref[...], w_ref[...], preferred_element_type=F32)

    blk = i % blocks_per_seq

    @pl.when((j == 0) & (blk == 0))
    def _():
        ext_ref[0:POOL_PAD, :] = jnp.zeros((POOL_PAD, POOL_WIDTH), F32)

    @pl.when((j == 0) & (blk > 0))
    def _():
        ext_ref[0:POOL_PAD, :] = ext_ref[tm:tm + POOL_PAD, :]

    @pl.when(j == 0)
    def _():
        ext_ref[POOL_PAD:, :] = res
        tail_ref[...] = res[tm - POOL_PAD:tm, :]
        for r0 in range(0, tm, POOL_ROWS):
            pos = lax.broadcasted_iota(jnp.int32, (POOL_ROWS, 1), 0) + (blk * tm + r0)
            for g, w in enumerate(POOL_WINDOWS):
                sl = slice(g * POOL_GROUP, (g + 1) * POOL_GROUP)
                cur = ext_ref[POOL_PAD + r0:POOL_PAD + r0 + POOL_ROWS, sl]
                win = cur
                for s in range(1, w):
                    win = win + ext_ref[POOL_PAD + r0 - s:POOL_PAD + r0 - s + POOL_ROWS, sl]
                cnt = jnp.minimum(pos + 1, w).astype(F32)
                dlt = win / cnt - cur
                y = jnp.dot(dlt.astype(BF16), pw_ref[g], preferred_element_type=F32) * ps_ref[:, sl]
                yp_ref[r0:r0 + POOL_ROWS, sl] = y.astype(BF16)

    @pl.when(j >= 1)
    def _():
        qkv_ref[...] = res.astype(BF16)

    def head_rows(dst_ref):
        if first:
            for slot in range(dst_ref.shape[0]):
                if slot != layer_j:
                    dst_ref[slot] = jnp.zeros(dst_ref.shape[1:], F32)
            dst_ref = dst_ref.at[layer_j]
        for h in range(DIFF_HEADS):
            dst_ref[pl.ds(h, tm, stride=DIFF_HEADS), :] = res[:, h * DIFF_HEAD_DIM:(h + 1) * DIFF_HEAD_DIM]

    @pl.when(j == 2)
    def _():
        head_rows(k_ref)

    @pl.when(j == 3)
    def _():
        head_rows(v_ref)


def even_proj_prompt(x, gain, w, idx, pool_w, pool_scale, seq_len, n_layers, kv_all, *, tm):
    m, d = x.shape
    tn = DIFF_WIDTH
    blocks_per_seq = seq_len // tm
    nseq = m // seq_len
    first = kv_all is None
    in_specs = [
        pl.BlockSpec((tm, d), lambda i, j: (i, 0)),
        pl.BlockSpec((1, d), lambda i, j: (0, 0)),
        pl.BlockSpec((None, d, tn), lambda i, j: (idx, 0, j)),
        pl.BlockSpec((len(POOL_WINDOWS), POOL_GROUP, POOL_GROUP), lambda i, j: (0, 0, 0)),
        pl.BlockSpec((1, POOL_WIDTH), lambda i, j: (0, 0)),
    ]
    args = [x, gain.reshape(1, d), w, pool_w, pool_scale.reshape(1, POOL_WIDTH)]
    kv_rows = tm * DIFF_HEADS
    if first:
        kv_spec = pl.BlockSpec((None, n_layers, kv_rows, DIFF_HEAD_DIM),
                               lambda i, j: (i // blocks_per_seq, 0, i % blocks_per_seq, 0))
        aliases = {}
    else:
        in_specs += [pl.BlockSpec(memory_space=pl.ANY), pl.BlockSpec(memory_space=pl.ANY)]
        args += list(kv_all)
        kv_spec = pl.BlockSpec((None, None, kv_rows, DIFF_HEAD_DIM),
                               lambda i, j: (i // blocks_per_seq, idx, i % blocks_per_seq, 0))
        aliases = {len(args) - 2: 2, len(args) - 1: 3}
    kv_shape = jax.ShapeDtypeStruct((nseq, n_layers, seq_len * DIFF_HEADS, DIFF_HEAD_DIM), F32)
    qkv, ypool, k_all, v_all, tail = pl.pallas_call(
        functools.partial(_even_proj_body, blocks_per_seq=blocks_per_seq, first=first, layer_j=idx),
        grid=(m // tm, EVEN_IN // tn),
        in_specs=in_specs,
        out_specs=[
            pl.BlockSpec((tm, tn), lambda i, j: (i, jnp.maximum(j - 1, 0))),
            pl.BlockSpec((tm, POOL_WIDTH), lambda i, j: (i, 0)),
            kv_spec,
            kv_spec,
            pl.BlockSpec((None, POOL_PAD, POOL_WIDTH), lambda i, j: (i // blocks_per_seq, 0, 0)),
        ],
        out_shape=[
            jax.ShapeDtypeStruct((m, 3 * DIFF_WIDTH), BF16),
            jax.ShapeDtypeStruct((m, POOL_WIDTH), BF16),
            kv_shape,
            kv_shape,
            jax.ShapeDtypeStruct((nseq, POOL_PAD, POOL_WIDTH), F32),
        ],
        scratch_shapes=[pltpu.VMEM((tm, d), BF16), pltpu.VMEM((POOL_PAD + tm, POOL_WIDTH), F32)],
        input_output_aliases=aliases,
        compiler_params=_cparams(("arbitrary", "arbitrary")),
        name="even_proj",
    )(*args)
    return qkv, ypool, tail, (k_all, v_all)


def _pool_decode_body(st_ref, u_ref, pw_ref, ps_ref, o_ref):
    for g, w in enumerate(POOL_WINDOWS):
        sl = slice(g * POOL_GROUP, (g + 1) * POOL_GROUP)
        cur = u_ref[:, sl]
        win = cur
        for i in range(1, w):
            win = win + st_ref[POOL_BUF - i, :, sl]
        dlt = win / float(w) - cur
        y = jnp.dot(dlt.astype(BF16), pw_ref[g], preferred_element_type=F32) * ps_ref[:, sl]
        o_ref[:, sl] = y.astype(BF16)


def pool_decode(state_t, proj, pool_w, pool_scale):
    nb = proj.shape[0]
    return pl.pallas_call(
        _pool_decode_body,
        grid=(1,),
        in_specs=[
            pl.BlockSpec((POOL_BUF, nb, POOL_WIDTH), lambda i: (0, 0, 0)),
            pl.BlockSpec((nb, POOL_WIDTH), lambda i: (0, 0)),
            pl.BlockSpec((len(POOL_WINDOWS), POOL_GROUP, POOL_GROUP), lambda i: (0, 0, 0)),
            pl.BlockSpec((1, POOL_WIDTH), lambda i: (0, 0)),
        ],
        out_specs=pl.BlockSpec((nb, POOL_WIDTH), lambda i: (0, 0)),
        out_shape=jax.ShapeDtypeStruct((nb, POOL_WIDTH), BF16),
        compiler_params=_cparams(("arbitrary",)),
        name="pool_decode",
    )(state_t, proj, pool_w, pool_scale.reshape(1, POOL_WIDTH))


def _rel_bucket(n):
    max_exact = REL_BUCKETS // 2
    nf = jnp.maximum(n, 1).astype(F32)
    large = max_exact + (jnp.log(nf / max_exact) / math.log(REL_MAX_DIST / max_exact)
                         * (REL_BUCKETS - max_exact)).astype(jnp.int32)
    large = jnp.minimum(large, REL_BUCKETS - 1)
    return jnp.where(n < max_exact, n, large)


def _table_lookup(tab_ref, bucket, h):
    out = jnp.zeros(bucket.shape, F32)
    for b in range(REL_BUCKETS):
        out = jnp.where(bucket == b, tab_ref[b, h], out)
    return out


def _rel_bias_body(tab_ref, bp_ref, bd_ref, bn_ref, *, past):
    ii = lax.broadcasted_iota(jnp.int32, (ATT_TQ, ATT_TK), 0)
    jj = lax.broadcasted_iota(jnp.int32, (ATT_TQ, ATT_TK), 1)
    for h in range(DIFF_HEADS):
        for t in range(3):
            dist = t * ATT_TK + ii - jj
            bias = _table_lookup(tab_ref, _rel_bucket(jnp.maximum(dist, 0)), h)
            bp_ref[h, t] = jnp.where(dist >= 0, bias, NEG_BIG)
    nrow = 2 * DIFF_HEADS
    row = lax.broadcasted_iota(jnp.int32, (nrow, past * DIFF_HEADS), 0)
    col = lax.broadcasted_iota(jnp.int32, (nrow, past * DIFF_HEADS), 1)
    bucket = _rel_bucket(past - col // DIFF_HEADS)
    rown = lax.broadcasted_iota(jnp.int32, (nrow, LANES), 0)
    bd = jnp.full((nrow, past * DIFF_HEADS), NEG_BIG, F32)
    bn = jnp.zeros((nrow, LANES), F32)
    for h in range(DIFF_HEADS):
        own = jnp.where(row // 2 == h, col % DIFF_HEADS, -1) == h
        bd = jnp.where(own, _table_lookup(tab_ref, bucket, h), bd)
        bn = jnp.where(rown // 2 == h, tab_ref[0, h], bn)
    bd_ref[...] = bd
    bn_ref[...] = bn


def rel_bias_tiles(rel_bias, past):
    return pl.pallas_call(
        functools.partial(_rel_bias_body, past=past),
        in_specs=[pl.BlockSpec(memory_space=pltpu.SMEM)],
        out_specs=[
            pl.BlockSpec(memory_space=pltpu.VMEM),
            pl.BlockSpec(memory_space=pltpu.VMEM),
            pl.BlockSpec(memory_space=pltpu.VMEM),
        ],
        out_shape=[
            jax.ShapeDtypeStruct((DIFF_HEADS, 3, ATT_TQ, ATT_TK), F32),
            jax.ShapeDtypeStruct((2 * DIFF_HEADS, past * DIFF_HEADS), F32),
            jax.ShapeDtypeStruct((2 * DIFF_HEADS, LANES), F32),
        ],
        compiler_params=pltpu.CompilerParams(vmem_limit_bytes=VMEM_LIMIT),
        name="rel_bias_tiles",
    )(rel_bias)


def _lambda(lam_ref, lam_init):
    lp = lam_ref[...]
    s1 = jnp.sum(lp[0:1, :] * lp[1:2, :], axis=-1, keepdims=True)
    s2 = jnp.sum(lp[2:3, :] * lp[3:4, :], axis=-1, keepdims=True)
    return jnp.exp(s1) - jnp.exp(s2) + lam_init


def _attn_prompt_body(q_ref, k_ref, v_ref, bias_ref, lam_ref, sub_ref, o_ref,
                      vb_ref, qq_ref, m_ref, acc_ref, *, lam_init):
    kb_ref = k_ref
    nq = k_ref.shape[0] // ATT_TQ
    vb_ref[:, 0:LANES] = v_ref[...]
    vb_ref[:, LANES:2 * LANES] = jnp.ones((vb_ref.shape[0], LANES), BF16)
    lane = lax.broadcasted_iota(jnp.int32, (ATT_TQ, LANES), 1)
    for qi in range(nq):
        q = q_ref[qi * ATT_TQ:(qi + 1) * ATT_TQ, :].astype(F32) * (DIFF_HALF ** -0.5)
        qq_ref[qi] = jnp.concatenate([jnp.where(lane < DIFF_HALF, q, 0.0),
                                      jnp.where(lane >= DIFF_HALF, q, 0.0)], axis=0).astype(BF16)
    m_ref[...] = jnp.full(m_ref.shape, -jnp.inf, F32)
    acc_ref[...] = jnp.zeros_like(acc_ref)

    half = ATT_TQ // 2
    for w in range(nq):
        pieces = []
        for qi in range(w, nq):
            tile = min(qi - w, 2)
            parts = [(0, half, half), (half, half, ATT_TK)] if tile == 0 else [(0, ATT_TQ, ATT_TK)]
            for idx in range(2):
                for r0, nr, nk in parts:
                    pieces.append((qi, slice(idx * ATT_TQ + r0, idx * ATT_TQ + r0 + nr), nk, tile, r0))
        s_l = [lax.dot_general(qq_ref[qi, rows, :], kb_ref[w * ATT_TK:w * ATT_TK + nk, :], _NT,
                               preferred_element_type=F32) + bias_ref[tile, r0:r0 + rows.stop - rows.start, 0:nk]
               for qi, rows, nk, tile, r0 in pieces]
        m_prev = [m_ref[qi, rows, :] for qi, rows, _, _, _ in pieces]
        m_new = [jnp.maximum(mp, jnp.max(s, axis=-1, keepdims=True)) for mp, s in zip(m_prev, s_l)]
        p_l = [jnp.exp(s - jnp.concatenate([mn] * (pc[2] // LANES), axis=1)).astype(BF16)
               for s, mn, pc in zip(s_l, m_new, pieces)]
        for (qi, rows, nk, _, _), mp, mn, p in zip(pieces, m_prev, m_new, p_l):
            alpha = jnp.exp(mp - mn)
            acc_ref[qi, rows, :] = (jnp.concatenate([alpha, alpha], axis=1) * acc_ref[qi, rows, :]
                                    + jnp.dot(p, vb_ref[w * ATT_TK:w * ATT_TK + nk, :],
                                              preferred_element_type=F32))
            m_ref[qi, rows, :] = mn

    lam = _lambda(lam_ref, lam_init)
    for qi in range(nq):
        acc = acc_ref[qi]
        o_all = acc[:, 0:LANES] / acc[:, LANES:2 * LANES]
        o = o_all[0:ATT_TQ, :] - lam * o_all[ATT_TQ:2 * ATT_TQ, :]
        o_ref[qi * ATT_TQ:(qi + 1) * ATT_TQ, :] = (_rms(o, sub_ref[...]) * (1.0 - lam_init)).astype(BF16)


def attn_prompt(qkv, bias_p, lam_params, subln, lam_init):
    b, t, _ = qkv.shape
    nh = DIFF_HEADS
    nq = t // ATT_TQ
    assert ATT_TQ == ATT_TK
    proj = qkv
    return pl.pallas_call(
        functools.partial(_attn_prompt_body, lam_init=lam_init),
        grid=(b, nh),
        in_specs=[
            pl.BlockSpec((None, t, LANES), lambda i, h: (i, 0, h)),
            pl.BlockSpec((None, t, LANES), lambda i, h: (i, 0, nh + h)),
            pl.BlockSpec((None, t, LANES), lambda i, h: (i, 0, 2 * nh + h)),
            pl.BlockSpec((None, 3, ATT_TQ, ATT_TK), lambda i, h: (h, 0, 0, 0)),
            pl.BlockSpec((4, DIFF_HALF), lambda i, h: (0, 0)),
            pl.BlockSpec((1, LANES), lambda i, h: (0, 0)),
        ],
        out_specs=pl.BlockSpec((None, t, LANES), lambda i, h: (i, 0, h)),
        out_shape=jax.ShapeDtypeStruct((b, t, DIFF_WIDTH), BF16),
        scratch_shapes=[
            pltpu.VMEM((t, 2 * LANES), BF16),
            pltpu.VMEM((nq, 2 * ATT_TQ, LANES), BF16),
            pltpu.VMEM((nq, 2 * ATT_TQ, LANES), F32),
            pltpu.VMEM((nq, 2 * ATT_TQ, 2 * LANES), F32),
        ],
        compiler_params=_cparams(("parallel", "parallel")),
        name="attn_prompt",
    )(proj, proj, proj, bias_p, lam_params, subln.reshape(1, LANES))


def _attn_decode_body(pt_ref, x_ref, ck_ref, cv_ref, bd_ref, bn_ref, lam_ref, sub_ref, o_ref,
                      kbuf_ref, vbuf_ref, sem_ref, *, n_pages, layer_j, lam_init):
    i = pl.program_id(0)
    page_rows = PAGE_SIZE * DIFF_HEADS

    def page_copies(b, slot):
        copies = []
        for p in range(n_pages):
            page = pt_ref[b * n_pages + p]
            copies.append(pltpu.make_async_copy(ck_ref.at[page, layer_j], kbuf_ref.at[slot, p], sem_ref.at[slot, 0]))
            copies.append(pltpu.make_async_copy(cv_ref.at[page, layer_j], vbuf_ref.at[slot, p], sem_ref.at[slot, 1]))
        return copies

    @pl.when(i == 0)
    def _():
        for b in range(ATT_DEC_BUFS - 1):
            for cp in page_copies(b, b):
                cp.start()

    ahead = i + (ATT_DEC_BUFS - 1)

    @pl.when(ahead < pl.num_programs(0))
    def _():
        for cp in page_copies(ahead, ahead % ATT_DEC_BUFS):
            cp.start()

    slot = i % ATT_DEC_BUFS
    for cp in page_copies(i, slot):
        cp.wait()
    k_refs = [kbuf_ref.at[slot, p] for p in range(n_pages)]
    v_refs = [vbuf_ref.at[slot, p] for p in range(n_pages)]
    lane = lax.broadcasted_iota(jnp.int32, (1, DIFF_HEAD_DIM), 1)
    rows_q, rows_k, rows_v = [], [], []
    for h in range(DIFF_HEADS):
        sl = slice(h * DIFF_HEAD_DIM, (h + 1) * DIFF_HEAD_DIM)
        qh = x_ref[:, DIFF_WIDTH:2 * DIFF_WIDTH][:, sl] * (DIFF_HALF ** -0.5)
        kh = x_ref[:, 2 * DIFF_WIDTH:3 * DIFF_WIDTH][:, sl]
        vh = x_ref[:, 3 * DIFF_WIDTH:4 * DIFF_WIDTH][:, sl]
        rows_q += [jnp.where(lane < DIFF_HALF, qh, 0.0), jnp.where(lane >= DIFF_HALF, qh, 0.0)]
        rows_k += [kh, kh]
        rows_v += [vh, vh]
    q8 = jnp.concatenate(rows_q, axis=0)
    k8 = jnp.concatenate(rows_k, axis=0)
    v8 = jnp.concatenate(rows_v, axis=0)
    q8b = q8.astype(BF16)
    s = jnp.concatenate(
        [lax.dot_general(q8b, k_refs[p][...].astype(BF16), _NT, preferred_element_type=F32)
         for p in range(n_pages)], axis=1) + bd_ref[...]
    s_new = jnp.sum(q8 * k8, axis=-1, keepdims=True) + bn_ref[:, 0:1]
    m = jnp.maximum(jnp.max(s, axis=-1, keepdims=True), s_new)
    p = jnp.exp(s - m)
    p_new = jnp.exp(s_new - m)
    inv_l = 1.0 / (jnp.sum(p, axis=-1, keepdims=True) + p_new)
    a = p * inv_l
    r = (p_new * inv_l) * v8
    for pg in range(n_pages):
        r = r + jnp.dot(a[:, pg * page_rows:(pg + 1) * page_rows].astype(BF16),
                        v_refs[pg][...].astype(BF16), preferred_element_type=F32)
    lam = _lambda(lam_ref, lam_init)
    for h in range(DIFF_HEADS):
        o = r[2 * h:2 * h + 1, :] - lam * r[2 * h + 1:2 * h + 2, :]
        o_ref[:, h * DIFF_HEAD_DIM:(h + 1) * DIFF_HEAD_DIM] = (
            _rms(o, sub_ref[...]) * (1.0 - lam_init)).astype(BF16)


def attn_decode(proj, cache_k, cache_v, page_table, layer_j, bias_d, bias_n, lam_params, subln, lam_init):
    nb = proj.shape[0]
    n_pages = page_table.shape[1]
    n_pool, n_even = cache_k.shape[:2]
    page_rows = PAGE_SIZE * DIFF_HEADS
    ck = cache_k.reshape(n_pool, n_even, page_rows, DIFF_HEAD_DIM)
    cv = cache_v.reshape(n_pool, n_even, page_rows, DIFF_HEAD_DIM)

    assert nb >= ATT_DEC_BUFS - 1
    full = lambda shape: pl.BlockSpec(shape, lambda i, pt: (0,) * len(shape))
    grid_spec = pltpu.PrefetchScalarGridSpec(
        num_scalar_prefetch=1,
        grid=(nb,),
        in_specs=[pl.BlockSpec((None, 1, EVEN_IN), lambda i, pt: (i, 0, 0)),
                  pl.BlockSpec(memory_space=pl.ANY), pl.BlockSpec(memory_space=pl.ANY),
                  full(bias_d.shape), full(bias_n.shape), full((4, DIFF_HALF)), full((1, LANES))],
        out_specs=pl.BlockSpec((None, 1, DIFF_WIDTH), lambda i, pt: (i, 0, 0)),
        scratch_shapes=[
            pltpu.VMEM((ATT_DEC_BUFS, n_pages, page_rows, DIFF_HEAD_DIM), F32),
            pltpu.VMEM((ATT_DEC_BUFS, n_pages, page_rows, DIFF_HEAD_DIM), F32),
            pltpu.SemaphoreType.DMA((ATT_DEC_BUFS, 2)),
        ],
    )
    out = pl.pallas_call(
        functools.partial(_attn_decode_body, n_pages=n_pages, layer_j=layer_j, lam_init=lam_init),
        grid_spec=grid_spec,
        out_shape=jax.ShapeDtypeStruct((nb, 1, DIFF_WIDTH), BF16),
        compiler_params=_cparams(("arbitrary",)),
        name="attn_decode",
    )(page_table.reshape(-1), proj.reshape(nb, 1, EVEN_IN), ck, cv,
      bias_d, bias_n, lam_params, subln.reshape(1, LANES))
    return out.reshape(nb, DIFF_WIDTH)


def _blockdiag(x):
    lane = lax.broadcasted_iota(jnp.int32, x.shape, 1)
    zero = jnp.zeros_like(x)
    return jnp.concatenate([jnp.where(lane < GDN_C, x, zero), jnp.where(lane >= GDN_C, x, zero)], axis=0)


def _mmp(a, b):
    return jnp.dot(a.astype(BF16), _blockdiag(b.astype(BF16)), preferred_element_type=F32)


def _unit_lower_inverse(lows, ii, jl):
    eye = jnp.where(ii == jl, 1.0, 0.0)
    in_block = (ii // INV_BLOCK) == (jl // INV_BLOCK)
    ps = [jnp.where(in_block, -low, 0.0) for low in lows]
    offs = [jnp.where(in_block, 0.0, low) for low in lows]
    dinvs = [eye + p for p in ps]
    span = 2
    while span < INV_BLOCK:
        ps = [_mmp(p, p) for p in ps]
        dinvs = [d + _mmp(d, p) for d, p in zip(dinvs, ps)]
        span *= 2
    powers = [[-_mmp(d, off) for d, off in zip(dinvs, offs)]]
    span = 2
    while span < GDN_C // INV_BLOCK:
        powers.append([_mmp(n, n) for n in powers[-1]])
        span *= 2
    xs = dinvs
    for pw in reversed(powers):
        xs = [x + _mmp(n, x) for n, x in zip(pw, xs)]
    return xs


def _gdn_prompt_body(h_ref, gn_ref, w_ref, wgt_ref, cw_ref, alog_ref, dtb_ref, on_ref,
                     o_ref, s_ref, cst_ref, ext_ref, z_ref):
    c = pl.program_id(1)
    nh = GDN_HEADS
    npair = nh // 2
    hk = nh * GDN_DK
    cc = GDN_C
    rows_step = GDN_STEP_CHUNKS * cc
    heads_per_tile = GDN_PROJ_TN // LANES

    @pl.when(c == 0)
    def _():
        s_ref[...] = jnp.zeros_like(s_ref)
        ext_ref[0:8, :] = jnp.zeros((8, GDN_QKV), F32)

    @pl.when(c > 0)
    def _():
        ext_ref[0:8, :] = ext_ref[rows_step:rows_step + 8, :]

    xn = _rms(h_ref[...], gn_ref[...]).astype(BF16)
    abt = lax.dot_general(wgt_ref[...], xn, _NT, preferred_element_type=F32)

    def project(n):
        cols = slice(n * GDN_PROJ_TN, (n + 1) * GDN_PROJ_TN)
        tile = jnp.dot(xn, w_ref[:, cols], preferred_element_type=F32)
        if n < GDN_QKV // GDN_PROJ_TN:
            ext_ref[8:8 + rows_step, cols] = tile
        else:
            z_ref[:, n * GDN_PROJ_TN - GDN_QKV:(n + 1) * GDN_PROJ_TN - GDN_QKV] = tile

    def conv_act(col, r0):
        sl = slice(col, col + LANES)
        acc = ext_ref[8 + r0:8 + r0 + cc, sl] * cw_ref[3:4, sl]
        for i in range(GDN_CONV - 1):
            acc = acc + ext_ref[5 + i + r0:5 + i + r0 + cc, sl] * cw_ref[i:i + 1, sl]
        return _silu(acc)

    def l2n(x):
        return x * lax.rsqrt(jnp.sum(x * x, axis=-1, keepdims=True) + EPS)

    def pair(xs):
        return [jnp.concatenate([xs[2 * p], xs[2 * p + 1]], axis=1) for p in range(npair)]

    def row_pair(x):
        return pair([x[h:h + 1, :] for h in range(nh)])

    project(0)
    lane = lax.broadcasted_iota(jnp.int32, (nh, cc), 1)
    q_p, k_p, v_p, gc_b, beta_b, kdec_b, gc_rows, s_decay = [], [], [], [], [], [], [], []
    for ck in range(GDN_STEP_CHUNKS):
        r0 = ck * cc
        g = -jnp.exp(alog_ref[...]) * _softplus(abt[0:nh, r0:r0 + cc] + dtb_ref[...])
        beta = _sigmoid(abt[nh:2 * nh, r0:r0 + cc])
        gc = g
        shift = 1
        while shift < cc:
            gc = gc + jnp.where(lane >= shift, pltpu.roll(gc, shift, 1), 0.0)
            shift *= 2
        g_last = jnp.broadcast_to(gc[:, cc - 1:cc], (nh, cc))
        s_decay.append(jnp.exp(g_last))
        cols = jnp.transpose(jnp.concatenate([gc, beta, jnp.zeros((LANES - 2 * nh, cc), F32)], axis=0))
        bcast = [jnp.broadcast_to(cols[:, n:n + 1], (cc, cc)) for n in range(2 * nh)]
        gcb = pair(bcast[0:nh])
        gc_b += gcb
        beta_b += pair(bcast[nh:2 * nh])
        kdec_b += [jnp.exp(gl - x) for gl, x in zip(row_pair(g_last), gcb)]
        gc_rows += row_pair(gc)
    gam_b = [jnp.exp(x) for x in gc_b]
    ii = lax.broadcasted_iota(jnp.int32, (cc, 2 * cc), 0)
    jl = lax.broadcasted_iota(jnp.int32, (cc, 2 * cc), 1) % cc
    decay = [jnp.exp(jnp.where(ii >= jl, gb - gr, -jnp.inf)) for gb, gr in zip(gc_b, gc_rows)]

    act = {}
    for n in range(1, GDN_MAIN // GDN_PROJ_TN + 1):
        if n < GDN_MAIN // GDN_PROJ_TN:
            project(n)
        if n <= GDN_QKV // GDN_PROJ_TN:
            for hh in range(heads_per_tile):
                col = (n - 1) * GDN_PROJ_TN + hh * LANES
                for ck in range(GDN_STEP_CHUNKS):
                    a = conv_act(col, ck * cc)
                    if col < hk:
                        a = l2n(a) * (GDN_DK ** -0.5)
                    elif col < 2 * hk:
                        a = l2n(a)
                    act[(col, ck)] = a
    for ck in range(GDN_STEP_CHUNKS):
        q_p += pair([act[(h * GDN_DK, ck)] for h in range(nh)])
        k_p += pair([act[(hk + h * GDN_DK, ck)] for h in range(nh)])
        v_p += pair([act[(2 * hk + h * GDN_DV, ck)] for h in range(nh)])

    gram = [lax.dot_general(jnp.concatenate([q, k], axis=0).astype(BF16), _blockdiag(k.astype(BF16)), _NT,
                            preferred_element_type=F32) for q, k in zip(q_p, k_p)]
    qk = [gm[0:cc, :] * d for gm, d in zip(gram, decay)]
    lows = [jnp.where(ii > jl, b * gm[cc:2 * cc, :] * d, 0.0) for b, gm, d in zip(beta_b, gram, decay)]
    tinv = _unit_lower_inverse(lows, ii, jl)
    w =[_mmp(t, b * gm * k) for t, b, gm, k in zip(tinv, beta_b, gam_b, k_p)]
    u0 = [_mmp(t, b * v) for t, b, v in zip(tinv, beta_b, v_p)]
    wq = [jnp.concatenate([wp, gm * q], axis=0).astype(BF16) for wp, gm, q in zip(w, gam_b, q_p)]
    kd = [(kdb * k).astype(BF16) for kdb, k in zip(kdec_b, k_p)]

    state = [s_ref[h] for h in range(nh)]
    zero = jnp.zeros((GDN_DK, GDN_DV), BF16)
    for ck in range(GDN_STEP_CHUNKS):
        r0 = ck * cc
        ent = range(ck * npair, (ck + 1) * npair)
        s_bd = [jnp.concatenate([jnp.concatenate([state[2 * p].astype(BF16), zero], axis=1),
                                 jnp.concatenate([zero, state[2 * p + 1].astype(BF16)], axis=1)], axis=0)
                for p in range(npair)]
        ws_qs = [jnp.dot(wq[e], sb, preferred_element_type=F32) for e, sb in zip(ent, s_bd)]
        u = [u0[e] - x[0:cc, :] for e, x in zip(ent, ws_qs)]
        o = [x[cc:2 * cc, :] + _mmp(qk[e], b) for e, x, b in zip(ent, ws_qs, u)]
        upd = [lax.dot_general(kd[e], b.astype(BF16), _TN, preferred_element_type=F32)
               for e, b in zip(ent, u)]
        new_state = []
        for h in range(nh):
            half = slice((h % 2) * cc, (h % 2 + 1) * cc)
            new_state.append(s_decay[ck][h:h + 1, :] * state[h] + upd[h // 2][half, half])
            z = z_ref[r0:r0 + cc, h * GDN_DV:(h + 1) * GDN_DV]
            o_ref[r0:r0 + cc, h * GDN_DV:(h + 1) * GDN_DV] = (
                _rms(o[h // 2][:, half], on_ref[...]) * _silu(z)).astype(BF16)
        state = new_state
    for h in range(nh):
        s_ref[h] = state[h]

    @pl.when(c == pl.num_programs(1) - 1)
    def _():
        cst_ref[...] = ext_ref[rows_step:rows_step + 8, :]


def gdn_prompt(h, gain, w_in, w_gates, idx, conv_w, a_log, dt_bias, o_norm):
    b, t, d = h.shape
    rows = GDN_STEP_CHUNKS * GDN_C
    nc = t // rows
    nh = GDN_HEADS
    return pl.pallas_call(
        _gdn_prompt_body,
        grid=(b, nc),
        in_specs=[
            pl.BlockSpec((None, rows, d), lambda i, c: (i, c, 0)),
            pl.BlockSpec((1, d), lambda i, c: (0, 0)),
            pl.BlockSpec((None, d, GDN_MAIN), lambda i, c: (idx, 0, 0), pipeline_mode=pl.Buffered(1)),
            pl.BlockSpec((None, LANES, d), lambda i, c: (idx, 0, 0), pipeline_mode=pl.Buffered(1)),
            pl.BlockSpec((GDN_CONV, GDN_QKV), lambda i, c: (0, 0)),
            pl.BlockSpec((nh, 1), lambda i, c: (0, 0)),
            pl.BlockSpec((nh, 1), lambda i, c: (0, 0)),
            pl.BlockSpec((1, GDN_DV), lambda i, c: (0, 0)),
        ],
        out_specs=[
            pl.BlockSpec((None, rows, nh * GDN_DV), lambda i, c: (i, c, 0)),
            pl.BlockSpec((None, nh, GDN_DK, GDN_DV), lambda i, c: (i, 0, 0, 0)),
            pl.BlockSpec((None, 8, GDN_QKV), lambda i, c: (i, 0, 0)),
        ],
        out_shape=[
            jax.ShapeDtypeStruct((b, t, nh * GDN_DV), BF16),
            jax.ShapeDtypeStruct((b, nh, GDN_DK, GDN_DV), F32),
            jax.ShapeDtypeStruct((b, 8, GDN_QKV), F32),
        ],
        scratch_shapes=[pltpu.VMEM((8 + rows, GDN_QKV), F32), pltpu.VMEM((rows, nh * GDN_DV), F32)],
        compiler_params=_cparams(("parallel", "arbitrary")),
        name="gdn_prompt",
    )(h, gain.reshape(1, d), w_in, w_gates, conv_w, a_log.reshape(nh, 1), dt_bias.reshape(nh, 1),
      o_norm.reshape(1, GDN_DV))


def _gdn_decode_body(x_ref, cs_ref, ab_ref, cw_ref, alog_ref, dtb_ref, on_ref, s0_ref, *rest, first, layer_j):
    o_ref, s_ref = rest[-2:]
    nh = GDN_HEADS
    pad = jnp.zeros((8 - 2, GDN_DK), F32)
    if first:
        for slot in range(s_ref.shape[0]):
            if slot != layer_j:
                s_ref[slot] = jnp.zeros(s_ref.shape[1:], F32)
    pad7 = jnp.zeros((8 - 1, GDN_DK), F32)
    seqs = range(GDN_DEC_ROWS)
    q8, k8, v8, gam8, beta8, qk8 = [], [], [], [], [], []
    for r in seqs:
        x = x_ref[r, 0:3 * nh, :]
        conv = x * cw_ref[GDN_CONV - 1]
        for i in range(GDN_CONV - 1):
            conv = conv + cs_ref[r, i] * cw_ref[i]
        act = conv * _sigmoid(conv)
        qa, ka = act[0:nh], act[nh:2 * nh]
        q8.append(qa * lax.rsqrt(jnp.sum(qa * qa, axis=-1, keepdims=True) + EPS) * (GDN_DK ** -0.5))
        k8.append(ka * lax.rsqrt(jnp.sum(ka * ka, axis=-1, keepdims=True) + EPS))
        v8.append(act[2 * nh:3 * nh])
        g = -jnp.exp(alog_ref[...]) * _softplus(ab_ref[r, 0:nh, :] + dtb_ref[...])
        gam8.append(jnp.broadcast_to(jnp.exp(g), (nh, GDN_DV)))
        beta8.append(jnp.broadcast_to(_sigmoid(ab_ref[r, nh:2 * nh, :]), (nh, GDN_DV)))
        qk8.append(jnp.broadcast_to(jnp.sum(q8[r] * k8[r], axis=-1, keepdims=True), (nh, GDN_DV)))
    pairs = [(r, h) for r in seqs for h in range(nh)]
    s_old = [s0_ref[r, h] for r, h in pairs]
    qk_s = [jnp.dot(jnp.concatenate([q8[r][h:h + 1, :], k8[r][h:h + 1, :], pad], axis=0).astype(BF16),
                    s.astype(BF16), preferred_element_type=F32) for (r, h), s in zip(pairs, s_old)]
    u = [beta8[r][h:h + 1, :] * (v8[r][h:h + 1, :] - gam8[r][h:h + 1, :] * x[1:2, :])
         for (r, h), x in zip(pairs, qk_s)]
    outs = [gam8[r][h:h + 1, :] * x[0:1, :] + qk8[r][h:h + 1, :] * b for (r, h), x, b in zip(pairs, qk_s, u)]
    outer = [lax.dot_general(jnp.concatenate([k8[r][h:h + 1, :], pad7], axis=0).astype(BF16),
                             jnp.concatenate([b, pad7], axis=0).astype(BF16), _TN, preferred_element_type=F32)
             for (r, h), b in zip(pairs, u)]
    for (r, h), s, x in zip(pairs, s_old, outer):
        new = gam8[r][h:h + 1, :] * s + x
        if first:
            s_ref[layer_j, r, h] = new
        else:
            s_ref[r, h] = new
    for r in seqs:
        o8 = jnp.concatenate(outs[r * nh:(r + 1) * nh], axis=0)
        z8 = x_ref[r, 3 * nh:4 * nh, :]
        o_ref[r] = (_rms(o8, on_ref[...]) * (z8 * _sigmoid(z8))).astype(BF16)


def gdn_decode(proj, gates, conv_state, s0_all, layer_j, s_new_all, conv_w, a_log, dt_bias, o_norm):
    nb = proj.shape[0]
    nh = GDN_HEADS
    nrow = GDN_QKV // LANES
    first = s_new_all is None
    in_specs = [
        pl.BlockSpec((GDN_DEC_ROWS, GDN_MAIN // LANES, LANES), lambda i: (i, 0, 0)),
        pl.BlockSpec((GDN_DEC_ROWS, GDN_CONV - 1, nrow, LANES), lambda i: (i, 0, 0, 0)),
        pl.BlockSpec((GDN_DEC_ROWS, 2 * nh, 1), lambda i: (i, 0, 0)),
        pl.BlockSpec((GDN_CONV, nrow, LANES), lambda i: (0, 0, 0)),
        pl.BlockSpec((nh, 1), lambda i: (0, 0)),
        pl.BlockSpec((nh, 1), lambda i: (0, 0)),
        pl.BlockSpec((1, GDN_DV), lambda i: (0, 0)),
        pl.BlockSpec((None, GDN_DEC_ROWS, nh, GDN_DK, GDN_DV), lambda i: (layer_j, i, 0, 0, 0)),
    ]
    args = [proj.reshape(nb, GDN_MAIN // LANES, LANES),
            conv_state.reshape(nb, GDN_CONV - 1, nrow, LANES),
            gates.reshape(nb, 2 * nh, 1),
            conv_w.reshape(GDN_CONV, nrow, LANES),
            a_log.reshape(nh, 1), dt_bias.reshape(nh, 1), o_norm.reshape(1, GDN_DV), s0_all]
    if first:
        s_spec = pl.BlockSpec((s0_all.shape[0], GDN_DEC_ROWS, nh, GDN_DK, GDN_DV), lambda i: (0, i, 0, 0, 0))
        aliases = {}
    else:
        in_specs.append(pl.BlockSpec(memory_space=pl.ANY))
        args.append(s_new_all)
        s_spec = pl.BlockSpec((None, GDN_DEC_ROWS, nh, GDN_DK, GDN_DV), lambda i: (layer_j, i, 0, 0, 0))
        aliases = {len(args) - 1: 1}
    o, s = pl.pallas_call(
        functools.partial(_gdn_decode_body, first=first, layer_j=layer_j),
        grid=(nb // GDN_DEC_ROWS,),
        in_specs=in_specs,
        out_specs=[pl.BlockSpec((GDN_DEC_ROWS, nh, GDN_DV), lambda i: (i, 0, 0)), s_spec],
        out_shape=[
            jax.ShapeDtypeStruct((nb, nh, GDN_DV), BF16),
            jax.ShapeDtypeStruct(s0_all.shape, F32),
        ],
        input_output_aliases=aliases,
        compiler_params=_cparams(("parallel",)),
        name="gdn_decode",
    )(*args)
    return o.reshape(nb, nh * GDN_DV), s


def kernel(x_prompt, x_sample, cache_k, cache_v, page_table, state_pool, state_conv, state_delta,
           norm_mix, norm_ffn, norm_final, rel_bias,
           w_in_even, pool_w, pool_scale, lambda_q1, lambda_k1, lambda_q2, lambda_k2, subln_w, w_out_even,
           w_in_odd, conv_w, a_log, dt_bias, o_norm, w_out_odd,
           w_gate_up, w_down):
    bp, t, d = x_prompt.shape
    bs = x_sample.shape[0]
    mp = bp * t
    nh = GDN_HEADS
    past = page_table.shape[1] * PAGE_SIZE
    tm_p = 1024
    tm_s = bs

    hp = x_prompt.reshape(mp, d)
    hs = x_sample.reshape(bs, d)
    bias_p, bias_d, bias_n = rel_bias_tiles(rel_bias, past)

    w_in_e = w_in_even.astype(BF16)
    w_in_o = w_in_odd.astype(BF16)
    w_gates = jnp.zeros((w_in_odd.shape[0], LANES, d), F32).at[:, 0:2 * nh, :].set(
        jnp.transpose(w_in_odd[:, :, GDN_MAIN:], (0, 2, 1))).astype(BF16)
    w_out_e = w_out_even.astype(BF16)
    w_out_o = w_out_odd.astype(BF16)
    w_gu = w_gate_up.astype(BF16)
    w_d = w_down.astype(BF16)
    pool_wb = pool_w.astype(BF16)

    k_s, v_s, pool_p, pool_s = [], [], [], []
    conv_p, conv_s, delta_p = [], [], []
    kv_p = None
    n_even = w_in_even.shape[0]
    delta_s = None
    for layer in range(DEPTH):
        j = layer // 2
        last = layer == DEPTH - 1
        if layer % 2 == 0:
            w_out = w_out_e
            pw = pool_wb[j]
            lam_init = 0.8 - 0.6 * math.exp(-0.3 * layer)
            lam_params = jnp.stack([lambda_q1[j], lambda_k1[j], lambda_q2[j], lambda_k2[j]])

            qkv_p, ypool_p, tail_p, kv_p = even_proj_prompt(hp, norm_mix[layer], w_in_e, j, pw, pool_scale[j],
                                                            t, n_even, kv_p, tm=tm_p)
            proj_s, ks_, vs_ = norm_matmul(hs, norm_mix[layer], w_in_e, j, tm=tm_s, tn=DIFF_WIDTH)

            oatt_p = attn_prompt(qkv_p.reshape(bp, t, 3 * DIFF_WIDTH), bias_p, lam_params, subln_w[j], lam_init)
            ypool_s = pool_decode(jnp.transpose(state_pool[j], (1, 0, 2)), proj_s, pw, pool_scale[j])
            oatt_s = attn_decode(proj_s, cache_k, cache_v, page_table, j, bias_d, bias_n,
                                 lam_params, subln_w[j], lam_init)

            kv_shape = (DIFF_HEADS, DIFF_HEAD_DIM)
            k_s.append(ks_.reshape(bs, 1, *kv_shape))
            v_s.append(vs_.reshape(bs, 1, *kv_shape))
            pool_p.append(tail_p[:, POOL_PAD - POOL_BUF:, :])
            pool_s.append(jnp.concatenate([state_pool[j][:, 1:], proj_s[:, None, 0:POOL_WIDTH]], axis=1))

            mix_p = ((ypool_p, 0), (oatt_p.reshape(mp, DIFF_WIDTH), 0))
            mix_s = ((ypool_s, 0), (oatt_s, 0))
        else:
            w_out = w_out_o

            proj_s, gates_s = norm_matmul(hs, norm_mix[layer], w_in_o, j, w_gates, tm=tm_s, tn=512)

            o_p, s_p, tail_p = gdn_prompt(hp.reshape(bp, t, d), norm_mix[layer], w_in_o, w_gates, j,
                                          conv_w[j], a_log[j], dt_bias[j], o_norm[j])
            o_s, delta_s = gdn_decode(proj_s, gates_s[0:2 * nh].T, state_conv[j], state_delta, j, delta_s,
                                      conv_w[j], a_log[j], dt_bias[j], o_norm[j])

            conv_p.append(tail_p[:, 8 - (GDN_CONV - 1):, :])
            conv_s.append(jnp.concatenate([state_conv[j][:, 1:], proj_s[:, None, 0:GDN_QKV]], axis=1))
            delta_p.append(s_p)

            o_p = o_p.reshape(mp, nh * GDN_DV)
            mix_p = ((o_p, 0), (o_p, 1))
            mix_s = ((o_s, 0), (o_s, 1))

        g_fin = norm_final if last else None
        hp = outproj_ffn(hp, mix_p[0], mix_p[1], w_out, j, norm_ffn[layer], w_gu, w_d, layer, g_fin, tm=FFN_TM)
        hs = outproj_ffn(hs, mix_s[0], mix_s[1], w_out, j, norm_ffn[layer], w_gu, w_d, layer, g_fin, tm=tm_s)

    return (hp.reshape(bp, t, d), hs.reshape(bs, 1, d),
            kv_p[0].reshape(bp, n_even, t, DIFF_HEADS, DIFF_HEAD_DIM),
            kv_p[1].reshape(bp, n_even, t, DIFF_HEADS, DIFF_HEAD_DIM),
            jnp.stack(k_s, axis=1), jnp.stack(v_s, axis=1),
            jnp.stack(pool_p), jnp.stack(pool_s), jnp.stack(conv_p), jnp.stack(conv_s),
            jnp.stack(delta_p), delta_s)
```

```python
import functools
import math

import jax
import jax.numpy as jnp
from jax import lax
from jax.experimental import pallas as pl
from jax.experimental.pallas import tpu as pltpu

F32 = jnp.float32
BF16 = jnp.bfloat16

D_MODEL = 1024
DEPTH = 4
PAGE_SIZE = 128
POOL_WIDTH = 512
POOL_WINDOWS = (2, 4, 8, 16)
POOL_GROUP = 128
POOL_BUF = 15
DIFF_HEADS = 4
DIFF_HALF = 64
DIFF_HEAD_DIM = 128
DIFF_WIDTH = 512
EVEN_IN = 2048
REL_BUCKETS = 32
REL_MAX_DIST = 128
GDN_HEADS = 8
GDN_DK = 128
GDN_DV = 128
GDN_CONV = 4
GDN_QKV = 3072
GDN_MAIN = 4096
D_FF = 2816
EPS = 1e-6

LANES = 128
VMEM_LIMIT = 48 * 1024 * 1024
NEG_BIG = -1e30

ATT_TQ = 512
ATT_TK = 512
ATT_HEADS = 1
GDN_C = 128
GDN_STEP_CHUNKS = 2
GDN_DEC_ROWS = 4
ATT_DEC_BUFS = 3
GDN_PROJ_TN = 512
INV_BLOCK = 16
FFN_TF = 256
FFN_TM = 512

_NT = (((1,), (1,)), ((), ()))
_TN = (((0,), (0,)), ((), ()))


def _cparams(sem):
    return pltpu.CompilerParams(dimension_semantics=sem, vmem_limit_bytes=VMEM_LIMIT)


def _sigmoid(x):
    return 1.0 / (1.0 + jnp.exp(-x))


def _silu(x):
    h = 0.5 * x
    return h + h * jnp.tanh(h)


def _softplus(x):
    return jnp.maximum(x, 0.0) + jnp.log1p(jnp.exp(-jnp.abs(x)))


def _rms(x, gain):
    return x * lax.rsqrt(jnp.mean(x * x, axis=-1, keepdims=True) + EPS) * gain


def _mm(a, b):
    return jnp.dot(a.astype(BF16), b.astype(BF16), preferred_element_type=F32)


def _norm_mm_kv_body(x_ref, g_ref, w_ref, o_ref, k_ref, v_ref, xn_ref):
    j = pl.program_id(1)

    @pl.when(j == 0)
    def _():
        xn_ref[...] = _rms(x_ref[...], g_ref[...]).astype(BF16)

    res = jnp.dot(xn_ref[...], w_ref[...], preferred_element_type=F32)
    o_ref[...] = res

    def head_rows(dst_ref):
        for h in range(DIFF_HEADS):
            dst_ref[pl.ds(h, res.shape[0], stride=DIFF_HEADS), :] = res[:, h * DIFF_HEAD_DIM:(h + 1) * DIFF_HEAD_DIM]

    @pl.when(j == 2)
    def _():
        head_rows(k_ref)

    @pl.when(j == 3)
    def _():
        head_rows(v_ref)


def _norm_mm_gate_body(x_ref, g_ref, w_ref, wst_ref, o_ref, ost_ref, xn_ref):
    @pl.when(pl.program_id(1) == 0)
    def _():
        xn = _rms(x_ref[...], g_ref[...]).astype(BF16)
        xn_ref[...] = xn
        ost_ref[...] = lax.dot_general(wst_ref[...], xn, _NT, preferred_element_type=F32)

    o_ref[...] = jnp.dot(xn_ref[...], w_ref[...], preferred_element_type=F32)


def norm_matmul(x, gain, w, idx, w_gates=None, *, tm, tn):
    m, d = x.shape
    n = EVEN_IN if w_gates is None else GDN_MAIN
    grid = (m // tm, n // tn)
    in_specs = [
        pl.BlockSpec((tm, d), lambda i, j: (i, 0)),
        pl.BlockSpec((1, d), lambda i, j: (0, 0)),
        pl.BlockSpec((None, d, tn), lambda i, j: (idx, 0, j)),
    ]
    out_specs = pl.BlockSpec((tm, tn), lambda i, j: (i, j))
    out_shape = jax.ShapeDtypeStruct((m, n), F32)
    args = [x, gain.reshape(1, d), w]
    if w_gates is None:
        assert tn == DIFF_WIDTH and n == EVEN_IN
        kv_spec = pl.BlockSpec((tm * DIFF_HEADS, DIFF_HEAD_DIM), lambda i, j: (i, 0))
        kv_shape = jax.ShapeDtypeStruct((m * DIFF_HEADS, DIFF_HEAD_DIM), F32)
        out_specs = [out_specs, kv_spec, kv_spec]
        out_shape = [out_shape, kv_shape, kv_shape]
        body = _norm_mm_kv_body
    else:
        in_specs.append(pl.BlockSpec((None, LANES, d), lambda i, j: (idx, 0, 0)))
        out_specs = [out_specs, pl.BlockSpec((LANES, tm), lambda i, j: (0, i))]
        out_shape = [out_shape, jax.ShapeDtypeStruct((LANES, m), F32)]
        args.append(w_gates)
        body = _norm_mm_gate_body
    return pl.pallas_call(
        body,
        grid=grid,
        in_specs=in_specs,
        out_specs=out_specs,
        out_shape=out_shape,
        scratch_shapes=[pltpu.VMEM((tm, d), BF16)],
        compiler_params=_cparams(("parallel", "arbitrary")),
        name="norm_matmul",
    )(*args)


def _ffn_body(*refs, final):
    if final:
        res_ref, a0_ref, a1_ref, wo_ref, gf_ref, wgu_ref, wd_ref, gfin_ref, o_ref = refs
    else:
        res_ref, a0_ref, a1_ref, wo_ref, gf_ref, wgu_ref, wd_ref, o_ref = refs
    mixed = jnp.concatenate([a0_ref[...], a1_ref[...]], axis=1)
    h1 = res_ref[...] + jnp.dot(mixed, wo_ref[...], preferred_element_type=F32)
    xn = _rms(h1, gf_ref[...]).astype(BF16)
    acts = []
    for f in range(D_FF // FFN_TF):
        g = jnp.dot(xn, wgu_ref[:, f * FFN_TF:(f + 1) * FFN_TF], preferred_element_type=F32)
        u = jnp.dot(xn, wgu_ref[:, D_FF + f * FFN_TF:D_FF + (f + 1) * FFN_TF], preferred_element_type=F32)
        acts.append((g * _sigmoid(g) * u).astype(BF16))
    y = h1 + jnp.dot(jnp.concatenate(acts, axis=1), wd_ref[...], preferred_element_type=F32)
    if final:
        y = _rms(y, gfin_ref[...])
    o_ref[...] = y


def outproj_ffn(res, mix0, mix1, w_out, idx_out, g_ffn, w_gu, w_d, layer, g_final=None, *, tm):
    m, d = res.shape
    half = w_out.shape[1] // 2
    (m0, c0), (m1, c1) = mix0, mix1
    final = g_final is not None
    once = pl.Buffered(1)
    in_specs = [
        pl.BlockSpec((tm, d), lambda i: (i, 0)),
        pl.BlockSpec((tm, half), lambda i: (i, c0)),
        pl.BlockSpec((tm, half), lambda i: (i, c1)),
        pl.BlockSpec((None, 2 * half, d), lambda i: (idx_out, 0, 0), pipeline_mode=once),
        pl.BlockSpec((1, d), lambda i: (0, 0)),
        pl.BlockSpec((None, d, 2 * D_FF), lambda i: (layer, 0, 0), pipeline_mode=once),
        pl.BlockSpec((None, D_FF, d), lambda i: (layer, 0, 0), pipeline_mode=once),
    ]
    args = [res, m0, m1, w_out, g_ffn.reshape(1, d), w_gu, w_d]
    if final:
        in_specs.append(pl.BlockSpec((1, d), lambda i: (0, 0)))
        args.append(g_final.reshape(1, d))
    return pl.pallas_call(
        functools.partial(_ffn_body, final=final),
        grid=(m // tm,),
        in_specs=in_specs,
        out_specs=pl.BlockSpec((tm, d), lambda i: (i, 0)),
        out_shape=jax.ShapeDtypeStruct((m, d), F32),
        compiler_params=_cparams(("parallel",)),
        name="outproj_ffn",
    )(*args)


POOL_PAD = 16
POOL_ROWS = 512


def _even_proj_body(x_ref, g_ref, w_ref, pw_ref, ps_ref, *rest, blocks_per_seq, first, layer_j):
    qkv_ref, yp_ref, k_ref, v_ref, tail_ref, xn_ref, ext_ref = rest[-7:]
    i = pl.program_id(0)
    j = pl.program_id(1)
    tm = x_ref.shape[0]

    @pl.when(j == 0)
    def _():
        xn_ref[...] = _rms(x_ref[...], g_ref[...]).astype(BF16)

    res = jnp.dot(xn_ref[...], w_ref[...], preferred_element_type=F32)

    blk = i % blocks_per_seq

    @pl.when((j == 0) & (blk == 0))
    def _():
        ext_ref[0:POOL_PAD, :] = jnp.zeros((POOL_PAD, POOL_WIDTH), F32)

    @pl.when((j == 0) & (blk > 0))
    def _():
        ext_ref[0:POOL_PAD, :] = ext_ref[tm:tm + POOL_PAD, :]

    @pl.when(j == 0)
    def _():
        ext_ref[POOL_PAD:, :] = res
        tail_ref[...] = res[tm - POOL_PAD:tm, :]
        for r0 in range(0, tm, POOL_ROWS):
            pos = lax.broadcasted_iota(jnp.int32, (POOL_ROWS, 1), 0) + (blk * tm + r0)
            for g, w in enumerate(POOL_WINDOWS):
                sl = slice(g * POOL_GROUP, (g + 1) * POOL_GROUP)
                cur = ext_ref[POOL_PAD + r0:POOL_PAD + r0 + POOL_ROWS, sl]
                win = cur
                for s in range(1, w):
                    win = win + ext_ref[POOL_PAD + r0 - s:POOL_PAD + r0 - s + POOL_ROWS, sl]
                cnt = jnp.minimum(pos + 1, w).astype(F32)
                dlt = win / cnt - cur
                y = jnp.dot(dlt.astype(BF16), pw_ref[g], preferred_element_type=F32) * ps_ref[:, sl]
                yp_ref[r0:r0 + POOL_ROWS, sl] = y.astype(BF16)

    @pl.when(j >= 1)
    def _():
        qkv_ref[...] = res.astype(BF16)

    def head_rows(dst_ref):
        if first:
            for slot in range(dst_ref.shape[0]):
                if slot != layer_j:
                    dst_ref[slot] = jnp.zeros(dst_ref.shape[1:], F32)
            dst_ref = dst_ref.at[layer_j]
        for h in range(DIFF_HEADS):
            dst_ref[pl.ds(h, tm, stride=DIFF_HEADS), :] = res[:, h * DIFF_HEAD_DIM:(h + 1) * DIFF_HEAD_DIM]

    @pl.when(j == 2)
    def _():
        head_rows(k_ref)

    @pl.when(j == 3)
    def _():
        head_rows(v_ref)


def even_proj_prompt(x, gain, w, idx, pool_w, pool_scale, seq_len, n_layers, kv_all, *, tm):
    m, d = x.shape
    tn = DIFF_WIDTH
    blocks_per_seq = seq_len // tm
    nseq = m // seq_len
    first = kv_all is None
    in_specs = [
        pl.BlockSpec((tm, d), lambda i, j: (i, 0)),
        pl.BlockSpec((1, d), lambda i, j: (0, 0)),
        pl.BlockSpec((None, d, tn), lambda i, j: (idx, 0, j)),
        pl.BlockSpec((len(POOL_WINDOWS), POOL_GROUP, POOL_GROUP), lambda i, j: (0, 0, 0)),
        pl.BlockSpec((1, POOL_WIDTH), lambda i, j: (0, 0)),
    ]
    args = [x, gain.reshape(1, d), w, pool_w, pool_scale.reshape(1, POOL_WIDTH)]
    kv_rows = tm * DIFF_HEADS
    if first:
        kv_spec = pl.BlockSpec((None, n_layers, kv_rows, DIFF_HEAD_DIM),
                               lambda i, j: (i // blocks_per_seq, 0, i % blocks_per_seq, 0))
        aliases = {}
    else:
        in_specs += [pl.BlockSpec(memory_space=pl.ANY), pl.BlockSpec(memory_space=pl.ANY)]
        args += list(kv_all)
        kv_spec = pl.BlockSpec((None, None, kv_rows, DIFF_HEAD_DIM),
                               lambda i, j: (i // blocks_per_seq, idx, i % blocks_per_seq, 0))
        aliases = {len(args) - 2: 2, len(args) - 1: 3}
    kv_shape = jax.ShapeDtypeStruct((nseq, n_layers, seq_len * DIFF_HEADS, DIFF_HEAD_DIM), F32)
    qkv, ypool, k_all, v_all, tail = pl.pallas_call(
        functools.partial(_even_proj_body, blocks_per_seq=blocks_per_seq, first=first, layer_j=idx),
        grid=(m // tm, EVEN_IN // tn),
        in_specs=in_specs,
        out_specs=[
            pl.BlockSpec((tm, tn), lambda i, j: (i, jnp.maximum(j - 1, 0))),
            pl.BlockSpec((tm, POOL_WIDTH), lambda i, j: (i, 0)),
            kv_spec,
            kv_spec,
            pl.BlockSpec((None, POOL_PAD, POOL_WIDTH), lambda i, j: (i // blocks_per_seq, 0, 0)),
        ],
        out_shape=[
            jax.ShapeDtypeStruct((m, 3 * DIFF_WIDTH), BF16),
            jax.ShapeDtypeStruct((m, POOL_WIDTH), BF16),
            kv_shape,
            kv_shape,
            jax.ShapeDtypeStruct((nseq, POOL_PAD, POOL_WIDTH), F32),
        ],
        scratch_shapes=[pltpu.VMEM((tm, d), BF16), pltpu.VMEM((POOL_PAD + tm, POOL_WIDTH), F32)],
        input_output_aliases=aliases,
        compiler_params=_cparams(("arbitrary", "arbitrary")),
        name="even_proj",
    )(*args)
    return qkv, ypool, tail, (k_all, v_all)


def _pool_decode_body(st_ref, u_ref, pw_ref, ps_ref, o_ref):
    for g, w in enumerate(POOL_WINDOWS):
        sl = slice(g * POOL_GROUP, (g + 1) * POOL_GROUP)
        cur = u_ref[:, sl]
        win = cur
        for i in range(1, w):
            win = win + st_ref[POOL_BUF - i, :, sl]
        dlt = win / float(w) - cur
        y = jnp.dot(dlt.astype(BF16), pw_ref[g], preferred_element_type=F32) * ps_ref[:, sl]
        o_ref[:, sl] = y.astype(BF16)


def pool_decode(state_t, proj, pool_w, pool_scale):
    nb = proj.shape[0]
    return pl.pallas_call(
        _pool_decode_body,
        grid=(1,),
        in_specs=[
            pl.BlockSpec((POOL_BUF, nb, POOL_WIDTH), lambda i: (0, 0, 0)),
            pl.BlockSpec((nb, POOL_WIDTH), lambda i: (0, 0)),
            pl.BlockSpec((len(POOL_WINDOWS), POOL_GROUP, POOL_GROUP), lambda i: (0, 0, 0)),
            pl.BlockSpec((1, POOL_WIDTH), lambda i: (0, 0)),
        ],
        out_specs=pl.BlockSpec((nb, POOL_WIDTH), lambda i: (0, 0)),
        out_shape=jax.ShapeDtypeStruct((nb, POOL_WIDTH), BF16),
        compiler_params=_cparams(("arbitrary",)),
        name="pool_decode",
    )(state_t, proj, pool_w, pool_scale.reshape(1, POOL_WIDTH))


def _rel_bucket(n):
    max_exact = REL_BUCKETS // 2
    nf = jnp.maximum(n, 1).astype(F32)
    large = max_exact + (jnp.log(nf / max_exact) / math.log(REL_MAX_DIST / max_exact)
                         * (REL_BUCKETS - max_exact)).astype(jnp.int32)
    large = jnp.minimum(large, REL_BUCKETS - 1)
    return jnp.where(n < max_exact, n, large)


def _table_lookup(tab_ref, bucket, h):
    out = jnp.zeros(bucket.shape, F32)
    for b in range(REL_BUCKETS):
        out = jnp.where(bucket == b, tab_ref[b, h], out)
    return out


def _rel_bias_body(tab_ref, bp_ref, bd_ref, bn_ref, *, past):
    ii = lax.broadcasted_iota(jnp.int32, (ATT_TQ, ATT_TK), 0)
    jj = lax.broadcasted_iota(jnp.int32, (ATT_TQ, ATT_TK), 1)
    for h in range(DIFF_HEADS):
        for t in range(3):
            dist = t * ATT_TK + ii - jj
            bias = _table_lookup(tab_ref, _rel_bucket(jnp.maximum(dist, 0)), h)
            bp_ref[h, t] = jnp.where(dist >= 0, bias, NEG_BIG)
    nrow = 2 * DIFF_HEADS
    row = lax.broadcasted_iota(jnp.int32, (nrow, past * DIFF_HEADS), 0)
    col = lax.broadcasted_iota(jnp.int32, (nrow, past * DIFF_HEADS), 1)
    bucket = _rel_bucket(past - col // DIFF_HEADS)
    rown = lax.broadcasted_iota(jnp.int32, (nrow, LANES), 0)
    bd = jnp.full((nrow, past * DIFF_HEADS), NEG_BIG, F32)
    bn = jnp.zeros((nrow, LANES), F32)
    for h in range(DIFF_HEADS):
        own = jnp.where(row // 2 == h, col % DIFF_HEADS, -1) == h
        bd = jnp.where(own, _table_lookup(tab_ref, bucket, h), bd)
        bn = jnp.where(rown // 2 == h, tab_ref[0, h], bn)
    bd_ref[...] = bd
    bn_ref[...] = bn


def rel_bias_tiles(rel_bias, past):
    return pl.pallas_call(
        functools.partial(_rel_bias_body, past=past),
        in_specs=[pl.BlockSpec(memory_space=pltpu.SMEM)],
        out_specs=[
            pl.BlockSpec(memory_space=pltpu.VMEM),
            pl.BlockSpec(memory_space=pltpu.VMEM),
            pl.BlockSpec(memory_space=pltpu.VMEM),
        ],
        out_shape=[
            jax.ShapeDtypeStruct((DIFF_HEADS, 3, ATT_TQ, ATT_TK), F32),
            jax.ShapeDtypeStruct((2 * DIFF_HEADS, past * DIFF_HEADS), F32),
            jax.ShapeDtypeStruct((2 * DIFF_HEADS, LANES), F32),
        ],
        compiler_params=pltpu.CompilerParams(vmem_limit_bytes=VMEM_LIMIT),
        name="rel_bias_tiles",
    )(rel_bias)


def _lambda(lam_ref, lam_init):
    lp = lam_ref[...]
    s1 = jnp.sum(lp[0:1, :] * lp[1:2, :], axis=-1, keepdims=True)
    s2 = jnp.sum(lp[2:3, :] * lp[3:4, :], axis=-1, keepdims=True)
    return jnp.exp(s1) - jnp.exp(s2) + lam_init


def _attn_prompt_body(q_ref, k_ref, v_ref, bias_ref, lam_ref, sub_ref, o_ref,
                      vb_ref, qq_ref, m_ref, acc_ref, *, lam_init):
    kb_ref = k_ref
    nq = k_ref.shape[0] // ATT_TQ
    lane = lax.broadcasted_iota(jnp.int32, (ATT_TQ, LANES), 1)
    for hh in range(ATT_HEADS):
        hl = slice(hh * LANES, (hh + 1) * LANES)
        vb_ref[hh, :, 0:LANES] = v_ref[:, hl]
        vb_ref[hh, :, LANES:2 * LANES] = jnp.ones((vb_ref.shape[1], LANES), BF16)
        for qi in range(nq):
            q = q_ref[qi * ATT_TQ:(qi + 1) * ATT_TQ, hl].astype(F32) * (DIFF_HALF ** -0.5)
            qq_ref[hh * nq + qi] = jnp.concatenate([jnp.where(lane < DIFF_HALF, q, 0.0),
                                                    jnp.where(lane >= DIFF_HALF, q, 0.0)], axis=0).astype(BF16)
    half = ATT_TQ // 2
    for w in range(nq):
        pieces = []
        for hh in range(ATT_HEADS):
            for qi in range(w, nq):
                tile = min(qi - w, 2)
                parts = [(0, half, half), (half, half, ATT_TK)] if tile == 0 else [(0, ATT_TQ, ATT_TK)]
                for idx in range(2):
                    for r0, nr, nk in parts:
                        pieces.append((hh, hh * nq + qi, slice(idx * ATT_TQ + r0, idx * ATT_TQ + r0 + nr),
                                       nk, tile, r0))
        s_l = [lax.dot_general(qq_ref[e, rows, :], kb_ref[w * ATT_TK:w * ATT_TK + nk, hh * LANES:(hh + 1) * LANES],
                               _NT, preferred_element_type=F32)
               + bias_ref[hh, tile, r0:r0 + rows.stop - rows.start, 0:nk]
               for hh, e, rows, nk, tile, r0 in pieces]
        if w == 0:
            m_prev = [None] * len(pieces)
            m_new = [jnp.broadcast_to(jnp.max(s, axis=-1, keepdims=True), (s.shape[0], LANES)) for s in s_l]
        else:
            m_prev = [m_ref[e, rows, :] for _, e, rows, _, _, _ in pieces]
            m_new = [jnp.maximum(mp, jnp.max(s, axis=-1, keepdims=True)) for mp, s in zip(m_prev, s_l)]
        p_l = [jnp.exp(s - jnp.concatenate([mn] * (pc[3] // LANES), axis=1)).astype(BF16)
               for s, mn, pc in zip(s_l, m_new, pieces)]
        for (hh, e, rows, nk, _, _), mp, mn, p in zip(pieces, m_prev, m_new, p_l):
            pv = jnp.dot(p, vb_ref[hh, w * ATT_TK:w * ATT_TK + nk, :], preferred_element_type=F32)
            if w == 0:
                acc_ref[e, rows, :] = pv
            else:
                alpha = jnp.exp(mp - mn)
                acc_ref[e, rows, :] = jnp.concatenate([alpha, alpha], axis=1) * acc_ref[e, rows, :] + pv
            m_ref[e, rows, :] = mn

    lam = _lambda(lam_ref, lam_init)
    for hh in range(ATT_HEADS):
        for qi in range(nq):
            acc = acc_ref[hh * nq + qi]
            o_all = acc[:, 0:LANES] / acc[:, LANES:2 * LANES]
            o = o_all[0:ATT_TQ, :] - lam * o_all[ATT_TQ:2 * ATT_TQ, :]
            o_ref[qi * ATT_TQ:(qi + 1) * ATT_TQ, hh * LANES:(hh + 1) * LANES] = (
                _rms(o, sub_ref[...]) * (1.0 - lam_init)).astype(BF16)


def attn_prompt(qkv, bias_p, lam_params, subln, lam_init):
    b, t, _ = qkv.shape
    nh = DIFF_HEADS
    nq = t // ATT_TQ
    assert ATT_TQ == ATT_TK
    proj = qkv
    hw = ATT_HEADS * LANES
    groups = nh // ATT_HEADS
    return pl.pallas_call(
        functools.partial(_attn_prompt_body, lam_init=lam_init),
        grid=(b, groups),
        in_specs=[
            pl.BlockSpec((None, t, hw), lambda i, h: (i, 0, h)),
            pl.BlockSpec((None, t, hw), lambda i, h: (i, 0, groups + h)),
            pl.BlockSpec((None, t, hw), lambda i, h: (i, 0, 2 * groups + h)),
            pl.BlockSpec((ATT_HEADS, 3, ATT_TQ, ATT_TK), lambda i, h: (h, 0, 0, 0)),
            pl.BlockSpec((4, DIFF_HALF), lambda i, h: (0, 0)),
            pl.BlockSpec((1, LANES), lambda i, h: (0, 0)),
        ],
        out_specs=pl.BlockSpec((None, t, hw), lambda i, h: (i, 0, h)),
        out_shape=jax.ShapeDtypeStruct((b, t, DIFF_WIDTH), BF16),
        scratch_shapes=[
            pltpu.VMEM((ATT_HEADS, t, 2 * LANES), BF16),
            pltpu.VMEM((ATT_HEADS * nq, 2 * ATT_TQ, LANES), BF16),
            pltpu.VMEM((ATT_HEADS * nq, 2 * ATT_TQ, LANES), F32),
            pltpu.VMEM((ATT_HEADS * nq, 2 * ATT_TQ, 2 * LANES), F32),
        ],
        compiler_params=_cparams(("parallel", "parallel")),
        name="attn_prompt",
    )(proj, proj, proj, bias_p, lam_params, subln.reshape(1, LANES))


def _attn_decode_body(pt_ref, x_ref, ck_ref, cv_ref, bd_ref, bn_ref, lam_ref, sub_ref, o_ref,
                      kbuf_ref, vbuf_ref, sem_ref, *, n_pages, layer_j, lam_init):
    i = pl.program_id(0)
    page_rows = PAGE_SIZE * DIFF_HEADS

    def page_copies(b, slot):
        copies = []
        for p in range(n_pages):
            page = pt_ref[b * n_pages + p]
            copies.append(pltpu.make_async_copy(ck_ref.at[page, layer_j], kbuf_ref.at[slot, p], sem_ref.at[slot, 0]))
            copies.append(pltpu.make_async_copy(cv_ref.at[page, layer_j], vbuf_ref.at[slot, p], sem_ref.at[slot, 1]))
        return copies

    @pl.when(i == 0)
    def _():
        for b in range(ATT_DEC_BUFS - 1):
            for cp in page_copies(b, b):
                cp.start()

    ahead = i + (ATT_DEC_BUFS - 1)

    @pl.when(ahead < pl.num_programs(0))
    def _():
        for cp in page_copies(ahead, ahead % ATT_DEC_BUFS):
            cp.start()

    slot = i % ATT_DEC_BUFS
    for cp in page_copies(i, slot):
        cp.wait()
    k_refs = [kbuf_ref.at[slot, p] for p in range(n_pages)]
    v_refs = [vbuf_ref.at[slot, p] for p in range(n_pages)]
    lane = lax.broadcasted_iota(jnp.int32, (1, DIFF_HEAD_DIM), 1)
    rows_q, rows_k, rows_v = [], [], []
    for h in range(DIFF_HEADS):
        sl = slice(h * DIFF_HEAD_DIM, (h + 1) * DIFF_HEAD_DIM)
        qh = x_ref[:, DIFF_WIDTH:2 * DIFF_WIDTH][:, sl] * (DIFF_HALF ** -0.5)
        kh = x_ref[:, 2 * DIFF_WIDTH:3 * DIFF_WIDTH][:, sl]
        vh = x_ref[:, 3 * DIFF_WIDTH:4 * DIFF_WIDTH][:, sl]
        rows_q += [jnp.where(lane < DIFF_HALF, qh, 0.0), jnp.where(lane >= DIFF_HALF, qh, 0.0)]
        rows_k += [kh, kh]
        rows_v += [vh, vh]
    q8 = jnp.concatenate(rows_q, axis=0)
    k8 = jnp.concatenate(rows_k, axis=0)
    v8 = jnp.concatenate(rows_v, axis=0)
    q8b = q8.astype(BF16)
    s = jnp.concatenate(
        [lax.dot_general(q8b, k_refs[p][...].astype(BF16), _NT, preferred_element_type=F32)
         for p in range(n_pages)], axis=1) + bd_ref[...]
    s_new = jnp.sum(q8 * k8, axis=-1, keepdims=True) + bn_ref[:, 0:1]
    m = jnp.maximum(jnp.max(s, axis=-1, keepdims=True), s_new)
    p = jnp.exp(s - m)
    p_new = jnp.exp(s_new - m)
    inv_l = 1.0 / (jnp.sum(p, axis=-1, keepdims=True) + p_new)
    a = p * inv_l
    r = (p_new * inv_l) * v8
    for pg in range(n_pages):
        r = r + jnp.dot(a[:, pg * page_rows:(pg + 1) * page_rows].astype(BF16),
                        v_refs[pg][...].astype(BF16), preferred_element_type=F32)
    lam = _lambda(lam_ref, lam_init)
    for h in range(DIFF_HEADS):
        o = r[2 * h:2 * h + 1, :] - lam * r[2 * h + 1:2 * h + 2, :]
        o_ref[:, h * DIFF_HEAD_DIM:(h + 1) * DIFF_HEAD_DIM] = (
            _rms(o, sub_ref[...]) * (1.0 - lam_init)).astype(BF16)


def attn_decode(proj, cache_k, cache_v, page_table, layer_j, bias_d, bias_n, lam_params, subln, lam_init):
    nb = proj.shape[0]
    n_pages = page_table.shape[1]
    n_pool, n_even = cache_k.shape[:2]
    page_rows = PAGE_SIZE * DIFF_HEADS
    ck = cache_k.reshape(n_pool, n_even, page_rows, DIFF_HEAD_DIM)
    cv = cache_v.reshape(n_pool, n_even, page_rows, DIFF_HEAD_DIM)

    assert nb >= ATT_DEC_BUFS - 1
    full = lambda shape: pl.BlockSpec(shape, lambda i, pt: (0,) * len(shape))
    grid_spec = pltpu.PrefetchScalarGridSpec(
        num_scalar_prefetch=1,
        grid=(nb,),
        in_specs=[pl.BlockSpec((None, 1, EVEN_IN), lambda i, pt: (i, 0, 0)),
                  pl.BlockSpec(memory_space=pl.ANY), pl.BlockSpec(memory_space=pl.ANY),
                  full(bias_d.shape), full(bias_n.shape), full((4, DIFF_HALF)), full((1, LANES))],
        out_specs=pl.BlockSpec((None, 1, DIFF_WIDTH), lambda i, pt: (i, 0, 0)),
        scratch_shapes=[
            pltpu.VMEM((ATT_DEC_BUFS, n_pages, page_rows, DIFF_HEAD_DIM), F32),
            pltpu.VMEM((ATT_DEC_BUFS, n_pages, page_rows, DIFF_HEAD_DIM), F32),
            pltpu.SemaphoreType.DMA((ATT_DEC_BUFS, 2)),
        ],
    )
    out = pl.pallas_call(
        functools.partial(_attn_decode_body, n_pages=n_pages, layer_j=layer_j, lam_init=lam_init),
        grid_spec=grid_spec,
        out_shape=jax.ShapeDtypeStruct((nb, 1, DIFF_WIDTH), BF16),
        compiler_params=_cparams(("arbitrary",)),
        name="attn_decode",
    )(page_table.reshape(-1), proj.reshape(nb, 1, EVEN_IN), ck, cv,
      bias_d, bias_n, lam_params, subln.reshape(1, LANES))
    return out.reshape(nb, DIFF_WIDTH)


def _blockdiag(x):
    lane = lax.broadcasted_iota(jnp.int32, x.shape, 1)
    zero = jnp.zeros_like(x)
    return jnp.concatenate([jnp.where(lane < GDN_C, x, zero), jnp.where(lane >= GDN_C, x, zero)], axis=0)


def _mmp(a, b):
    return jnp.dot(a.astype(BF16), _blockdiag(b.astype(BF16)), preferred_element_type=F32)


def _unit_lower_inverse(lows, ii, jl):
    eye = jnp.where(ii == jl, 1.0, 0.0)
    in_block = (ii // INV_BLOCK) == (jl // INV_BLOCK)
    ps = [jnp.where(in_block, -low, 0.0) for low in lows]
    offs = [jnp.where(in_block, 0.0, low) for low in lows]
    dinvs = [eye + p for p in ps]
    span = 2
    while span < INV_BLOCK:
        ps = [_mmp(p, p) for p in ps]
        dinvs = [d + _mmp(d, p) for d, p in zip(dinvs, ps)]
        span *= 2
    powers = [[-_mmp(d, off) for d, off in zip(dinvs, offs)]]
    span = 2
    while span < GDN_C // INV_BLOCK:
        powers.append([_mmp(n, n) for n in powers[-1]])
        span *= 2
    xs = dinvs
    for pw in reversed(powers):
        xs = [x + _mmp(n, x) for n, x in zip(pw, xs)]
    return xs


def _gdn_prompt_body(h_ref, gn_ref, w_ref, wgt_ref, cw_ref, alog_ref, dtb_ref, on_ref,
                     o_ref, s_ref, cst_ref, ext_ref, z_ref):
    c = pl.program_id(1)
    nh = GDN_HEADS
    npair = nh // 2
    hk = nh * GDN_DK
    cc = GDN_C
    rows_step = GDN_STEP_CHUNKS * cc
    heads_per_tile = GDN_PROJ_TN // LANES

    @pl.when(c == 0)
    def _():
        s_ref[...] = jnp.zeros_like(s_ref)
        ext_ref[0:8, :] = jnp.zeros((8, GDN_QKV), F32)

    @pl.when(c > 0)
    def _():
        ext_ref[0:8, :] = ext_ref[rows_step:rows_step + 8, :]

    xn = _rms(h_ref[...], gn_ref[...]).astype(BF16)
    abt = lax.dot_general(wgt_ref[...], xn, _NT, preferred_element_type=F32)

    def project(n):
        cols = slice(n * GDN_PROJ_TN, (n + 1) * GDN_PROJ_TN)
        tile = jnp.dot(xn, w_ref[:, cols], preferred_element_type=F32)
        if n < GDN_QKV // GDN_PROJ_TN:
            ext_ref[8:8 + rows_step, cols] = tile
        else:
            z_ref[:, n * GDN_PROJ_TN - GDN_QKV:(n + 1) * GDN_PROJ_TN - GDN_QKV] = tile

    def conv_act(col, r0):
        sl = slice(col, col + LANES)
        acc = ext_ref[8 + r0:8 + r0 + cc, sl] * cw_ref[3:4, sl]
        for i in range(GDN_CONV - 1):
            acc = acc + ext_ref[5 + i + r0:5 + i + r0 + cc, sl] * cw_ref[i:i + 1, sl]
        return _silu(acc)

    def l2n(x):
        return x * lax.rsqrt(jnp.sum(x * x, axis=-1, keepdims=True) + EPS)

    def pair(xs):
        return [jnp.concatenate([xs[2 * p], xs[2 * p + 1]], axis=1) for p in range(npair)]

    def row_pair(x):
        return pair([x[h:h + 1, :] for h in range(nh)])

    project(0)
    lane = lax.broadcasted_iota(jnp.int32, (nh, cc), 1)
    q_p, k_p, v_p, gc_b, beta_b, kdec_b, gc_rows, s_decay = [], [], [], [], [], [], [], []
    for ck in range(GDN_STEP_CHUNKS):
        r0 = ck * cc
        g = -jnp.exp(alog_ref[...]) * _softplus(abt[0:nh, r0:r0 + cc] + dtb_ref[...])
        beta = _sigmoid(abt[nh:2 * nh, r0:r0 + cc])
        gc = g
        shift = 1
        while shift < cc:
            gc = gc + jnp.where(lane >= shift, pltpu.roll(gc, shift, 1), 0.0)
            shift *= 2
        g_last = jnp.broadcast_to(gc[:, cc - 1:cc], (nh, cc))
        s_decay.append(jnp.exp(g_last))
        cols = jnp.transpose(jnp.concatenate([gc, beta, jnp.zeros((LANES - 2 * nh, cc), F32)], axis=0))
        bcast = [jnp.broadcast_to(cols[:, n:n + 1], (cc, cc)) for n in range(2 * nh)]
        gcb = pair(bcast[0:nh])
        gc_b += gcb
        beta_b += pair(bcast[nh:2 * nh])
        kdec_b += [jnp.exp(gl - x) for gl, x in zip(row_pair(g_last), gcb)]
        gc_rows += row_pair(gc)
    gam_b = [jnp.exp(x) for x in gc_b]
    ii = lax.broadcasted_iota(jnp.int32, (cc, 2 * cc), 0)
    jl = lax.broadcasted_iota(jnp.int32, (cc, 2 * cc), 1) % cc
    decay = [jnp.exp(jnp.where(ii >= jl, gb - gr, -jnp.inf)) for gb, gr in zip(gc_b, gc_rows)]

    act = {}
    for n in range(1, GDN_MAIN // GDN_PROJ_TN + 1):
        if n < GDN_MAIN // GDN_PROJ_TN:
            project(n)
        if n <= GDN_QKV // GDN_PROJ_TN:
            for hh in range(heads_per_tile):
                col = (n - 1) * GDN_PROJ_TN + hh * LANES
                for ck in range(GDN_STEP_CHUNKS):
                    a = conv_act(col, ck * cc)
                    if col < hk:
                        a = l2n(a) * (GDN_DK ** -0.5)
                    elif col < 2 * hk:
                        a = l2n(a)
                    act[(col, ck)] = a
    for ck in range(GDN_STEP_CHUNKS):
        q_p += pair([act[(h * GDN_DK, ck)] for h in range(nh)])
        k_p += pair([act[(hk + h * GDN_DK, ck)] for h in range(nh)])
        v_p += pair([act[(2 * hk + h * GDN_DV, ck)] for h in range(nh)])

    gram = [lax.dot_general(jnp.concatenate([q, k], axis=0).astype(BF16), _blockdiag(k.astype(BF16)), _NT,
                            preferred_element_type=F32) for q, k in zip(q_p, k_p)]
    qk = [gm[0:cc, :] * d for gm, d in zip(gram, decay)]
    lows = [jnp.where(ii > jl, b * gm[cc:2 * cc, :] * d, 0.0) for b, gm, d in zip(beta_b, gram, decay)]
    tinv = _unit_lower_inverse(lows, ii, jl)
    w =[_mmp(t, b * gm * k) for t, b, gm, k in zip(tinv, beta_b, gam_b, k_p)]
    u0 = [_mmp(t, b * v) for t, b, v in zip(tinv, beta_b, v_p)]
    wq = [jnp.concatenate([wp, gm * q], axis=0).astype(BF16) for wp, gm, q in zip(w, gam_b, q_p)]
    kd = [(kdb * k).astype(BF16) for kdb, k in zip(kdec_b, k_p)]

    state = [s_ref[h] for h in range(nh)]
    zero = jnp.zeros((GDN_DK, GDN_DV), BF16)
    for ck in range(GDN_STEP_CHUNKS):
        r0 = ck * cc
        ent = range(ck * npair, (ck + 1) * npair)
        s_bd = [jnp.concatenate([jnp.concatenate([state[2 * p].astype(BF16), zero], axis=1),
                                 jnp.concatenate([zero, state[2 * p + 1].astype(BF16)], axis=1)], axis=0)
                for p in range(npair)]
        ws_qs = [jnp.dot(wq[e], sb, preferred_element_type=F32) for e, sb in zip(ent, s_bd)]
        u = [u0[e] - x[0:cc, :] for e, x in zip(ent, ws_qs)]
        o = [x[cc:2 * cc, :] + _mmp(qk[e], b) for e, x, b in zip(ent, ws_qs, u)]
        upd = [lax.dot_general(kd[e], b.astype(BF16), _TN, preferred_element_type=F32)
               for e, b in zip(ent, u)]
        new_state = []
        for h in range(nh):
            half = slice((h % 2) * cc, (h % 2 + 1) * cc)
            new_state.append(s_decay[ck][h:h + 1, :] * state[h] + upd[h // 2][half, half])
            z = z_ref[r0:r0 + cc, h * GDN_DV:(h + 1) * GDN_DV]
            o_ref[r0:r0 + cc, h * GDN_DV:(h + 1) * GDN_DV] = (
                _rms(o[h // 2][:, half], on_ref[...]) * _silu(z)).astype(BF16)
        state = new_state
    for h in range(nh):
        s_ref[h] = state[h]

    @pl.when(c == pl.num_programs(1) - 1)
    def _():
        cst_ref[...] = ext_ref[rows_step:rows_step + 8, :]


def gdn_prompt(h, gain, w_in, w_gates, idx, conv_w, a_log, dt_bias, o_norm):
    b, t, d = h.shape
    rows = GDN_STEP_CHUNKS * GDN_C
    nc = t // rows
    nh = GDN_HEADS
    return pl.pallas_call(
        _gdn_prompt_body,
        grid=(b, nc),
        in_specs=[
            pl.BlockSpec((None, rows, d), lambda i, c: (i, c, 0)),
            pl.BlockSpec((1, d), lambda i, c: (0, 0)),
            pl.BlockSpec((None, d, GDN_MAIN), lambda i, c: (idx, 0, 0), pipeline_mode=pl.Buffered(1)),
            pl.BlockSpec((None, LANES, d), lambda i, c: (idx, 0, 0), pipeline_mode=pl.Buffered(1)),
            pl.BlockSpec((GDN_CONV, GDN_QKV), lambda i, c: (0, 0)),
            pl.BlockSpec((nh, 1), lambda i, c: (0, 0)),
            pl.BlockSpec((nh, 1), lambda i, c: (0, 0)),
            pl.BlockSpec((1, GDN_DV), lambda i, c: (0, 0)),
        ],
        out_specs=[
            pl.BlockSpec((None, rows, nh * GDN_DV), lambda i, c: (i, c, 0)),
            pl.BlockSpec((None, nh, GDN_DK, GDN_DV), lambda i, c: (i, 0, 0, 0)),
            pl.BlockSpec((None, 8, GDN_QKV), lambda i, c: (i, 0, 0)),
        ],
        out_shape=[
            jax.ShapeDtypeStruct((b, t, nh * GDN_DV), BF16),
            jax.ShapeDtypeStruct((b, nh, GDN_DK, GDN_DV), F32),
            jax.ShapeDtypeStruct((b, 8, GDN_QKV), F32),
        ],
        scratch_shapes=[pltpu.VMEM((8 + rows, GDN_QKV), F32), pltpu.VMEM((rows, nh * GDN_DV), F32)],
        compiler_params=_cparams(("parallel", "arbitrary")),
        name="gdn_prompt",
    )(h, gain.reshape(1, d), w_in, w_gates, conv_w, a_log.reshape(nh, 1), dt_bias.reshape(nh, 1),
      o_norm.reshape(1, GDN_DV))


def _gdn_decode_body(x_ref, cs_ref, ab_ref, cw_ref, alog_ref, dtb_ref, on_ref, s0_ref, *rest, first, layer_j):
    o_ref, s_ref = rest[-2:]
    nh = GDN_HEADS
    pad = jnp.zeros((8 - 2, GDN_DK), F32)
    if first:
        for slot in range(s_ref.shape[0]):
            if slot != layer_j:
                s_ref[slot] = jnp.zeros(s_ref.shape[1:], F32)
    pad7 = jnp.zeros((8 - 1, GDN_DK), F32)
    seqs = range(GDN_DEC_ROWS)
    q8, k8, v8, gam8, beta8, qk8 = [], [], [], [], [], []
    for r in seqs:
        x = x_ref[r, 0:3 * nh, :]
        conv = x * cw_ref[GDN_CONV - 1]
        for i in range(GDN_CONV - 1):
            conv = conv + cs_ref[r, i] * cw_ref[i]
        act = conv * _sigmoid(conv)
        qa, ka = act[0:nh], act[nh:2 * nh]
        q8.append(qa * lax.rsqrt(jnp.sum(qa * qa, axis=-1, keepdims=True) + EPS) * (GDN_DK ** -0.5))
        k8.append(ka * lax.rsqrt(jnp.sum(ka * ka, axis=-1, keepdims=True) + EPS))
        v8.append(act[2 * nh:3 * nh])
        g = -jnp.exp(alog_ref[...]) * _softplus(ab_ref[r, 0:nh, :] + dtb_ref[...])
        gam8.append(jnp.broadcast_to(jnp.exp(g), (nh, GDN_DV)))
        beta8.append(jnp.broadcast_to(_sigmoid(ab_ref[r, nh:2 * nh, :]), (nh, GDN_DV)))
        qk8.append(jnp.broadcast_to(jnp.sum(q8[r] * k8[r], axis=-1, keepdims=True), (nh, GDN_DV)))
    pairs = [(r, h) for r in seqs for h in range(nh)]
    s_old = [s0_ref[r, h] for r, h in pairs]
    qk_s = [jnp.dot(jnp.concatenate([q8[r][h:h + 1, :], k8[r][h:h + 1, :], pad], axis=0).astype(BF16),
                    s.astype(BF16), preferred_element_type=F32) for (r, h), s in zip(pairs, s_old)]
    u = [beta8[r][h:h + 1, :] * (v8[r][h:h + 1, :] - gam8[r][h:h + 1, :] * x[1:2, :])
         for (r, h), x in zip(pairs, qk_s)]
    outs = [gam8[r][h:h + 1, :] * x[0:1, :] + qk8[r][h:h + 1, :] * b for (r, h), x, b in zip(pairs, qk_s, u)]
    outer = [lax.dot_general(jnp.concatenate([k8[r][h:h + 1, :], pad7], axis=0).astype(BF16),
                             jnp.concatenate([b, pad7], axis=0).astype(BF16), _TN, preferred_element_type=F32)
             for (r, h), b in zip(pairs, u)]
    for (r, h), s, x in zip(pairs, s_old, outer):
        new = gam8[r][h:h + 1, :] * s + x
        if first:
            s_ref[layer_j, r, h] = new
        else:
            s_ref[r, h] = new
    for r in seqs:
        o8 = jnp.concatenate(outs[r * nh:(r + 1) * nh], axis=0)
        z8 = x_ref[r, 3 * nh:4 * nh, :]
        o_ref[r] = (_rms(o8, on_ref[...]) * (z8 * _sigmoid(z8))).astype(BF16)


def gdn_decode(proj, gates, conv_state, s0_all, layer_j, s_new_all, conv_w, a_log, dt_bias, o_norm):
    nb = proj.shape[0]
    nh = GDN_HEADS
    nrow = GDN_QKV // LANES
    first = s_new_all is None
    in_specs = [
        pl.BlockSpec((GDN_DEC_ROWS, GDN_MAIN // LANES, LANES), lambda i: (i, 0, 0)),
        pl.BlockSpec((GDN_DEC_ROWS, GDN_CONV - 1, nrow, LANES), lambda i: (i, 0, 0, 0)),
        pl.BlockSpec((GDN_DEC_ROWS, 2 * nh, 1), lambda i: (i, 0, 0)),
        pl.BlockSpec((GDN_CONV, nrow, LANES), lambda i: (0, 0, 0)),
        pl.BlockSpec((nh, 1), lambda i: (0, 0)),
        pl.BlockSpec((nh, 1), lambda i: (0, 0)),
        pl.BlockSpec((1, GDN_DV), lambda i: (0, 0)),
        pl.BlockSpec((None, GDN_DEC_ROWS, nh, GDN_DK, GDN_DV), lambda i: (layer_j, i, 0, 0, 0)),
    ]
    args = [proj.reshape(nb, GDN_MAIN // LANES, LANES),
            conv_state.reshape(nb, GDN_CONV - 1, nrow, LANES),
            gates.reshape(nb, 2 * nh, 1),
            conv_w.reshape(GDN_CONV, nrow, LANES),
            a_log.reshape(nh, 1), dt_bias.reshape(nh, 1), o_norm.reshape(1, GDN_DV), s0_all]
    if first:
        s_spec = pl.BlockSpec((s0_all.shape[0], GDN_DEC_ROWS, nh, GDN_DK, GDN_DV), lambda i: (0, i, 0, 0, 0))
        aliases = {}
    else:
        in_specs.append(pl.BlockSpec(memory_space=pl.ANY))
        args.append(s_new_all)
        s_spec = pl.BlockSpec((None, GDN_DEC_ROWS, nh, GDN_DK, GDN_DV), lambda i: (layer_j, i, 0, 0, 0))
        aliases = {len(args) - 1: 1}
    o, s = pl.pallas_call(
        functools.partial(_gdn_decode_body, first=first, layer_j=layer_j),
        grid=(nb // GDN_DEC_ROWS,),
        in_specs=in_specs,
        out_specs=[pl.BlockSpec((GDN_DEC_ROWS, nh, GDN_DV), lambda i: (i, 0, 0)), s_spec],
        out_shape=[
            jax.ShapeDtypeStruct((nb, nh, GDN_DV), BF16),
            jax.ShapeDtypeStruct(s0_all.shape, F32),
        ],
        input_output_aliases=aliases,
        compiler_params=_cparams(("parallel",)),
        name="gdn_decode",
    )(*args)
    return o.reshape(nb, nh * GDN_DV), s


def kernel(x_prompt, x_sample, cache_k, cache_v, page_table, state_pool, state_conv, state_delta,
           norm_mix, norm_ffn, norm_final, rel_bias,
           w_in_even, pool_w, pool_scale, lambda_q1, lambda_k1, lambda_q2, lambda_k2, subln_w, w_out_even,
           w_in_odd, conv_w, a_log, dt_bias, o_norm, w_out_odd,
           w_gate_up, w_down):
    bp, t, d = x_prompt.shape
    bs = x_sample.shape[0]
    mp = bp * t
    nh = GDN_HEADS
    past = page_table.shape[1] * PAGE_SIZE
    tm_p = 1024
    tm_s = bs

    hp = x_prompt.reshape(mp, d)
    hs = x_sample.reshape(bs, d)
    bias_p, bias_d, bias_n = rel_bias_tiles(rel_bias, past)

    w_in_e = w_in_even.astype(BF16)
    w_in_o = w_in_odd.astype(BF16)
    w_gate_cols = lax.optimization_barrier(w_in_odd[:, :, GDN_MAIN:])
    w_gates = jnp.zeros((w_in_odd.shape[0], LANES, d), F32).at[:, 0:2 * nh, :].set(
        jnp.transpose(w_gate_cols, (0, 2, 1))).astype(BF16)
    w_out_e = w_out_even.astype(BF16)
    w_out_o = w_out_odd.astype(BF16)
    w_gu = w_gate_up.astype(BF16)
    w_d = w_down.astype(BF16)
    pool_wb = pool_w.astype(BF16)

    k_s, v_s, pool_p, pool_s = [], [], [], []
    conv_p, conv_s, delta_p = [], [], []
    kv_p = None
    n_even = w_in_even.shape[0]
    delta_s = None
    for layer in range(DEPTH):
        j = layer // 2
        last = layer == DEPTH - 1
        if layer % 2 == 0:
            w_out = w_out_e
            pw = pool_wb[j]
            lam_init = 0.8 - 0.6 * math.exp(-0.3 * layer)
            lam_params = jnp.stack([lambda_q1[j], lambda_k1[j], lambda_q2[j], lambda_k2[j]])

            qkv_p, ypool_p, tail_p, kv_p = even_proj_prompt(hp, norm_mix[layer], w_in_e, j, pw, pool_scale[j],
                                                            t, n_even, kv_p, tm=tm_p)
            proj_s, ks_, vs_ = norm_matmul(hs, norm_mix[layer], w_in_e, j, tm=tm_s, tn=DIFF_WIDTH)

            oatt_p = attn_prompt(qkv_p.reshape(bp, t, 3 * DIFF_WIDTH), bias_p, lam_params, subln_w[j], lam_init)
            ypool_s = pool_decode(jnp.transpose(state_pool[j], (1, 0, 2)), proj_s, pw, pool_scale[j])
            oatt_s = attn_decode(proj_s, cache_k, cache_v, page_table, j, bias_d, bias_n,
                                 lam_params, subln_w[j], lam_init)

            kv_shape = (DIFF_HEADS, DIFF_HEAD_DIM)
            k_s.append(ks_.reshape(bs, 1, *kv_shape))
            v_s.append(vs_.reshape(bs, 1, *kv_shape))
            pool_p.append(tail_p[:, POOL_PAD - POOL_BUF:, :])
            pool_s.append(jnp.concatenate([state_pool[j][:, 1:], proj_s[:, None, 0:POOL_WIDTH]], axis=1))

            mix_p = ((ypool_p, 0), (oatt_p.reshape(mp, DIFF_WIDTH), 0))
            mix_s = ((ypool_s, 0), (oatt_s, 0))
        else:
            w_out = w_out_o

            proj_s, gates_s = norm_matmul(hs, norm_mix[layer], w_in_o, j, w_gates, tm=tm_s, tn=512)

            o_p, s_p, tail_p = gdn_prompt(hp.reshape(bp, t, d), norm_mix[layer], w_in_o, w_gates, j,
                                          conv_w[j], a_log[j], dt_bias[j], o_norm[j])
            o_s, delta_s = gdn_decode(proj_s, gates_s[0:2 * nh].T, state_conv[j], state_delta, j, delta_s,
                                      conv_w[j], a_log[j], dt_bias[j], o_norm[j])

            conv_p.append(tail_p[:, 8 - (GDN_CONV - 1):, :])
            conv_s.append(jnp.concatenate([state_conv[j][:, 1:], proj_s[:, None, 0:GDN_QKV]], axis=1))
            delta_p.append(s_p)

            o_p = o_p.reshape(mp, nh * GDN_DV)
            mix_p = ((o_p, 0), (o_p, 1))
            mix_s = ((o_s, 0), (o_s, 1))

        g_fin = norm_final if last else None
        hp = outproj_ffn(hp, mix_p[0], mix_p[1], w_out, j, norm_ffn[layer], w_gu, w_d, layer, g_fin, tm=FFN_TM)
        hs = outproj_ffn(hs, mix_s[0], mix_s[1], w_out, j, norm_ffn[layer], w_gu, w_d, layer, g_fin, tm=tm_s)

    return (hp.reshape(bp, t, d), hs.reshape(bs, 1, d),
            kv_p[0].reshape(bp, n_even, t, DIFF_HEADS, DIFF_HEAD_DIM),
            kv_p[1].reshape(bp, n_even, t, DIFF_HEADS, DIFF_HEAD_DIM),
            jnp.stack(k_s, axis=1), jnp.stack(v_s, axis=1),
            jnp.stack(pool_p), jnp.stack(pool_s), jnp.stack(conv_p), jnp.stack(conv_s),
            jnp.stack(delta_p), delta_s)
```

```python
import functools
import math

import jax
import jax.numpy as jnp
from jax import lax
from jax.experimental import pallas as pl
from jax.experimental.pallas import tpu as pltpu

F32 = jnp.float32
BF16 = jnp.bfloat16

D_MODEL = 1024
DEPTH = 4
PAGE_SIZE = 128
POOL_WIDTH = 512
POOL_WINDOWS = (2, 4, 8, 16)
POOL_GROUP = 128
POOL_BUF = 15
DIFF_HEADS = 4
DIFF_HALF = 64
DIFF_HEAD_DIM = 128
DIFF_WIDTH = 512
EVEN_IN = 2048
REL_BUCKETS = 32
REL_MAX_DIST = 128
GDN_HEADS = 8
GDN_DK = 128
GDN_DV = 128
GDN_CONV = 4
GDN_QKV = 3072
GDN_MAIN = 4096
D_FF = 2816
EPS = 1e-6

LANES = 128
VMEM_LIMIT = 48 * 1024 * 1024
NEG_BIG = -1e30

ATT_TQ = 512
ATT_TK = 512
ATT_HEADS = 1
GDN_C = 128
GDN_STEP_CHUNKS = 2
GDN_DEC_ROWS = 4
ATT_DEC_BUFS = 3
GDN_PROJ_TN = 512
INV_BLOCK = 16
FFN_TF = 256
FFN_TM = 512
EVEN_TM = 512

_NT = (((1,), (1,)), ((), ()))
_TN = (((0,), (0,)), ((), ()))


def _cparams(sem):
    return pltpu.CompilerParams(dimension_semantics=sem, vmem_limit_bytes=VMEM_LIMIT)


def _sigmoid(x):
    return 1.0 / (1.0 + jnp.exp(-x))


def _silu(x):
    h = 0.5 * x
    return h + h * jnp.tanh(h)


def _softplus(x):
    return jnp.maximum(x, 0.0) + jnp.log1p(jnp.exp(-jnp.abs(x)))


def _rms(x, gain):
    return x * lax.rsqrt(jnp.mean(x * x, axis=-1, keepdims=True) + EPS) * gain


def _norm_mm_kv_body(x_ref, g_ref, w_ref, o_ref, k_ref, v_ref, xn_ref):
    j = pl.program_id(1)

    @pl.when(j == 0)
    def _():
        xn_ref[...] = _rms(x_ref[...], g_ref[...]).astype(BF16)

    res = jnp.dot(xn_ref[...], w_ref[...], preferred_element_type=F32)
    o_ref[...] = res

    def head_rows(dst_ref):
        for h in range(DIFF_HEADS):
            dst_ref[pl.ds(h, res.shape[0], stride=DIFF_HEADS), :] = res[:, h * DIFF_HEAD_DIM:(h + 1) * DIFF_HEAD_DIM]

    @pl.when(j == 2)
    def _():
        head_rows(k_ref)

    @pl.when(j == 3)
    def _():
        head_rows(v_ref)


def _norm_mm_gate_body(x_ref, g_ref, w_ref, wst_ref, o_ref, ost_ref, xn_ref):
    @pl.when(pl.program_id(1) == 0)
    def _():
        xn = _rms(x_ref[...], g_ref[...]).astype(BF16)
        xn_ref[...] = xn
        ost_ref[...] = lax.dot_general(wst_ref[...], xn, _NT, preferred_element_type=F32)

    o_ref[...] = jnp.dot(xn_ref[...], w_ref[...], preferred_element_type=F32)


def norm_matmul(x, gain, w, idx, w_gates=None, *, tm, tn):
    m, d = x.shape
    n = EVEN_IN if w_gates is None else GDN_MAIN
    grid = (m // tm, n // tn)
    in_specs = [
        pl.BlockSpec((tm, d), lambda i, j: (i, 0)),
        pl.BlockSpec((1, d), lambda i, j: (0, 0)),
        pl.BlockSpec((None, d, tn), lambda i, j: (idx, 0, j)),
    ]
    out_specs = pl.BlockSpec((tm, tn), lambda i, j: (i, j))
    out_shape = jax.ShapeDtypeStruct((m, n), F32)
    args = [x, gain.reshape(1, d), w]
    if w_gates is None:
        assert tn == DIFF_WIDTH and n == EVEN_IN
        kv_spec = pl.BlockSpec((tm * DIFF_HEADS, DIFF_HEAD_DIM), lambda i, j: (i, 0))
        kv_shape = jax.ShapeDtypeStruct((m * DIFF_HEADS, DIFF_HEAD_DIM), F32)
        out_specs = [out_specs, kv_spec, kv_spec]
        out_shape = [out_shape, kv_shape, kv_shape]
        body = _norm_mm_kv_body
    else:
        in_specs.append(pl.BlockSpec((None, LANES, d), lambda i, j: (idx, 0, 0)))
        out_specs = [out_specs, pl.BlockSpec((LANES, tm), lambda i, j: (0, i))]
        out_shape = [out_shape, jax.ShapeDtypeStruct((LANES, m), F32)]
        args.append(w_gates)
        body = _norm_mm_gate_body
    return pl.pallas_call(
        body,
        grid=grid,
        in_specs=in_specs,
        out_specs=out_specs,
        out_shape=out_shape,
        scratch_shapes=[pltpu.VMEM((tm, d), BF16)],
        compiler_params=_cparams(("parallel", "arbitrary")),
        name="norm_matmul",
    )(*args)


def _ffn_body(*refs, final):
    if final:
        res_ref, a0_ref, a1_ref, wo_ref, gf_ref, wgu_ref, wd_ref, gfin_ref, o_ref = refs
    else:
        res_ref, a0_ref, a1_ref, wo_ref, gf_ref, wgu_ref, wd_ref, o_ref = refs
    mixed = jnp.concatenate([a0_ref[...], a1_ref[...]], axis=1)
    h1 = res_ref[...] + jnp.dot(mixed, wo_ref[...], preferred_element_type=F32)
    xn = _rms(h1, gf_ref[...]).astype(BF16)
    acts = []
    for f in range(D_FF // FFN_TF):
        g = jnp.dot(xn, wgu_ref[:, f * FFN_TF:(f + 1) * FFN_TF], preferred_element_type=F32)
        u = jnp.dot(xn, wgu_ref[:, D_FF + f * FFN_TF:D_FF + (f + 1) * FFN_TF], preferred_element_type=F32)
        acts.append((g * _sigmoid(g) * u).astype(BF16))
    y = h1 + jnp.dot(jnp.concatenate(acts, axis=1), wd_ref[...], preferred_element_type=F32)
    if final:
        y = _rms(y, gfin_ref[...])
    o_ref[...] = y


def outproj_ffn(res, mix0, mix1, w_out, idx_out, g_ffn, w_gu, w_d, layer, g_final=None, *, tm):
    m, d = res.shape
    half = w_out.shape[1] // 2
    (m0, c0), (m1, c1) = mix0, mix1
    final = g_final is not None
    once = pl.Buffered(1)
    in_specs = [
        pl.BlockSpec((tm, d), lambda i: (i, 0)),
        pl.BlockSpec((tm, half), lambda i: (i, c0)),
        pl.BlockSpec((tm, half), lambda i: (i, c1)),
        pl.BlockSpec((None, 2 * half, d), lambda i: (idx_out, 0, 0), pipeline_mode=once),
        pl.BlockSpec((1, d), lambda i: (0, 0)),
        pl.BlockSpec((None, d, 2 * D_FF), lambda i: (layer, 0, 0), pipeline_mode=once),
        pl.BlockSpec((None, D_FF, d), lambda i: (layer, 0, 0), pipeline_mode=once),
    ]
    args = [res, m0, m1, w_out, g_ffn.reshape(1, d), w_gu, w_d]
    if final:
        in_specs.append(pl.BlockSpec((1, d), lambda i: (0, 0)))
        args.append(g_final.reshape(1, d))
    return pl.pallas_call(
        functools.partial(_ffn_body, final=final),
        grid=(m // tm,),
        in_specs=in_specs,
        out_specs=pl.BlockSpec((tm, d), lambda i: (i, 0)),
        out_shape=jax.ShapeDtypeStruct((m, d), F32),
        compiler_params=_cparams(("parallel",)),
        name="outproj_ffn",
    )(*args)


POOL_PAD = 16
POOL_ROWS = 512


def _even_proj_body(x_ref, g_ref, w_ref, pw_ref, ps_ref, *rest, blocks_per_seq, first, layer_j):
    qkv_ref, yp_ref, k_ref, v_ref, tail_ref, ext_ref = rest[-6:]
    i = pl.program_id(0)
    tm = x_ref.shape[0]
    tn = DIFF_WIDTH
    blk = i % blocks_per_seq

    @pl.when(blk == 0)
    def _():
        ext_ref[0:POOL_PAD, :] = jnp.zeros((POOL_PAD, POOL_WIDTH), F32)

    @pl.when(blk > 0)
    def _():
        ext_ref[0:POOL_PAD, :] = ext_ref[tm:tm + POOL_PAD, :]

    xn = _rms(x_ref[...], g_ref[...]).astype(BF16)

    def project(n):
        return jnp.dot(xn, w_ref[:, n * tn:(n + 1) * tn], preferred_element_type=F32)

    def head_rows(dst_ref, res):
        if first:
            for slot in range(dst_ref.shape[0]):
                if slot != layer_j:
                    dst_ref[slot] = jnp.zeros(dst_ref.shape[1:], F32)
            dst_ref = dst_ref.at[layer_j]
        for h in range(DIFF_HEADS):
            dst_ref[pl.ds(h, tm, stride=DIFF_HEADS), :] = res[:, h * DIFF_HEAD_DIM:(h + 1) * DIFF_HEAD_DIM]

    u = project(0)
    ext_ref[POOL_PAD:, :] = u
    tail_ref[...] = u[tm - POOL_PAD:tm, :]

    for n in range(1, EVEN_IN // tn):
        res = project(n)
        qkv_ref[:, (n - 1) * tn:n * tn] = res.astype(BF16)
        if n == 2:
            head_rows(k_ref, res)
        elif n == 3:
            head_rows(v_ref, res)

    for r0 in range(0, tm, POOL_ROWS):
        pos = lax.broadcasted_iota(jnp.int32, (POOL_ROWS, 1), 0) + (blk * tm + r0)
        for g, w in enumerate(POOL_WINDOWS):
            sl = slice(g * POOL_GROUP, (g + 1) * POOL_GROUP)
            cur = ext_ref[POOL_PAD + r0:POOL_PAD + r0 + POOL_ROWS, sl]
            win = cur
            for s in range(1, w):
                win = win + ext_ref[POOL_PAD + r0 - s:POOL_PAD + r0 - s + POOL_ROWS, sl]
            cnt = jnp.minimum(pos + 1, w).astype(F32)
            dlt = win / cnt - cur
            y = jnp.dot(dlt.astype(BF16), pw_ref[g], preferred_element_type=F32) * ps_ref[:, sl]
            yp_ref[r0:r0 + POOL_ROWS, sl] = y.astype(BF16)


def even_proj_prompt(x, gain, w, idx, pool_w, pool_scale, seq_len, n_layers, kv_all, *, tm):
    m, d = x.shape
    blocks_per_seq = seq_len // tm
    nseq = m // seq_len
    first = kv_all is None
    in_specs = [
        pl.BlockSpec((tm, d), lambda i: (i, 0)),
        pl.BlockSpec((1, d), lambda i: (0, 0)),
        pl.BlockSpec((None, d, EVEN_IN), lambda i: (idx, 0, 0), pipeline_mode=pl.Buffered(1)),
        pl.BlockSpec((len(POOL_WINDOWS), POOL_GROUP, POOL_GROUP), lambda i: (0, 0, 0)),
        pl.BlockSpec((1, POOL_WIDTH), lambda i: (0, 0)),
    ]
    args = [x, gain.reshape(1, d), w, pool_w, pool_scale.reshape(1, POOL_WIDTH)]
    kv_rows = tm * DIFF_HEADS
    if first:
        kv_spec = pl.BlockSpec((None, n_layers, kv_rows, DIFF_HEAD_DIM),
                               lambda i: (i // blocks_per_seq, 0, i % blocks_per_seq, 0))
        aliases = {}
    else:
        in_specs += [pl.BlockSpec(memory_space=pl.ANY), pl.BlockSpec(memory_space=pl.ANY)]
        args += list(kv_all)
        kv_spec = pl.BlockSpec((None, None, kv_rows, DIFF_HEAD_DIM),
                               lambda i: (i // blocks_per_seq, idx, i % blocks_per_seq, 0))
        aliases = {len(args) - 2: 2, len(args) - 1: 3}
    kv_shape = jax.ShapeDtypeStruct((nseq, n_layers, seq_len * DIFF_HEADS, DIFF_HEAD_DIM), F32)
    qkv, ypool, k_all, v_all, tail = pl.pallas_call(
        functools.partial(_even_proj_body, blocks_per_seq=blocks_per_seq, first=first, layer_j=idx),
        grid=(m // tm,),
        in_specs=in_specs,
        out_specs=[
            pl.BlockSpec((tm, 3 * DIFF_WIDTH), lambda i: (i, 0)),
            pl.BlockSpec((tm, POOL_WIDTH), lambda i: (i, 0)),
            kv_spec,
            kv_spec,
            pl.BlockSpec((None, POOL_PAD, POOL_WIDTH), lambda i: (i // blocks_per_seq, 0, 0)),
        ],
        out_shape=[
            jax.ShapeDtypeStruct((m, 3 * DIFF_WIDTH), BF16),
            jax.ShapeDtypeStruct((m, POOL_WIDTH), BF16),
            kv_shape,
            kv_shape,
            jax.ShapeDtypeStruct((nseq, POOL_PAD, POOL_WIDTH), F32),
        ],
        scratch_shapes=[pltpu.VMEM((POOL_PAD + tm, POOL_WIDTH), F32)],
        input_output_aliases=aliases,
        compiler_params=_cparams(("arbitrary",)),
        name="even_proj",
    )(*args)
    return qkv, ypool, tail, (k_all, v_all)


def _pool_decode_body(st_ref, u_ref, pw_ref, ps_ref, o_ref):
    for g, w in enumerate(POOL_WINDOWS):
        sl = slice(g * POOL_GROUP, (g + 1) * POOL_GROUP)
        cur = u_ref[:, sl]
        win = cur
        for i in range(1, w):
            win = win + st_ref[POOL_BUF - i, :, sl]
        dlt = win / float(w) - cur
        y = jnp.dot(dlt.astype(BF16), pw_ref[g], preferred_element_type=F32) * ps_ref[:, sl]
        o_ref[:, sl] = y.astype(BF16)


def pool_decode(state_t, proj, pool_w, pool_scale):
    nb = proj.shape[0]
    return pl.pallas_call(
        _pool_decode_body,
        grid=(1,),
        in_specs=[
            pl.BlockSpec((POOL_BUF, nb, POOL_WIDTH), lambda i: (0, 0, 0)),
            pl.BlockSpec((nb, POOL_WIDTH), lambda i: (0, 0)),
            pl.BlockSpec((len(POOL_WINDOWS), POOL_GROUP, POOL_GROUP), lambda i: (0, 0, 0)),
            pl.BlockSpec((1, POOL_WIDTH), lambda i: (0, 0)),
        ],
        out_specs=pl.BlockSpec((nb, POOL_WIDTH), lambda i: (0, 0)),
        out_shape=jax.ShapeDtypeStruct((nb, POOL_WIDTH), BF16),
        compiler_params=_cparams(("arbitrary",)),
        name="pool_decode",
    )(state_t, proj, pool_w, pool_scale.reshape(1, POOL_WIDTH))


def _rel_bucket(n):
    max_exact = REL_BUCKETS // 2
    nf = jnp.maximum(n, 1).astype(F32)
    large = max_exact + (jnp.log(nf / max_exact) / math.log(REL_MAX_DIST / max_exact)
                         * (REL_BUCKETS - max_exact)).astype(jnp.int32)
    large = jnp.minimum(large, REL_BUCKETS - 1)
    return jnp.where(n < max_exact, n, large)


def _table_lookup(tab_ref, bucket, h):
    out = jnp.zeros(bucket.shape, F32)
    for b in range(REL_BUCKETS):
        out = jnp.where(bucket == b, tab_ref[b, h], out)
    return out


def _rel_bias_body(tab_ref, bp_ref, bd_ref, bn_ref, *, past):
    ii = lax.broadcasted_iota(jnp.int32, (ATT_TQ, ATT_TK), 0)
    jj = lax.broadcasted_iota(jnp.int32, (ATT_TQ, ATT_TK), 1)
    for h in range(DIFF_HEADS):
        for t in range(3):
            dist = t * ATT_TK + ii - jj
            bias = _table_lookup(tab_ref, _rel_bucket(jnp.maximum(dist, 0)), h)
            bp_ref[h, t] = jnp.where(dist >= 0, bias, NEG_BIG)
    nrow = 2 * DIFF_HEADS
    row = lax.broadcasted_iota(jnp.int32, (nrow, past * DIFF_HEADS), 0)
    col = lax.broadcasted_iota(jnp.int32, (nrow, past * DIFF_HEADS), 1)
    bucket = _rel_bucket(past - col // DIFF_HEADS)
    rown = lax.broadcasted_iota(jnp.int32, (nrow, LANES), 0)
    bd = jnp.full((nrow, past * DIFF_HEADS), NEG_BIG, F32)
    bn = jnp.zeros((nrow, LANES), F32)
    for h in range(DIFF_HEADS):
        own = jnp.where(row // 2 == h, col % DIFF_HEADS, -1) == h
        bd = jnp.where(own, _table_lookup(tab_ref, bucket, h), bd)
        bn = jnp.where(rown // 2 == h, tab_ref[0, h], bn)
    bd_ref[...] = bd
    bn_ref[...] = bn


def rel_bias_tiles(rel_bias, past):
    return pl.pallas_call(
        functools.partial(_rel_bias_body, past=past),
        in_specs=[pl.BlockSpec(memory_space=pltpu.SMEM)],
        out_specs=[
            pl.BlockSpec(memory_space=pltpu.VMEM),
            pl.BlockSpec(memory_space=pltpu.VMEM),
            pl.BlockSpec(memory_space=pltpu.VMEM),
        ],
        out_shape=[
            jax.ShapeDtypeStruct((DIFF_HEADS, 3, ATT_TQ, ATT_TK), F32),
            jax.ShapeDtypeStruct((2 * DIFF_HEADS, past * DIFF_HEADS), F32),
            jax.ShapeDtypeStruct((2 * DIFF_HEADS, LANES), F32),
        ],
        compiler_params=pltpu.CompilerParams(vmem_limit_bytes=VMEM_LIMIT),
        name="rel_bias_tiles",
    )(rel_bias)


def _lambda(lam_ref, lam_init):
    lp = lam_ref[...]
    s1 = jnp.sum(lp[0:1, :] * lp[1:2, :], axis=-1, keepdims=True)
    s2 = jnp.sum(lp[2:3, :] * lp[3:4, :], axis=-1, keepdims=True)
    return jnp.exp(s1) - jnp.exp(s2) + lam_init


def _attn_prompt_body(q_ref, k_ref, v_ref, bias_ref, lam_ref, sub_ref, o_ref,
                      vb_ref, qq_ref, m_ref, acc_ref, *, lam_init):
    kb_ref = k_ref
    nq = k_ref.shape[0] // ATT_TQ
    lane = lax.broadcasted_iota(jnp.int32, (ATT_TQ, LANES), 1)
    for hh in range(ATT_HEADS):
        hl = slice(hh * LANES, (hh + 1) * LANES)
        vb_ref[hh, :, 0:LANES] = v_ref[:, hl]
        vb_ref[hh, :, LANES:2 * LANES] = jnp.ones((vb_ref.shape[1], LANES), BF16)
        for qi in range(nq):
            q = q_ref[qi * ATT_TQ:(qi + 1) * ATT_TQ, hl].astype(F32) * (DIFF_HALF ** -0.5)
            qq_ref[hh * nq + qi] = jnp.concatenate([jnp.where(lane < DIFF_HALF, q, 0.0),
                                                    jnp.where(lane >= DIFF_HALF, q, 0.0)], axis=0).astype(BF16)
    half = ATT_TQ // 2
    for w in range(nq):
        pieces = []
        for hh in range(ATT_HEADS):
            for qi in range(w, nq):
                tile = min(qi - w, 2)
                parts = [(0, half, half), (half, half, ATT_TK)] if tile == 0 else [(0, ATT_TQ, ATT_TK)]
                for idx in range(2):
                    for r0, nr, nk in parts:
                        pieces.append((hh, hh * nq + qi, slice(idx * ATT_TQ + r0, idx * ATT_TQ + r0 + nr),
                                       nk, tile, r0))
        s_l = [lax.dot_general(qq_ref[e, rows, :], kb_ref[w * ATT_TK:w * ATT_TK + nk, hh * LANES:(hh + 1) * LANES],
                               _NT, preferred_element_type=F32)
               + bias_ref[hh, tile, r0:r0 + rows.stop - rows.start, 0:nk]
               for hh, e, rows, nk, tile, r0 in pieces]
        if w == 0:
            m_prev = [None] * len(pieces)
            m_new = [jnp.broadcast_to(jnp.max(s, axis=-1, keepdims=True), (s.shape[0], LANES)) for s in s_l]
        else:
            m_prev = [m_ref[e, rows, :] for _, e, rows, _, _, _ in pieces]
            m_new = [jnp.maximum(mp, jnp.max(s, axis=-1, keepdims=True)) for mp, s in zip(m_prev, s_l)]
        p_l = [jnp.exp(s - jnp.concatenate([mn] * (pc[3] // LANES), axis=1)).astype(BF16)
               for s, mn, pc in zip(s_l, m_new, pieces)]
        for (hh, e, rows, nk, _, _), mp, mn, p in zip(pieces, m_prev, m_new, p_l):
            pv = jnp.dot(p, vb_ref[hh, w * ATT_TK:w * ATT_TK + nk, :], preferred_element_type=F32)
            if w == 0:
                acc_ref[e, rows, :] = pv
            else:
                alpha = jnp.exp(mp - mn)
                acc_ref[e, rows, :] = jnp.concatenate([alpha, alpha], axis=1) * acc_ref[e, rows, :] + pv
            m_ref[e, rows, :] = mn

    lam = _lambda(lam_ref, lam_init)
    for hh in range(ATT_HEADS):
        for qi in range(nq):
            acc = acc_ref[hh * nq + qi]
            o_all = acc[:, 0:LANES] / acc[:, LANES:2 * LANES]
            o = o_all[0:ATT_TQ, :] - lam * o_all[ATT_TQ:2 * ATT_TQ, :]
            o_ref[qi * ATT_TQ:(qi + 1) * ATT_TQ, hh * LANES:(hh + 1) * LANES] = (
                _rms(o, sub_ref[...]) * (1.0 - lam_init)).astype(BF16)


def attn_prompt(qkv, bias_p, lam_params, subln, lam_init):
    b, t, _ = qkv.shape
    nh = DIFF_HEADS
    nq = t // ATT_TQ
    assert ATT_TQ == ATT_TK
    hw = ATT_HEADS * LANES
    groups = nh // ATT_HEADS
    return pl.pallas_call(
        functools.partial(_attn_prompt_body, lam_init=lam_init),
        grid=(b, groups),
        in_specs=[
            pl.BlockSpec((None, t, hw), lambda i, h: (i, 0, h)),
            pl.BlockSpec((None, t, hw), lambda i, h: (i, 0, groups + h)),
            pl.BlockSpec((None, t, hw), lambda i, h: (i, 0, 2 * groups + h)),
            pl.BlockSpec((ATT_HEADS, 3, ATT_TQ, ATT_TK), lambda i, h: (h, 0, 0, 0)),
            pl.BlockSpec((4, DIFF_HALF), lambda i, h: (0, 0)),
            pl.BlockSpec((1, LANES), lambda i, h: (0, 0)),
        ],
        out_specs=pl.BlockSpec((None, t, hw), lambda i, h: (i, 0, h)),
        out_shape=jax.ShapeDtypeStruct((b, t, DIFF_WIDTH), BF16),
        scratch_shapes=[
            pltpu.VMEM((ATT_HEADS, t, 2 * LANES), BF16),
            pltpu.VMEM((ATT_HEADS * nq, 2 * ATT_TQ, LANES), BF16),
            pltpu.VMEM((ATT_HEADS * nq, 2 * ATT_TQ, LANES), F32),
            pltpu.VMEM((ATT_HEADS * nq, 2 * ATT_TQ, 2 * LANES), F32),
        ],
        compiler_params=_cparams(("parallel", "parallel")),
        name="attn_prompt",
    )(qkv, qkv, qkv, bias_p, lam_params, subln.reshape(1, LANES))


def _attn_decode_body(pt_ref, x_ref, ck_ref, cv_ref, bd_ref, bn_ref, lam_ref, sub_ref, o_ref,
                      kbuf_ref, vbuf_ref, sem_ref, *, n_pages, layer_j, lam_init):
    i = pl.program_id(0)
    page_rows = PAGE_SIZE * DIFF_HEADS

    def page_copies(b, slot):
        copies = []
        for p in range(n_pages):
            page = pt_ref[b * n_pages + p]
            copies.append(pltpu.make_async_copy(ck_ref.at[page, layer_j], kbuf_ref.at[slot, p], sem_ref.at[slot, 0]))
            copies.append(pltpu.make_async_copy(cv_ref.at[page, layer_j], vbuf_ref.at[slot, p], sem_ref.at[slot, 1]))
        return copies

    @pl.when(i == 0)
    def _():
        for b in range(ATT_DEC_BUFS - 1):
            for cp in page_copies(b, b):
                cp.start()

    ahead = i + (ATT_DEC_BUFS - 1)

    @pl.when(ahead < pl.num_programs(0))
    def _():
        for cp in page_copies(ahead, ahead % ATT_DEC_BUFS):
            cp.start()

    slot = i % ATT_DEC_BUFS
    for cp in page_copies(i, slot):
        cp.wait()
    k_refs = [kbuf_ref.at[slot, p] for p in range(n_pages)]
    v_refs = [vbuf_ref.at[slot, p] for p in range(n_pages)]
    lane = lax.broadcasted_iota(jnp.int32, (1, DIFF_HEAD_DIM), 1)
    rows_q, rows_k, rows_v = [], [], []
    for h in range(DIFF_HEADS):
        sl = slice(h * DIFF_HEAD_DIM, (h + 1) * DIFF_HEAD_DIM)
        qh = x_ref[:, DIFF_WIDTH:2 * DIFF_WIDTH][:, sl] * (DIFF_HALF ** -0.5)
        kh = x_ref[:, 2 * DIFF_WIDTH:3 * DIFF_WIDTH][:, sl]
        vh = x_ref[:, 3 * DIFF_WIDTH:4 * DIFF_WIDTH][:, sl]
        rows_q += [jnp.where(lane < DIFF_HALF, qh, 0.0), jnp.where(lane >= DIFF_HALF, qh, 0.0)]
        rows_k += [kh, kh]
        rows_v += [vh, vh]
    q8 = jnp.concatenate(rows_q, axis=0)
    k8 = jnp.concatenate(rows_k, axis=0)
    v8 = jnp.concatenate(rows_v, axis=0)
    q8b = q8.astype(BF16)
    s = jnp.concatenate(
        [lax.dot_general(q8b, k_refs[p][...].astype(BF16), _NT, preferred_element_type=F32)
         for p in range(n_pages)], axis=1) + bd_ref[...]
    s_new = jnp.sum(q8 * k8, axis=-1, keepdims=True) + bn_ref[:, 0:1]
    m = jnp.maximum(jnp.max(s, axis=-1, keepdims=True), s_new)
    p = jnp.exp(s - m)
    p_new = jnp.exp(s_new - m)
    inv_l = 1.0 / (jnp.sum(p, axis=-1, keepdims=True) + p_new)
    a = p * inv_l
    r = (p_new * inv_l) * v8
    for pg in range(n_pages):
        r = r + jnp.dot(a[:, pg * page_rows:(pg + 1) * page_rows].astype(BF16),
                        v_refs[pg][...].astype(BF16), preferred_element_type=F32)
    lam = _lambda(lam_ref, lam_init)
    for h in range(DIFF_HEADS):
        o = r[2 * h:2 * h + 1, :] - lam * r[2 * h + 1:2 * h + 2, :]
        o_ref[:, h * DIFF_HEAD_DIM:(h + 1) * DIFF_HEAD_DIM] = (
            _rms(o, sub_ref[...]) * (1.0 - lam_init)).astype(BF16)


def attn_decode(proj, cache_k, cache_v, page_table, layer_j, bias_d, bias_n, lam_params, subln, lam_init):
    nb = proj.shape[0]
    n_pages = page_table.shape[1]
    n_pool, n_even = cache_k.shape[:2]
    page_rows = PAGE_SIZE * DIFF_HEADS
    ck = cache_k.reshape(n_pool, n_even, page_rows, DIFF_HEAD_DIM)
    cv = cache_v.reshape(n_pool, n_even, page_rows, DIFF_HEAD_DIM)

    assert nb >= ATT_DEC_BUFS - 1
    full = lambda shape: pl.BlockSpec(shape, lambda i, pt: (0,) * len(shape))
    grid_spec = pltpu.PrefetchScalarGridSpec(
        num_scalar_prefetch=1,
        grid=(nb,),
        in_specs=[pl.BlockSpec((None, 1, EVEN_IN), lambda i, pt: (i, 0, 0)),
                  pl.BlockSpec(memory_space=pl.ANY), pl.BlockSpec(memory_space=pl.ANY),
                  full(bias_d.shape), full(bias_n.shape), full((4, DIFF_HALF)), full((1, LANES))],
        out_specs=pl.BlockSpec((None, 1, DIFF_WIDTH), lambda i, pt: (i, 0, 0)),
        scratch_shapes=[
            pltpu.VMEM((ATT_DEC_BUFS, n_pages, page_rows, DIFF_HEAD_DIM), F32),
            pltpu.VMEM((ATT_DEC_BUFS, n_pages, page_rows, DIFF_HEAD_DIM), F32),
            pltpu.SemaphoreType.DMA((ATT_DEC_BUFS, 2)),
        ],
    )
    out = pl.pallas_call(
        functools.partial(_attn_decode_body, n_pages=n_pages, layer_j=layer_j, lam_init=lam_init),
        grid_spec=grid_spec,
        out_shape=jax.ShapeDtypeStruct((nb, 1, DIFF_WIDTH), BF16),
        compiler_params=_cparams(("arbitrary",)),
        name="attn_decode",
    )(page_table.reshape(-1), proj.reshape(nb, 1, EVEN_IN), ck, cv,
      bias_d, bias_n, lam_params, subln.reshape(1, LANES))
    return out.reshape(nb, DIFF_WIDTH)


def _blockdiag(x):
    lane = lax.broadcasted_iota(jnp.int32, x.shape, 1)
    zero = jnp.zeros_like(x)
    return jnp.concatenate([jnp.where(lane < GDN_C, x, zero), jnp.where(lane >= GDN_C, x, zero)], axis=0)


def _mmp(a, b):
    return jnp.dot(a.astype(BF16), _blockdiag(b.astype(BF16)), preferred_element_type=F32)


def _unit_lower_inverse(lows, ii, jl):
    eye = jnp.where(ii == jl, 1.0, 0.0)
    in_block = (ii // INV_BLOCK) == (jl // INV_BLOCK)
    ps = [jnp.where(in_block, -low, 0.0) for low in lows]
    offs = [jnp.where(in_block, 0.0, low) for low in lows]
    dinvs = [eye + p for p in ps]
    span = 2
    while span < INV_BLOCK:
        ps = [_mmp(p, p) for p in ps]
        dinvs = [d + _mmp(d, p) for d, p in zip(dinvs, ps)]
        span *= 2
    powers = [[-_mmp(d, off) for d, off in zip(dinvs, offs)]]
    span = 2
    while span < GDN_C // INV_BLOCK:
        powers.append([_mmp(n, n) for n in powers[-1]])
        span *= 2
    xs = dinvs
    for pw in reversed(powers):
        xs = [x + _mmp(n, x) for n, x in zip(pw, xs)]
    return xs


def _gdn_prompt_body(h_ref, gn_ref, w_ref, wgt_ref, cw_ref, alog_ref, dtb_ref, on_ref,
                     o_ref, s_ref, cst_ref, ext_ref, z_ref):
    c = pl.program_id(1)
    nh = GDN_HEADS
    npair = nh // 2
    hk = nh * GDN_DK
    cc = GDN_C
    rows_step = GDN_STEP_CHUNKS * cc
    heads_per_tile = GDN_PROJ_TN // LANES

    @pl.when(c == 0)
    def _():
        s_ref[...] = jnp.zeros_like(s_ref)
        ext_ref[0:8, :] = jnp.zeros((8, GDN_QKV), F32)

    @pl.when(c > 0)
    def _():
        ext_ref[0:8, :] = ext_ref[rows_step:rows_step + 8, :]

    xn = _rms(h_ref[...], gn_ref[...]).astype(BF16)
    abt = lax.dot_general(wgt_ref[...], xn, _NT, preferred_element_type=F32)

    def project(n):
        cols = slice(n * GDN_PROJ_TN, (n + 1) * GDN_PROJ_TN)
        tile = jnp.dot(xn, w_ref[:, cols], preferred_element_type=F32)
        if n < GDN_QKV // GDN_PROJ_TN:
            ext_ref[8:8 + rows_step, cols] = tile
        else:
            z_ref[:, n * GDN_PROJ_TN - GDN_QKV:(n + 1) * GDN_PROJ_TN - GDN_QKV] = tile

    def conv_act(col, r0):
        sl = slice(col, col + LANES)
        acc = ext_ref[8 + r0:8 + r0 + cc, sl] * cw_ref[3:4, sl]
        for i in range(GDN_CONV - 1):
            acc = acc + ext_ref[5 + i + r0:5 + i + r0 + cc, sl] * cw_ref[i:i + 1, sl]
        return _silu(acc)

    def l2n(x):
        return x * lax.rsqrt(jnp.sum(x * x, axis=-1, keepdims=True) + EPS)

    def pair(xs):
        return [jnp.concatenate([xs[2 * p], xs[2 * p + 1]], axis=1) for p in range(npair)]

    def row_pair(x):
        return pair([x[h:h + 1, :] for h in range(nh)])

    project(0)
    lane = lax.broadcasted_iota(jnp.int32, (nh, cc), 1)
    q_p, k_p, v_p, gc_b, beta_b, kdec_b, gc_rows, s_decay = [], [], [], [], [], [], [], []
    for ck in range(GDN_STEP_CHUNKS):
        r0 = ck * cc
        g = -jnp.exp(alog_ref[...]) * _softplus(abt[0:nh, r0:r0 + cc] + dtb_ref[...])
        beta = _sigmoid(abt[nh:2 * nh, r0:r0 + cc])
        gc = g
        shift = 1
        while shift < cc:
            gc = gc + jnp.where(lane >= shift, pltpu.roll(gc, shift, 1), 0.0)
            shift *= 2
        g_last = jnp.broadcast_to(gc[:, cc - 1:cc], (nh, cc))
        s_decay.append(jnp.exp(g_last))
        cols = jnp.transpose(jnp.concatenate([gc, beta, jnp.zeros((LANES - 2 * nh, cc), F32)], axis=0))
        bcast = [jnp.broadcast_to(cols[:, n:n + 1], (cc, cc)) for n in range(2 * nh)]
        gcb = pair(bcast[0:nh])
        gc_b += gcb
        beta_b += pair(bcast[nh:2 * nh])
        kdec_b += [jnp.exp(gl - x) for gl, x in zip(row_pair(g_last), gcb)]
        gc_rows += row_pair(gc)
    gam_b = [jnp.exp(x) for x in gc_b]
    ii = lax.broadcasted_iota(jnp.int32, (cc, 2 * cc), 0)
    jl = lax.broadcasted_iota(jnp.int32, (cc, 2 * cc), 1) % cc
    decay = [jnp.exp(jnp.where(ii >= jl, gb - gr, -jnp.inf)) for gb, gr in zip(gc_b, gc_rows)]

    act = {}
    for n in range(1, GDN_MAIN // GDN_PROJ_TN + 1):
        if n < GDN_MAIN // GDN_PROJ_TN:
            project(n)
        if n <= GDN_QKV // GDN_PROJ_TN:
            for hh in range(heads_per_tile):
                col = (n - 1) * GDN_PROJ_TN + hh * LANES
                for ck in range(GDN_STEP_CHUNKS):
                    a = conv_act(col, ck * cc)
                    if col < hk:
                        a = l2n(a) * (GDN_DK ** -0.5)
                    elif col < 2 * hk:
                        a = l2n(a)
                    act[(col, ck)] = a
    for ck in range(GDN_STEP_CHUNKS):
        q_p += pair([act[(h * GDN_DK, ck)] for h in range(nh)])
        k_p += pair([act[(hk + h * GDN_DK, ck)] for h in range(nh)])
        v_p += pair([act[(2 * hk + h * GDN_DV, ck)] for h in range(nh)])

    gram = [lax.dot_general(jnp.concatenate([q, k], axis=0).astype(BF16), _blockdiag(k.astype(BF16)), _NT,
                            preferred_element_type=F32) for q, k in zip(q_p, k_p)]
    qk = [gm[0:cc, :] * d for gm, d in zip(gram, decay)]
    lows = [jnp.where(ii > jl, b * gm[cc:2 * cc, :] * d, 0.0) for b, gm, d in zip(beta_b, gram, decay)]
    tinv = _unit_lower_inverse(lows, ii, jl)
    w =[_mmp(t, b * gm * k) for t, b, gm, k in zip(tinv, beta_b, gam_b, k_p)]
    u0 = [_mmp(t, b * v) for t, b, v in zip(tinv, beta_b, v_p)]
    wq = [jnp.concatenate([wp, gm * q], axis=0).astype(BF16) for wp, gm, q in zip(w, gam_b, q_p)]
    kd = [(kdb * k).astype(BF16) for kdb, k in zip(kdec_b, k_p)]

    state = [s_ref[h] for h in range(nh)]
    zero = jnp.zeros((GDN_DK, GDN_DV), BF16)
    for ck in range(GDN_STEP_CHUNKS):
        r0 = ck * cc
        ent = range(ck * npair, (ck + 1) * npair)
        s_bd = [jnp.concatenate([jnp.concatenate([state[2 * p].astype(BF16), zero], axis=1),
                                 jnp.concatenate([zero, state[2 * p + 1].astype(BF16)], axis=1)], axis=0)
                for p in range(npair)]
        ws_qs = [jnp.dot(wq[e], sb, preferred_element_type=F32) for e, sb in zip(ent, s_bd)]
        u = [u0[e] - x[0:cc, :] for e, x in zip(ent, ws_qs)]
        o = [x[cc:2 * cc, :] + _mmp(qk[e], b) for e, x, b in zip(ent, ws_qs, u)]
        upd = [lax.dot_general(kd[e], b.astype(BF16), _TN, preferred_element_type=F32)
               for e, b in zip(ent, u)]
        new_state = []
        for h in range(nh):
            half = slice((h % 2) * cc, (h % 2 + 1) * cc)
            new_state.append(s_decay[ck][h:h + 1, :] * state[h] + upd[h // 2][half, half])
            z = z_ref[r0:r0 + cc, h * GDN_DV:(h + 1) * GDN_DV]
            o_ref[r0:r0 + cc, h * GDN_DV:(h + 1) * GDN_DV] = (
                _rms(o[h // 2][:, half], on_ref[...]) * _silu(z)).astype(BF16)
        state = new_state
    for h in range(nh):
        s_ref[h] = state[h]

    @pl.when(c == pl.num_programs(1) - 1)
    def _():
        cst_ref[...] = ext_ref[rows_step:rows_step + 8, :]


def gdn_prompt(h, gain, w_in, w_gates, idx, conv_w, a_log, dt_bias, o_norm):
    b, t, d = h.shape
    rows = GDN_STEP_CHUNKS * GDN_C
    nc = t // rows
    nh = GDN_HEADS
    return pl.pallas_call(
        _gdn_prompt_body,
        grid=(b, nc),
        in_specs=[
            pl.BlockSpec((None, rows, d), lambda i, c: (i, c, 0)),
            pl.BlockSpec((1, d), lambda i, c: (0, 0)),
            pl.BlockSpec((None, d, GDN_MAIN), lambda i, c: (idx, 0, 0), pipeline_mode=pl.Buffered(1)),
            pl.BlockSpec((None, LANES, d), lambda i, c: (idx, 0, 0), pipeline_mode=pl.Buffered(1)),
            pl.BlockSpec((GDN_CONV, GDN_QKV), lambda i, c: (0, 0)),
            pl.BlockSpec((nh, 1), lambda i, c: (0, 0)),
            pl.BlockSpec((nh, 1), lambda i, c: (0, 0)),
            pl.BlockSpec((1, GDN_DV), lambda i, c: (0, 0)),
        ],
        out_specs=[
            pl.BlockSpec((None, rows, nh * GDN_DV), lambda i, c: (i, c, 0)),
            pl.BlockSpec((None, nh, GDN_DK, GDN_DV), lambda i, c: (i, 0, 0, 0)),
            pl.BlockSpec((None, 8, GDN_QKV), lambda i, c: (i, 0, 0)),
        ],
        out_shape=[
            jax.ShapeDtypeStruct((b, t, nh * GDN_DV), BF16),
            jax.ShapeDtypeStruct((b, nh, GDN_DK, GDN_DV), F32),
            jax.ShapeDtypeStruct((b, 8, GDN_QKV), F32),
        ],
        scratch_shapes=[pltpu.VMEM((8 + rows, GDN_QKV), F32), pltpu.VMEM((rows, nh * GDN_DV), F32)],
        compiler_params=_cparams(("parallel", "arbitrary")),
        name="gdn_prompt",
    )(h, gain.reshape(1, d), w_in, w_gates, conv_w, a_log.reshape(nh, 1), dt_bias.reshape(nh, 1),
      o_norm.reshape(1, GDN_DV))


def _gdn_decode_body(x_ref, cs_ref, ab_ref, cw_ref, alog_ref, dtb_ref, on_ref, s0_ref, *rest, first, layer_j):
    o_ref, s_ref = rest[-2:]
    nh = GDN_HEADS
    pad = jnp.zeros((8 - 2, GDN_DK), F32)
    if first:
        for slot in range(s_ref.shape[0]):
            if slot != layer_j:
                s_ref[slot] = jnp.zeros(s_ref.shape[1:], F32)
    pad7 = jnp.zeros((8 - 1, GDN_DK), F32)
    seqs = range(GDN_DEC_ROWS)
    q8, k8, v8, gam8, beta8, qk8 = [], [], [], [], [], []
    for r in seqs:
        x = x_ref[r, 0:3 * nh, :]
        conv = x * cw_ref[GDN_CONV - 1]
        for i in range(GDN_CONV - 1):
            conv = conv + cs_ref[r, i] * cw_ref[i]
        act = conv * _sigmoid(conv)
        qa, ka = act[0:nh], act[nh:2 * nh]
        q8.append(qa * lax.rsqrt(jnp.sum(qa * qa, axis=-1, keepdims=True) + EPS) * (GDN_DK ** -0.5))
        k8.append(ka * lax.rsqrt(jnp.sum(ka * ka, axis=-1, keepdims=True) + EPS))
        v8.append(act[2 * nh:3 * nh])
        g = -jnp.exp(alog_ref[...]) * _softplus(ab_ref[r, 0:nh, :] + dtb_ref[...])
        gam8.append(jnp.broadcast_to(jnp.exp(g), (nh, GDN_DV)))
        beta8.append(jnp.broadcast_to(_sigmoid(ab_ref[r, nh:2 * nh, :]), (nh, GDN_DV)))
        qk8.append(jnp.broadcast_to(jnp.sum(q8[r] * k8[r], axis=-1, keepdims=True), (nh, GDN_DV)))
    pairs = [(r, h) for r in seqs for h in range(nh)]
    s_old = [s0_ref[r, h] for r, h in pairs]
    qk_s = [jnp.dot(jnp.concatenate([q8[r][h:h + 1, :], k8[r][h:h + 1, :], pad], axis=0).astype(BF16),
                    s.astype(BF16), preferred_element_type=F32) for (r, h), s in zip(pairs, s_old)]
    u = [beta8[r][h:h + 1, :] * (v8[r][h:h + 1, :] - gam8[r][h:h + 1, :] * x[1:2, :])
         for (r, h), x in zip(pairs, qk_s)]
    outs = [gam8[r][h:h + 1, :] * x[0:1, :] + qk8[r][h:h + 1, :] * b for (r, h), x, b in zip(pairs, qk_s, u)]
    outer = [lax.dot_general(jnp.concatenate([k8[r][h:h + 1, :], pad7], axis=0).astype(BF16),
                             jnp.concatenate([b, pad7], axis=0).astype(BF16), _TN, preferred_element_type=F32)
             for (r, h), b in zip(pairs, u)]
    for (r, h), s, x in zip(pairs, s_old, outer):
        new = gam8[r][h:h + 1, :] * s + x
        if first:
            s_ref[layer_j, r, h] = new
        else:
            s_ref[r, h] = new
    for r in seqs:
        o8 = jnp.concatenate(outs[r * nh:(r + 1) * nh], axis=0)
        z8 = x_ref[r, 3 * nh:4 * nh, :]
        o_ref[r] = (_rms(o8, on_ref[...]) * (z8 * _sigmoid(z8))).astype(BF16)


def gdn_decode(proj, gates, conv_state, s0_all, layer_j, s_new_all, conv_w, a_log, dt_bias, o_norm):
    nb = proj.shape[0]
    nh = GDN_HEADS
    nrow = GDN_QKV // LANES
    first = s_new_all is None
    in_specs = [
        pl.BlockSpec((GDN_DEC_ROWS, GDN_MAIN // LANES, LANES), lambda i: (i, 0, 0)),
        pl.BlockSpec((GDN_DEC_ROWS, GDN_CONV - 1, nrow, LANES), lambda i: (i, 0, 0, 0)),
        pl.BlockSpec((GDN_DEC_ROWS, 2 * nh, 1), lambda i: (i, 0, 0)),
        pl.BlockSpec((GDN_CONV, nrow, LANES), lambda i: (0, 0, 0)),
        pl.BlockSpec((nh, 1), lambda i: (0, 0)),
        pl.BlockSpec((nh, 1), lambda i: (0, 0)),
        pl.BlockSpec((1, GDN_DV), lambda i: (0, 0)),
        pl.BlockSpec((None, GDN_DEC_ROWS, nh, GDN_DK, GDN_DV), lambda i: (layer_j, i, 0, 0, 0)),
    ]
    args = [proj.reshape(nb, GDN_MAIN // LANES, LANES),
            conv_state.reshape(nb, GDN_CONV - 1, nrow, LANES),
            gates.reshape(nb, 2 * nh, 1),
            conv_w.reshape(GDN_CONV, nrow, LANES),
            a_log.reshape(nh, 1), dt_bias.reshape(nh, 1), o_norm.reshape(1, GDN_DV), s0_all]
    if first:
        s_spec = pl.BlockSpec((s0_all.shape[0], GDN_DEC_ROWS, nh, GDN_DK, GDN_DV), lambda i: (0, i, 0, 0, 0))
        aliases = {}
    else:
        in_specs.append(pl.BlockSpec(memory_space=pl.ANY))
        args.append(s_new_all)
        s_spec = pl.BlockSpec((None, GDN_DEC_ROWS, nh, GDN_DK, GDN_DV), lambda i: (layer_j, i, 0, 0, 0))
        aliases = {len(args) - 1: 1}
    o, s = pl.pallas_call(
        functools.partial(_gdn_decode_body, first=first, layer_j=layer_j),
        grid=(nb // GDN_DEC_ROWS,),
        in_specs=in_specs,
        out_specs=[pl.BlockSpec((GDN_DEC_ROWS, nh, GDN_DV), lambda i: (i, 0, 0)), s_spec],
        out_shape=[
            jax.ShapeDtypeStruct((nb, nh, GDN_DV), BF16),
            jax.ShapeDtypeStruct(s0_all.shape, F32),
        ],
        input_output_aliases=aliases,
        compiler_params=_cparams(("parallel",)),
        name="gdn_decode",
    )(*args)
    return o.reshape(nb, nh * GDN_DV), s


def kernel(x_prompt, x_sample, cache_k, cache_v, page_table, state_pool, state_conv, state_delta,
           norm_mix, norm_ffn, norm_final, rel_bias,
           w_in_even, pool_w, pool_scale, lambda_q1, lambda_k1, lambda_q2, lambda_k2, subln_w, w_out_even,
           w_in_odd, conv_w, a_log, dt_bias, o_norm, w_out_odd,
           w_gate_up, w_down):
    bp, t, d = x_prompt.shape
    bs = x_sample.shape[0]
    mp = bp * t
    nh = GDN_HEADS
    past = page_table.shape[1] * PAGE_SIZE
    tm_s = bs

    hp = x_prompt.reshape(mp, d)
    hs = x_sample.reshape(bs, d)
    bias_p, bias_d, bias_n = rel_bias_tiles(rel_bias, past)

    w_in_e = w_in_even.astype(BF16)
    w_in_o = w_in_odd.astype(BF16)
    w_gate_cols = lax.optimization_barrier(w_in_odd[:, :, GDN_MAIN:])
    w_gates = jnp.zeros((w_in_odd.shape[0], LANES, d), F32).at[:, 0:2 * nh, :].set(
        jnp.transpose(w_gate_cols, (0, 2, 1))).astype(BF16)
    w_out_e = w_out_even.astype(BF16)
    w_out_o = w_out_odd.astype(BF16)
    w_gu = w_gate_up.astype(BF16)
    w_d = w_down.astype(BF16)
    pool_wb = pool_w.astype(BF16)

    k_s, v_s, pool_p, pool_s = [], [], [], []
    conv_p, conv_s, delta_p = [], [], []
    kv_p = None
    n_even = w_in_even.shape[0]
    delta_s = None
    for layer in range(DEPTH):
        j = layer // 2
        last = layer == DEPTH - 1
        if layer % 2 == 0:
            w_out = w_out_e
            pw = pool_wb[j]
            lam_init = 0.8 - 0.6 * math.exp(-0.3 * layer)
            lam_params = jnp.stack([lambda_q1[j], lambda_k1[j], lambda_q2[j], lambda_k2[j]])

            qkv_p, ypool_p, tail_p, kv_p = even_proj_prompt(hp, norm_mix[layer], w_in_e, j, pw, pool_scale[j],
                                                            t, n_even, kv_p, tm=EVEN_TM)
            proj_s, ks_, vs_ = norm_matmul(hs, norm_mix[layer], w_in_e, j, tm=tm_s, tn=DIFF_WIDTH)

            oatt_p = attn_prompt(qkv_p.reshape(bp, t, 3 * DIFF_WIDTH), bias_p, lam_params, subln_w[j], lam_init)
            ypool_s = pool_decode(jnp.transpose(state_pool[j], (1, 0, 2)), proj_s, pw, pool_scale[j])
            oatt_s = attn_decode(proj_s, cache_k, cache_v, page_table, j, bias_d, bias_n,
                                 lam_params, subln_w[j], lam_init)

            kv_shape = (DIFF_HEADS, DIFF_HEAD_DIM)
            k_s.append(ks_.reshape(bs, 1, *kv_shape))
            v_s.append(vs_.reshape(bs, 1, *kv_shape))
            pool_p.append(tail_p[:, POOL_PAD - POOL_BUF:, :])
            pool_s.append(jnp.concatenate([state_pool[j][:, 1:], proj_s[:, None, 0:POOL_WIDTH]], axis=1))

            mix_p = ((ypool_p, 0), (oatt_p.reshape(mp, DIFF_WIDTH), 0))
            mix_s = ((ypool_s, 0), (oatt_s, 0))
        else:
            w_out = w_out_o

            proj_s, gates_s = norm_matmul(hs, norm_mix[layer], w_in_o, j, w_gates, tm=tm_s, tn=512)

            o_p, s_p, tail_p = gdn_prompt(hp.reshape(bp, t, d), norm_mix[layer], w_in_o, w_gates, j,
                                          conv_w[j], a_log[j], dt_bias[j], o_norm[j])
            o_s, delta_s = gdn_decode(proj_s, gates_s[0:2 * nh].T, state_conv[j], state_delta, j, delta_s,
                                      conv_w[j], a_log[j], dt_bias[j], o_norm[j])

            conv_p.append(tail_p[:, 8 - (GDN_CONV - 1):, :])
            conv_s.append(jnp.concatenate([state_conv[j][:, 1:], proj_s[:, None, 0:GDN_QKV]], axis=1))
            delta_p.append(s_p)

            o_p = o_p.reshape(mp, nh * GDN_DV)
            mix_p = ((o_p, 0), (o_p, 1))
            mix_s = ((o_s, 0), (o_s, 1))

        g_fin = norm_final if last else None
        hp = outproj_ffn(hp, mix_p[0], mix_p[1], w_out, j, norm_ffn[layer], w_gu, w_d, layer, g_fin, tm=FFN_TM)
        hs = outproj_ffn(hs, mix_s[0], mix_s[1], w_out, j, norm_ffn[layer], w_gu, w_d, layer, g_fin, tm=tm_s)

    return (hp.reshape(bp, t, d), hs.reshape(bs, 1, d),
            kv_p[0].reshape(bp, n_even, t, DIFF_HEADS, DIFF_HEAD_DIM),
            kv_p[1].reshape(bp, n_even, t, DIFF_HEADS, DIFF_HEAD_DIM),
            jnp.stack(k_s, axis=1), jnp.stack(v_s, axis=1),
            jnp.stack(pool_p), jnp.stack(pool_s), jnp.stack(conv_p), jnp.stack(conv_s),
            jnp.stack(delta_p), delta_s)
```

```python
import functools
import math

import jax
import jax.numpy as jnp
from jax import lax
from jax.experimental import pallas as pl
from jax.experimental.pallas import tpu as pltpu

F32 = jnp.float32
BF16 = jnp.bfloat16

D_MODEL = 1024
DEPTH = 4
PAGE_SIZE = 128
POOL_WIDTH = 512
POOL_WINDOWS = (2, 4, 8, 16)
POOL_GROUP = 128
POOL_BUF = 15
DIFF_HEADS = 4
DIFF_HALF = 64
DIFF_HEAD_DIM = 128
DIFF_WIDTH = 512
EVEN_IN = 2048
REL_BUCKETS = 32
REL_MAX_DIST = 128
GDN_HEADS = 8
GDN_DK = 128
GDN_DV = 128
GDN_CONV = 4
GDN_QKV = 3072
GDN_MAIN = 4096
D_FF = 2816
EPS = 1e-6

LANES = 128
VMEM_LIMIT = 48 * 1024 * 1024
NEG_BIG = -1e30

ATT_TQ = 512
ATT_TK = 512
ATT_HEADS = 1
GDN_C = 128
GDN_STEP_CHUNKS = 2
GDN_DEC_ROWS = 8
ATT_DEC_BUFS = 3
GDN_PROJ_TN = 512
INV_BLOCK = 16
FFN_TF = 256
FFN_TM = 512
EVEN_TM = 1024

_NT = (((1,), (1,)), ((), ()))
_TN = (((0,), (0,)), ((), ()))


def _cparams(sem):
    return pltpu.CompilerParams(dimension_semantics=sem, vmem_limit_bytes=VMEM_LIMIT)


def _sigmoid(x):
    return 1.0 / (1.0 + jnp.exp(-x))


def _silu(x):
    h = 0.5 * x
    return h + h * jnp.tanh(h)


def _softplus(x):
    return jnp.maximum(x, 0.0) + jnp.log1p(jnp.exp(-jnp.abs(x)))


def _rms(x, gain):
    return x * lax.rsqrt(jnp.mean(x * x, axis=-1, keepdims=True) + EPS) * gain


def _norm_mm_kv_body(x_ref, g_ref, w_ref, o_ref, k_ref, v_ref, xn_ref):
    j = pl.program_id(1)

    @pl.when(j == 0)
    def _():
        xn_ref[...] = _rms(x_ref[...], g_ref[...]).astype(BF16)

    res = jnp.dot(xn_ref[...], w_ref[...], preferred_element_type=F32)
    o_ref[...] = res

    def head_rows(dst_ref):
        for h in range(DIFF_HEADS):
            dst_ref[pl.ds(h, res.shape[0], stride=DIFF_HEADS), :] = res[:, h * DIFF_HEAD_DIM:(h + 1) * DIFF_HEAD_DIM]

    @pl.when(j == 2)
    def _():
        head_rows(k_ref)

    @pl.when(j == 3)
    def _():
        head_rows(v_ref)


def _norm_mm_gate_body(x_ref, g_ref, w_ref, wst_ref, o_ref, ost_ref, xn_ref):
    @pl.when(pl.program_id(1) == 0)
    def _():
        xn = _rms(x_ref[...], g_ref[...]).astype(BF16)
        xn_ref[...] = xn
        ost_ref[...] = lax.dot_general(wst_ref[...], xn, _NT, preferred_element_type=F32)

    o_ref[...] = jnp.dot(xn_ref[...], w_ref[...], preferred_element_type=F32)


def norm_matmul(x, gain, w, idx, w_gates=None, *, tm, tn):
    m, d = x.shape
    n = EVEN_IN if w_gates is None else GDN_MAIN
    grid = (m // tm, n // tn)
    in_specs = [
        pl.BlockSpec((tm, d), lambda i, j: (i, 0)),
        pl.BlockSpec((1, d), lambda i, j: (0, 0)),
        pl.BlockSpec((None, d, tn), lambda i, j: (idx, 0, j)),
    ]
    out_specs = pl.BlockSpec((tm, tn), lambda i, j: (i, j))
    out_shape = jax.ShapeDtypeStruct((m, n), F32)
    args = [x, gain.reshape(1, d), w]
    if w_gates is None:
        assert tn == DIFF_WIDTH and n == EVEN_IN
        kv_spec = pl.BlockSpec((tm * DIFF_HEADS, DIFF_HEAD_DIM), lambda i, j: (i, 0))
        kv_shape = jax.ShapeDtypeStruct((m * DIFF_HEADS, DIFF_HEAD_DIM), F32)
        out_specs = [out_specs, kv_spec, kv_spec]
        out_shape = [out_shape, kv_shape, kv_shape]
        body = _norm_mm_kv_body
    else:
        in_specs.append(pl.BlockSpec((None, LANES, d), lambda i, j: (idx, 0, 0)))
        out_specs = [out_specs, pl.BlockSpec((LANES, tm), lambda i, j: (0, i))]
        out_shape = [out_shape, jax.ShapeDtypeStruct((LANES, m), F32)]
        args.append(w_gates)
        body = _norm_mm_gate_body
    return pl.pallas_call(
        body,
        grid=grid,
        in_specs=in_specs,
        out_specs=out_specs,
        out_shape=out_shape,
        scratch_shapes=[pltpu.VMEM((tm, d), BF16)],
        compiler_params=_cparams(("parallel", "arbitrary")),
        name="norm_matmul",
    )(*args)


def _ffn_body(*refs, final):
    if final:
        res_ref, a0_ref, a1_ref, wo_ref, gf_ref, wgu_ref, wd_ref, gfin_ref, o_ref = refs
    else:
        res_ref, a0_ref, a1_ref, wo_ref, gf_ref, wgu_ref, wd_ref, o_ref = refs
    mixed = jnp.concatenate([a0_ref[...], a1_ref[...]], axis=1)
    h1 = res_ref[...] + jnp.dot(mixed, wo_ref[...], preferred_element_type=F32)
    xn = _rms(h1, gf_ref[...]).astype(BF16)
    acts = []
    for f in range(D_FF // FFN_TF):
        g = jnp.dot(xn, wgu_ref[:, f * FFN_TF:(f + 1) * FFN_TF], preferred_element_type=F32)
        u = jnp.dot(xn, wgu_ref[:, D_FF + f * FFN_TF:D_FF + (f + 1) * FFN_TF], preferred_element_type=F32)
        acts.append((g * _sigmoid(g) * u).astype(BF16))
    y = h1 + jnp.dot(jnp.concatenate(acts, axis=1), wd_ref[...], preferred_element_type=F32)
    if final:
        y = _rms(y, gfin_ref[...])
    o_ref[...] = y


def outproj_ffn(res, mix0, mix1, w_out, idx_out, g_ffn, w_gu, w_d, layer, g_final=None, *, tm):
    m, d = res.shape
    half = w_out.shape[1] // 2
    (m0, c0), (m1, c1) = mix0, mix1
    final = g_final is not None
    once = pl.Buffered(1)
    in_specs = [
        pl.BlockSpec((tm, d), lambda i: (i, 0)),
        pl.BlockSpec((tm, half), lambda i: (i, c0)),
        pl.BlockSpec((tm, half), lambda i: (i, c1)),
        pl.BlockSpec((None, 2 * half, d), lambda i: (idx_out, 0, 0), pipeline_mode=once),
        pl.BlockSpec((1, d), lambda i: (0, 0)),
        pl.BlockSpec((None, d, 2 * D_FF), lambda i: (layer, 0, 0), pipeline_mode=once),
        pl.BlockSpec((None, D_FF, d), lambda i: (layer, 0, 0), pipeline_mode=once),
    ]
    args = [res, m0, m1, w_out, g_ffn.reshape(1, d), w_gu, w_d]
    if final:
        in_specs.append(pl.BlockSpec((1, d), lambda i: (0, 0)))
        args.append(g_final.reshape(1, d))
    return pl.pallas_call(
        functools.partial(_ffn_body, final=final),
        grid=(m // tm,),
        in_specs=in_specs,
        out_specs=pl.BlockSpec((tm, d), lambda i: (i, 0)),
        out_shape=jax.ShapeDtypeStruct((m, d), F32),
        compiler_params=_cparams(("parallel",)),
        name="outproj_ffn",
    )(*args)


POOL_PAD = 16
POOL_ROWS = 512


def _even_proj_body(x_ref, g_ref, w_ref, pw_ref, ps_ref, *rest, blocks_per_seq, first, layer_j):
    qkv_ref, yp_ref, k_ref, v_ref, tail_ref, ext_ref = rest[-6:]
    i = pl.program_id(0)
    tm = x_ref.shape[0]
    tn = DIFF_WIDTH
    blk = i % blocks_per_seq

    @pl.when(blk == 0)
    def _():
        ext_ref[0:POOL_PAD, :] = jnp.zeros((POOL_PAD, POOL_WIDTH), F32)

    @pl.when(blk > 0)
    def _():
        ext_ref[0:POOL_PAD, :] = ext_ref[tm:tm + POOL_PAD, :]

    xn = _rms(x_ref[...], g_ref[...]).astype(BF16)

    def project(n):
        return jnp.dot(xn, w_ref[:, n * tn:(n + 1) * tn], preferred_element_type=F32)

    def head_rows(dst_ref, res):
        if first:
            for slot in range(dst_ref.shape[0]):
                if slot != layer_j:
                    dst_ref[slot] = jnp.zeros(dst_ref.shape[1:], F32)
            dst_ref = dst_ref.at[layer_j]
        for h in range(DIFF_HEADS):
            dst_ref[pl.ds(h, tm, stride=DIFF_HEADS), :] = res[:, h * DIFF_HEAD_DIM:(h + 1) * DIFF_HEAD_DIM]

    u = project(0)
    ext_ref[POOL_PAD:, :] = u
    tail_ref[...] = u[tm - POOL_PAD:tm, :]

    for n in range(1, EVEN_IN // tn):
        res = project(n)
        qkv_ref[:, (n - 1) * tn:n * tn] = res.astype(BF16)
        if n == 2:
            head_rows(k_ref, res)
        elif n == 3:
            head_rows(v_ref, res)

    for r0 in range(0, tm, POOL_ROWS):
        pos = lax.broadcasted_iota(jnp.int32, (POOL_ROWS, 1), 0) + (blk * tm + r0)
        for g, w in enumerate(POOL_WINDOWS):
            sl = slice(g * POOL_GROUP, (g + 1) * POOL_GROUP)
            cur = ext_ref[POOL_PAD + r0:POOL_PAD + r0 + POOL_ROWS, sl]
            win = cur
            for s in range(1, w):
                win = win + ext_ref[POOL_PAD + r0 - s:POOL_PAD + r0 - s + POOL_ROWS, sl]
            cnt = jnp.minimum(pos + 1, w).astype(F32)
            dlt = win / cnt - cur
            y = jnp.dot(dlt.astype(BF16), pw_ref[g], preferred_element_type=F32) * ps_ref[:, sl]
            yp_ref[r0:r0 + POOL_ROWS, sl] = y.astype(BF16)


def even_proj_prompt(x, gain, w, idx, pool_w, pool_scale, seq_len, n_layers, kv_all, *, tm):
    m, d = x.shape
    blocks_per_seq = seq_len // tm
    nseq = m // seq_len
    first = kv_all is None
    in_specs = [
        pl.BlockSpec((tm, d), lambda i: (i, 0)),
        pl.BlockSpec((1, d), lambda i: (0, 0)),
        pl.BlockSpec((None, d, EVEN_IN), lambda i: (idx, 0, 0), pipeline_mode=pl.Buffered(1)),
        pl.BlockSpec((len(POOL_WINDOWS), POOL_GROUP, POOL_GROUP), lambda i: (0, 0, 0)),
        pl.BlockSpec((1, POOL_WIDTH), lambda i: (0, 0)),
    ]
    args = [x, gain.reshape(1, d), w, pool_w, pool_scale.reshape(1, POOL_WIDTH)]
    kv_rows = tm * DIFF_HEADS
    if first:
        kv_spec = pl.BlockSpec((None, n_layers, kv_rows, DIFF_HEAD_DIM),
                               lambda i: (i // blocks_per_seq, 0, i % blocks_per_seq, 0))
        aliases = {}
    else:
        in_specs += [pl.BlockSpec(memory_space=pl.ANY), pl.BlockSpec(memory_space=pl.ANY)]
        args += list(kv_all)
        kv_spec = pl.BlockSpec((None, None, kv_rows, DIFF_HEAD_DIM),
                               lambda i: (i // blocks_per_seq, idx, i % blocks_per_seq, 0))
        aliases = {len(args) - 2: 2, len(args) - 1: 3}
    kv_shape = jax.ShapeDtypeStruct((nseq, n_layers, seq_len * DIFF_HEADS, DIFF_HEAD_DIM), F32)
    qkv, ypool, k_all, v_all, tail = pl.pallas_call(
        functools.partial(_even_proj_body, blocks_per_seq=blocks_per_seq, first=first, layer_j=idx),
        grid=(m // tm,),
        in_specs=in_specs,
        out_specs=[
            pl.BlockSpec((tm, 3 * DIFF_WIDTH), lambda i: (i, 0)),
            pl.BlockSpec((tm, POOL_WIDTH), lambda i: (i, 0)),
            kv_spec,
            kv_spec,
            pl.BlockSpec((None, POOL_PAD, POOL_WIDTH), lambda i: (i // blocks_per_seq, 0, 0)),
        ],
        out_shape=[
            jax.ShapeDtypeStruct((m, 3 * DIFF_WIDTH), BF16),
            jax.ShapeDtypeStruct((m, POOL_WIDTH), BF16),
            kv_shape,
            kv_shape,
            jax.ShapeDtypeStruct((nseq, POOL_PAD, POOL_WIDTH), F32),
        ],
        scratch_shapes=[pltpu.VMEM((POOL_PAD + tm, POOL_WIDTH), F32)],
        input_output_aliases=aliases,
        compiler_params=_cparams(("arbitrary",)),
        name="even_proj",
    )(*args)
    return qkv, ypool, tail, (k_all, v_all)


def _pool_decode_body(st_ref, u_ref, pw_ref, ps_ref, o_ref):
    for g, w in enumerate(POOL_WINDOWS):
        sl = slice(g * POOL_GROUP, (g + 1) * POOL_GROUP)
        cur = u_ref[:, sl]
        win = cur
        for i in range(1, w):
            win = win + st_ref[POOL_BUF - i, :, sl]
        dlt = win / float(w) - cur
        y = jnp.dot(dlt.astype(BF16), pw_ref[g], preferred_element_type=F32) * ps_ref[:, sl]
        o_ref[:, sl] = y.astype(BF16)


def pool_decode(state_t, proj, pool_w, pool_scale):
    nb = proj.shape[0]
    return pl.pallas_call(
        _pool_decode_body,
        grid=(1,),
        in_specs=[
            pl.BlockSpec((POOL_BUF, nb, POOL_WIDTH), lambda i: (0, 0, 0)),
            pl.BlockSpec((nb, POOL_WIDTH), lambda i: (0, 0)),
            pl.BlockSpec((len(POOL_WINDOWS), POOL_GROUP, POOL_GROUP), lambda i: (0, 0, 0)),
            pl.BlockSpec((1, POOL_WIDTH), lambda i: (0, 0)),
        ],
        out_specs=pl.BlockSpec((nb, POOL_WIDTH), lambda i: (0, 0)),
        out_shape=jax.ShapeDtypeStruct((nb, POOL_WIDTH), BF16),
        compiler_params=_cparams(("arbitrary",)),
        name="pool_decode",
    )(state_t, proj, pool_w, pool_scale.reshape(1, POOL_WIDTH))


def _rel_bucket(n):
    max_exact = REL_BUCKETS // 2
    nf = jnp.maximum(n, 1).astype(F32)
    large = max_exact + (jnp.log(nf / max_exact) / math.log(REL_MAX_DIST / max_exact)
                         * (REL_BUCKETS - max_exact)).astype(jnp.int32)
    large = jnp.minimum(large, REL_BUCKETS - 1)
    return jnp.where(n < max_exact, n, large)


def _table_lookup(tab_ref, bucket, h):
    out = jnp.zeros(bucket.shape, F32)
    for b in range(REL_BUCKETS):
        out = jnp.where(bucket == b, tab_ref[b, h], out)
    return out


def _rel_bias_body(tab_ref, bp_ref, bd_ref, bn_ref, *, past):
    ii = lax.broadcasted_iota(jnp.int32, (ATT_TQ, ATT_TK), 0)
    jj = lax.broadcasted_iota(jnp.int32, (ATT_TQ, ATT_TK), 1)
    for h in range(DIFF_HEADS):
        for t in range(3):
            dist = t * ATT_TK + ii - jj
            bias = _table_lookup(tab_ref, _rel_bucket(jnp.maximum(dist, 0)), h)
            bp_ref[h, t] = jnp.where(dist >= 0, bias, NEG_BIG)
    nrow = 2 * DIFF_HEADS
    row = lax.broadcasted_iota(jnp.int32, (nrow, past * DIFF_HEADS), 0)
    col = lax.broadcasted_iota(jnp.int32, (nrow, past * DIFF_HEADS), 1)
    bucket = _rel_bucket(past - col // DIFF_HEADS)
    rown = lax.broadcasted_iota(jnp.int32, (nrow, LANES), 0)
    bd = jnp.full((nrow, past * DIFF_HEADS), NEG_BIG, F32)
    bn = jnp.zeros((nrow, LANES), F32)
    for h in range(DIFF_HEADS):
        own = jnp.where(row // 2 == h, col % DIFF_HEADS, -1) == h
        bd = jnp.where(own, _table_lookup(tab_ref, bucket, h), bd)
        bn = jnp.where(rown // 2 == h, tab_ref[0, h], bn)
    bd_ref[...] = bd
    bn_ref[...] = bn


def rel_bias_tiles(rel_bias, past):
    return pl.pallas_call(
        functools.partial(_rel_bias_body, past=past),
        in_specs=[pl.BlockSpec(memory_space=pltpu.SMEM)],
        out_specs=[
            pl.BlockSpec(memory_space=pltpu.VMEM),
            pl.BlockSpec(memory_space=pltpu.VMEM),
            pl.BlockSpec(memory_space=pltpu.VMEM),
        ],
        out_shape=[
            jax.ShapeDtypeStruct((DIFF_HEADS, 3, ATT_TQ, ATT_TK), F32),
            jax.ShapeDtypeStruct((2 * DIFF_HEADS, past * DIFF_HEADS), F32),
            jax.ShapeDtypeStruct((2 * DIFF_HEADS, LANES), F32),
        ],
        compiler_params=pltpu.CompilerParams(vmem_limit_bytes=VMEM_LIMIT),
        name="rel_bias_tiles",
    )(rel_bias)


def _lambda(lam_ref, lam_init):
    lp = lam_ref[...]
    s1 = jnp.sum(lp[0:1, :] * lp[1:2, :], axis=-1, keepdims=True)
    s2 = jnp.sum(lp[2:3, :] * lp[3:4, :], axis=-1, keepdims=True)
    return jnp.exp(s1) - jnp.exp(s2) + lam_init


def _attn_prompt_body(q_ref, k_ref, v_ref, bias_ref, lam_ref, sub_ref, o_ref,
                      vb_ref, qq_ref, m_ref, acc_ref, *, lam_init):
    kb_ref = k_ref
    nq = k_ref.shape[0] // ATT_TQ
    lane = lax.broadcasted_iota(jnp.int32, (ATT_TQ, LANES), 1)
    for hh in range(ATT_HEADS):
        hl = slice(hh * LANES, (hh + 1) * LANES)
        vb_ref[hh, :, 0:LANES] = v_ref[:, hl]
        vb_ref[hh, :, LANES:2 * LANES] = jnp.ones((vb_ref.shape[1], LANES), BF16)
        for qi in range(nq):
            q = q_ref[qi * ATT_TQ:(qi + 1) * ATT_TQ, hl].astype(F32) * (DIFF_HALF ** -0.5)
            qq_ref[hh * nq + qi] = jnp.concatenate([jnp.where(lane < DIFF_HALF, q, 0.0),
                                                    jnp.where(lane >= DIFF_HALF, q, 0.0)], axis=0).astype(BF16)
    half = ATT_TQ // 2
    for w in range(nq):
        pieces = []
        for hh in range(ATT_HEADS):
            for qi in range(w, nq):
                tile = min(qi - w, 2)
                parts = [(0, half, half), (half, half, ATT_TK)] if tile == 0 else [(0, ATT_TQ, ATT_TK)]
                for idx in range(2):
                    for r0, nr, nk in parts:
                        pieces.append((hh, hh * nq + qi, slice(idx * ATT_TQ + r0, idx * ATT_TQ + r0 + nr),
                                       nk, tile, r0))
        s_l = [lax.dot_general(qq_ref[e, rows, :], kb_ref[w * ATT_TK:w * ATT_TK + nk, hh * LANES:(hh + 1) * LANES],
                               _NT, preferred_element_type=F32)
               + bias_ref[hh, tile, r0:r0 + rows.stop - rows.start, 0:nk]
               for hh, e, rows, nk, tile, r0 in pieces]
        if w == 0:
            m_prev = [None] * len(pieces)
            m_new = [jnp.broadcast_to(jnp.max(s, axis=-1, keepdims=True), (s.shape[0], LANES)) for s in s_l]
        else:
            m_prev = [m_ref[e, rows, :] for _, e, rows, _, _, _ in pieces]
            m_new = [jnp.maximum(mp, jnp.max(s, axis=-1, keepdims=True)) for mp, s in zip(m_prev, s_l)]
        p_l = [jnp.exp(s - jnp.concatenate([mn] * (pc[3] // LANES), axis=1)).astype(BF16)
               for s, mn, pc in zip(s_l, m_new, pieces)]
        for (hh, e, rows, nk, _, _), mp, mn, p in zip(pieces, m_prev, m_new, p_l):
            pv = jnp.dot(p, vb_ref[hh, w * ATT_TK:w * ATT_TK + nk, :], preferred_element_type=F32)
            if w == 0:
                acc_ref[e, rows, :] = pv
            else:
                alpha = jnp.exp(mp - mn)
                acc_ref[e, rows, :] = jnp.concatenate([alpha, alpha], axis=1) * acc_ref[e, rows, :] + pv
            m_ref[e, rows, :] = mn

    lam = _lambda(lam_ref, lam_init)
    for hh in range(ATT_HEADS):
        for qi in range(nq):
            acc = acc_ref[hh * nq + qi]
            o_all = acc[:, 0:LANES] / acc[:, LANES:2 * LANES]
            o = o_all[0:ATT_TQ, :] - lam * o_all[ATT_TQ:2 * ATT_TQ, :]
            o_ref[qi * ATT_TQ:(qi + 1) * ATT_TQ, hh * LANES:(hh + 1) * LANES] = (
                _rms(o, sub_ref[...]) * (1.0 - lam_init)).astype(BF16)


def attn_prompt(qkv, bias_p, lam_params, subln, lam_init):
    b, t, _ = qkv.shape
    nh = DIFF_HEADS
    nq = t // ATT_TQ
    assert ATT_TQ == ATT_TK
    hw = ATT_HEADS * LANES
    groups = nh // ATT_HEADS
    return pl.pallas_call(
        functools.partial(_attn_prompt_body, lam_init=lam_init),
        grid=(b, groups),
        in_specs=[
            pl.BlockSpec((None, t, hw), lambda i, h: (i, 0, h)),
            pl.BlockSpec((None, t, hw), lambda i, h: (i, 0, groups + h)),
            pl.BlockSpec((None, t, hw), lambda i, h: (i, 0, 2 * groups + h)),
            pl.BlockSpec((ATT_HEADS, 3, ATT_TQ, ATT_TK), lambda i, h: (h, 0, 0, 0)),
            pl.BlockSpec((4, DIFF_HALF), lambda i, h: (0, 0)),
            pl.BlockSpec((1, LANES), lambda i, h: (0, 0)),
        ],
        out_specs=pl.BlockSpec((None, t, hw), lambda i, h: (i, 0, h)),
        out_shape=jax.ShapeDtypeStruct((b, t, DIFF_WIDTH), BF16),
        scratch_shapes=[
            pltpu.VMEM((ATT_HEADS, t, 2 * LANES), BF16),
            pltpu.VMEM((ATT_HEADS * nq, 2 * ATT_TQ, LANES), BF16),
            pltpu.VMEM((ATT_HEADS * nq, 2 * ATT_TQ, LANES), F32),
            pltpu.VMEM((ATT_HEADS * nq, 2 * ATT_TQ, 2 * LANES), F32),
        ],
        compiler_params=_cparams(("parallel", "parallel")),
        name="attn_prompt",
    )(qkv, qkv, qkv, bias_p, lam_params, subln.reshape(1, LANES))


def _attn_decode_body(pt_ref, x_ref, ck_ref, cv_ref, bd_ref, bn_ref, lam_ref, sub_ref, o_ref,
                      kbuf_ref, vbuf_ref, sem_ref, *, n_pages, layer_j, lam_init):
    i = pl.program_id(0)
    page_rows = PAGE_SIZE * DIFF_HEADS

    def page_copies(b, slot):
        copies = []
        for p in range(n_pages):
            page = pt_ref[b * n_pages + p]
            copies.append(pltpu.make_async_copy(ck_ref.at[page, layer_j], kbuf_ref.at[slot, p], sem_ref.at[slot, 0]))
            copies.append(pltpu.make_async_copy(cv_ref.at[page, layer_j], vbuf_ref.at[slot, p], sem_ref.at[slot, 1]))
        return copies

    @pl.when(i == 0)
    def _():
        for b in range(ATT_DEC_BUFS - 1):
            for cp in page_copies(b, b):
                cp.start()

    ahead = i + (ATT_DEC_BUFS - 1)

    @pl.when(ahead < pl.num_programs(0))
    def _():
        for cp in page_copies(ahead, ahead % ATT_DEC_BUFS):
            cp.start()

    slot = i % ATT_DEC_BUFS
    for cp in page_copies(i, slot):
        cp.wait()
    k_refs = [kbuf_ref.at[slot, p] for p in range(n_pages)]
    v_refs = [vbuf_ref.at[slot, p] for p in range(n_pages)]
    lane = lax.broadcasted_iota(jnp.int32, (1, DIFF_HEAD_DIM), 1)
    rows_q, rows_k, rows_v = [], [], []
    for h in range(DIFF_HEADS):
        sl = slice(h * DIFF_HEAD_DIM, (h + 1) * DIFF_HEAD_DIM)
        qh = x_ref[:, DIFF_WIDTH:2 * DIFF_WIDTH][:, sl] * (DIFF_HALF ** -0.5)
        kh = x_ref[:, 2 * DIFF_WIDTH:3 * DIFF_WIDTH][:, sl]
        vh = x_ref[:, 3 * DIFF_WIDTH:4 * DIFF_WIDTH][:, sl]
        rows_q += [jnp.where(lane < DIFF_HALF, qh, 0.0), jnp.where(lane >= DIFF_HALF, qh, 0.0)]
        rows_k += [kh, kh]
        rows_v += [vh, vh]
    q8 = jnp.concatenate(rows_q, axis=0)
    k8 = jnp.concatenate(rows_k, axis=0)
    v8 = jnp.concatenate(rows_v, axis=0)
    q8b = q8.astype(BF16)
    s = jnp.concatenate(
        [lax.dot_general(q8b, k_refs[p][...].astype(BF16), _NT, preferred_element_type=F32)
         for p in range(n_pages)], axis=1) + bd_ref[...]
    s_new = jnp.sum(q8 * k8, axis=-1, keepdims=True) + bn_ref[:, 0:1]
    m = jnp.maximum(jnp.max(s, axis=-1, keepdims=True), s_new)
    p = jnp.exp(s - m)
    p_new = jnp.exp(s_new - m)
    inv_l = 1.0 / (jnp.sum(p, axis=-1, keepdims=True) + p_new)
    a = p * inv_l
    r = (p_new * inv_l) * v8
    for pg in range(n_pages):
        r = r + jnp.dot(a[:, pg * page_rows:(pg + 1) * page_rows].astype(BF16),
                        v_refs[pg][...].astype(BF16), preferred_element_type=F32)
    lam = _lambda(lam_ref, lam_init)
    for h in range(DIFF_HEADS):
        o = r[2 * h:2 * h + 1, :] - lam * r[2 * h + 1:2 * h + 2, :]
        o_ref[:, h * DIFF_HEAD_DIM:(h + 1) * DIFF_HEAD_DIM] = (
            _rms(o, sub_ref[...]) * (1.0 - lam_init)).astype(BF16)


def attn_decode(proj, cache_k, cache_v, page_table, layer_j, bias_d, bias_n, lam_params, subln, lam_init):
    nb = proj.shape[0]
    n_pages = page_table.shape[1]
    n_pool, n_even = cache_k.shape[:2]
    page_rows = PAGE_SIZE * DIFF_HEADS
    ck = cache_k.reshape(n_pool, n_even, page_rows, DIFF_HEAD_DIM)
    cv = cache_v.reshape(n_pool, n_even, page_rows, DIFF_HEAD_DIM)

    assert nb >= ATT_DEC_BUFS - 1
    full = lambda shape: pl.BlockSpec(shape, lambda i, pt: (0,) * len(shape))
    grid_spec = pltpu.PrefetchScalarGridSpec(
        num_scalar_prefetch=1,
        grid=(nb,),
        in_specs=[pl.BlockSpec((None, 1, EVEN_IN), lambda i, pt: (i, 0, 0)),
                  pl.BlockSpec(memory_space=pl.ANY), pl.BlockSpec(memory_space=pl.ANY),
                  full(bias_d.shape), full(bias_n.shape), full((4, DIFF_HALF)), full((1, LANES))],
        out_specs=pl.BlockSpec((None, 1, DIFF_WIDTH), lambda i, pt: (i, 0, 0)),
        scratch_shapes=[
            pltpu.VMEM((ATT_DEC_BUFS, n_pages, page_rows, DIFF_HEAD_DIM), F32),
            pltpu.VMEM((ATT_DEC_BUFS, n_pages, page_rows, DIFF_HEAD_DIM), F32),
            pltpu.SemaphoreType.DMA((ATT_DEC_BUFS, 2)),
        ],
    )
    out = pl.pallas_call(
        functools.partial(_attn_decode_body, n_pages=n_pages, layer_j=layer_j, lam_init=lam_init),
        grid_spec=grid_spec,
        out_shape=jax.ShapeDtypeStruct((nb, 1, DIFF_WIDTH), BF16),
        compiler_params=_cparams(("arbitrary",)),
        name="attn_decode",
    )(page_table.reshape(-1), proj.reshape(nb, 1, EVEN_IN), ck, cv,
      bias_d, bias_n, lam_params, subln.reshape(1, LANES))
    return out.reshape(nb, DIFF_WIDTH)


def _blockdiag(x):
    lane = lax.broadcasted_iota(jnp.int32, x.shape, 1)
    zero = jnp.zeros_like(x)
    return jnp.concatenate([jnp.where(lane < GDN_C, x, zero), jnp.where(lane >= GDN_C, x, zero)], axis=0)


def _mmp(a, b):
    return jnp.dot(a.astype(BF16), _blockdiag(b.astype(BF16)), preferred_element_type=F32)


def _unit_lower_inverse(lows, ii, jl):
    eye = jnp.where(ii == jl, 1.0, 0.0)
    in_block = (ii // INV_BLOCK) == (jl // INV_BLOCK)
    ps = [jnp.where(in_block, -low, 0.0) for low in lows]
    offs = [jnp.where(in_block, 0.0, low) for low in lows]
    dinvs = [eye + p for p in ps]
    span = 2
    while span < INV_BLOCK:
        ps = [_mmp(p, p) for p in ps]
        dinvs = [d + _mmp(d, p) for d, p in zip(dinvs, ps)]
        span *= 2
    powers = [[-_mmp(d, off) for d, off in zip(dinvs, offs)]]
    span = 2
    while span < GDN_C // INV_BLOCK:
        powers.append([_mmp(n, n) for n in powers[-1]])
        span *= 2
    xs = dinvs
    for pw in reversed(powers):
        xs = [x + _mmp(n, x) for n, x in zip(pw, xs)]
    return xs


def _gdn_prompt_body(h_ref, gn_ref, w_ref, wgt_ref, cw_ref, alog_ref, dtb_ref, on_ref,
                     o_ref, s_ref, cst_ref, ext_ref, z_ref):
    c = pl.program_id(1)
    nh = GDN_HEADS
    npair = nh // 2
    hk = nh * GDN_DK
    cc = GDN_C
    rows_step = GDN_STEP_CHUNKS * cc
    heads_per_tile = GDN_PROJ_TN // LANES

    @pl.when(c == 0)
    def _():
        s_ref[...] = jnp.zeros_like(s_ref)
        ext_ref[0:8, :] = jnp.zeros((8, GDN_QKV), F32)

    @pl.when(c > 0)
    def _():
        ext_ref[0:8, :] = ext_ref[rows_step:rows_step + 8, :]

    xn = _rms(h_ref[...], gn_ref[...]).astype(BF16)
    abt = lax.dot_general(wgt_ref[...], xn, _NT, preferred_element_type=F32)

    def project(n):
        cols = slice(n * GDN_PROJ_TN, (n + 1) * GDN_PROJ_TN)
        tile = jnp.dot(xn, w_ref[:, cols], preferred_element_type=F32)
        if n < GDN_QKV // GDN_PROJ_TN:
            ext_ref[8:8 + rows_step, cols] = tile
        else:
            z_ref[:, n * GDN_PROJ_TN - GDN_QKV:(n + 1) * GDN_PROJ_TN - GDN_QKV] = tile

    def conv_act(col, r0):
        sl = slice(col, col + LANES)
        acc = ext_ref[8 + r0:8 + r0 + cc, sl] * cw_ref[3:4, sl]
        for i in range(GDN_CONV - 1):
            acc = acc + ext_ref[5 + i + r0:5 + i + r0 + cc, sl] * cw_ref[i:i + 1, sl]
        return _silu(acc)

    def l2n(x):
        return x * lax.rsqrt(jnp.sum(x * x, axis=-1, keepdims=True) + EPS)

    def pair(xs):
        return [jnp.concatenate([xs[2 * p], xs[2 * p + 1]], axis=1) for p in range(npair)]

    def row_pair(x):
        return pair([x[h:h + 1, :] for h in range(nh)])

    project(0)
    lane = lax.broadcasted_iota(jnp.int32, (nh, cc), 1)
    q_p, k_p, v_p, gc_b, beta_b, kdec_b, gc_rows, s_decay = [], [], [], [], [], [], [], []
    for ck in range(GDN_STEP_CHUNKS):
        r0 = ck * cc
        g = -jnp.exp(alog_ref[...]) * _softplus(abt[0:nh, r0:r0 + cc] + dtb_ref[...])
        beta = _sigmoid(abt[nh:2 * nh, r0:r0 + cc])
        gc = g
        shift = 1
        while shift < cc:
            gc = gc + jnp.where(lane >= shift, pltpu.roll(gc, shift, 1), 0.0)
            shift *= 2
        g_last = jnp.broadcast_to(gc[:, cc - 1:cc], (nh, cc))
        s_decay.append(jnp.exp(g_last))
        cols = jnp.transpose(jnp.concatenate([gc, beta, jnp.zeros((LANES - 2 * nh, cc), F32)], axis=0))
        bcast = [jnp.broadcast_to(cols[:, n:n + 1], (cc, cc)) for n in range(2 * nh)]
        gcb = pair(bcast[0:nh])
        gc_b += gcb
        beta_b += pair(bcast[nh:2 * nh])
        kdec_b += [jnp.exp(gl - x) for gl, x in zip(row_pair(g_last), gcb)]
        gc_rows += row_pair(gc)
    gam_b = [jnp.exp(x) for x in gc_b]
    ii = lax.broadcasted_iota(jnp.int32, (cc, 2 * cc), 0)
    jl = lax.broadcasted_iota(jnp.int32, (cc, 2 * cc), 1) % cc
    decay = [jnp.exp(jnp.where(ii >= jl, gb - gr, -jnp.inf)) for gb, gr in zip(gc_b, gc_rows)]

    act = {}
    for n in range(1, GDN_MAIN // GDN_PROJ_TN + 1):
        if n < GDN_MAIN // GDN_PROJ_TN:
            project(n)
        if n <= GDN_QKV // GDN_PROJ_TN:
            for hh in range(heads_per_tile):
                col = (n - 1) * GDN_PROJ_TN + hh * LANES
                for ck in range(GDN_STEP_CHUNKS):
                    a = conv_act(col, ck * cc)
                    if col < hk:
                        a = l2n(a) * (GDN_DK ** -0.5)
                    elif col < 2 * hk:
                        a = l2n(a)
                    act[(col, ck)] = a
    for ck in range(GDN_STEP_CHUNKS):
        q_p += pair([act[(h * GDN_DK, ck)] for h in range(nh)])
        k_p += pair([act[(hk + h * GDN_DK, ck)] for h in range(nh)])
        v_p += pair([act[(2 * hk + h * GDN_DV, ck)] for h in range(nh)])

    gram = [lax.dot_general(jnp.concatenate([q, k], axis=0).astype(BF16), _blockdiag(k.astype(BF16)), _NT,
                            preferred_element_type=F32) for q, k in zip(q_p, k_p)]
    qk = [gm[0:cc, :] * d for gm, d in zip(gram, decay)]
    lows = [jnp.where(ii > jl, b * gm[cc:2 * cc, :] * d, 0.0) for b, gm, d in zip(beta_b, gram, decay)]
    tinv = _unit_lower_inverse(lows, ii, jl)
    w =[_mmp(t, b * gm * k) for t, b, gm, k in zip(tinv, beta_b, gam_b, k_p)]
    u0 = [_mmp(t, b * v) for t, b, v in zip(tinv, beta_b, v_p)]
    wq = [jnp.concatenate([wp, gm * q], axis=0).astype(BF16) for wp, gm, q in zip(w, gam_b, q_p)]
    kd = [(kdb * k).astype(BF16) for kdb, k in zip(kdec_b, k_p)]

    state = [s_ref[h] for h in range(nh)]
    zero = jnp.zeros((GDN_DK, GDN_DV), BF16)
    for ck in range(GDN_STEP_CHUNKS):
        r0 = ck * cc
        ent = range(ck * npair, (ck + 1) * npair)
        s_bd = [jnp.concatenate([jnp.concatenate([state[2 * p].astype(BF16), zero], axis=1),
                                 jnp.concatenate([zero, state[2 * p + 1].astype(BF16)], axis=1)], axis=0)
                for p in range(npair)]
        ws_qs = [jnp.dot(wq[e], sb, preferred_element_type=F32) for e, sb in zip(ent, s_bd)]
        u = [u0[e] - x[0:cc, :] for e, x in zip(ent, ws_qs)]
        o = [x[cc:2 * cc, :] + _mmp(qk[e], b) for e, x, b in zip(ent, ws_qs, u)]
        upd = [lax.dot_general(kd[e], b.astype(BF16), _TN, preferred_element_type=F32)
               for e, b in zip(ent, u)]
        new_state = []
        for h in range(nh):
            half = slice((h % 2) * cc, (h % 2 + 1) * cc)
            new_state.append(s_decay[ck][h:h + 1, :] * state[h] + upd[h // 2][half, half])
            z = z_ref[r0:r0 + cc, h * GDN_DV:(h + 1) * GDN_DV]
            o_ref[r0:r0 + cc, h * GDN_DV:(h + 1) * GDN_DV] = (
                _rms(o[h // 2][:, half], on_ref[...]) * _silu(z)).astype(BF16)
        state = new_state
    for h in range(nh):
        s_ref[h] = state[h]

    @pl.when(c == pl.num_programs(1) - 1)
    def _():
        cst_ref[...] = ext_ref[rows_step:rows_step + 8, :]


def gdn_prompt(h, gain, w_in, w_gates, idx, conv_w, a_log, dt_bias, o_norm):
    b, t, d = h.shape
    rows = GDN_STEP_CHUNKS * GDN_C
    nc = t // rows
    nh = GDN_HEADS
    return pl.pallas_call(
        _gdn_prompt_body,
        grid=(b, nc),
        in_specs=[
            pl.BlockSpec((None, rows, d), lambda i, c: (i, c, 0)),
            pl.BlockSpec((1, d), lambda i, c: (0, 0)),
            pl.BlockSpec((None, d, GDN_MAIN), lambda i, c: (idx, 0, 0), pipeline_mode=pl.Buffered(1)),
            pl.BlockSpec((None, LANES, d), lambda i, c: (idx, 0, 0), pipeline_mode=pl.Buffered(1)),
            pl.BlockSpec((GDN_CONV, GDN_QKV), lambda i, c: (0, 0)),
            pl.BlockSpec((nh, 1), lambda i, c: (0, 0)),
            pl.BlockSpec((nh, 1), lambda i, c: (0, 0)),
            pl.BlockSpec((1, GDN_DV), lambda i, c: (0, 0)),
        ],
        out_specs=[
            pl.BlockSpec((None, rows, nh * GDN_DV), lambda i, c: (i, c, 0)),
            pl.BlockSpec((None, nh, GDN_DK, GDN_DV), lambda i, c: (i, 0, 0, 0)),
            pl.BlockSpec((None, 8, GDN_QKV), lambda i, c: (i, 0, 0)),
        ],
        out_shape=[
            jax.ShapeDtypeStruct((b, t, nh * GDN_DV), BF16),
            jax.ShapeDtypeStruct((b, nh, GDN_DK, GDN_DV), F32),
            jax.ShapeDtypeStruct((b, 8, GDN_QKV), F32),
        ],
        scratch_shapes=[pltpu.VMEM((8 + rows, GDN_QKV), F32), pltpu.VMEM((rows, nh * GDN_DV), F32)],
        compiler_params=_cparams(("parallel", "arbitrary")),
        name="gdn_prompt",
    )(h, gain.reshape(1, d), w_in, w_gates, conv_w, a_log.reshape(nh, 1), dt_bias.reshape(nh, 1),
      o_norm.reshape(1, GDN_DV))


def _gdn_decode_body(x_ref, cs_ref, ab_ref, cw_ref, alog_ref, dtb_ref, on_ref, s0_ref, *rest, first, layer_j):
    o_ref, s_ref = rest[-2:]
    nh = GDN_HEADS
    pad = jnp.zeros((8 - 2, GDN_DK), F32)
    if first:
        for slot in range(s_ref.shape[0]):
            if slot != layer_j:
                s_ref[slot] = jnp.zeros(s_ref.shape[1:], F32)
    pad7 = jnp.zeros((8 - 1, GDN_DK), F32)
    seqs = range(GDN_DEC_ROWS)
    q8, k8, v8, gam8, beta8, qk8 = [], [], [], [], [], []
    for r in seqs:
        x = x_ref[r, 0:3 * nh, :]
        conv = x * cw_ref[GDN_CONV - 1]
        for i in range(GDN_CONV - 1):
            conv = conv + cs_ref[r, i] * cw_ref[i]
        act = conv * _sigmoid(conv)
        qa, ka = act[0:nh], act[nh:2 * nh]
        q8.append(qa * lax.rsqrt(jnp.sum(qa * qa, axis=-1, keepdims=True) + EPS) * (GDN_DK ** -0.5))
        k8.append(ka * lax.rsqrt(jnp.sum(ka * ka, axis=-1, keepdims=True) + EPS))
        v8.append(act[2 * nh:3 * nh])
        g = -jnp.exp(alog_ref[...]) * _softplus(ab_ref[r, 0:nh, :] + dtb_ref[...])
        gam8.append(jnp.broadcast_to(jnp.exp(g), (nh, GDN_DV)))
        beta8.append(jnp.broadcast_to(_sigmoid(ab_ref[r, nh:2 * nh, :]), (nh, GDN_DV)))
        qk8.append(jnp.broadcast_to(jnp.sum(q8[r] * k8[r], axis=-1, keepdims=True), (nh, GDN_DV)))
    pairs = [(r, h) for r in seqs for h in range(nh)]
    s_old = [s0_ref[r, h] for r, h in pairs]
    qk_s = [jnp.dot(jnp.concatenate([q8[r][h:h + 1, :], k8[r][h:h + 1, :], pad], axis=0).astype(BF16),
                    s.astype(BF16), preferred_element_type=F32) for (r, h), s in zip(pairs, s_old)]
    u = [beta8[r][h:h + 1, :] * (v8[r][h:h + 1, :] - gam8[r][h:h + 1, :] * x[1:2, :])
         for (r, h), x in zip(pairs, qk_s)]
    outs = [gam8[r][h:h + 1, :] * x[0:1, :] + qk8[r][h:h + 1, :] * b for (r, h), x, b in zip(pairs, qk_s, u)]
    outer = [lax.dot_general(jnp.concatenate([k8[r][h:h + 1, :], pad7], axis=0).astype(BF16),
                             jnp.concatenate([b, pad7], axis=0).astype(BF16), _TN, preferred_element_type=F32)
             for (r, h), b in zip(pairs, u)]
    for (r, h), s, x in zip(pairs, s_old, outer):
        new = gam8[r][h:h + 1, :] * s + x
        if first:
            s_ref[layer_j, r, h] = new
        else:
            s_ref[r, h] = new
    for r in seqs:
        o8 = jnp.concatenate(outs[r * nh:(r + 1) * nh], axis=0)
        z8 = x_ref[r, 3 * nh:4 * nh, :]
        o_ref[r] = (_rms(o8, on_ref[...]) * (z8 * _sigmoid(z8))).astype(BF16)


def gdn_decode(proj, gates, conv_state, s0_all, layer_j, s_new_all, conv_w, a_log, dt_bias, o_norm):
    nb = proj.shape[0]
    nh = GDN_HEADS
    nrow = GDN_QKV // LANES
    first = s_new_all is None
    in_specs = [
        pl.BlockSpec((GDN_DEC_ROWS, GDN_MAIN // LANES, LANES), lambda i: (i, 0, 0)),
        pl.BlockSpec((GDN_DEC_ROWS, GDN_CONV - 1, nrow, LANES), lambda i: (i, 0, 0, 0)),
        pl.BlockSpec((GDN_DEC_ROWS, 2 * nh, 1), lambda i: (i, 0, 0)),
        pl.BlockSpec((GDN_CONV, nrow, LANES), lambda i: (0, 0, 0)),
        pl.BlockSpec((nh, 1), lambda i: (0, 0)),
        pl.BlockSpec((nh, 1), lambda i: (0, 0)),
        pl.BlockSpec((1, GDN_DV), lambda i: (0, 0)),
        pl.BlockSpec((None, GDN_DEC_ROWS, nh, GDN_DK, GDN_DV), lambda i: (layer_j, i, 0, 0, 0)),
    ]
    args = [proj.reshape(nb, GDN_MAIN // LANES, LANES),
            conv_state.reshape(nb, GDN_CONV - 1, nrow, LANES),
            gates.reshape(nb, 2 * nh, 1),
            conv_w.reshape(GDN_CONV, nrow, LANES),
            a_log.reshape(nh, 1), dt_bias.reshape(nh, 1), o_norm.reshape(1, GDN_DV), s0_all]
    if first:
        s_spec = pl.BlockSpec((s0_all.shape[0], GDN_DEC_ROWS, nh, GDN_DK, GDN_DV), lambda i: (0, i, 0, 0, 0))
        aliases = {}
    else:
        in_specs.append(pl.BlockSpec(memory_space=pl.ANY))
        args.append(s_new_all)
        s_spec = pl.BlockSpec((None, GDN_DEC_ROWS, nh, GDN_DK, GDN_DV), lambda i: (layer_j, i, 0, 0, 0))
        aliases = {len(args) - 1: 1}
    o, s = pl.pallas_call(
        functools.partial(_gdn_decode_body, first=first, layer_j=layer_j),
        grid=(nb // GDN_DEC_ROWS,),
        in_specs=in_specs,
        out_specs=[pl.BlockSpec((GDN_DEC_ROWS, nh, GDN_DV), lambda i: (i, 0, 0)), s_spec],
        out_shape=[
            jax.ShapeDtypeStruct((nb, nh, GDN_DV), BF16),
            jax.ShapeDtypeStruct(s0_all.shape, F32),
        ],
        input_output_aliases=aliases,
        compiler_params=_cparams(("parallel",)),
        name="gdn_decode",
    )(*args)
    return o.reshape(nb, nh * GDN_DV), s


def kernel(x_prompt, x_sample, cache_k, cache_v, page_table, state_pool, state_conv, state_delta,
           norm_mix, norm_ffn, norm_final, rel_bias,
           w_in_even, pool_w, pool_scale, lambda_q1, lambda_k1, lambda_q2, lambda_k2, subln_w, w_out_even,
           w_in_odd, conv_w, a_log, dt_bias, o_norm, w_out_odd,
           w_gate_up, w_down):
    bp, t, d = x_prompt.shape
    bs = x_sample.shape[0]
    mp = bp * t
    nh = GDN_HEADS
    past = page_table.shape[1] * PAGE_SIZE
    tm_s = bs

    hp = x_prompt.reshape(mp, d)
    hs = x_sample.reshape(bs, d)
    bias_p, bias_d, bias_n = rel_bias_tiles(rel_bias, past)

    w_in_e = w_in_even.astype(BF16)
    w_in_o = w_in_odd.astype(BF16)
    w_gate_cols = lax.optimization_barrier(w_in_odd[:, :, GDN_MAIN:])
    w_gates = jnp.zeros((w_in_odd.shape[0], LANES, d), F32).at[:, 0:2 * nh, :].set(
        jnp.transpose(w_gate_cols, (0, 2, 1))).astype(BF16)
    w_out_e = w_out_even.astype(BF16)
    w_out_o = w_out_odd.astype(BF16)
    w_gu = w_gate_up.astype(BF16)
    w_d = w_down.astype(BF16)
    pool_wb = pool_w.astype(BF16)

    k_s, v_s, pool_p, pool_s = [], [], [], []
    conv_p, conv_s, delta_p = [], [], []
    kv_p = None
    n_even = w_in_even.shape[0]
    delta_s = None
    for layer in range(DEPTH):
        j = layer // 2
        last = layer == DEPTH - 1
        if layer % 2 == 0:
            w_out = w_out_e
            pw = pool_wb[j]
            lam_init = 0.8 - 0.6 * math.exp(-0.3 * layer)
            lam_params = jnp.stack([lambda_q1[j], lambda_k1[j], lambda_q2[j], lambda_k2[j]])

            qkv_p, ypool_p, tail_p, kv_p = even_proj_prompt(hp, norm_mix[layer], w_in_e, j, pw, pool_scale[j],
                                                            t, n_even, kv_p, tm=EVEN_TM)
            proj_s, ks_, vs_ = norm_matmul(hs, norm_mix[layer], w_in_e, j, tm=tm_s, tn=DIFF_WIDTH)

            oatt_p = attn_prompt(qkv_p.reshape(bp, t, 3 * DIFF_WIDTH), bias_p, lam_params, subln_w[j], lam_init)
            ypool_s = pool_decode(jnp.transpose(state_pool[j], (1, 0, 2)), proj_s, pw, pool_scale[j])
            oatt_s = attn_decode(proj_s, cache_k, cache_v, page_table, j, bias_d, bias_n,
                                 lam_params, subln_w[j], lam_init)

            kv_shape = (DIFF_HEADS, DIFF_HEAD_DIM)
            k_s.append(ks_.reshape(bs, 1, *kv_shape))
            v_s.append(vs_.reshape(bs, 1, *kv_shape))
            pool_p.append(tail_p[:, POOL_PAD - POOL_BUF:, :])
            pool_s.append(jnp.concatenate([state_pool[j][:, 1:], proj_s[:, None, 0:POOL_WIDTH]], axis=1))

            mix_p = ((ypool_p, 0), (oatt_p.reshape(mp, DIFF_WIDTH), 0))
            mix_s = ((ypool_s, 0), (oatt_s, 0))
        else:
            w_out = w_out_o

            proj_s, gates_s = norm_matmul(hs, norm_mix[layer], w_in_o, j, w_gates, tm=tm_s, tn=512)

            o_p, s_p, tail_p = gdn_prompt(hp.reshape(bp, t, d), norm_mix[layer], w_in_o, w_gates, j,
                                          conv_w[j], a_log[j], dt_bias[j], o_norm[j])
            o_s, delta_s = gdn_decode(proj_s, gates_s[0:2 * nh].T, state_conv[j], state_delta, j, delta_s,
                                      conv_w[j], a_log[j], dt_bias[j], o_norm[j])

            conv_p.append(tail_p[:, 8 - (GDN_CONV - 1):, :])
            conv_s.append(jnp.concatenate([state_conv[j][:, 1:], proj_s[:, None, 0:GDN_QKV]], axis=1))
            delta_p.append(s_p)

            o_p = o_p.reshape(mp, nh * GDN_DV)
            mix_p = ((o_p, 0), (o_p, 1))
            mix_s = ((o_s, 0), (o_s, 1))

        g_fin = norm_final if last else None
        hp = outproj_ffn(hp, mix_p[0], mix_p[1], w_out, j, norm_ffn[layer], w_gu, w_d, layer, g_fin, tm=FFN_TM)
        hs = outproj_ffn(hs, mix_s[0], mix_s[1], w_out, j, norm_ffn[layer], w_gu, w_d, layer, g_fin, tm=tm_s)

    return (hp.reshape(bp, t, d), hs.reshape(bs, 1, d),
            kv_p[0].reshape(bp, n_even, t, DIFF_HEADS, DIFF_HEAD_DIM),
            kv_p[1].reshape(bp, n_even, t, DIFF_HEADS, DIFF_HEAD_DIM),
            jnp.stack(k_s, axis=1), jnp.stack(v_s, axis=1),
            jnp.stack(pool_p), jnp.stack(pool_s), jnp.stack(conv_p), jnp.stack(conv_s),
            jnp.stack(delta_p), delta_s)
```
